```python
import jax
import jax.numpy as jnp
from jax import lax

D_MODEL = 1024
BATCH = 8
SEQ = 8192
DEPTH = 1
DEC_BATCH = 16
DEC_SEQ = 16
PAST_LEN = 1024

CHUNK = 64
BAND_CHUNKS = 8
WINDOW = BAND_CHUNKS * CHUNK
N_MEM = 256
A_HEADS = 8
A_HEAD_DIM = 64
A_WIDTH = A_HEADS * A_HEAD_DIM
REL_MAX = 128
G_HEADS = 4
G_DK = 64
G_DV = 128
G_KW = G_HEADS * G_DK
G_VW = G_HEADS * G_DV
G_RANK = 16
G_TAU = 16.0
C_HEADS = 4
C_HEAD_DIM = 128
C_WIDTH = C_HEADS * C_HEAD_DIM
N_BRANCH = 3
N_EXPERTS = 32
TOP_K = 4
D_EXPERT = 1024
SWIGLU_LIMIT = 7.0
SWIGLU_ALPHA = 1.702
MOE_BLOCK = 128
EPS = 1e-6
NEG_INF = -1e30
F32 = jnp.float32

IN_SIZES = (A_WIDTH, A_WIDTH, A_WIDTH, G_KW, G_KW, G_VW, G_VW, G_RANK, C_WIDTH, N_BRANCH * D_MODEL)
IN_WIDTH = sum(IN_SIZES)
IN_OFFSETS = tuple(sum(IN_SIZES[:i + 1]) for i in range(len(IN_SIZES) - 1))

kernel_name = 'chunk_stream_hybrid'


def rmsnorm(x, g):
    xf = x.astype(F32)
    y = xf * lax.rsqrt(jnp.mean(xf * xf, axis=-1, keepdims=True) + EPS)
    return (y * g.astype(F32)).astype(x.dtype)


def rel_position_bias(table, n_q, n_k, offset):
    rel = jnp.arange(n_q)[:, None] + offset - jnp.arange(n_k)[None, :]
    idx = jnp.clip(rel, -REL_MAX, REL_MAX) + REL_MAX
    return jnp.take(table, idx, axis=1).astype(F32)


def band_attend(q, k, v, bias, valid):
    s = jnp.einsum('bthd,blhd->bhtl', q, k).astype(F32) * (q.shape[-1] ** -0.5) + bias
    if valid is not None:
        s = jnp.where(valid, s, NEG_INF)
    p = jax.nn.softmax(s, axis=-1).astype(v.dtype)
    return jnp.einsum('bhtl,blhd->bthd', p, v)


def chunk_attention_prompt(q, k, v, table):
    B, S, H, D = q.shape
    n_chunks = S // CHUNK
    band = WINDOW + CHUNK
    pad = ((0, 0), (WINDOW, 0), (0, 0), (0, 0))
    kp = jnp.pad(k, pad)
    vp = jnp.pad(v, pad)
    bias = rel_position_bias(table, CHUNK, band, WINDOW)
    kidx = jnp.arange(band)

    def one_chunk(c):
        start = c * CHUNK
        qc = lax.dynamic_slice_in_dim(q, start, CHUNK, axis=1)
        kc = lax.dynamic_slice_in_dim(kp, start, band, axis=1)
        vc = lax.dynamic_slice_in_dim(vp, start, band, axis=1)
        valid = (start + kidx >= WINDOW)[None, :]
        return band_attend(qc, kc, vc, bias, valid)

    out = lax.map(one_chunk, jnp.arange(n_chunks))
    return jnp.moveaxis(out, 0, 1).reshape(B, S, H, D)


def chunk_attention_sample(q, k, v, cache_k, cache_v, table):
    T = q.shape[1]
    P = cache_k.shape[1]
    kk = jnp.concatenate([cache_k.astype(k.dtype), k], axis=1)
    vv = jnp.concatenate([cache_v.astype(v.dtype), v], axis=1)
    bias = rel_position_bias(table, T, P + T, P)
    return band_attend(q, kk, vv, bias, None)


def gla_chunked(q, k, v, log_a, s0, block):
    B, L, H, K = q.shape
    V = v.shape[-1]
    n = L // block

    def to_blocks(t):
        return jnp.moveaxis(t.reshape(B, n, block, H, t.shape[-1]), 1, 0)

    causal = jnp.tril(jnp.ones((block, block), bool))[None, :, :, None, None]

    def step(S, inp):
        qb, kb, vb, ab = inp
        qf, kf, vf = qb.astype(F32), kb.astype(F32), vb.astype(F32)
        b = jnp.cumsum(ab.astype(F32), axis=1)
        o_inter = jnp.einsum('bthk,bhkv->bthv', qf * jnp.exp(b), S)
        decay = jnp.exp(jnp.where(causal, b[:, :, None] - b[:, None, :], -jnp.inf))
        att = jnp.einsum('bthk,bshk,btshk->bhts', qf, kf, decay)
        o_intra = jnp.einsum('bhts,bshv->bthv', att, vf)
        b_last = b[:, -1]
        k_dec = kf * jnp.exp(b_last[:, None] - b)
        S = jnp.exp(b_last)[..., None] * S + jnp.einsum('bshk,bshv->bhkv', k_dec, vf)
        return S, o_inter + o_intra

    S, o = lax.scan(step, s0.astype(F32), (to_blocks(q), to_blocks(k), to_blocks(v), to_blocks(log_a)))
    o = jnp.moveaxis(o, 0, 1).reshape(B, L, H, V)
    return o.astype(v.dtype), S


def memory_kv(mem, g_mem, w_kv, g_k):
    B, M, _ = mem.shape
    k, v = jnp.split(rmsnorm(mem, g_mem) @ w_kv, 2, axis=-1)
    k = rmsnorm(k.reshape(B, M, C_HEADS, C_HEAD_DIM), g_k)
    return k, v.reshape(B, M, C_HEADS, C_HEAD_DIM)


def memory_attention(q, k, v):
    s = jnp.einsum('bthd,bmhd->bhtm', q, k.astype(q.dtype)).astype(F32) * (C_HEAD_DIM ** -0.5)
    p = jax.nn.softmax(s, axis=-1).astype(q.dtype)
    return jnp.einsum('bhtm,bmhd->bthd', p, v.astype(q.dtype))


def moe_ffn(h, w_router, b_router, w_gate_up, b_gate_up, w_down, b_down):
    N, D = h.shape
    logits = (h @ w_router + b_router).astype(F32)
    top_vals, top_idx = lax.top_k(logits, TOP_K)
    weights = jax.nn.softmax(top_vals, axis=-1)
    A = N * TOP_K
    e_flat = top_idx.reshape(-1).astype(jnp.int32)
    tok_flat = jnp.arange(A, dtype=jnp.int32) // TOP_K
    order = jnp.argsort(e_flat, stable=True)
    e_sorted = e_flat[order]
    counts = jnp.bincount(e_flat, length=N_EXPERTS).astype(jnp.int32)
    starts = jnp.cumsum(counts) - counts
    padded = (counts + MOE_BLOCK - 1) // MOE_BLOCK * MOE_BLOCK
    pends = jnp.cumsum(padded)
    pstarts = pends - padded
    dest_sorted = pstarts[e_sorted] + (jnp.arange(A, dtype=jnp.int32) - starts[e_sorted])
    dest = jnp.zeros((A,), jnp.int32).at[order].set(dest_sorted.astype(jnp.int32))
    P = (A + N_EXPERTS * (MOE_BLOCK - 1) + MOE_BLOCK - 1) // MOE_BLOCK * MOE_BLOCK
    n_blocks = P // MOE_BLOCK
    row_tok = jnp.full((P,), N, jnp.int32).at[dest].set(tok_flat)
    h_pad = jnp.concatenate([h, jnp.zeros((1, D), h.dtype)], axis=0)
    block_expert = jnp.minimum(
        jnp.searchsorted(pends, jnp.arange(n_blocks, dtype=jnp.int32) * MOE_BLOCK, side='right'),
        N_EXPERTS - 1)

    def run_block(args):
        rows, e = args
        xb = h_pad[rows]
        gate, up = jnp.split(xb @ w_gate_up[e] + b_gate_up[e], 2, axis=-1)
        gate = jnp.minimum(gate, SWIGLU_LIMIT)
        up = jnp.clip(up, -SWIGLU_LIMIT, SWIGLU_LIMIT)
        act = (up + 1.0) * (gate * jax.nn.sigmoid(SWIGLU_ALPHA * gate))
        return act @ w_down[e] + b_down[e]

    out_rows = lax.map(run_block, (row_tok.reshape(n_blocks, MOE_BLOCK), block_expert)).reshape(P, D)
    per_assign = out_rows[dest].reshape(N, TOP_K, D)
    return jnp.einsum('nkd,nk->nd', per_assign, weights.astype(h.dtype))


def setup_inputs(seed: int = 0) -> dict:
    key = jax.random.key(seed)
    ks = jax.random.split(key, 32)
    L = DEPTH
    a_past = min(WINDOW, PAST_LEN)

    def nrm(i, shape, scale):
        return jax.random.normal(ks[i], shape, F32) * scale

    return {
        'x_prompt': nrm(0, (BATCH, SEQ, D_MODEL), 1.0),
        'x_sample': nrm(1, (DEC_BATCH, DEC_SEQ, D_MODEL), 1.0),
        'mem_prompt': nrm(2, (BATCH, N_MEM, D_MODEL), 1.0),
        'cache_attn_k': nrm(3, (L, DEC_BATCH, a_past, A_HEADS, A_HEAD_DIM), 1.0),
        'cache_attn_v': nrm(4, (L, DEC_BATCH, a_past, A_HEADS, A_HEAD_DIM), 1.0),
        'state_gla': nrm(5, (L, DEC_BATCH, G_HEADS, G_DK, G_DV), 0.5),
        'cache_mem_k': nrm(6, (L, DEC_BATCH, N_MEM, C_HEADS, C_HEAD_DIM), 1.0),
        'cache_mem_v': nrm(7, (L, DEC_BATCH, N_MEM, C_HEADS, C_HEAD_DIM), 1.0),
        'norm_mix': 1.0 + nrm(8, (L, D_MODEL), 0.05),
        'w_in': nrm(9, (L, D_MODEL, IN_WIDTH), D_MODEL ** -0.5),
        'a_q_norm': 1.0 + nrm(10, (L, A_HEAD_DIM), 0.05),
        'a_k_norm': 1.0 + nrm(11, (L, A_HEAD_DIM), 0.05),
        'rel_bias_table': nrm(12, (L, A_HEADS, 2 * REL_MAX + 1), 0.5),
        'w_a_o': nrm(13, (L, A_WIDTH, D_MODEL), A_WIDTH ** -0.5),
        'w_gla_a_up': nrm(14, (L, G_RANK, G_KW), G_RANK ** -0.5),
        'b_gla_a': nrm(15, (L, G_KW), 0.1),
        'gla_out_norm': 1.0 + nrm(16, (L, G_DV), 0.05),
        'w_b_o': nrm(17, (L, G_VW, D_MODEL), G_VW ** -0.5),
        'c_q_norm': 1.0 + nrm(18, (L, C_HEAD_DIM), 0.05),
        'c_k_norm': 1.0 + nrm(19, (L, C_HEAD_DIM), 0.05),
        'norm_mem': 1.0 + nrm(20, (L, D_MODEL), 0.05),
        'w_mem_kv': nrm(21, (L, D_MODEL, 2 * C_WIDTH), D_MODEL ** -0.5),
        'w_c_o': nrm(22, (L, C_WIDTH, D_MODEL), C_WIDTH ** -0.5),
        'b_gate': nrm(23, (L, N_BRANCH * D_MODEL), 0.1),
        'w_out': nrm(24, (L, D_MODEL, D_MODEL), D_MODEL ** -0.5),
        'norm_ffn': 1.0 + nrm(25, (L, D_MODEL), 0.05),
        'w_router': nrm(26, (L, D_MODEL, N_EXPERTS), D_MODEL ** -0.5),
        'b_router': nrm(27, (L, N_EXPERTS), 0.01),
        'w_gate_up': nrm(28, (L, N_EXPERTS, D_MODEL, 2 * D_EXPERT), D_MODEL ** -0.5),
        'b_gate_up': nrm(29, (L, N_EXPERTS, 2 * D_EXPERT), 0.02),
        'w_down': nrm(30, (L, N_EXPERTS, D_EXPERT, D_MODEL), D_EXPERT ** -0.5),
        'b_down': nrm(31, (L, N_EXPERTS, D_MODEL), 0.02),
    }


def reference(x_prompt, x_sample, mem_prompt, cache_attn_k, cache_attn_v, state_gla,
              cache_mem_k, cache_mem_v, norm_mix, w_in, a_q_norm, a_k_norm, rel_bias_table,
              w_a_o, w_gla_a_up, b_gla_a, gla_out_norm, w_b_o, c_q_norm, c_k_norm, norm_mem,
              w_mem_kv, w_c_o, b_gate, w_out, norm_ffn, w_router, b_router, w_gate_up,
              b_gate_up, w_down, b_down):

    def layer(x, l, attn_fn, gla_fn, mem_k, mem_v):
        B, T, _ = x.shape
        h = rmsnorm(x, norm_mix[l])
        a_q, a_k, a_v, g_q, g_k, g_v, g_r, g_lr, c_q, gate_logits = jnp.split(
            h @ w_in[l], IN_OFFSETS, axis=-1)
        a_q = rmsnorm(a_q.reshape(B, T, A_HEADS, A_HEAD_DIM), a_q_norm[l])
        a_k = rmsnorm(a_k.reshape(B, T, A_HEADS, A_HEAD_DIM), a_k_norm[l])
        a_v = a_v.reshape(B, T, A_HEADS, A_HEAD_DIM)
        y_a = attn_fn(a_q, a_k, a_v).reshape(B, T, A_WIDTH) @ w_a_o[l]
        log_a = jax.nn.log_sigmoid((g_lr @ w_gla_a_up[l] + b_gla_a[l]).astype(F32)) / G_TAU
        o_g, s_g = gla_fn(g_q.reshape(B, T, G_HEADS, G_DK) * (G_DK ** -0.5),
                          g_k.reshape(B, T, G_HEADS, G_DK),
                          g_v.reshape(B, T, G_HEADS, G_DV),
                          log_a.reshape(B, T, G_HEADS, G_DK))
        o_g = rmsnorm(o_g, gla_out_norm[l]).reshape(B, T, G_VW) * jax.nn.silu(g_r)
        y_b = o_g @ w_b_o[l]
        c_q = rmsnorm(c_q.reshape(B, T, C_HEADS, C_HEAD_DIM), c_q_norm[l])
        y_c = memory_attention(c_q, mem_k, mem_v).reshape(B, T, C_WIDTH) @ w_c_o[l]
        g = jax.nn.sigmoid(gate_logits + b_gate[l]).reshape(B, T, N_BRANCH, D_MODEL)
        merged = g[:, :, 0] * y_a + g[:, :, 1] * y_b + g[:, :, 2] * y_c
        x = x + merged @ w_out[l]
        h2 = rmsnorm(x, norm_ffn[l])
        y_ffn = moe_ffn(h2.reshape(B * T, D_MODEL), w_router[l], b_router[l], w_gate_up[l],
                        b_gate_up[l], w_down[l], b_down[l]).reshape(B, T, D_MODEL)
        return x + y_ffn, a_k, a_v, s_g

    xp = x_prompt
    xs = x_sample
    t_sample = x_sample.shape[1]
    keep = min(WINDOW, x_prompt.shape[1])
    p_k, p_v, p_s, p_mk, p_mv, s_k, s_v, s_s = [], [], [], [], [], [], [], []
    for l in range(DEPTH):
        mk, mv = memory_kv(mem_prompt, norm_mem[l], w_mem_kv[l], c_k_norm[l])
        s0 = jnp.zeros((xp.shape[0], G_HEADS, G_DK, G_DV), F32)
        xp, ak, av, sg = layer(
            xp, l,
            lambda q, k, v, l=l: chunk_attention_prompt(q, k, v, rel_bias_table[l]),
            lambda q, k, v, a, s0=s0: gla_chunked(q, k, v, a, s0, CHUNK),
            mk, mv)
        p_k.append(ak[:, -keep:])
        p_v.append(av[:, -keep:])
        p_s.append(sg)
        p_mk.append(mk)
        p_mv.append(mv)
        xs, ak, av, sg = layer(
            xs, l,
            lambda q, k, v, l=l: chunk_attention_sample(q, k, v, cache_attn_k[l], cache_attn_v[l],
                                                        rel_bias_table[l]),
            lambda q, k, v, a, l=l: gla_chunked(q, k, v, a, state_gla[l], t_sample),
            cache_mem_k[l], cache_mem_v[l])
        s_k.append(ak)
        s_v.append(av)
        s_s.append(sg)

    attn_k_prompt = jnp.stack(p_k)
    attn_v_prompt = jnp.stack(p_v)
    gla_state_prompt = jnp.stack(p_s)
    mem_k_prompt = jnp.stack(p_mk)
    mem_v_prompt = jnp.stack(p_mv)
    attn_k_sample = jnp.stack(s_k)
    attn_v_sample = jnp.stack(s_v)
    gla_state_sample = jnp.stack(s_s)
    return (xp, xs, attn_k_prompt, attn_v_prompt, gla_state_prompt, mem_k_prompt, mem_v_prompt,
            attn_k_sample, attn_v_sample, gla_state_sample)
```

```python
import functools

import jax
import jax.numpy as jnp
from jax import lax
from jax.experimental import pallas as pl
from jax.experimental.pallas import tpu as pltpu

F32 = jnp.float32
BF16 = jnp.bfloat16
U32 = jnp.uint32
I32 = jnp.int32

D_MODEL = 1024
CHUNK = 64
BAND_CHUNKS = 8
WINDOW = BAND_CHUNKS * CHUNK
N_MEM = 256
A_HEADS, A_HEAD_DIM = 8, 64
A_WIDTH = A_HEADS * A_HEAD_DIM
REL_MAX = 128
G_HEADS, G_DK, G_DV = 4, 64, 128
G_KW, G_VW = G_HEADS * G_DK, G_HEADS * G_DV
G_RANK = 16
G_TAU = 16.0
G_SUB = 16
C_HEADS, C_HEAD_DIM = 4, 128
C_WIDTH = C_HEADS * C_HEAD_DIM
N_BRANCH = 3
N_EXPERTS = 32
TOP_K = 4
D_EXPERT = 1024
SWIGLU_LIMIT = 7.0
SWIGLU_ALPHA = 1.702
EPS = 1e-6
NEG_INF = -1e30

LANES = 128
HALF = D_MODEL // 2
ROW_TILE = 512
ATTN_SUB = 128
FFN_BLOCK = 256
VMEM_LIMIT = 56 * 1024 * 1024

OFF_AQ, OFF_AK, OFF_AV = 0, 512, 1024
OFF_GQ, OFF_GK, OFF_GV, OFF_GR = 1536, 1792, 2048, 2560
OFF_CQ, OFF_GATE, OFF_LR = 3072, 3584, 6656
IN_COLS = OFF_LR + LANES


def _params(sem):
    return pltpu.CompilerParams(dimension_semantics=sem, vmem_limit_bytes=VMEM_LIMIT)


def _sigmoid(x):
    return 0.5 * jnp.tanh(0.5 * x) + 0.5


def _head_rms(y, head_dim):
    cols = []
    for p in range(y.shape[1] // LANES):
        blk = y[:, p * LANES:(p + 1) * LANES]
        sq = blk * blk
        if head_dim == LANES:
            sc = lax.rsqrt(jnp.sum(sq, -1, keepdims=True) * (1.0 / LANES) + EPS)
        else:
            lo = lax.broadcasted_iota(I32, blk.shape, 1) < head_dim
            s_lo = jnp.sum(jnp.where(lo, sq, 0.0), -1, keepdims=True)
            s_hi = jnp.sum(jnp.where(lo, 0.0, sq), -1, keepdims=True)
            sc = jnp.where(lo, lax.rsqrt(s_lo * (1.0 / head_dim) + EPS),
                           lax.rsqrt(s_hi * (1.0 / head_dim) + EPS))
        cols.append(blk * sc)
    return jnp.concatenate(cols, axis=-1)


def _split_bf16(x):
    hi = x.astype(BF16)
    lo = (x - hi.astype(F32)).astype(BF16)
    return hi, lo


def _pack_rows(x):
    lo = lax.bitcast_convert_type(x[:, :HALF].astype(BF16).astype(F32), U32)
    hi = lax.bitcast_convert_type(x[:, HALF:].astype(BF16).astype(F32), U32)
    return (lo >> 16) | (hi & jnp.uint32(0xFFFF0000))


def _unpack_rows(u):
    lo = lax.bitcast_convert_type(u << 16, F32)
    hi = lax.bitcast_convert_type(u & jnp.uint32(0xFFFF0000), F32)
    return lo, hi


def _inproj_kernel(x_ref, nm_ref, w_ref, aqn_ref, akn_ref, cqn_ref, wup_ref, bla_ref, bg_ref,
                   aq_ref, ak_ref, av_ref, gq_ref, gk_ref, gv_ref, gr_ref, la_ref, cq_ref, gt_ref,
                   akt_ref, avt_ref, *, n_tiles, n_tail):
    j = pl.program_id(1)
    x = x_ref[0]
    h = (x * lax.rsqrt(jnp.mean(x * x, -1, keepdims=True) + EPS) * nm_ref[...]).astype(BF16)

    def seg(off, width):
        return jnp.dot(h, w_ref[:, off:off + width], preferred_element_type=F32)

    in_tail = j >= n_tiles - n_tail

    aq = _head_rms(seg(OFF_AQ, A_WIDTH), A_HEAD_DIM) * aqn_ref[...] * (A_HEAD_DIM ** -0.5)
    aq_ref[0] = aq.astype(BF16)

    ak = _head_rms(seg(OFF_AK, A_WIDTH), A_HEAD_DIM) * akn_ref[...]
    ak_ref[0] = ak.astype(BF16)

    @pl.when(in_tail)
    def _():
        akt_ref[0] = ak

    av = seg(OFF_AV, A_WIDTH)
    av_ref[0] = av.astype(BF16)

    @pl.when(in_tail)
    def _():
        avt_ref[0] = av

    gq_ref[0] = (seg(OFF_GQ, G_KW) * (G_DK ** -0.5)).astype(BF16)
    gk_ref[0] = seg(OFF_GK, G_KW).astype(BF16)
    gv_ref[0] = seg(OFF_GV, G_VW).astype(BF16)
    gr = seg(OFF_GR, G_VW)
    gr_ref[0] = (gr * _sigmoid(gr)).astype(BF16)

    lr = seg(OFF_LR, LANES).astype(BF16)
    z = jnp.dot(lr, wup_ref[...], preferred_element_type=F32) + bla_ref[...]
    la_ref[0] = (jnp.minimum(z, 0.0) - jnp.log1p(jnp.exp(-jnp.abs(z)))) * (1.0 / G_TAU)

    cq = _head_rms(seg(OFF_CQ, C_WIDTH), C_HEAD_DIM) * cqn_ref[...]
    cq_ref[0] = cq.astype(BF16)

    gate_chunk = 512
    for c in range(N_BRANCH * D_MODEL // gate_chunk):
        lo = c * gate_chunk
        g = seg(OFF_GATE + lo, gate_chunk) + bg_ref[:, lo:lo + gate_chunk]
        gt_ref[0, :, lo:lo + gate_chunk] = _sigmoid(g).astype(BF16)


def _inproj(x, keep, nm, w_r, aqn, akn, cqn, wup, bla, bg):
    G, R, _ = x.shape
    tm = min(ROW_TILE, R)
    n_tiles = R // tm
    n_tail = keep // tm
    assert R % tm == 0 and keep % tm == 0 and n_tail >= 1

    def row(width, dtype):
        return (jax.ShapeDtypeStruct((G, R, width), dtype),
                pl.BlockSpec((1, tm, width), lambda g, j: (g, j, 0)))

    def tail(width):
        return (jax.ShapeDtypeStruct((G, keep, width), F32),
                pl.BlockSpec((1, tm, width), lambda g, j: (g, jnp.maximum(j - (n_tiles - n_tail), 0), 0)))

    outs = [row(A_WIDTH, BF16), row(A_WIDTH, BF16), row(A_WIDTH, BF16), row(G_KW, BF16),
            row(G_KW, BF16), row(G_VW, BF16), row(G_VW, BF16), row(G_KW, F32), row(C_WIDTH, BF16),
            row(N_BRANCH * D_MODEL, BF16), tail(A_WIDTH), tail(A_WIDTH)]

    def full(a):
        return pl.BlockSpec(a.shape, lambda g, j: (0,) * a.ndim)

    return pl.pallas_call(
        functools.partial(_inproj_kernel, n_tiles=n_tiles, n_tail=n_tail),
        grid=(G, n_tiles),
        in_specs=[pl.BlockSpec((1, tm, D_MODEL), lambda g, j: (g, j, 0)), full(nm), full(w_r),
                  full(aqn), full(akn), full(cqn), full(wup), full(bla), full(bg)],
        out_specs=[o[1] for o in outs],
        out_shape=[o[0] for o in outs],
        compiler_params=_params(("arbitrary", "arbitrary")),
        name="inproj",
    )(x, nm, w_r, aqn, akn, cqn, wup, bla, bg)


def _attend(q, parts, lo_mask):
    outs = []
    for hh in range(2):
        keep = lo_mask if hh == 0 else jnp.logical_not(lo_mask)
        qm = jnp.where(keep, q, jnp.zeros_like(q))
        scores = []
        for (k, _, b0, b1, valid) in parts:
            s = lax.dot_general(qm, k, (((1,), (1,)), ((), ())), preferred_element_type=F32)
            s = s + (b0 if hh == 0 else b1)
            if valid is not None:
                s = jnp.where(valid, s, NEG_INF)
            scores.append(s)
        m = scores[0].max(-1, keepdims=True)
        for s in scores[1:]:
            m = jnp.maximum(m, s.max(-1, keepdims=True))
        l = jnp.zeros_like(m)
        o = jnp.zeros((q.shape[0], LANES), F32)
        for s, (_, v, _, _, _) in zip(scores, parts):
            p = jnp.exp(s - m)
            l = l + p.sum(-1, keepdims=True)
            o = o + jnp.dot(p.astype(BF16), v, preferred_element_type=F32)
        outs.append(o / l)
    return jnp.where(lo_mask, outs[0], outs[1])


def _attn_prompt_kernel(q_ref, kp_ref, kc_ref, vp_ref, vc_ref, bias_ref, o_ref, *, tb):
    j = pl.program_id(1)
    has_prev = j > 0
    lo_mask = lax.broadcasted_iota(I32, (ATTN_SUB, LANES), 1) < A_HEAD_DIM
    for s in range(tb // ATTN_SUB):
        r0 = s * ATTN_SUB
        len_a = tb - r0
        len_b = r0 + ATTN_SUB
        for p in range(A_WIDTH // LANES):
            c0 = p * LANES
            q = q_ref[0, r0:r0 + ATTN_SUB, c0:c0 + LANES]
            parts = [
                (kp_ref[0, r0:tb, c0:c0 + LANES], vp_ref[0, r0:tb, c0:c0 + LANES],
                 bias_ref[2 * p, :, 0:len_a], bias_ref[2 * p + 1, :, 0:len_a], has_prev),
                (kc_ref[0, 0:len_b, c0:c0 + LANES], vc_ref[0, 0:len_b, c0:c0 + LANES],
                 bias_ref[2 * p, :, len_a:len_a + len_b], bias_ref[2 * p + 1, :, len_a:len_a + len_b],
                 None),
            ]
            o_ref[0, r0:r0 + ATTN_SUB, c0:c0 + LANES] = _attend(q, parts, lo_mask).astype(BF16)


def _band_bias(table):
    i = jnp.arange(ATTN_SUB)[:, None]
    jj = jnp.arange(ATTN_SUB + WINDOW)[None, :]
    idx = jnp.clip(i + WINDOW - jj, -REL_MAX, REL_MAX) + REL_MAX
    qc, kc = i // CHUNK, jj // CHUNK
    ok = (kc >= qc) & (kc <= qc + BAND_CHUNKS)
    return jnp.where(ok[None], jnp.take(table, idx, axis=1).astype(F32), NEG_INF)


def _attn_prompt(aq, ak, av, table):
    B, S, _ = aq.shape
    tb = WINDOW
    assert S % tb == 0
    bias = _band_bias(table)
    cur = pl.BlockSpec((1, tb, A_WIDTH), lambda b, j: (b, j, 0))
    prev = pl.BlockSpec((1, tb, A_WIDTH), lambda b, j: (b, jnp.maximum(j - 1, 0), 0))
    return pl.pallas_call(
        functools.partial(_attn_prompt_kernel, tb=tb),
        grid=(B, S // tb),
        in_specs=[cur, prev, cur, prev, cur, pl.BlockSpec(bias.shape, lambda b, j: (0, 0, 0))],
        out_specs=cur,
        out_shape=jax.ShapeDtypeStruct((B, S, A_WIDTH), BF16),
        compiler_params=_params(("arbitrary", "arbitrary")),
        name="attn_prompt",
    )(aq, ak, ak, av, av, bias)


def _attn_sample_kernel(q_ref, k_ref, v_ref, bias_ref, o_ref):
    T = q_ref.shape[1]
    lo_mask = lax.broadcasted_iota(I32, (T, LANES), 1) < A_HEAD_DIM
    for p in range(A_WIDTH // LANES):
        c0 = p * LANES
        parts = [(k_ref[0, :, c0:c0 + LANES], v_ref[0, :, c0:c0 + LANES],
                  bias_ref[2 * p], bias_ref[2 * p + 1], None)]
        o_ref[0, :, c0:c0 + LANES] = _attend(q_ref[0, :, c0:c0 + LANES], parts, lo_mask).astype(BF16)


def _attn_sample(aq, ak, av, cache_k, cache_v, table):
    B, T, _ = aq.shape
    P = cache_k.shape[1]
    L = (P + T + LANES - 1) // LANES * LANES
    pad = jnp.zeros((B, L - P - T, A_WIDTH), BF16)
    kk = jnp.concatenate([cache_k.astype(BF16), ak, pad], axis=1)
    vv = jnp.concatenate([cache_v.astype(BF16), av, pad], axis=1)
    rel = jnp.arange(T)[:, None] + P - jnp.arange(L)[None, :]
    bias = jnp.take(table, jnp.clip(rel, -REL_MAX, REL_MAX) + REL_MAX, axis=1).astype(F32)
    bias = jnp.where((jnp.arange(L) < P + T)[None, None, :], bias, NEG_INF)
    new = pl.BlockSpec((1, T, A_WIDTH), lambda b: (b, 0, 0))
    old = pl.BlockSpec((1, L, A_WIDTH), lambda b: (b, 0, 0))
    return pl.pallas_call(
        _attn_sample_kernel,
        grid=(B,),
        in_specs=[new, old, old, pl.BlockSpec(bias.shape, lambda b: (0, 0, 0))],
        out_specs=new,
        out_shape=jax.ShapeDtypeStruct((B, T, A_WIDTH), BF16),
        compiler_params=_params(("arbitrary",)),
        name="attn_sample",
    )(aq, kk, vv, bias)


def _gla_kernel(q_ref, k_ref, v_ref, la_ref, gr_ref, gain_ref, s0_ref, o_ref, sf_ref, s_scr, *, C, n_chunks):
    j = pl.program_id(1)

    @pl.when(j == 0)
    def _():
        s_scr[...] = s0_ref[0]

    n_sub = C // G_SUB
    ri = lax.broadcasted_iota(I32, (C, C), 0)
    ci = lax.broadcasted_iota(I32, (C, C), 1)
    tril = (ci <= ri).astype(BF16)
    lane_kw = lax.broadcasted_iota(I32, (1, G_KW), 1)
    head_of_lane = lane_kw // G_DK
    row_kw = lax.broadcasted_iota(I32, (C, G_KW), 0)
    ur = lax.broadcasted_iota(I32, (2 * C, 4 * C), 0) - C
    uc = lax.broadcasted_iota(I32, (2 * C, 4 * C), 1)
    u_mat = ((ur >= 0) & ((uc >= C) | (ur <= uc))).astype(BF16)

    def chunk(c, carry):
        r = pl.multiple_of(c * C, C)
        q = q_ref[0, pl.ds(r, C), :].astype(F32)
        k = k_ref[0, pl.ds(r, C), :].astype(F32)
        v = v_ref[0, pl.ds(r, C), :]
        la = la_ref[0, pl.ds(r, C), :]
        S = s_scr[...]

        la_hi, la_lo = _split_bf16(la)
        b = (jnp.dot(tril, la_hi, preferred_element_type=F32)
             + jnp.dot(tril, la_lo, preferred_element_type=F32))

        qs = q * jnp.exp(b)
        q4 = jnp.concatenate([jnp.where(head_of_lane == h, qs, 0.0) for h in range(G_HEADS)], axis=0)
        r_inter = jnp.dot(q4.astype(BF16), S.astype(BF16), preferred_element_type=F32)
        o = jnp.concatenate([r_inter[h * C:(h + 1) * C] for h in range(G_HEADS)], axis=1)

        o_rows = []
        for i in range(n_sub):
            r0, r1 = i * G_SUB, (i + 1) * G_SUB
            bs = b[r0 - 1:r0] if i > 0 else jnp.zeros((1, G_KW), F32)
            qe = q[r0:r1] * jnp.exp(b[r0:r1] - bs)
            ke = (k * jnp.exp(jnp.where(row_kw < r1, bs - b, -jnp.inf))).astype(BF16)
            qst = jnp.concatenate([jnp.where(head_of_lane == h, qe, 0.0) for h in range(G_HEADS)],
                                  axis=0).astype(BF16)
            att = lax.dot_general(qst, ke, (((1,), (1,)), ((), ())), preferred_element_type=F32)
            tt = lax.broadcasted_iota(I32, att.shape, 0) % G_SUB + r0
            ss = lax.broadcasted_iota(I32, att.shape, 1)
            att = jnp.where(ss <= tt, att, 0.0).astype(BF16)
            ov = jnp.dot(att, v, preferred_element_type=F32)
            o_rows.append(jnp.concatenate(
                [ov[h * G_SUB:(h + 1) * G_SUB, h * G_DV:(h + 1) * G_DV] for h in range(G_HEADS)], axis=1))
        o = o + jnp.concatenate(o_rows, axis=0)

        on = _head_rms(o, G_DV) * gain_ref[...] * gr_ref[0, pl.ds(r, C), :].astype(F32)
        o_ref[0, pl.ds(r, C), :] = on.astype(BF16)

        xt = jnp.concatenate([k, la], axis=0).T
        xt_hi, xt_lo = _split_bf16(xt)
        xb = (jnp.dot(xt_hi, u_mat, preferred_element_type=F32)
              + jnp.dot(xt_lo, u_mat, preferred_element_type=F32))
        b_last = xb[:, LANES:]
        kd = (xt * jnp.exp(b_last - xb[:, :LANES])).astype(BF16)
        v_ext = jnp.concatenate([v, jnp.zeros((C, G_VW), BF16)], axis=0)
        kv = jnp.dot(kd, v_ext, preferred_element_type=F32)
        kv_d = jnp.concatenate(
            [kv[h * G_DK:(h + 1) * G_DK, h * G_DV:(h + 1) * G_DV] for h in range(G_HEADS)], axis=0)
        s_scr[...] = jnp.exp(b_last) * S + kv_d
        return carry

    lax.fori_loop(0, n_chunks, chunk, 0)

    @pl.when(j == pl.num_programs(1) - 1)
    def _():
        sf_ref[0] = s_scr[...]


def _gla(gq, gk, gv, la, gr, gain, s0):
    B, T, _ = gq.shape
    C = CHUNK
    tb = min(ROW_TILE, T)
    assert T % tb == 0 and tb % C == 0 and C % G_SUB == 0 and 2 * C == LANES
    kw = pl.BlockSpec((1, tb, G_KW), lambda b, j: (b, j, 0))
    vw = pl.BlockSpec((1, tb, G_VW), lambda b, j: (b, j, 0))
    st = pl.BlockSpec((1, G_KW, G_DV), lambda b, j: (b, 0, 0))
    return pl.pallas_call(
        functools.partial(_gla_kernel, C=C, n_chunks=tb // C),
        grid=(B, T // tb),
        in_specs=[kw, kw, vw, kw, vw, pl.BlockSpec(gain.shape, lambda b, j: (0, 0)), st],
        out_specs=[vw, st],
        out_shape=[jax.ShapeDtypeStruct((B, T, G_VW), BF16), jax.ShapeDtypeStruct((B, G_KW, G_DV), F32)],
        scratch_shapes=[pltpu.VMEM((G_KW, G_DV), F32)],
        compiler_params=_params(("arbitrary", "arbitrary")),
        name="gla",
    )(gq, gk, gv, la, gr, gain, s0)


def _memkv_kernel(mem_ref, gm_ref, w_ref, gk_ref, k_ref, v_ref):
    x = mem_ref[0]
    h = (x * lax.rsqrt(jnp.mean(x * x, -1, keepdims=True) + EPS) * gm_ref[...]).astype(BF16)
    k = jnp.dot(h, w_ref[:, :C_WIDTH], preferred_element_type=F32)
    k_ref[0] = _head_rms(k, C_HEAD_DIM) * gk_ref[...]
    v_ref[0] = jnp.dot(h, w_ref[:, C_WIDTH:], preferred_element_type=F32)


def _memkv(mem, gm, w_kv, gk):
    B, M, _ = mem.shape
    out = pl.BlockSpec((1, M, C_WIDTH), lambda b: (b, 0, 0))
    return pl.pallas_call(
        _memkv_kernel,
        grid=(B,),
        in_specs=[pl.BlockSpec((1, M, D_MODEL), lambda b: (b, 0, 0)),
                  pl.BlockSpec(gm.shape, lambda b: (0, 0)),
                  pl.BlockSpec(w_kv.shape, lambda b: (0, 0)),
                  pl.BlockSpec(gk.shape, lambda b: (0, 0))],
        out_specs=[out, out],
        out_shape=[jax.ShapeDtypeStruct((B, M, C_WIDTH), F32)] * 2,
        compiler_params=_params(("arbitrary",)),
        name="memkv",
    )(mem, gm, w_kv, gk)


def _merge_kernel(x_ref, ya_ref, yb_ref, cq_ref, gt_ref, mk_ref, mv_ref, wa_ref, wb_ref, wc_ref,
                  wo_ref, nf_ref, wr_ref, br_ref,
                  x1_ref, hp_ref, ri_ref, rw_ref, cnt_ref, run_scr):
    first = jnp.logical_and(pl.program_id(0) == 0, pl.program_id(1) == 0)

    @pl.when(first)
    def _():
        run_scr[...] = jnp.zeros_like(run_scr)

    tm = x_ref.shape[1]
    cols = []
    for h in range(C_HEADS):
        c0 = h * C_HEAD_DIM
        q = cq_ref[0, :, c0:c0 + C_HEAD_DIM]
        kh = mk_ref[0, :, c0:c0 + C_HEAD_DIM].astype(BF16)
        vh = mv_ref[0, :, c0:c0 + C_HEAD_DIM].astype(BF16)
        s = lax.dot_general(q, kh, (((1,), (1,)), ((), ())), preferred_element_type=F32) * (C_HEAD_DIM ** -0.5)
        p = jnp.exp(s - s.max(-1, keepdims=True))
        l = p.sum(-1, keepdims=True)
        cols.append(jnp.dot(p.astype(BF16), vh, preferred_element_type=F32) / l)
    yc_in = jnp.concatenate(cols, axis=-1).astype(BF16)

    y_a = jnp.dot(ya_ref[0], wa_ref[...], preferred_element_type=F32)
    y_b = jnp.dot(yb_ref[0], wb_ref[...], preferred_element_type=F32)
    y_c = jnp.dot(yc_in, wc_ref[...], preferred_element_type=F32)
    merged = (gt_ref[0, :, 0:D_MODEL].astype(F32) * y_a
              + gt_ref[0, :, D_MODEL:2 * D_MODEL].astype(F32) * y_b
              + gt_ref[0, :, 2 * D_MODEL:3 * D_MODEL].astype(F32) * y_c)
    x1 = x_ref[0] + jnp.dot(merged.astype(BF16), wo_ref[...], preferred_element_type=F32)
    x1_ref[0] = x1

    h2 = x1 * lax.rsqrt(jnp.mean(x1 * x1, -1, keepdims=True) + EPS) * nf_ref[...]
    hp_ref[0] = _pack_rows(h2)

    logits = jnp.dot(h2.astype(BF16), wr_ref[...], preferred_element_type=F32) + br_ref[...]
    lane = lax.broadcasted_iota(I32, (tm, LANES), 1)
    lane_f = lane.astype(F32)
    vals, sels, idxs = [], [], []
    l = logits
    for _ in range(TOP_K):
        m = l.max(-1, keepdims=True)
        idx = jnp.min(jnp.where(l == m, lane_f, float(LANES)), -1, keepdims=True)
        sel = lane_f == idx
        vals.append(m)
        idxs.append(idx)
        sels.append(sel)
        l = jnp.where(sel, -3e38, l)
    es = [jnp.exp(vk - vals[0]) for vk in vals]
    den = es[0] + es[1] + es[2] + es[3]
    cnt = jnp.zeros((tm, LANES), F32)
    for sel in sels:
        cnt = cnt + jnp.where(sel, 1.0, 0.0)
    tp = max(tm, LANES)
    cnt_p = cnt if tp == tm else jnp.concatenate([cnt, jnp.zeros((tp - tm, LANES), F32)], axis=0)
    ri = lax.broadcasted_iota(I32, (tp, tp), 0)
    ci = lax.broadcasted_iota(I32, (tp, tp), 1)
    before = jnp.dot(jnp.where(ci < ri, 1.0, 0.0).astype(BF16), cnt_p.astype(BF16),
                     preferred_element_type=F32)[0:tm] + run_scr[0:1, :]
    r_i = jnp.zeros((tm, LANES), I32)
    r_w = jnp.zeros((tm, LANES), F32)
    for kk in range(TOP_K):
        rank = jnp.sum(jnp.where(sels[kk], before, 0.0), -1, keepdims=True)
        r_i = jnp.where(lane == kk, idxs[kk].astype(I32), r_i)
        r_i = jnp.where(lane == TOP_K + kk, rank.astype(I32), r_i)
        r_w = jnp.where(lane == kk, es[kk] / den, r_w)
    ri_ref[0] = r_i
    rw_ref[0] = r_w
    run_scr[...] = run_scr[...] + jnp.sum(cnt, axis=0, keepdims=True)
    cnt_ref[...] = run_scr[...]


def _merge(x, ya, yb, cq, gt, mk, mv, wa, wb, wc, wo, nf, wr, br):
    B, T, _ = x.shape
    tm = min(ROW_TILE, T)
    assert T % tm == 0

    def row(width):
        return pl.BlockSpec((1, tm, width), lambda b, j: (b, j, 0))

    def full(a):
        return pl.BlockSpec(a.shape, lambda b, j: (0,) * a.ndim)

    mem = pl.BlockSpec((1, N_MEM, C_WIDTH), lambda b, j: (b, 0, 0))
    return pl.pallas_call(
        _merge_kernel,
        grid=(B, T // tm),
        in_specs=[row(D_MODEL), row(A_WIDTH), row(G_VW), row(C_WIDTH), row(N_BRANCH * D_MODEL), mem, mem,
                  full(wa), full(wb), full(wc), full(wo), full(nf), full(wr), full(br)],
        out_specs=[row(D_MODEL), row(HALF), row(LANES), row(LANES), pl.BlockSpec((8, LANES), lambda b, j: (0, 0))],
        out_shape=[jax.ShapeDtypeStruct((B, T, D_MODEL), F32), jax.ShapeDtypeStruct((B, T, HALF), U32),
                   jax.ShapeDtypeStruct((B, T, LANES), I32), jax.ShapeDtypeStruct((B, T, LANES), F32),
                   jax.ShapeDtypeStruct((8, LANES), F32)],
        scratch_shapes=[pltpu.VMEM((8, LANES), F32)],
        compiler_params=_params(("arbitrary", "arbitrary")),
        name="merge",
    )(x, ya, yb, cq, gt, mk, mv, wa, wb, wc, wo, nf, wr, br)


def _dispatch_kernel(dest_ref, pad_lo_ref, pad_n_ref, hp_ref, *rest, tm):
    xs_ref, zero_scr, sem, psem = rest[-4:]
    i = pl.program_id(0)

    def row_copy(t, kk):
        return pltpu.make_async_copy(hp_ref.at[pl.ds(i * tm + t, 1)],
                                     xs_ref.at[pl.ds(dest_ref[0, 0, t * TOP_K + kk], 1)], sem)

    def pad_copy(e, n):
        return pltpu.make_async_copy(zero_scr, xs_ref.at[pl.ds(pad_lo_ref[e] + n, 1)], psem)

    def issue(t, c):
        for kk in range(TOP_K):
            row_copy(t, kk).start()
        return c

    lax.fori_loop(0, tm, issue, 0)

    @pl.when(i == 0)
    def _():
        zero_scr[...] = jnp.zeros_like(zero_scr)

        def per_expert(e, c):
            lax.fori_loop(0, pad_n_ref[e], lambda n, cc: (pad_copy(e, n).start(), cc)[1], 0)
            return c

        lax.fori_loop(0, N_EXPERTS, per_expert, 0)

        def per_expert_wait(e, c):
            lax.fori_loop(0, pad_n_ref[e], lambda n, cc: (pad_copy(e, n).wait(), cc)[1], 0)
            return c

        lax.fori_loop(0, N_EXPERTS, per_expert_wait, 0)

    def drain(t, c):
        for kk in range(TOP_K):
            row_copy(t, kk).wait()
        return c

    lax.fori_loop(0, tm, drain, 0)


def _dispatch(dest, pad_lo, pad_n, hp, rows, xs=None):
    N = hp.shape[0]
    tm = min(ROW_TILE, N)
    assert N % tm == 0
    n_steps = N // tm
    dest3 = dest.reshape(n_steps, 1, tm * TOP_K)
    smem = functools.partial(pl.BlockSpec, memory_space=pltpu.SMEM)
    chained = xs is not None
    any_spec = pl.BlockSpec(memory_space=pl.ANY)
    return pl.pallas_call(
        functools.partial(_dispatch_kernel, tm=tm),
        grid=(n_steps,),
        in_specs=[smem((1, 1, tm * TOP_K), lambda i: (i, 0, 0)),
                  smem((N_EXPERTS,), lambda i: (0,)), smem((N_EXPERTS,), lambda i: (0,)),
                  any_spec] + ([any_spec] if chained else []),
        out_specs=any_spec,
        out_shape=jax.ShapeDtypeStruct((rows, HALF), U32),
        scratch_shapes=[pltpu.VMEM((1, HALF), U32), pltpu.SemaphoreType.DMA(()), pltpu.SemaphoreType.DMA(())],
        input_output_aliases={4: 0} if chained else {},
        compiler_params=_params(("arbitrary",)),
        name="moe_dispatch",
    )(dest3, pad_lo, pad_n, hp, *([xs] if chained else []))


def _ffn_kernel(be_ref, nu_ref, x_ref, wgu_ref, bgu_ref, wd_ref, bd_ref, y_ref, wgu_bf, wd_bf):
    i = pl.program_id(0)
    prev = be_ref[jnp.maximum(i - 1, 0)]
    new_expert = jnp.logical_or(i == 0, be_ref[i] != prev)

    @pl.when(new_expert)
    def _():
        wgu_bf[...] = wgu_ref[0].astype(BF16)
        wd_bf[...] = wd_ref[0].astype(BF16)

    @pl.when(i < nu_ref[0])
    def _():
        x_lo, x_hi = _unpack_rows(x_ref[...])
        gu = (jnp.dot(x_lo.astype(BF16), wgu_bf[0:HALF, :], preferred_element_type=F32)
              + jnp.dot(x_hi.astype(BF16), wgu_bf[HALF:, :], preferred_element_type=F32)
              + bgu_ref[0])
        gate = jnp.minimum(gu[:, :D_EXPERT], SWIGLU_LIMIT)
        up = jnp.clip(gu[:, D_EXPERT:], -SWIGLU_LIMIT, SWIGLU_LIMIT)
        act = (up + 1.0) * (gate * _sigmoid(SWIGLU_ALPHA * gate))
        y = jnp.dot(act.astype(BF16), wd_bf[...], preferred_element_type=F32) + bd_ref[0]
        y_ref[...] = _pack_rows(y)

    @pl.when(i >= nu_ref[0])
    def _():
        y_ref[...] = jnp.zeros_like(y_ref)


def _ffn(blk_expert, n_used, xs, w_gu, b_gu, w_d, b_d):
    P = xs.shape[0]
    nblk = P // FFN_BLOCK
    grid_spec = pltpu.PrefetchScalarGridSpec(
        num_scalar_prefetch=2,
        grid=(nblk,),
        in_specs=[pl.BlockSpec((FFN_BLOCK, HALF), lambda i, be, nu: (jnp.minimum(i, nu[0] - 1), 0)),
                  pl.BlockSpec((1, D_MODEL, 2 * D_EXPERT), lambda i, be, nu: (be[i], 0, 0)),
                  pl.BlockSpec((1, 1, 2 * D_EXPERT), lambda i, be, nu: (be[i], 0, 0)),
                  pl.BlockSpec((1, D_EXPERT, D_MODEL), lambda i, be, nu: (be[i], 0, 0)),
                  pl.BlockSpec((1, 1, D_MODEL), lambda i, be, nu: (be[i], 0, 0))],
        out_specs=pl.BlockSpec((FFN_BLOCK, HALF), lambda i, be, nu: (i, 0)),
        scratch_shapes=[pltpu.VMEM((D_MODEL, 2 * D_EXPERT), BF16), pltpu.VMEM((D_EXPERT, D_MODEL), BF16)],
    )
    return pl.pallas_call(
        _ffn_kernel,
        grid_spec=grid_spec,
        out_shape=jax.ShapeDtypeStruct((P, HALF), U32),
        compiler_params=_params(("arbitrary",)),
        name="moe_ffn",
    )(blk_expert, n_used, xs, w_gu, b_gu.reshape(N_EXPERTS, 1, -1), w_d, b_d.reshape(N_EXPERTS, 1, -1))


def _combine_kernel(dcur_ref, dnext_ref, x1_ref, rw_ref, ys_ref, o_ref, buf, sem, *, tm, n_steps):
    i = pl.program_id(0)
    slot = i % 2

    def row_copy(dref, s, t, kk):
        return pltpu.make_async_copy(ys_ref.at[pl.ds(dref[0, 0, t * TOP_K + kk], 1)],
                                     buf.at[s, kk, pl.ds(t, 1)], sem.at[s])

    def issue(dref, s):
        def body(t, c):
            for kk in range(TOP_K):
                row_copy(dref, s, t, kk).start()
            return c
        lax.fori_loop(0, tm, body, 0)

    @pl.when(i == 0)
    def _():
        issue(dcur_ref, 0)

    @pl.when(i + 1 < n_steps)
    def _():
        issue(dnext_ref, 1 - slot)

    def drain(t, c):
        for kk in range(TOP_K):
            row_copy(dcur_ref, slot, t, kk).wait()
        return c

    lax.fori_loop(0, tm, drain, 0)

    w = rw_ref[...]
    acc_lo = x1_ref[:, :HALF]
    acc_hi = x1_ref[:, HALF:]
    for kk in range(TOP_K):
        lo, hi = _unpack_rows(buf[slot, kk])
        wk = w[:, kk:kk + 1]
        acc_lo = acc_lo + wk * lo
        acc_hi = acc_hi + wk * hi
    o_ref[:, :HALF] = acc_lo
    o_ref[:, HALF:] = acc_hi


def _combine(dest, x1, rw, ys):
    N = x1.shape[0]
    tm = min(256, N)
    assert N % tm == 0
    n_steps = N // tm
    dest3 = dest.reshape(n_steps, 1, tm * TOP_K)
    smem = functools.partial(pl.BlockSpec, memory_space=pltpu.SMEM)
    return pl.pallas_call(
        functools.partial(_combine_kernel, tm=tm, n_steps=n_steps),
        grid=(n_steps,),
        in_specs=[smem((1, 1, tm * TOP_K), lambda i: (i, 0, 0)),
                  smem((1, 1, tm * TOP_K), lambda i: (jnp.minimum(i + 1, n_steps - 1), 0, 0)),
                  pl.BlockSpec((tm, D_MODEL), lambda i: (i, 0)),
                  pl.BlockSpec((tm, LANES), lambda i: (i, 0)),
                  pl.BlockSpec(memory_space=pl.ANY)],
        out_specs=pl.BlockSpec((tm, D_MODEL), lambda i: (i, 0)),
        out_shape=jax.ShapeDtypeStruct((N, D_MODEL), F32),
        scratch_shapes=[pltpu.VMEM((2, TOP_K, tm, HALF), U32), pltpu.SemaphoreType.DMA((2,))],
        compiler_params=_params(("arbitrary",)),
        name="moe_combine",
    )(dest3, dest3, x1, rw, ys)


def _moe(groups, w_gu, b_gu, w_d, b_d):
    n_assign = sum(g[0].shape[0] for g in groups) * TOP_K
    nblk = (n_assign + N_EXPERTS * (FFN_BLOCK - 1) + FFN_BLOCK - 1) // FFN_BLOCK
    counts = [g[4].astype(I32) for g in groups]
    total = sum(counts)
    padded = (total + FFN_BLOCK - 1) // FFN_BLOCK * FFN_BLOCK
    pend = jnp.cumsum(padded)
    pstart = pend - padded
    n_used = pend[-1:] // FFN_BLOCK
    blk = jnp.minimum(jnp.arange(nblk, dtype=I32), n_used[0] - 1) * FFN_BLOCK
    blk_expert = jnp.minimum(jnp.searchsorted(pend, blk, side='right'), N_EXPERTS - 1).astype(I32)

    dests = []
    base = pstart
    for g, c in zip(groups, counts):
        idx, rank = g[2][:, :TOP_K], g[2][:, TOP_K:2 * TOP_K]
        dests.append(jnp.take(base, idx) + rank)
        base = base + c

    xs = None
    for g, d in zip(groups, dests):
        pad_n = (padded - total) if xs is None else jnp.zeros((N_EXPERTS,), I32)
        xs = _dispatch(d, pstart + total, pad_n, g[1], nblk * FFN_BLOCK, xs)
    ys = _ffn(blk_expert, n_used.astype(I32), xs, w_gu, b_gu, w_d, b_d)
    return [_combine(d, g[0], g[3], ys) for g, d in zip(groups, dests)]


def _tile_lanes(g, reps):
    return jnp.tile(g.astype(F32), reps)[None, :]


def kernel(x_prompt, x_sample, mem_prompt, cache_attn_k, cache_attn_v, state_gla, cache_mem_k, cache_mem_v, norm_mix, w_in, a_q_norm, a_k_norm, rel_bias_table, w_a_o, w_gla_a_up, b_gla_a, gla_out_norm, w_b_o, c_q_norm, c_k_norm, norm_mem, w_mem_kv, w_c_o, b_gate, w_out, norm_ffn, w_router, b_router, w_gate_up, b_gate_up, w_down, b_down):
    depth = norm_mix.shape[0]
    assert depth == 1
    l = 0
    B, S, _ = x_prompt.shape
    Bs, Ts, _ = x_sample.shape
    keep = min(WINDOW, S)

    w = w_in[l]
    sizes = (A_WIDTH, A_WIDTH, A_WIDTH, G_KW, G_KW, G_VW, G_VW, G_RANK, C_WIDTH, N_BRANCH * D_MODEL)
    offs = [0]
    for s_ in sizes:
        offs.append(offs[-1] + s_)
    seg = [w[:, offs[i]:offs[i + 1]] for i in range(len(sizes))]
    w_r = jnp.concatenate(seg[0:7] + [seg[8], seg[9], seg[7], jnp.zeros((D_MODEL, LANES - G_RANK), F32)],
                          axis=1).astype(BF16)
    nm = norm_mix[l][None, :]
    aqn = _tile_lanes(a_q_norm[l], A_HEADS)
    akn = _tile_lanes(a_k_norm[l], A_HEADS)
    cqn = _tile_lanes(c_q_norm[l], C_HEADS)
    ckn = _tile_lanes(c_k_norm[l], C_HEADS)
    gon = _tile_lanes(gla_out_norm[l], G_HEADS)
    wup = jnp.concatenate([w_gla_a_up[l], jnp.zeros((LANES - G_RANK, G_KW), F32)], axis=0).astype(BF16)
    bla = b_gla_a[l][None, :]
    bg = b_gate[l][None, :]
    wa, wb, wc, wo = (t[l].astype(BF16) for t in (w_a_o, w_b_o, w_c_o, w_out))
    nf = norm_ffn[l][None, :]
    wr = jnp.concatenate([w_router[l], jnp.zeros((D_MODEL, LANES - N_EXPERTS), F32)], axis=1).astype(BF16)
    br = jnp.concatenate([b_router[l], jnp.full((LANES - N_EXPERTS,), NEG_INF, F32)])[None, :]
    table = rel_bias_table[l]

    mk, mv = _memkv(mem_prompt, norm_mem[l][None, :], w_mem_kv[l].astype(BF16), ckn)
    (aq, ak, av, gq, gk, gv, gr, la, cq, gt, ak_tail, av_tail) = _inproj(
        x_prompt, keep, nm, w_r, aqn, akn, cqn, wup, bla, bg)
    ya = _attn_prompt(aq, ak, av, table)
    yb, s_prompt = _gla(gq, gk, gv, la, gr, gon, jnp.zeros((B, G_KW, G_DV), F32))
    x1_p, hp_p, ri_p, rw_p, cnt_p = _merge(x_prompt, ya, yb, cq, gt, mk, mv, wa, wb, wc, wo, nf, wr, br)

    (aq, ak, av, gq, gk, gv, gr, la, cq, gt, ak_new, av_new) = _inproj(
        x_sample.reshape(1, Bs * Ts, D_MODEL), Bs * Ts, nm, w_r, aqn, akn, cqn, wup, bla, bg)
    rs = lambda t: t.reshape(Bs, Ts, t.shape[-1])
    P = cache_attn_k.shape[2]
    ya = _attn_sample(rs(aq), rs(ak), rs(av), cache_attn_k[l].reshape(Bs, P, A_WIDTH),
                      cache_attn_v[l].reshape(Bs, P, A_WIDTH), table)
    t_pad = (Ts + CHUNK - 1) // CHUNK * CHUNK
    zp = lambda t: jnp.pad(rs(t), ((0, 0), (0, t_pad - Ts), (0, 0)))
    yb, s_sample = _gla(zp(gq), zp(gk), zp(gv), zp(la), zp(gr), gon, state_gla[l].reshape(Bs, G_KW, G_DV))
    yb = yb[:, :Ts]
    x1_s, hp_s, ri_s, rw_s, cnt_s = _merge(
        x_sample, ya, yb, rs(cq), rs(gt), cache_mem_k[l].reshape(Bs, N_MEM, C_WIDTH),
        cache_mem_v[l].reshape(Bs, N_MEM, C_WIDTH), wa, wb, wc, wo, nf, wr, br)

    flat = lambda t: t.reshape(-1, t.shape[-1])
    y_p, y_s = _moe(
        [(flat(x1_p), flat(hp_p), flat(ri_p), flat(rw_p), cnt_p[0, :N_EXPERTS]),
         (flat(x1_s), flat(hp_s), flat(ri_s), flat(rw_s), cnt_s[0, :N_EXPERTS])],
        w_gate_up[l], b_gate_up[l], w_down[l], b_down[l])

    return (y_p.reshape(B, S, D_MODEL), y_s.reshape(Bs, Ts, D_MODEL),
            ak_tail.reshape(1, B, keep, A_HEADS, A_HEAD_DIM), av_tail.reshape(1, B, keep, A_HEADS, A_HEAD_DIM),
            s_prompt.reshape(1, B, G_HEADS, G_DK, G_DV),
            mk.reshape(1, B, N_MEM, C_HEADS, C_HEAD_DIM), mv.reshape(1, B, N_MEM, C_HEADS, C_HEAD_DIM),
            ak_new.reshape(1, Bs, Ts, A_HEADS, A_HEAD_DIM), av_new.reshape(1, Bs, Ts, A_HEADS, A_HEAD_DIM),
            s_sample.reshape(1, Bs, G_HEADS, G_DK, G_DV))
```

```python
import functools

import jax
import jax.numpy as jnp
from jax import lax
from jax.experimental import pallas as pl
from jax.experimental.pallas import tpu as pltpu

F32 = jnp.float32
BF16 = jnp.bfloat16
U32 = jnp.uint32
I32 = jnp.int32

D_MODEL = 1024
CHUNK = 64
BAND_CHUNKS = 8
WINDOW = BAND_CHUNKS * CHUNK
N_MEM = 256
A_HEADS, A_HEAD_DIM = 8, 64
A_WIDTH = A_HEADS * A_HEAD_DIM
REL_MAX = 128
G_HEADS, G_DK, G_DV = 4, 64, 128
G_KW, G_VW = G_HEADS * G_DK, G_HEADS * G_DV
G_RANK = 16
G_TAU = 16.0
G_SUB = 16
C_HEADS, C_HEAD_DIM = 4, 128
C_WIDTH = C_HEADS * C_HEAD_DIM
N_BRANCH = 3
N_EXPERTS = 32
TOP_K = 4
D_EXPERT = 1024
SWIGLU_LIMIT = 7.0
SWIGLU_ALPHA = 1.702
EPS = 1e-6
NEG_INF = -1e30

LANES = 128
HALF = D_MODEL // 2
ROW_TILE = 512
ATTN_SUB = 128
FFN_BLOCK = 256
VMEM_LIMIT = 56 * 1024 * 1024

OFF_AQ, OFF_AK, OFF_AV = 0, 512, 1024
OFF_GQ, OFF_GK, OFF_GV, OFF_GR = 1536, 1792, 2048, 2560
OFF_CQ, OFF_GATE, OFF_LR = 3072, 3584, 6656
IN_COLS = OFF_LR + LANES


def _params(sem):
    return pltpu.CompilerParams(dimension_semantics=sem, vmem_limit_bytes=VMEM_LIMIT)


def _sigmoid(x):
    return 0.5 * jnp.tanh(0.5 * x) + 0.5


def _head_rms(y, head_dim):
    cols = []
    for p in range(y.shape[1] // LANES):
        blk = y[:, p * LANES:(p + 1) * LANES]
        sq = blk * blk
        if head_dim == LANES:
            sc = lax.rsqrt(jnp.sum(sq, -1, keepdims=True) * (1.0 / LANES) + EPS)
        else:
            lo = lax.broadcasted_iota(I32, blk.shape, 1) < head_dim
            s_lo = jnp.sum(jnp.where(lo, sq, 0.0), -1, keepdims=True)
            s_hi = jnp.sum(jnp.where(lo, 0.0, sq), -1, keepdims=True)
            sc = jnp.where(lo, lax.rsqrt(s_lo * (1.0 / head_dim) + EPS),
                           lax.rsqrt(s_hi * (1.0 / head_dim) + EPS))
        cols.append(blk * sc)
    return jnp.concatenate(cols, axis=-1)


def _split_bf16(x):
    hi = x.astype(BF16)
    lo = (x - hi.astype(F32)).astype(BF16)
    return hi, lo


def _pack_rows(x):
    lo = lax.bitcast_convert_type(x[:, :HALF].astype(BF16).astype(F32), U32)
    hi = lax.bitcast_convert_type(x[:, HALF:].astype(BF16).astype(F32), U32)
    return (lo >> 16) | (hi & jnp.uint32(0xFFFF0000))


def _unpack_rows(u):
    lo = lax.bitcast_convert_type(u << 16, F32)
    hi = lax.bitcast_convert_type(u & jnp.uint32(0xFFFF0000), F32)
    return lo, hi


SLABS = HALF // LANES


def _store_slabs(ref, lead, u):
    for j in range(SLABS):
        ref[lead + (slice(None), j, slice(None))] = u[:, j * LANES:(j + 1) * LANES]


def _load_slabs(ref, lead):
    return jnp.concatenate([ref[lead + (slice(None), j, slice(None))] for j in range(SLABS)], axis=-1)


def _inproj_kernel(x_ref, nm_ref, w_ref, aqn_ref, akn_ref, cqn_ref, wup_ref, bla_ref, bg_ref,
                   aq_ref, ak_ref, av_ref, gq_ref, gk_ref, gv_ref, gr_ref, la_ref, cq_ref, gt_ref,
                   akt_ref, avt_ref, *, n_tiles, n_tail):
    j = pl.program_id(1)
    x = x_ref[0]
    h = (x * lax.rsqrt(jnp.mean(x * x, -1, keepdims=True) + EPS) * nm_ref[...]).astype(BF16)

    def seg(off, width):
        return jnp.dot(h, w_ref[:, off:off + width], preferred_element_type=F32)

    in_tail = j >= n_tiles - n_tail

    aq = _head_rms(seg(OFF_AQ, A_WIDTH), A_HEAD_DIM) * aqn_ref[...] * (A_HEAD_DIM ** -0.5)
    aq_ref[0] = aq.astype(BF16)

    ak = _head_rms(seg(OFF_AK, A_WIDTH), A_HEAD_DIM) * akn_ref[...]
    ak_ref[0] = ak.astype(BF16)

    @pl.when(in_tail)
    def _():
        akt_ref[0] = ak

    av = seg(OFF_AV, A_WIDTH)
    av_ref[0] = av.astype(BF16)

    @pl.when(in_tail)
    def _():
        avt_ref[0] = av

    gq_ref[0] = (seg(OFF_GQ, G_KW) * (G_DK ** -0.5)).astype(BF16)
    gk_ref[0] = seg(OFF_GK, G_KW).astype(BF16)
    gv_ref[0] = seg(OFF_GV, G_VW).astype(BF16)
    gr = seg(OFF_GR, G_VW)
    gr_ref[0] = (gr * _sigmoid(gr)).astype(BF16)

    lr = seg(OFF_LR, LANES).astype(BF16)
    z = jnp.dot(lr, wup_ref[...], preferred_element_type=F32) + bla_ref[...]
    la_ref[0] = (jnp.minimum(z, 0.0) - jnp.log1p(jnp.exp(-jnp.abs(z)))) * (1.0 / G_TAU)

    cq = _head_rms(seg(OFF_CQ, C_WIDTH), C_HEAD_DIM) * cqn_ref[...]
    cq_ref[0] = cq.astype(BF16)

    gate_chunk = 512
    for c in range(N_BRANCH * D_MODEL // gate_chunk):
        lo = c * gate_chunk
        g = seg(OFF_GATE + lo, gate_chunk) + bg_ref[:, lo:lo + gate_chunk]
        gt_ref[0, :, lo:lo + gate_chunk] = _sigmoid(g).astype(BF16)


def _inproj(x, keep, nm, w_r, aqn, akn, cqn, wup, bla, bg):
    G, R, _ = x.shape
    tm = min(ROW_TILE, R)
    n_tiles = R // tm
    n_tail = keep // tm
    assert R % tm == 0 and keep % tm == 0 and n_tail >= 1

    def row(width, dtype):
        return (jax.ShapeDtypeStruct((G, R, width), dtype),
                pl.BlockSpec((1, tm, width), lambda g, j: (g, j, 0)))

    def tail(width):
        return (jax.ShapeDtypeStruct((G, keep, width), F32),
                pl.BlockSpec((1, tm, width), lambda g, j: (g, jnp.maximum(j - (n_tiles - n_tail), 0), 0)))

    outs = [row(A_WIDTH, BF16), row(A_WIDTH, BF16), row(A_WIDTH, BF16), row(G_KW, BF16),
            row(G_KW, BF16), row(G_VW, BF16), row(G_VW, BF16), row(G_KW, F32), row(C_WIDTH, BF16),
            row(N_BRANCH * D_MODEL, BF16), tail(A_WIDTH), tail(A_WIDTH)]

    def full(a):
        return pl.BlockSpec(a.shape, lambda g, j: (0,) * a.ndim)

    return pl.pallas_call(
        functools.partial(_inproj_kernel, n_tiles=n_tiles, n_tail=n_tail),
        grid=(G, n_tiles),
        in_specs=[pl.BlockSpec((1, tm, D_MODEL), lambda g, j: (g, j, 0)), full(nm), full(w_r),
                  full(aqn), full(akn), full(cqn), full(wup), full(bla), full(bg)],
        out_specs=[o[1] for o in outs],
        out_shape=[o[0] for o in outs],
        compiler_params=_params(("arbitrary", "arbitrary")),
        name="inproj",
    )(x, nm, w_r, aqn, akn, cqn, wup, bla, bg)


def _attend(q, parts, lo_mask):
    outs = []
    for hh in range(2):
        keep = lo_mask if hh == 0 else jnp.logical_not(lo_mask)
        qm = jnp.where(keep, q, jnp.zeros_like(q))
        scores = []
        for (k, _, b0, b1, valid) in parts:
            s = lax.dot_general(qm, k, (((1,), (1,)), ((), ())), preferred_element_type=F32)
            s = s + (b0 if hh == 0 else b1)
            if valid is not None:
                s = jnp.where(valid, s, NEG_INF)
            scores.append(s)
        m = scores[0].max(-1, keepdims=True)
        for s in scores[1:]:
            m = jnp.maximum(m, s.max(-1, keepdims=True))
        l = jnp.zeros_like(m)
        o = jnp.zeros((q.shape[0], LANES), F32)
        for s, (_, v, _, _, _) in zip(scores, parts):
            p = jnp.exp(s - m)
            l = l + p.sum(-1, keepdims=True)
            o = o + jnp.dot(p.astype(BF16), v, preferred_element_type=F32)
        outs.append(o / l)
    return jnp.where(lo_mask, outs[0], outs[1])


def _attn_prompt_kernel(q_ref, kp_ref, kc_ref, vp_ref, vc_ref, bias_ref, o_ref, *, tb):
    j = pl.program_id(1)
    has_prev = j > 0
    lo_mask = lax.broadcasted_iota(I32, (ATTN_SUB, LANES), 1) < A_HEAD_DIM
    for s in range(tb // ATTN_SUB):
        r0 = s * ATTN_SUB
        len_a = tb - r0
        len_b = r0 + ATTN_SUB
        for p in range(A_WIDTH // LANES):
            c0 = p * LANES
            q = q_ref[0, r0:r0 + ATTN_SUB, c0:c0 + LANES]
            parts = [
                (kp_ref[0, r0:tb, c0:c0 + LANES], vp_ref[0, r0:tb, c0:c0 + LANES],
                 bias_ref[2 * p, :, 0:len_a], bias_ref[2 * p + 1, :, 0:len_a], has_prev),
                (kc_ref[0, 0:len_b, c0:c0 + LANES], vc_ref[0, 0:len_b, c0:c0 + LANES],
                 bias_ref[2 * p, :, len_a:len_a + len_b], bias_ref[2 * p + 1, :, len_a:len_a + len_b],
                 None),
            ]
            o_ref[0, r0:r0 + ATTN_SUB, c0:c0 + LANES] = _attend(q, parts, lo_mask).astype(BF16)


def _rel_bias(table, n_q, n_k, offset):
    period = n_q + n_k - 1
    m = jnp.arange(period)
    u = table[:, jnp.clip(n_q - 1 + offset - m, -REL_MAX, REL_MAX) + REL_MAX].astype(F32)
    rows = jnp.tile(u, (1, n_q + 1))[:, :n_q * (period + 1)].reshape(-1, n_q, period + 1)[:, :, :n_k]
    return rows[:, ::-1, :]


def _band_bias(table):
    qc = jnp.arange(ATTN_SUB)[:, None] // CHUNK
    kc = jnp.arange(ATTN_SUB + WINDOW)[None, :] // CHUNK
    ok = (kc >= qc) & (kc <= qc + BAND_CHUNKS)
    return jnp.where(ok[None], _rel_bias(table, ATTN_SUB, ATTN_SUB + WINDOW, WINDOW), NEG_INF)


def _attn_prompt(aq, ak, av, table):
    B, S, _ = aq.shape
    tb = WINDOW
    assert S % tb == 0
    bias = _band_bias(table)
    cur = pl.BlockSpec((1, tb, A_WIDTH), lambda b, j: (b, j, 0))
    prev = pl.BlockSpec((1, tb, A_WIDTH), lambda b, j: (b, jnp.maximum(j - 1, 0), 0))
    return pl.pallas_call(
        functools.partial(_attn_prompt_kernel, tb=tb),
        grid=(B, S // tb),
        in_specs=[cur, prev, cur, prev, cur, pl.BlockSpec(bias.shape, lambda b, j: (0, 0, 0))],
        out_specs=cur,
        out_shape=jax.ShapeDtypeStruct((B, S, A_WIDTH), BF16),
        compiler_params=_params(("arbitrary", "arbitrary")),
        name="attn_prompt",
    )(aq, ak, ak, av, av, bias)


def _attn_sample_kernel(q_ref, k_ref, v_ref, bias_ref, o_ref):
    T = q_ref.shape[1]
    lo_mask = lax.broadcasted_iota(I32, (T, LANES), 1) < A_HEAD_DIM
    for p in range(A_WIDTH // LANES):
        c0 = p * LANES
        parts = [(k_ref[0, :, c0:c0 + LANES], v_ref[0, :, c0:c0 + LANES],
                  bias_ref[2 * p], bias_ref[2 * p + 1], None)]
        o_ref[0, :, c0:c0 + LANES] = _attend(q_ref[0, :, c0:c0 + LANES], parts, lo_mask).astype(BF16)


def _attn_sample(aq, ak, av, cache_k, cache_v, table):
    B, T, _ = aq.shape
    P = cache_k.shape[1]
    L = (P + T + LANES - 1) // LANES * LANES
    pad = jnp.zeros((B, L - P - T, A_WIDTH), BF16)
    kk = jnp.concatenate([cache_k.astype(BF16), ak, pad], axis=1)
    vv = jnp.concatenate([cache_v.astype(BF16), av, pad], axis=1)
    bias = jnp.where((jnp.arange(L) < P + T)[None, None, :], _rel_bias(table, T, L, P), NEG_INF)
    new = pl.BlockSpec((1, T, A_WIDTH), lambda b: (b, 0, 0))
    old = pl.BlockSpec((1, L, A_WIDTH), lambda b: (b, 0, 0))
    return pl.pallas_call(
        _attn_sample_kernel,
        grid=(B,),
        in_specs=[new, old, old, pl.BlockSpec(bias.shape, lambda b: (0, 0, 0))],
        out_specs=new,
        out_shape=jax.ShapeDtypeStruct((B, T, A_WIDTH), BF16),
        compiler_params=_params(("arbitrary",)),
        name="attn_sample",
    )(aq, kk, vv, bias)


def _gla_kernel(q_ref, k_ref, v_ref, la_ref, gr_ref, gain_ref, s0_ref, o_ref, sf_ref, s_scr, *, C, n_chunks):
    j = pl.program_id(1)

    @pl.when(j == 0)
    def _():
        s_scr[...] = s0_ref[0]

    n_sub = C // G_SUB
    ri = lax.broadcasted_iota(I32, (C, C), 0)
    ci = lax.broadcasted_iota(I32, (C, C), 1)
    tril = (ci <= ri).astype(BF16)
    lane_kw = lax.broadcasted_iota(I32, (1, G_KW), 1)
    head_of_lane = lane_kw // G_DK
    row_kw = lax.broadcasted_iota(I32, (C, G_KW), 0)
    ur = lax.broadcasted_iota(I32, (2 * C, 4 * C), 0) - C
    uc = lax.broadcasted_iota(I32, (2 * C, 4 * C), 1)
    u_mat = ((ur >= 0) & ((uc >= C) | (ur <= uc))).astype(BF16)

    def chunk(c, carry):
        r = pl.multiple_of(c * C, C)
        q = q_ref[0, pl.ds(r, C), :].astype(F32)
        k = k_ref[0, pl.ds(r, C), :].astype(F32)
        v = v_ref[0, pl.ds(r, C), :]
        la = la_ref[0, pl.ds(r, C), :]
        S = s_scr[...]

        la_hi, la_lo = _split_bf16(la)
        b = (jnp.dot(tril, la_hi, preferred_element_type=F32)
             + jnp.dot(tril, la_lo, preferred_element_type=F32))

        qs = q * jnp.exp(b)
        q4 = jnp.concatenate([jnp.where(head_of_lane == h, qs, 0.0) for h in range(G_HEADS)], axis=0)
        r_inter = jnp.dot(q4.astype(BF16), S.astype(BF16), preferred_element_type=F32)
        o = jnp.concatenate([r_inter[h * C:(h + 1) * C] for h in range(G_HEADS)], axis=1)

        o_rows = []
        for i in range(n_sub):
            r0, r1 = i * G_SUB, (i + 1) * G_SUB
            bs = b[r0 - 1:r0] if i > 0 else jnp.zeros((1, G_KW), F32)
            qe = q[r0:r1] * jnp.exp(b[r0:r1] - bs)
            ke = (k * jnp.exp(jnp.where(row_kw < r1, bs - b, -jnp.inf))).astype(BF16)
            qst = jnp.concatenate([jnp.where(head_of_lane == h, qe, 0.0) for h in range(G_HEADS)],
                                  axis=0).astype(BF16)
            att = lax.dot_general(qst, ke, (((1,), (1,)), ((), ())), preferred_element_type=F32)
            tt = lax.broadcasted_iota(I32, att.shape, 0) % G_SUB + r0
            ss = lax.broadcasted_iota(I32, att.shape, 1)
            att = jnp.where(ss <= tt, att, 0.0).astype(BF16)
            ov = jnp.dot(att, v, preferred_element_type=F32)
            o_rows.append(jnp.concatenate(
                [ov[h * G_SUB:(h + 1) * G_SUB, h * G_DV:(h + 1) * G_DV] for h in range(G_HEADS)], axis=1))
        o = o + jnp.concatenate(o_rows, axis=0)

        on = _head_rms(o, G_DV) * gain_ref[...] * gr_ref[0, pl.ds(r, C), :].astype(F32)
        o_ref[0, pl.ds(r, C), :] = on.astype(BF16)

        xt = jnp.concatenate([k, la], axis=0).T
        xt_hi, xt_lo = _split_bf16(xt)
        xb = (jnp.dot(xt_hi, u_mat, preferred_element_type=F32)
              + jnp.dot(xt_lo, u_mat, preferred_element_type=F32))
        b_last = xb[:, LANES:]
        kd = (xt * jnp.exp(b_last - xb[:, :LANES])).astype(BF16)
        v_ext = jnp.concatenate([v, jnp.zeros((C, G_VW), BF16)], axis=0)
        kv = jnp.dot(kd, v_ext, preferred_element_type=F32)
        kv_d = jnp.concatenate(
            [kv[h * G_DK:(h + 1) * G_DK, h * G_DV:(h + 1) * G_DV] for h in range(G_HEADS)], axis=0)
        s_scr[...] = jnp.exp(b_last) * S + kv_d
        return carry

    lax.fori_loop(0, n_chunks, chunk, 0)

    @pl.when(j == pl.num_programs(1) - 1)
    def _():
        sf_ref[0] = s_scr[...]


def _gla(gq, gk, gv, la, gr, gain, s0):
    B, T, _ = gq.shape
    C = CHUNK
    tb = min(ROW_TILE, T)
    assert T % tb == 0 and tb % C == 0 and C % G_SUB == 0 and 2 * C == LANES
    kw = pl.BlockSpec((1, tb, G_KW), lambda b, j: (b, j, 0))
    vw = pl.BlockSpec((1, tb, G_VW), lambda b, j: (b, j, 0))
    st = pl.BlockSpec((1, G_KW, G_DV), lambda b, j: (b, 0, 0))
    return pl.pallas_call(
        functools.partial(_gla_kernel, C=C, n_chunks=tb // C),
        grid=(B, T // tb),
        in_specs=[kw, kw, vw, kw, vw, pl.BlockSpec(gain.shape, lambda b, j: (0, 0)), st],
        out_specs=[vw, st],
        out_shape=[jax.ShapeDtypeStruct((B, T, G_VW), BF16), jax.ShapeDtypeStruct((B, G_KW, G_DV), F32)],
        scratch_shapes=[pltpu.VMEM((G_KW, G_DV), F32)],
        compiler_params=_params(("arbitrary", "arbitrary")),
        name="gla",
    )(gq, gk, gv, la, gr, gain, s0)


def _memkv_kernel(mem_ref, gm_ref, w_ref, gk_ref, k_ref, v_ref):
    x = mem_ref[0]
    h = (x * lax.rsqrt(jnp.mean(x * x, -1, keepdims=True) + EPS) * gm_ref[...]).astype(BF16)
    k = jnp.dot(h, w_ref[:, :C_WIDTH], preferred_element_type=F32)
    k_ref[0] = _head_rms(k, C_HEAD_DIM) * gk_ref[...]
    v_ref[0] = jnp.dot(h, w_ref[:, C_WIDTH:], preferred_element_type=F32)


def _memkv(mem, gm, w_kv, gk):
    B, M, _ = mem.shape
    out = pl.BlockSpec((1, M, C_WIDTH), lambda b: (b, 0, 0))
    return pl.pallas_call(
        _memkv_kernel,
        grid=(B,),
        in_specs=[pl.BlockSpec((1, M, D_MODEL), lambda b: (b, 0, 0)),
                  pl.BlockSpec(gm.shape, lambda b: (0, 0)),
                  pl.BlockSpec(w_kv.shape, lambda b: (0, 0)),
                  pl.BlockSpec(gk.shape, lambda b: (0, 0))],
        out_specs=[out, out],
        out_shape=[jax.ShapeDtypeStruct((B, M, C_WIDTH), F32)] * 2,
        compiler_params=_params(("arbitrary",)),
        name="memkv",
    )(mem, gm, w_kv, gk)


def _merge_kernel(x_ref, ya_ref, yb_ref, cq_ref, gt_ref, mk_ref, mv_ref, wa_ref, wb_ref, wc_ref,
                  wo_ref, nf_ref, wr_ref, br_ref,
                  x1_ref, hp_ref, ri_ref, rw_ref, cnt_ref, run_scr):
    first = jnp.logical_and(pl.program_id(0) == 0, pl.program_id(1) == 0)

    @pl.when(first)
    def _():
        run_scr[...] = jnp.zeros_like(run_scr)

    tm = x_ref.shape[1]
    cols = []
    for h in range(C_HEADS):
        c0 = h * C_HEAD_DIM
        q = cq_ref[0, :, c0:c0 + C_HEAD_DIM]
        kh = mk_ref[0, :, c0:c0 + C_HEAD_DIM].astype(BF16)
        vh = mv_ref[0, :, c0:c0 + C_HEAD_DIM].astype(BF16)
        s = lax.dot_general(q, kh, (((1,), (1,)), ((), ())), preferred_element_type=F32) * (C_HEAD_DIM ** -0.5)
        p = jnp.exp(s - s.max(-1, keepdims=True))
        l = p.sum(-1, keepdims=True)
        cols.append(jnp.dot(p.astype(BF16), vh, preferred_element_type=F32) / l)
    yc_in = jnp.concatenate(cols, axis=-1).astype(BF16)

    y_a = jnp.dot(ya_ref[0], wa_ref[...], preferred_element_type=F32)
    y_b = jnp.dot(yb_ref[0], wb_ref[...], preferred_element_type=F32)
    y_c = jnp.dot(yc_in, wc_ref[...], preferred_element_type=F32)
    merged = (gt_ref[0, :, 0:D_MODEL].astype(F32) * y_a
              + gt_ref[0, :, D_MODEL:2 * D_MODEL].astype(F32) * y_b
              + gt_ref[0, :, 2 * D_MODEL:3 * D_MODEL].astype(F32) * y_c)
    x1 = x_ref[0] + jnp.dot(merged.astype(BF16), wo_ref[...], preferred_element_type=F32)
    x1_ref[0] = x1

    h2 = x1 * lax.rsqrt(jnp.mean(x1 * x1, -1, keepdims=True) + EPS) * nf_ref[...]
    _store_slabs(hp_ref, (0,), _pack_rows(h2))

    logits = jnp.dot(h2.astype(BF16), wr_ref[...], preferred_element_type=F32) + br_ref[...]
    lane = lax.broadcasted_iota(I32, (tm, LANES), 1)
    lane_f = lane.astype(F32)
    vals, sels, idxs = [], [], []
    l = logits
    for _ in range(TOP_K):
        m = l.max(-1, keepdims=True)
        idx = jnp.min(jnp.where(l == m, lane_f, float(LANES)), -1, keepdims=True)
        sel = lane_f == idx
        vals.append(m)
        idxs.append(idx)
        sels.append(sel)
        l = jnp.where(sel, -3e38, l)
    es = [jnp.exp(vk - vals[0]) for vk in vals]
    den = es[0] + es[1] + es[2] + es[3]
    cnt = jnp.zeros((tm, LANES), F32)
    for sel in sels:
        cnt = cnt + jnp.where(sel, 1.0, 0.0)
    tp = max(tm, LANES)
    cnt_p = cnt if tp == tm else jnp.concatenate([cnt, jnp.zeros((tp - tm, LANES), F32)], axis=0)
    ri = lax.broadcasted_iota(I32, (tp, tp), 0)
    ci = lax.broadcasted_iota(I32, (tp, tp), 1)
    before = jnp.dot(jnp.where(ci < ri, 1.0, 0.0).astype(BF16), cnt_p.astype(BF16),
                     preferred_element_type=F32)[0:tm] + run_scr[0:1, :]
    r_i = jnp.zeros((tm, LANES), I32)
    r_w = jnp.zeros((tm, LANES), F32)
    for kk in range(TOP_K):
        rank = jnp.sum(jnp.where(sels[kk], before, 0.0), -1, keepdims=True)
        r_i = jnp.where(lane == kk, idxs[kk].astype(I32), r_i)
        r_i = jnp.where(lane == TOP_K + kk, rank.astype(I32), r_i)
        r_w = jnp.where(lane == kk, es[kk] / den, r_w)
    ri_ref[0] = r_i
    rw_ref[0] = r_w
    run_scr[...] = run_scr[...] + jnp.sum(cnt, axis=0, keepdims=True)
    cnt_ref[...] = run_scr[...]


def _merge(x, ya, yb, cq, gt, mk, mv, wa, wb, wc, wo, nf, wr, br):
    B, T, _ = x.shape
    tm = min(ROW_TILE, T)
    assert T % tm == 0

    def row(width):
        return pl.BlockSpec((1, tm, width), lambda b, j: (b, j, 0))

    def full(a):
        return pl.BlockSpec(a.shape, lambda b, j: (0,) * a.ndim)

    mem = pl.BlockSpec((1, N_MEM, C_WIDTH), lambda b, j: (b, 0, 0))
    return pl.pallas_call(
        _merge_kernel,
        grid=(B, T // tm),
        in_specs=[row(D_MODEL), row(A_WIDTH), row(G_VW), row(C_WIDTH), row(N_BRANCH * D_MODEL), mem, mem,
                  full(wa), full(wb), full(wc), full(wo), full(nf), full(wr), full(br)],
        out_specs=[row(D_MODEL), pl.BlockSpec((1, tm, SLABS, LANES), lambda b, j: (b, j, 0, 0)), row(LANES), row(LANES),
                   pl.BlockSpec((8, LANES), lambda b, j: (0, 0))],
        out_shape=[jax.ShapeDtypeStruct((B, T, D_MODEL), F32), jax.ShapeDtypeStruct((B, T, SLABS, LANES), U32),
                   jax.ShapeDtypeStruct((B, T, LANES), I32), jax.ShapeDtypeStruct((B, T, LANES), F32),
                   jax.ShapeDtypeStruct((8, LANES), F32)],
        scratch_shapes=[pltpu.VMEM((8, LANES), F32)],
        compiler_params=_params(("arbitrary", "arbitrary")),
        name="merge",
    )(x, ya, yb, cq, gt, mk, mv, wa, wb, wc, wo, nf, wr, br)


def _dispatch_kernel(dest_ref, dest2_ref, pad_lo_ref, pad_n_ref, hp_ref, hp2_ref, xs_ref, zero_scr, sem, sem2,
                     psem, *, tm, tm2, n_pad):
    def scatter(dref, href, s, n):
        def issue(t, c):
            for kk in range(TOP_K):
                pltpu.make_async_copy(href.at[t], xs_ref.at[dref[0, 0, t * TOP_K + kk]], s).start()
            return c
        lax.fori_loop(0, n, issue, 0)

    def drain(href, s, n):
        for _ in range(TOP_K):
            pltpu.make_async_copy(href, xs_ref.at[pl.ds(0, n)], s).wait()

    scatter(dest_ref, hp_ref, sem, tm)

    @pl.when(pl.program_id(0) == 0)
    def _():
        scatter(dest2_ref, hp2_ref, sem2, tm2)
        zero_scr[...] = jnp.zeros_like(zero_scr)

        def pad_copy(e, n):
            return pltpu.make_async_copy(zero_scr, xs_ref.at[pad_lo_ref[e] + n], psem)

        def fill(e, c):
            lax.fori_loop(0, pad_n_ref[e], lambda n, cc: (pad_copy(e, n).start(), cc)[1], 0)
            return c

        def fill_wait(e, c):
            lax.fori_loop(0, pad_n_ref[e], lambda n, cc: (pad_copy(e, n).wait(), cc)[1], 0)
            return c

        lax.fori_loop(0, n_pad, fill, 0)
        lax.fori_loop(0, n_pad, fill_wait, 0)
        drain(hp2_ref, sem2, tm2)

    drain(hp_ref, sem, tm)


def _dispatch(dest, hp, dest2, hp2, pad_lo, pad_n, rows):
    N, N2 = hp.shape[0], hp2.shape[0]
    tm = min(ROW_TILE, N)
    assert N % tm == 0
    n_steps = N // tm
    smem = functools.partial(pl.BlockSpec, memory_space=pltpu.SMEM)
    n_pad = pad_lo.shape[0]
    return pl.pallas_call(
        functools.partial(_dispatch_kernel, tm=tm, tm2=N2, n_pad=n_pad),
        grid=(n_steps,),
        in_specs=[smem((1, 1, tm * TOP_K), lambda i: (i, 0, 0)),
                  smem((1, 1, N2 * TOP_K), lambda i: (0, 0, 0)),
                  smem((n_pad,), lambda i: (0,)), smem((n_pad,), lambda i: (0,)),
                  pl.BlockSpec((tm, SLABS, LANES), lambda i: (i, 0, 0)),
                  pl.BlockSpec((N2, SLABS, LANES), lambda i: (0, 0, 0))],
        out_specs=pl.BlockSpec(memory_space=pl.ANY),
        out_shape=jax.ShapeDtypeStruct((rows, SLABS, LANES), U32),
        scratch_shapes=[pltpu.VMEM((SLABS, LANES), U32), pltpu.SemaphoreType.DMA(()),
                        pltpu.SemaphoreType.DMA(()), pltpu.SemaphoreType.DMA(())],
        compiler_params=_params(("arbitrary",)),
        name="moe_dispatch",
    )(dest.reshape(n_steps, 1, tm * TOP_K), dest2.reshape(1, 1, N2 * TOP_K), pad_lo, pad_n, hp, hp2)


def _ffn_kernel(be_ref, nu_ref, x_ref, wgu_ref, bgu_ref, wd_ref, bd_ref, y_ref, wgu_bf, wd_bf):
    i = pl.program_id(0)
    prev = be_ref[jnp.maximum(i - 1, 0)]
    new_expert = jnp.logical_or(i == 0, be_ref[i] != prev)

    @pl.when(new_expert)
    def _():
        wgu_bf[...] = wgu_ref[0].astype(BF16)
        wd_bf[...] = wd_ref[0].astype(BF16)

    @pl.when(i < nu_ref[0])
    def _():
        x_lo, x_hi = _unpack_rows(_load_slabs(x_ref, ()))
        gu = (jnp.dot(x_lo.astype(BF16), wgu_bf[0:HALF, :], preferred_element_type=F32)
              + jnp.dot(x_hi.astype(BF16), wgu_bf[HALF:, :], preferred_element_type=F32)
              + bgu_ref[0])
        gate = jnp.minimum(gu[:, :D_EXPERT], SWIGLU_LIMIT)
        up = jnp.clip(gu[:, D_EXPERT:], -SWIGLU_LIMIT, SWIGLU_LIMIT)
        act = (up + 1.0) * (gate * _sigmoid(SWIGLU_ALPHA * gate))
        y = jnp.dot(act.astype(BF16), wd_bf[...], preferred_element_type=F32) + bd_ref[0]
        _store_slabs(y_ref, (), _pack_rows(y))

    @pl.when(i >= nu_ref[0])
    def _():
        y_ref[...] = jnp.zeros_like(y_ref)


def _ffn(blk_expert, n_used, xs, w_gu, b_gu, w_d, b_d):
    P = xs.shape[0]
    nblk = P // FFN_BLOCK
    rows = (FFN_BLOCK, SLABS, LANES)
    grid_spec = pltpu.PrefetchScalarGridSpec(
        num_scalar_prefetch=2,
        grid=(nblk,),
        in_specs=[pl.BlockSpec(rows, lambda i, be, nu: (jnp.minimum(i, nu[0] - 1), 0, 0)),
                  pl.BlockSpec((1, D_MODEL, 2 * D_EXPERT), lambda i, be, nu: (be[i], 0, 0)),
                  pl.BlockSpec((1, 1, 2 * D_EXPERT), lambda i, be, nu: (be[i], 0, 0)),
                  pl.BlockSpec((1, D_EXPERT, D_MODEL), lambda i, be, nu: (be[i], 0, 0)),
                  pl.BlockSpec((1, 1, D_MODEL), lambda i, be, nu: (be[i], 0, 0))],
        out_specs=pl.BlockSpec(rows, lambda i, be, nu: (i, 0, 0)),
        scratch_shapes=[pltpu.VMEM((D_MODEL, 2 * D_EXPERT), BF16), pltpu.VMEM((D_EXPERT, D_MODEL), BF16)],
    )
    return pl.pallas_call(
        _ffn_kernel,
        grid_spec=grid_spec,
        out_shape=jax.ShapeDtypeStruct((P, SLABS, LANES), U32),
        compiler_params=_params(("arbitrary",)),
        name="moe_ffn",
    )(blk_expert, n_used, xs, w_gu, b_gu.reshape(N_EXPERTS, 1, -1), w_d, b_d.reshape(N_EXPERTS, 1, -1))


def _combine_kernel(dcur_ref, dnext_ref, x1_ref, rw_ref, ys_ref, o_ref, buf, sem, *, tm, n_steps):
    i = pl.program_id(0)
    slot = i % 2
    tile = 8

    def base(s, kk):
        return (s * TOP_K + kk) * (tm * tile)

    def issue(dref, s):
        def body(t, c):
            for kk in range(TOP_K):
                row = pl.multiple_of(base(s, kk) + t * tile, tile)
                pltpu.make_async_copy(ys_ref.at[dref[0, 0, t * TOP_K + kk]], buf.at[pl.ds(row, SLABS)],
                                      sem.at[s]).start()
            return c
        lax.fori_loop(0, tm, body, 0)

    @pl.when(i == 0)
    def _():
        issue(dcur_ref, 0)

    @pl.when(i + 1 < n_steps)
    def _():
        issue(dnext_ref, 1 - slot)

    for kk in range(TOP_K):
        pltpu.make_async_copy(buf.at[pl.ds(0, tm * SLABS)], buf.at[pl.ds(tm * SLABS, tm * SLABS)],
                              sem.at[slot]).wait()

    w = rw_ref[...]
    acc_lo = x1_ref[:, :HALF]
    acc_hi = x1_ref[:, HALF:]
    for kk in range(TOP_K):
        start = pl.multiple_of(base(slot, kk), tile)
        u = jnp.concatenate([buf[pl.ds(start + j, tm, stride=tile), :] for j in range(SLABS)], axis=-1)
        lo, hi = _unpack_rows(u)
        wk = w[:, kk:kk + 1]
        acc_lo = acc_lo + wk * lo
        acc_hi = acc_hi + wk * hi
    o_ref[:, :HALF] = acc_lo
    o_ref[:, HALF:] = acc_hi


def _combine(dest, x1, rw, ys):
    N = x1.shape[0]
    tm = min(256, N)
    assert N % tm == 0
    n_steps = N // tm
    dest3 = dest.reshape(n_steps, 1, tm * TOP_K)
    smem = functools.partial(pl.BlockSpec, memory_space=pltpu.SMEM)
    return pl.pallas_call(
        functools.partial(_combine_kernel, tm=tm, n_steps=n_steps),
        grid=(n_steps,),
        in_specs=[smem((1, 1, tm * TOP_K), lambda i: (i, 0, 0)),
                  smem((1, 1, tm * TOP_K), lambda i: (jnp.minimum(i + 1, n_steps - 1), 0, 0)),
                  pl.BlockSpec((tm, D_MODEL), lambda i: (i, 0)),
                  pl.BlockSpec((tm, LANES), lambda i: (i, 0)),
                  pl.BlockSpec(memory_space=pl.ANY)],
        out_specs=pl.BlockSpec((tm, D_MODEL), lambda i: (i, 0)),
        out_shape=jax.ShapeDtypeStruct((N, D_MODEL), F32),
        scratch_shapes=[pltpu.VMEM((2 * TOP_K * tm * 8, LANES), U32), pltpu.SemaphoreType.DMA((2,))],
        compiler_params=_params(("arbitrary",)),
        name="moe_combine",
    )(dest3, dest3, x1, rw, ys)


def _moe(groups, w_gu, b_gu, w_d, b_d):
    n_assign = sum(g[0].shape[0] for g in groups) * TOP_K
    nblk = (n_assign + N_EXPERTS * (FFN_BLOCK - 1) + FFN_BLOCK - 1) // FFN_BLOCK
    counts = [g[4].astype(I32) for g in groups]
    total = sum(counts)
    padded = (total + FFN_BLOCK - 1) // FFN_BLOCK * FFN_BLOCK
    pend = jnp.cumsum(padded)
    pstart = pend - padded
    n_used = pend[-1:] // FFN_BLOCK
    blk = jnp.minimum(jnp.arange(nblk, dtype=I32), n_used[0] - 1) * FFN_BLOCK
    blk_expert = jnp.minimum(jnp.sum(pend[None, :] <= blk[:, None], axis=1), N_EXPERTS - 1).astype(I32)

    experts = jnp.arange(N_EXPERTS, dtype=I32)
    dests = []
    base = pstart
    for g, c in zip(groups, counts):
        idx, rank = g[2][:, :TOP_K], g[2][:, TOP_K:2 * TOP_K]
        dests.append(jnp.sum(jnp.where(idx[..., None] == experts, base, 0), axis=-1) + rank)
        base = base + c

    rows = nblk * FFN_BLOCK
    pad_lo = jnp.concatenate([pstart + total, pend[-1:]])
    pad_n = jnp.concatenate([padded - total, rows - pend[-1:]])
    (g_main, g_small), (d_main, d_small) = groups, dests
    xs = _dispatch(d_main, g_main[1], d_small, g_small[1], pad_lo, pad_n, rows)
    ys = _ffn(blk_expert, n_used.astype(I32), xs, w_gu, b_gu, w_d, b_d)
    return [_combine(d, g[0], g[3], ys) for g, d in zip(groups, dests)]


def _tile_lanes(g, reps):
    return jnp.tile(g.astype(F32), reps)[None, :]


def kernel(x_prompt, x_sample, mem_prompt, cache_attn_k, cache_attn_v, state_gla, cache_mem_k, cache_mem_v, norm_mix, w_in, a_q_norm, a_k_norm, rel_bias_table, w_a_o, w_gla_a_up, b_gla_a, gla_out_norm, w_b_o, c_q_norm, c_k_norm, norm_mem, w_mem_kv, w_c_o, b_gate, w_out, norm_ffn, w_router, b_router, w_gate_up, b_gate_up, w_down, b_down):
    depth = norm_mix.shape[0]
    assert depth == 1
    l = 0
    B, S, _ = x_prompt.shape
    Bs, Ts, _ = x_sample.shape
    keep = min(WINDOW, S)

    w = w_in[l]
    sizes = (A_WIDTH, A_WIDTH, A_WIDTH, G_KW, G_KW, G_VW, G_VW, G_RANK, C_WIDTH, N_BRANCH * D_MODEL)
    offs = [0]
    for s_ in sizes:
        offs.append(offs[-1] + s_)
    seg = [w[:, offs[i]:offs[i + 1]] for i in range(len(sizes))]
    w_r = jnp.concatenate(seg[0:7] + [seg[8], seg[9], seg[7], jnp.zeros((D_MODEL, LANES - G_RANK), F32)],
                          axis=1).astype(BF16)
    nm = norm_mix[l][None, :]
    aqn = _tile_lanes(a_q_norm[l], A_HEADS)
    akn = _tile_lanes(a_k_norm[l], A_HEADS)
    cqn = _tile_lanes(c_q_norm[l], C_HEADS)
    ckn = _tile_lanes(c_k_norm[l], C_HEADS)
    gon = _tile_lanes(gla_out_norm[l], G_HEADS)
    wup = jnp.concatenate([w_gla_a_up[l], jnp.zeros((LANES - G_RANK, G_KW), F32)], axis=0).astype(BF16)
    bla = b_gla_a[l][None, :]
    bg = b_gate[l][None, :]
    wa, wb, wc, wo = (t[l].astype(BF16) for t in (w_a_o, w_b_o, w_c_o, w_out))
    nf = norm_ffn[l][None, :]
    wr = jnp.concatenate([w_router[l], jnp.zeros((D_MODEL, LANES - N_EXPERTS), F32)], axis=1).astype(BF16)
    br = jnp.concatenate([b_router[l], jnp.full((LANES - N_EXPERTS,), NEG_INF, F32)])[None, :]
    table = rel_bias_table[l]

    mk, mv = _memkv(mem_prompt, norm_mem[l][None, :], w_mem_kv[l].astype(BF16), ckn)
    (aq, ak, av, gq, gk, gv, gr, la, cq, gt, ak_tail, av_tail) = _inproj(
        x_prompt, keep, nm, w_r, aqn, akn, cqn, wup, bla, bg)
    ya = _attn_prompt(aq, ak, av, table)
    yb, s_prompt = _gla(gq, gk, gv, la, gr, gon, jnp.zeros((B, G_KW, G_DV), F32))
    x1_p, hp_p, ri_p, rw_p, cnt_p = _merge(x_prompt, ya, yb, cq, gt, mk, mv, wa, wb, wc, wo, nf, wr, br)

    (aq, ak, av, gq, gk, gv, gr, la, cq, gt, ak_new, av_new) = _inproj(
        x_sample.reshape(1, Bs * Ts, D_MODEL), Bs * Ts, nm, w_r, aqn, akn, cqn, wup, bla, bg)
    rs = lambda t: t.reshape(Bs, Ts, t.shape[-1])
    P = cache_attn_k.shape[2]
    ya = _attn_sample(rs(aq), rs(ak), rs(av), cache_attn_k[l].reshape(Bs, P, A_WIDTH),
                      cache_attn_v[l].reshape(Bs, P, A_WIDTH), table)
    t_pad = (Ts + CHUNK - 1) // CHUNK * CHUNK
    zp = lambda t: jnp.pad(rs(t), ((0, 0), (0, t_pad - Ts), (0, 0)))
    yb, s_sample = _gla(zp(gq), zp(gk), zp(gv), zp(la), zp(gr), gon, state_gla[l].reshape(Bs, G_KW, G_DV))
    yb = yb[:, :Ts]
    x1_s, hp_s, ri_s, rw_s, cnt_s = _merge(
        x_sample, ya, yb, rs(cq), rs(gt), cache_mem_k[l].reshape(Bs, N_MEM, C_WIDTH),
        cache_mem_v[l].reshape(Bs, N_MEM, C_WIDTH), wa, wb, wc, wo, nf, wr, br)

    flat = lambda t: t.reshape((-1,) + t.shape[2:])
    y_p, y_s = _moe(
        [(flat(x1_p), flat(hp_p), flat(ri_p), flat(rw_p), cnt_p[0, :N_EXPERTS]),
         (flat(x1_s), flat(hp_s), flat(ri_s), flat(rw_s), cnt_s[0, :N_EXPERTS])],
        w_gate_up[l], b_gate_up[l], w_down[l], b_down[l])

    return (y_p.reshape(B, S, D_MODEL), y_s.reshape(Bs, Ts, D_MODEL),
            ak_tail.reshape(1, B, keep, A_HEADS, A_HEAD_DIM), av_tail.reshape(1, B, keep, A_HEADS, A_HEAD_DIM),
            s_prompt.reshape(1, B, G_HEADS, G_DK, G_DV),
            mk.reshape(1, B, N_MEM, C_HEADS, C_HEAD_DIM), mv.reshape(1, B, N_MEM, C_HEADS, C_HEAD_DIM),
            ak_new.reshape(1, Bs, Ts, A_HEADS, A_HEAD_DIM), av_new.reshape(1, Bs, Ts, A_HEADS, A_HEAD_DIM),
            s_sample.reshape(1, Bs, G_HEADS, G_DK, G_DV))
```

```python
import functools

import jax
import jax.numpy as jnp
from jax import lax
from jax.experimental import pallas as pl
from jax.experimental.pallas import tpu as pltpu

F32 = jnp.float32
BF16 = jnp.bfloat16
U32 = jnp.uint32
I32 = jnp.int32

D_MODEL = 1024
CHUNK = 64
BAND_CHUNKS = 8
WINDOW = BAND_CHUNKS * CHUNK
N_MEM = 256
A_HEADS, A_HEAD_DIM = 8, 64
A_WIDTH = A_HEADS * A_HEAD_DIM
REL_MAX = 128
G_HEADS, G_DK, G_DV = 4, 64, 128
G_KW, G_VW = G_HEADS * G_DK, G_HEADS * G_DV
G_RANK = 16
G_TAU = 16.0
G_SUB = 16
C_HEADS, C_HEAD_DIM = 4, 128
C_WIDTH = C_HEADS * C_HEAD_DIM
N_BRANCH = 3
N_EXPERTS = 32
TOP_K = 4
D_EXPERT = 1024
SWIGLU_LIMIT = 7.0
SWIGLU_ALPHA = 1.702
EPS = 1e-6
NEG_INF = -1e30

LANES = 128
HALF = D_MODEL // 2
ROW_TILE = 512
ATTN_SUB = 128
FFN_BLOCK = 256
VMEM_LIMIT = 56 * 1024 * 1024

OFF_AQ, OFF_AK, OFF_AV = 0, 512, 1024
OFF_GQ, OFF_GK, OFF_GV, OFF_GR = 1536, 1792, 2048, 2560
OFF_CQ, OFF_GATE, OFF_LR = 3072, 3584, 6656
IN_COLS = OFF_LR + LANES


def _params(sem):
    return pltpu.CompilerParams(dimension_semantics=sem, vmem_limit_bytes=VMEM_LIMIT)


def _sigmoid(x):
    return 0.5 * jnp.tanh(0.5 * x) + 0.5


def _head_rms(y, head_dim):
    cols = []
    for p in range(y.shape[1] // LANES):
        blk = y[:, p * LANES:(p + 1) * LANES]
        sq = blk * blk
        if head_dim == LANES:
            sc = lax.rsqrt(jnp.sum(sq, -1, keepdims=True) * (1.0 / LANES) + EPS)
        else:
            lo = lax.broadcasted_iota(I32, blk.shape, 1) < head_dim
            s_lo = jnp.sum(jnp.where(lo, sq, 0.0), -1, keepdims=True)
            s_hi = jnp.sum(jnp.where(lo, 0.0, sq), -1, keepdims=True)
            sc = jnp.where(lo, lax.rsqrt(s_lo * (1.0 / head_dim) + EPS),
                           lax.rsqrt(s_hi * (1.0 / head_dim) + EPS))
        cols.append(blk * sc)
    return jnp.concatenate(cols, axis=-1)


def _split_bf16(x):
    hi = x.astype(BF16)
    lo = (x - hi.astype(F32)).astype(BF16)
    return hi, lo


def _pack_rows(x):
    lo = lax.bitcast_convert_type(x[:, :HALF].astype(BF16).astype(F32), U32)
    hi = lax.bitcast_convert_type(x[:, HALF:].astype(BF16).astype(F32), U32)
    return (lo >> 16) | (hi & jnp.uint32(0xFFFF0000))


def _unpack_rows(u):
    lo = lax.bitcast_convert_type(u << 16, F32)
    hi = lax.bitcast_convert_type(u & jnp.uint32(0xFFFF0000), F32)
    return lo, hi


SLABS = HALF // LANES


def _store_slabs(ref, lead, u):
    for j in range(SLABS):
        ref[lead + (slice(None), j, slice(None))] = u[:, j * LANES:(j + 1) * LANES]


def _load_slabs(ref, lead):
    return jnp.concatenate([ref[lead + (slice(None), j, slice(None))] for j in range(SLABS)], axis=-1)


def _inproj_kernel(x_ref, nm_ref, w_ref, aqn_ref, akn_ref, cqn_ref, wup_ref, bla_ref, bg_ref,
                   aq_ref, ak_ref, av_ref, gq_ref, gk_ref, gv_ref, gr_ref, la_ref, cq_ref, gt_ref,
                   akt_ref, avt_ref, *, n_tiles, n_tail):
    j = pl.program_id(1)
    x = x_ref[0]
    h = (x * lax.rsqrt(jnp.mean(x * x, -1, keepdims=True) + EPS) * nm_ref[...]).astype(BF16)

    def seg(off, width):
        return jnp.dot(h, w_ref[:, off:off + width], preferred_element_type=F32)

    in_tail = j >= n_tiles - n_tail

    aq = _head_rms(seg(OFF_AQ, A_WIDTH), A_HEAD_DIM) * aqn_ref[...] * (A_HEAD_DIM ** -0.5)
    aq_ref[0] = aq.astype(BF16)

    ak = _head_rms(seg(OFF_AK, A_WIDTH), A_HEAD_DIM) * akn_ref[...]
    ak_ref[0] = ak.astype(BF16)

    @pl.when(in_tail)
    def _():
        akt_ref[0] = ak

    av = seg(OFF_AV, A_WIDTH)
    av_ref[0] = av.astype(BF16)

    @pl.when(in_tail)
    def _():
        avt_ref[0] = av

    gq_ref[0] = (seg(OFF_GQ, G_KW) * (G_DK ** -0.5)).astype(BF16)
    gk_ref[0] = seg(OFF_GK, G_KW).astype(BF16)
    gv_ref[0] = seg(OFF_GV, G_VW).astype(BF16)
    gr = seg(OFF_GR, G_VW)
    gr_ref[0] = (gr * _sigmoid(gr)).astype(BF16)

    lr = seg(OFF_LR, LANES).astype(BF16)
    z = jnp.dot(lr, wup_ref[...], preferred_element_type=F32) + bla_ref[...]
    la_ref[0] = (jnp.minimum(z, 0.0) - jnp.log1p(jnp.exp(-jnp.abs(z)))) * (1.0 / G_TAU)

    cq = _head_rms(seg(OFF_CQ, C_WIDTH), C_HEAD_DIM) * cqn_ref[...]
    cq_ref[0] = cq.astype(BF16)

    gate_chunk = 512
    for c in range(N_BRANCH * D_MODEL // gate_chunk):
        lo = c * gate_chunk
        g = seg(OFF_GATE + lo, gate_chunk) + bg_ref[:, lo:lo + gate_chunk]
        gt_ref[0, :, lo:lo + gate_chunk] = _sigmoid(g).astype(BF16)


def _inproj(x, keep, nm, w_r, aqn, akn, cqn, wup, bla, bg):
    G, R, _ = x.shape
    tm = min(ROW_TILE, R)
    n_tiles = R // tm
    n_tail = keep // tm
    assert R % tm == 0 and keep % tm == 0 and n_tail >= 1

    def row(width, dtype):
        return (jax.ShapeDtypeStruct((G, R, width), dtype),
                pl.BlockSpec((1, tm, width), lambda g, j: (g, j, 0)))

    def tail(width):
        return (jax.ShapeDtypeStruct((G, keep, width), F32),
                pl.BlockSpec((1, tm, width), lambda g, j: (g, jnp.maximum(j - (n_tiles - n_tail), 0), 0)))

    outs = [row(A_WIDTH, BF16), row(A_WIDTH, BF16), row(A_WIDTH, BF16), row(G_KW, BF16),
            row(G_KW, BF16), row(G_VW, BF16), row(G_VW, BF16), row(G_KW, F32), row(C_WIDTH, BF16),
            row(N_BRANCH * D_MODEL, BF16), tail(A_WIDTH), tail(A_WIDTH)]

    def full(a):
        return pl.BlockSpec(a.shape, lambda g, j: (0,) * a.ndim)

    return pl.pallas_call(
        functools.partial(_inproj_kernel, n_tiles=n_tiles, n_tail=n_tail),
        grid=(G, n_tiles),
        in_specs=[pl.BlockSpec((1, tm, D_MODEL), lambda g, j: (g, j, 0)), full(nm), full(w_r),
                  full(aqn), full(akn), full(cqn), full(wup), full(bla), full(bg)],
        out_specs=[o[1] for o in outs],
        out_shape=[o[0] for o in outs],
        compiler_params=_params(("arbitrary", "arbitrary")),
        name="inproj",
    )(x, nm, w_r, aqn, akn, cqn, wup, bla, bg)


def _attend(q, parts, lo_mask):
    T = q.shape[0]
    zero = jnp.zeros_like(q)
    q2 = jnp.concatenate([jnp.where(lo_mask, q, zero), jnp.where(lo_mask, zero, q)], axis=0)
    scores = []
    for (k, _, bias2, valid) in parts:
        s = lax.dot_general(q2, k, (((1,), (1,)), ((), ())), preferred_element_type=F32) + bias2
        if valid is not None:
            s = jnp.where(valid, s, NEG_INF)
        scores.append(s)
    m = scores[0].max(-1, keepdims=True)
    for s in scores[1:]:
        m = jnp.maximum(m, s.max(-1, keepdims=True))
    l = jnp.zeros_like(m)
    o = jnp.zeros((2 * T, LANES), F32)
    for s, (_, v, _, _) in zip(scores, parts):
        p = jnp.exp(s - m)
        l = l + p.sum(-1, keepdims=True)
        o = o + jnp.dot(p.astype(BF16), v, preferred_element_type=F32)
    o = o / l
    return jnp.where(lo_mask, o[:T], o[T:])


def _attn_prompt_kernel(q_ref, kp_ref, kc_ref, vp_ref, vc_ref, bias_ref, o_ref, *, tb):
    j = pl.program_id(1)
    has_prev = j > 0
    lo_mask = lax.broadcasted_iota(I32, (ATTN_SUB, LANES), 1) < A_HEAD_DIM
    for s in range(tb // ATTN_SUB):
        r0 = s * ATTN_SUB
        len_a = tb - r0
        len_b = r0 + ATTN_SUB
        for p in range(A_WIDTH // LANES):
            c0 = p * LANES
            q = q_ref[0, r0:r0 + ATTN_SUB, c0:c0 + LANES]
            parts = [
                (kp_ref[0, r0:tb, c0:c0 + LANES], vp_ref[0, r0:tb, c0:c0 + LANES],
                 bias_ref[p, :, 0:len_a], has_prev),
                (kc_ref[0, 0:len_b, c0:c0 + LANES], vc_ref[0, 0:len_b, c0:c0 + LANES],
                 bias_ref[p, :, len_a:len_a + len_b], None),
            ]
            o_ref[0, r0:r0 + ATTN_SUB, c0:c0 + LANES] = _attend(q, parts, lo_mask).astype(BF16)


def _rel_bias(table, n_q, n_k, offset):
    period = n_q + n_k - 1
    m = jnp.arange(period)
    u = table[:, jnp.clip(n_q - 1 + offset - m, -REL_MAX, REL_MAX) + REL_MAX].astype(F32)
    rows = jnp.tile(u, (1, n_q + 1))[:, :n_q * (period + 1)].reshape(-1, n_q, period + 1)[:, :, :n_k]
    return rows[:, ::-1, :]


def _band_bias(table):
    qc = jnp.arange(ATTN_SUB)[:, None] // CHUNK
    kc = jnp.arange(ATTN_SUB + WINDOW)[None, :] // CHUNK
    ok = (kc >= qc) & (kc <= qc + BAND_CHUNKS)
    return jnp.where(ok[None], _rel_bias(table, ATTN_SUB, ATTN_SUB + WINDOW, WINDOW), NEG_INF)


def _attn_prompt(aq, ak, av, table):
    B, S, _ = aq.shape
    tb = WINDOW
    assert S % tb == 0
    bias = _band_bias(table).reshape(A_WIDTH // LANES, 2 * ATTN_SUB, ATTN_SUB + WINDOW)
    cur = pl.BlockSpec((1, tb, A_WIDTH), lambda b, j: (b, j, 0))
    prev = pl.BlockSpec((1, tb, A_WIDTH), lambda b, j: (b, jnp.maximum(j - 1, 0), 0))
    return pl.pallas_call(
        functools.partial(_attn_prompt_kernel, tb=tb),
        grid=(B, S // tb),
        in_specs=[cur, prev, cur, prev, cur, pl.BlockSpec(bias.shape, lambda b, j: (0, 0, 0))],
        out_specs=cur,
        out_shape=jax.ShapeDtypeStruct((B, S, A_WIDTH), BF16),
        compiler_params=_params(("arbitrary", "arbitrary")),
        name="attn_prompt",
    )(aq, ak, ak, av, av, bias)


def _attn_sample_kernel(q_ref, k_ref, v_ref, bias_ref, o_ref):
    T = q_ref.shape[1]
    lo_mask = lax.broadcasted_iota(I32, (T, LANES), 1) < A_HEAD_DIM
    for p in range(A_WIDTH // LANES):
        c0 = p * LANES
        parts = [(k_ref[0, :, c0:c0 + LANES], v_ref[0, :, c0:c0 + LANES], bias_ref[p], None)]
        o_ref[0, :, c0:c0 + LANES] = _attend(q_ref[0, :, c0:c0 + LANES], parts, lo_mask).astype(BF16)


def _attn_sample(aq, ak, av, cache_k, cache_v, table):
    B, T, _ = aq.shape
    P = cache_k.shape[1]
    L = (P + T + LANES - 1) // LANES * LANES
    pad = jnp.zeros((B, L - P - T, A_WIDTH), BF16)
    kk = jnp.concatenate([cache_k.astype(BF16), ak, pad], axis=1)
    vv = jnp.concatenate([cache_v.astype(BF16), av, pad], axis=1)
    bias = jnp.where((jnp.arange(L) < P + T)[None, None, :], _rel_bias(table, T, L, P), NEG_INF)
    bias = bias.reshape(A_WIDTH // LANES, 2 * T, L)
    new = pl.BlockSpec((1, T, A_WIDTH), lambda b: (b, 0, 0))
    old = pl.BlockSpec((1, L, A_WIDTH), lambda b: (b, 0, 0))
    return pl.pallas_call(
        _attn_sample_kernel,
        grid=(B,),
        in_specs=[new, old, old, pl.BlockSpec(bias.shape, lambda b: (0, 0, 0))],
        out_specs=new,
        out_shape=jax.ShapeDtypeStruct((B, T, A_WIDTH), BF16),
        compiler_params=_params(("arbitrary",)),
        name="attn_sample",
    )(aq, kk, vv, bias)


def _gla_kernel(q_ref, k_ref, v_ref, la_ref, gr_ref, gain_ref, s0_ref, o_ref, sf_ref, s_scr, *, C, n_chunks):
    j = pl.program_id(1)

    @pl.when(j == 0)
    def _():
        s_scr[...] = s0_ref[0]

    n_sub = C // G_SUB
    ri = lax.broadcasted_iota(I32, (C, C), 0)
    ci = lax.broadcasted_iota(I32, (C, C), 1)
    tril = (ci <= ri).astype(BF16)
    lane_kw = lax.broadcasted_iota(I32, (1, G_KW), 1)
    head_of_lane = lane_kw // G_DK
    row_kw = lax.broadcasted_iota(I32, (C, G_KW), 0)
    ur = lax.broadcasted_iota(I32, (2 * C, 4 * C), 0) - C
    uc = lax.broadcasted_iota(I32, (2 * C, 4 * C), 1)
    u_mat = ((ur >= 0) & ((uc >= C) | (ur <= uc))).astype(BF16)

    def chunk(c, S):
        r = pl.multiple_of(c * C, C)
        q = q_ref[0, pl.ds(r, C), :].astype(F32)
        k = k_ref[0, pl.ds(r, C), :].astype(F32)
        v = v_ref[0, pl.ds(r, C), :]
        la = la_ref[0, pl.ds(r, C), :]

        la_hi, la_lo = _split_bf16(la)
        b = (jnp.dot(tril, la_hi, preferred_element_type=F32)
             + jnp.dot(tril, la_lo, preferred_element_type=F32))

        qs = q * jnp.exp(b)
        q4 = jnp.concatenate([jnp.where(head_of_lane == h, qs, 0.0) for h in range(G_HEADS)], axis=0)
        r_inter = jnp.dot(q4.astype(BF16), S.astype(BF16), preferred_element_type=F32)
        o = jnp.concatenate([r_inter[h * C:(h + 1) * C] for h in range(G_HEADS)], axis=1)

        o_rows = []
        for i in range(n_sub):
            r0, r1 = i * G_SUB, (i + 1) * G_SUB
            bs = b[r0 - 1:r0] if i > 0 else jnp.zeros((1, G_KW), F32)
            qe = q[r0:r1] * jnp.exp(b[r0:r1] - bs)
            ke = (k * jnp.exp(jnp.where(row_kw < r1, bs - b, -jnp.inf))).astype(BF16)
            qst = jnp.concatenate([jnp.where(head_of_lane == h, qe, 0.0) for h in range(G_HEADS)],
                                  axis=0).astype(BF16)
            att = lax.dot_general(qst, ke, (((1,), (1,)), ((), ())), preferred_element_type=F32)
            tt = lax.broadcasted_iota(I32, att.shape, 0) % G_SUB + r0
            ss = lax.broadcasted_iota(I32, att.shape, 1)
            att = jnp.where(ss <= tt, att, 0.0).astype(BF16)
            ov = jnp.dot(att, v, preferred_element_type=F32)
            o_rows.append(jnp.concatenate(
                [ov[h * G_SUB:(h + 1) * G_SUB, h * G_DV:(h + 1) * G_DV] for h in range(G_HEADS)], axis=1))
        o = o + jnp.concatenate(o_rows, axis=0)

        on = _head_rms(o, G_DV) * gain_ref[...] * gr_ref[0, pl.ds(r, C), :].astype(F32)
        o_ref[0, pl.ds(r, C), :] = on.astype(BF16)

        xt = jnp.concatenate([k, la], axis=0).T
        xt_hi, xt_lo = _split_bf16(xt)
        xb = (jnp.dot(xt_hi, u_mat, preferred_element_type=F32)
              + jnp.dot(xt_lo, u_mat, preferred_element_type=F32))
        b_last = xb[:, LANES:]
        kd = (xt * jnp.exp(b_last - xb[:, :LANES])).astype(BF16)
        v_ext = jnp.concatenate([v, jnp.zeros((C, G_VW), BF16)], axis=0)
        kv = jnp.dot(kd, v_ext, preferred_element_type=F32)
        kv_d = jnp.concatenate(
            [kv[h * G_DK:(h + 1) * G_DK, h * G_DV:(h + 1) * G_DV] for h in range(G_HEADS)], axis=0)
        return jnp.exp(b_last) * S + kv_d

    s_scr[...] = lax.fori_loop(0, n_chunks, chunk, s_scr[...], unroll=2 if n_chunks % 2 == 0 else 1)

    @pl.when(j == pl.num_programs(1) - 1)
    def _():
        sf_ref[0] = s_scr[...]


def _gla(gq, gk, gv, la, gr, gain, s0):
    B, T, _ = gq.shape
    C = CHUNK
    tb = min(ROW_TILE, T)
    assert T % tb == 0 and tb % C == 0 and C % G_SUB == 0 and 2 * C == LANES
    kw = pl.BlockSpec((1, tb, G_KW), lambda b, j: (b, j, 0))
    vw = pl.BlockSpec((1, tb, G_VW), lambda b, j: (b, j, 0))
    st = pl.BlockSpec((1, G_KW, G_DV), lambda b, j: (b, 0, 0))
    return pl.pallas_call(
        functools.partial(_gla_kernel, C=C, n_chunks=tb // C),
        grid=(B, T // tb),
        in_specs=[kw, kw, vw, kw, vw, pl.BlockSpec(gain.shape, lambda b, j: (0, 0)), st],
        out_specs=[vw, st],
        out_shape=[jax.ShapeDtypeStruct((B, T, G_VW), BF16), jax.ShapeDtypeStruct((B, G_KW, G_DV), F32)],
        scratch_shapes=[pltpu.VMEM((G_KW, G_DV), F32)],
        compiler_params=_params(("arbitrary", "arbitrary")),
        name="gla",
    )(gq, gk, gv, la, gr, gain, s0)


def _memkv_kernel(mem_ref, gm_ref, w_ref, gk_ref, k_ref, v_ref):
    x = mem_ref[0]
    h = (x * lax.rsqrt(jnp.mean(x * x, -1, keepdims=True) + EPS) * gm_ref[...]).astype(BF16)
    k = jnp.dot(h, w_ref[:, :C_WIDTH], preferred_element_type=F32)
    k_ref[0] = _head_rms(k, C_HEAD_DIM) * gk_ref[...]
    v_ref[0] = jnp.dot(h, w_ref[:, C_WIDTH:], preferred_element_type=F32)


def _memkv(mem, gm, w_kv, gk):
    B, M, _ = mem.shape
    out = pl.BlockSpec((1, M, C_WIDTH), lambda b: (b, 0, 0))
    return pl.pallas_call(
        _memkv_kernel,
        grid=(B,),
        in_specs=[pl.BlockSpec((1, M, D_MODEL), lambda b: (b, 0, 0)),
                  pl.BlockSpec(gm.shape, lambda b: (0, 0)),
                  pl.BlockSpec(w_kv.shape, lambda b: (0, 0)),
                  pl.BlockSpec(gk.shape, lambda b: (0, 0))],
        out_specs=[out, out],
        out_shape=[jax.ShapeDtypeStruct((B, M, C_WIDTH), F32)] * 2,
        compiler_params=_params(("arbitrary",)),
        name="memkv",
    )(mem, gm, w_kv, gk)


def _merge_kernel(x_ref, ya_ref, yb_ref, cq_ref, gt_ref, mk_ref, mv_ref, wa_ref, wb_ref, wc_ref,
                  wo_ref, nf_ref, wr_ref, br_ref,
                  x1_ref, hp_ref, ri_ref, rw_ref, cnt_ref, run_scr):
    first = jnp.logical_and(pl.program_id(0) == 0, pl.program_id(1) == 0)

    @pl.when(first)
    def _():
        run_scr[...] = jnp.zeros_like(run_scr)

    tm = x_ref.shape[1]
    cols = []
    for h in range(C_HEADS):
        c0 = h * C_HEAD_DIM
        q = cq_ref[0, :, c0:c0 + C_HEAD_DIM]
        kh = mk_ref[0, :, c0:c0 + C_HEAD_DIM].astype(BF16)
        vh = mv_ref[0, :, c0:c0 + C_HEAD_DIM].astype(BF16)
        s = lax.dot_general(q, kh, (((1,), (1,)), ((), ())), preferred_element_type=F32) * (C_HEAD_DIM ** -0.5)
        p = jnp.exp(s - s.max(-1, keepdims=True))
        l = p.sum(-1, keepdims=True)
        cols.append(jnp.dot(p.astype(BF16), vh, preferred_element_type=F32) / l)
    yc_in = jnp.concatenate(cols, axis=-1).astype(BF16)

    y_a = jnp.dot(ya_ref[0], wa_ref[...], preferred_element_type=F32)
    y_b = jnp.dot(yb_ref[0], wb_ref[...], preferred_element_type=F32)
    y_c = jnp.dot(yc_in, wc_ref[...], preferred_element_type=F32)
    merged = (gt_ref[0, :, 0:D_MODEL].astype(F32) * y_a
              + gt_ref[0, :, D_MODEL:2 * D_MODEL].astype(F32) * y_b
              + gt_ref[0, :, 2 * D_MODEL:3 * D_MODEL].astype(F32) * y_c)
    x1 = x_ref[0] + jnp.dot(merged.astype(BF16), wo_ref[...], preferred_element_type=F32)
    x1_ref[0] = x1

    h2 = x1 * lax.rsqrt(jnp.mean(x1 * x1, -1, keepdims=True) + EPS) * nf_ref[...]
    _store_slabs(hp_ref, (0,), _pack_rows(h2))

    logits = jnp.dot(h2.astype(BF16), wr_ref[...], preferred_element_type=F32) + br_ref[...]
    lane = lax.broadcasted_iota(I32, (tm, LANES), 1)
    lane_f = lane.astype(F32)
    vals, sels, idxs = [], [], []
    l = logits
    for _ in range(TOP_K):
        m = l.max(-1, keepdims=True)
        idx = jnp.min(jnp.where(l == m, lane_f, float(LANES)), -1, keepdims=True)
        sel = lane_f == idx
        vals.append(m)
        idxs.append(idx)
        sels.append(sel)
        l = jnp.where(sel, -3e38, l)
    es = [jnp.exp(vk - vals[0]) for vk in vals]
    den = es[0] + es[1] + es[2] + es[3]
    cnt = jnp.zeros((tm, LANES), F32)
    for sel in sels:
        cnt = cnt + jnp.where(sel, 1.0, 0.0)
    tp = max(tm, LANES)
    cnt_p = cnt if tp == tm else jnp.concatenate([cnt, jnp.zeros((tp - tm, LANES), F32)], axis=0)
    ri = lax.broadcasted_iota(I32, (tp, tp), 0)
    ci = lax.broadcasted_iota(I32, (tp, tp), 1)
    before = jnp.dot(jnp.where(ci < ri, 1.0, 0.0).astype(BF16), cnt_p.astype(BF16),
                     preferred_element_type=F32)[0:tm] + run_scr[0:1, :]
    r_i = jnp.zeros((tm, LANES), I32)
    r_w = jnp.zeros((tm, LANES), F32)
    for kk in range(TOP_K):
        rank = jnp.sum(jnp.where(sels[kk], before, 0.0), -1, keepdims=True)
        r_i = jnp.where(lane == kk, idxs[kk].astype(I32), r_i)
        r_i = jnp.where(lane == TOP_K + kk, rank.astype(I32), r_i)
        r_w = jnp.where(lane == kk, es[kk] / den, r_w)
    ri_ref[0] = r_i
    rw_ref[0] = r_w
    run_scr[...] = run_scr[...] + jnp.sum(cnt, axis=0, keepdims=True)
    cnt_ref[...] = run_scr[...]


def _merge(x, ya, yb, cq, gt, mk, mv, wa, wb, wc, wo, nf, wr, br):
    B, T, _ = x.shape
    tm = min(ROW_TILE, T)
    assert T % tm == 0

    def row(width):
        return pl.BlockSpec((1, tm, width), lambda b, j: (b, j, 0))

    def full(a):
        return pl.BlockSpec(a.shape, lambda b, j: (0,) * a.ndim)

    mem = pl.BlockSpec((1, N_MEM, C_WIDTH), lambda b, j: (b, 0, 0))
    return pl.pallas_call(
        _merge_kernel,
        grid=(B, T // tm),
        in_specs=[row(D_MODEL), row(A_WIDTH), row(G_VW), row(C_WIDTH), row(N_BRANCH * D_MODEL), mem, mem,
                  full(wa), full(wb), full(wc), full(wo), full(nf), full(wr), full(br)],
        out_specs=[row(D_MODEL), pl.BlockSpec((1, tm, SLABS, LANES), lambda b, j: (b, j, 0, 0)), row(LANES), row(LANES),
                   pl.BlockSpec((8, LANES), lambda b, j: (0, 0))],
        out_shape=[jax.ShapeDtypeStruct((B, T, D_MODEL), F32), jax.ShapeDtypeStruct((B, T, SLABS, LANES), U32),
                   jax.ShapeDtypeStruct((B, T, LANES), I32), jax.ShapeDtypeStruct((B, T, LANES), F32),
                   jax.ShapeDtypeStruct((8, LANES), F32)],
        scratch_shapes=[pltpu.VMEM((8, LANES), F32)],
        compiler_params=_params(("arbitrary", "arbitrary")),
        name="merge",
    )(x, ya, yb, cq, gt, mk, mv, wa, wb, wc, wo, nf, wr, br)


def _dispatch_kernel(dest_ref, dest2_ref, pad_lo_ref, pad_n_ref, hp_ref, hp2_ref, xs_ref, zero_scr, sem, sem2,
                     psem, *, tm, tm2, n_pad):
    def scatter(dref, href, s, n):
        def issue(t, c):
            for kk in range(TOP_K):
                pltpu.make_async_copy(href.at[t], xs_ref.at[dref[0, 0, t * TOP_K + kk]], s).start(priority=kk % 2)
            return c
        lax.fori_loop(0, n, issue, 0)

    def drain(href, s, n):
        for _ in range(TOP_K):
            pltpu.make_async_copy(href, xs_ref.at[pl.ds(0, n)], s).wait()

    scatter(dest_ref, hp_ref, sem, tm)

    @pl.when(pl.program_id(0) == 0)
    def _():
        scatter(dest2_ref, hp2_ref, sem2, tm2)
        zero_scr[...] = jnp.zeros_like(zero_scr)

        def pad_copy(e, n):
            return pltpu.make_async_copy(zero_scr, xs_ref.at[pad_lo_ref[e] + n], psem)

        def fill(e, c):
            lax.fori_loop(0, pad_n_ref[e], lambda n, cc: (pad_copy(e, n).start(), cc)[1], 0)
            return c

        def fill_wait(e, c):
            lax.fori_loop(0, pad_n_ref[e], lambda n, cc: (pad_copy(e, n).wait(), cc)[1], 0)
            return c

        lax.fori_loop(0, n_pad, fill, 0)
        lax.fori_loop(0, n_pad, fill_wait, 0)
        drain(hp2_ref, sem2, tm2)

    drain(hp_ref, sem, tm)


def _dispatch(dest, hp, dest2, hp2, pad_lo, pad_n, rows):
    N, N2 = hp.shape[0], hp2.shape[0]
    tm = min(ROW_TILE, N)
    assert N % tm == 0
    n_steps = N // tm
    smem = functools.partial(pl.BlockSpec, memory_space=pltpu.SMEM)
    n_pad = pad_lo.shape[0]
    return pl.pallas_call(
        functools.partial(_dispatch_kernel, tm=tm, tm2=N2, n_pad=n_pad),
        grid=(n_steps,),
        in_specs=[smem((1, 1, tm * TOP_K), lambda i: (i, 0, 0)),
                  smem((1, 1, N2 * TOP_K), lambda i: (0, 0, 0)),
                  smem((n_pad,), lambda i: (0,)), smem((n_pad,), lambda i: (0,)),
                  pl.BlockSpec((tm, SLABS, LANES), lambda i: (i, 0, 0)),
                  pl.BlockSpec((N2, SLABS, LANES), lambda i: (0, 0, 0))],
        out_specs=pl.BlockSpec(memory_space=pl.ANY),
        out_shape=jax.ShapeDtypeStruct((rows, SLABS, LANES), U32),
        scratch_shapes=[pltpu.VMEM((SLABS, LANES), U32), pltpu.SemaphoreType.DMA(()),
                        pltpu.SemaphoreType.DMA(()), pltpu.SemaphoreType.DMA(())],
        compiler_params=_params(("arbitrary",)),
        name="moe_dispatch",
    )(dest.reshape(n_steps, 1, tm * TOP_K), dest2.reshape(1, 1, N2 * TOP_K), pad_lo, pad_n, hp, hp2)


def _ffn_kernel(be_ref, nu_ref, x_ref, wgu_ref, bgu_ref, wd_ref, bd_ref, y_ref, wgu_bf, wd_bf):
    i = pl.program_id(0)
    prev = be_ref[jnp.maximum(i - 1, 0)]
    new_expert = jnp.logical_or(i == 0, be_ref[i] != prev)

    @pl.when(new_expert)
    def _():
        wgu_bf[...] = wgu_ref[0].astype(BF16)
        wd_bf[...] = wd_ref[0].astype(BF16)

    @pl.when(i < nu_ref[0])
    def _():
        x_lo, x_hi = _unpack_rows(_load_slabs(x_ref, ()))
        gu = (jnp.dot(x_lo.astype(BF16), wgu_bf[0:HALF, :], preferred_element_type=F32)
              + jnp.dot(x_hi.astype(BF16), wgu_bf[HALF:, :], preferred_element_type=F32)
              + bgu_ref[0])
        gate = jnp.minimum(gu[:, :D_EXPERT], SWIGLU_LIMIT)
        up = jnp.clip(gu[:, D_EXPERT:], -SWIGLU_LIMIT, SWIGLU_LIMIT)
        act = (up + 1.0) * (gate * _sigmoid(SWIGLU_ALPHA * gate))
        y = jnp.dot(act.astype(BF16), wd_bf[...], preferred_element_type=F32) + bd_ref[0]
        _store_slabs(y_ref, (), _pack_rows(y))

    @pl.when(i >= nu_ref[0])
    def _():
        y_ref[...] = jnp.zeros_like(y_ref)


def _ffn(blk_expert, n_used, xs, w_gu, b_gu, w_d, b_d):
    P = xs.shape[0]
    nblk = P // FFN_BLOCK
    rows = (FFN_BLOCK, SLABS, LANES)
    grid_spec = pltpu.PrefetchScalarGridSpec(
        num_scalar_prefetch=2,
        grid=(nblk,),
        in_specs=[pl.BlockSpec(rows, lambda i, be, nu: (jnp.minimum(i, nu[0] - 1), 0, 0)),
                  pl.BlockSpec((1, D_MODEL, 2 * D_EXPERT), lambda i, be, nu: (be[i], 0, 0)),
                  pl.BlockSpec((1, 1, 2 * D_EXPERT), lambda i, be, nu: (be[i], 0, 0)),
                  pl.BlockSpec((1, D_EXPERT, D_MODEL), lambda i, be, nu: (be[i], 0, 0)),
                  pl.BlockSpec((1, 1, D_MODEL), lambda i, be, nu: (be[i], 0, 0))],
        out_specs=pl.BlockSpec(rows, lambda i, be, nu: (i, 0, 0)),
        scratch_shapes=[pltpu.VMEM((D_MODEL, 2 * D_EXPERT), BF16), pltpu.VMEM((D_EXPERT, D_MODEL), BF16)],
    )
    return pl.pallas_call(
        _ffn_kernel,
        grid_spec=grid_spec,
        out_shape=jax.ShapeDtypeStruct((P, SLABS, LANES), U32),
        compiler_params=_params(("arbitrary",)),
        name="moe_ffn",
    )(blk_expert, n_used, xs, w_gu, b_gu.reshape(N_EXPERTS, 1, -1), w_d, b_d.reshape(N_EXPERTS, 1, -1))


def _combine_kernel(dcur_ref, dnext_ref, x1_ref, rw_ref, ys_ref, o_ref, buf, sem, *, tm, n_steps):
    i = pl.program_id(0)
    slot = i % 2
    tile = 8

    def base(s, kk):
        return (s * TOP_K + kk) * (tm * tile)

    def issue(dref, s):
        def body(t, c):
            for kk in range(TOP_K):
                row = pl.multiple_of(base(s, kk) + t * tile, tile)
                pltpu.make_async_copy(ys_ref.at[dref[0, 0, t * TOP_K + kk]], buf.at[pl.ds(row, SLABS)],
                                      sem.at[s]).start(priority=kk % 2)
            return c
        lax.fori_loop(0, tm, body, 0)

    @pl.when(i == 0)
    def _():
        issue(dcur_ref, 0)

    @pl.when(i + 1 < n_steps)
    def _():
        issue(dnext_ref, 1 - slot)

    for kk in range(TOP_K):
        pltpu.make_async_copy(buf.at[pl.ds(0, tm * SLABS)], buf.at[pl.ds(tm * SLABS, tm * SLABS)],
                              sem.at[slot]).wait()

    w = rw_ref[...]
    acc_lo = x1_ref[:, :HALF]
    acc_hi = x1_ref[:, HALF:]
    for kk in range(TOP_K):
        start = pl.multiple_of(base(slot, kk), tile)
        u = jnp.concatenate([buf[pl.ds(start + j, tm, stride=tile), :] for j in range(SLABS)], axis=-1)
        lo, hi = _unpack_rows(u)
        wk = w[:, kk:kk + 1]
        acc_lo = acc_lo + wk * lo
        acc_hi = acc_hi + wk * hi
    o_ref[:, :HALF] = acc_lo
    o_ref[:, HALF:] = acc_hi


def _combine(dest, x1, rw, ys):
    N = x1.shape[0]
    tm = min(256, N)
    assert N % tm == 0
    n_steps = N // tm
    dest3 = dest.reshape(n_steps, 1, tm * TOP_K)
    smem = functools.partial(pl.BlockSpec, memory_space=pltpu.SMEM)
    return pl.pallas_call(
        functools.partial(_combine_kernel, tm=tm, n_steps=n_steps),
        grid=(n_steps,),
        in_specs=[smem((1, 1, tm * TOP_K), lambda i: (i, 0, 0)),
                  smem((1, 1, tm * TOP_K), lambda i: (jnp.minimum(i + 1, n_steps - 1), 0, 0)),
                  pl.BlockSpec((tm, D_MODEL), lambda i: (i, 0)),
                  pl.BlockSpec((tm, LANES), lambda i: (i, 0)),
                  pl.BlockSpec(memory_space=pl.ANY)],
        out_specs=pl.BlockSpec((tm, D_MODEL), lambda i: (i, 0)),
        out_shape=jax.ShapeDtypeStruct((N, D_MODEL), F32),
        scratch_shapes=[pltpu.VMEM((2 * TOP_K * tm * 8, LANES), U32), pltpu.SemaphoreType.DMA((2,))],
        compiler_params=_params(("arbitrary",)),
        name="moe_combine",
    )(dest3, dest3, x1, rw, ys)


def _moe(groups, w_gu, b_gu, w_d, b_d):
    n_assign = sum(g[0].shape[0] for g in groups) * TOP_K
    nblk = (n_assign + N_EXPERTS * (FFN_BLOCK - 1) + FFN_BLOCK - 1) // FFN_BLOCK
    counts = [g[4].astype(I32) for g in groups]
    total = sum(counts)
    padded = (total + FFN_BLOCK - 1) // FFN_BLOCK * FFN_BLOCK
    pend = jnp.cumsum(padded)
    pstart = pend - padded
    n_used = pend[-1:] // FFN_BLOCK
    blk = jnp.minimum(jnp.arange(nblk, dtype=I32), n_used[0] - 1) * FFN_BLOCK
    blk_expert = jnp.minimum(jnp.sum(pend[None, :] <= blk[:, None], axis=1), N_EXPERTS - 1).astype(I32)

    experts = jnp.arange(N_EXPERTS, dtype=I32)
    dests = []
    base = pstart
    for g, c in zip(groups, counts):
        idx, rank = g[2][:, :TOP_K], g[2][:, TOP_K:2 * TOP_K]
        dests.append(jnp.sum(jnp.where(idx[..., None] == experts, base, 0), axis=-1) + rank)
        base = base + c

    rows = nblk * FFN_BLOCK
    pad_lo = jnp.concatenate([pstart + total, pend[-1:]])
    pad_n = jnp.concatenate([padded - total, rows - pend[-1:]])
    (g_main, g_small), (d_main, d_small) = groups, dests
    xs = _dispatch(d_main, g_main[1], d_small, g_small[1], pad_lo, pad_n, rows)
    ys = _ffn(blk_expert, n_used.astype(I32), xs, w_gu, b_gu, w_d, b_d)
    return [_combine(d, g[0], g[3], ys) for g, d in zip(groups, dests)]


def _tile_lanes(g, reps):
    return jnp.tile(g.astype(F32), reps)[None, :]


def kernel(x_prompt, x_sample, mem_prompt, cache_attn_k, cache_attn_v, state_gla, cache_mem_k, cache_mem_v, norm_mix, w_in, a_q_norm, a_k_norm, rel_bias_table, w_a_o, w_gla_a_up, b_gla_a, gla_out_norm, w_b_o, c_q_norm, c_k_norm, norm_mem, w_mem_kv, w_c_o, b_gate, w_out, norm_ffn, w_router, b_router, w_gate_up, b_gate_up, w_down, b_down):
    depth = norm_mix.shape[0]
    assert depth == 1
    l = 0
    B, S, _ = x_prompt.shape
    Bs, Ts, _ = x_sample.shape
    keep = min(WINDOW, S)

    w = w_in[l]
    sizes = (A_WIDTH, A_WIDTH, A_WIDTH, G_KW, G_KW, G_VW, G_VW, G_RANK, C_WIDTH, N_BRANCH * D_MODEL)
    offs = [0]
    for s_ in sizes:
        offs.append(offs[-1] + s_)
    seg = [w[:, offs[i]:offs[i + 1]] for i in range(len(sizes))]
    w_r = jnp.concatenate(seg[0:7] + [seg[8], seg[9], seg[7], jnp.zeros((D_MODEL, LANES - G_RANK), F32)],
                          axis=1).astype(BF16)
    nm = norm_mix[l][None, :]
    aqn = _tile_lanes(a_q_norm[l], A_HEADS)
    akn = _tile_lanes(a_k_norm[l], A_HEADS)
    cqn = _tile_lanes(c_q_norm[l], C_HEADS)
    ckn = _tile_lanes(c_k_norm[l], C_HEADS)
    gon = _tile_lanes(gla_out_norm[l], G_HEADS)
    wup = jnp.concatenate([w_gla_a_up[l], jnp.zeros((LANES - G_RANK, G_KW), F32)], axis=0).astype(BF16)
    bla = b_gla_a[l][None, :]
    bg = b_gate[l][None, :]
    wa, wb, wc, wo = (t[l].astype(BF16) for t in (w_a_o, w_b_o, w_c_o, w_out))
    nf = norm_ffn[l][None, :]
    wr = jnp.concatenate([w_router[l], jnp.zeros((D_MODEL, LANES - N_EXPERTS), F32)], axis=1).astype(BF16)
    br = jnp.concatenate([b_router[l], jnp.full((LANES - N_EXPERTS,), NEG_INF, F32)])[None, :]
    table = rel_bias_table[l]

    mk, mv = _memkv(mem_prompt, norm_mem[l][None, :], w_mem_kv[l].astype(BF16), ckn)
    (aq, ak, av, gq, gk, gv, gr, la, cq, gt, ak_tail, av_tail) = _inproj(
        x_prompt, keep, nm, w_r, aqn, akn, cqn, wup, bla, bg)
    ya = _attn_prompt(aq, ak, av, table)
    yb, s_prompt = _gla(gq, gk, gv, la, gr, gon, jnp.zeros((B, G_KW, G_DV), F32))
    x1_p, hp_p, ri_p, rw_p, cnt_p = _merge(x_prompt, ya, yb, cq, gt, mk, mv, wa, wb, wc, wo, nf, wr, br)

    (aq, ak, av, gq, gk, gv, gr, la, cq, gt, ak_new, av_new) = _inproj(
        x_sample.reshape(1, Bs * Ts, D_MODEL), Bs * Ts, nm, w_r, aqn, akn, cqn, wup, bla, bg)
    rs = lambda t: t.reshape(Bs, Ts, t.shape[-1])
    P = cache_attn_k.shape[2]
    ya = _attn_sample(rs(aq), rs(ak), rs(av), cache_attn_k[l].reshape(Bs, P, A_WIDTH),
                      cache_attn_v[l].reshape(Bs, P, A_WIDTH), table)
    t_pad = (Ts + CHUNK - 1) // CHUNK * CHUNK
    zp = lambda t: jnp.pad(rs(t), ((0, 0), (0, t_pad - Ts), (0, 0)))
    yb, s_sample = _gla(zp(gq), zp(gk), zp(gv), zp(la), zp(gr), gon, state_gla[l].reshape(Bs, G_KW, G_DV))
    yb = yb[:, :Ts]
    x1_s, hp_s, ri_s, rw_s, cnt_s = _merge(
        x_sample, ya, yb, rs(cq), rs(gt), cache_mem_k[l].reshape(Bs, N_MEM, C_WIDTH),
        cache_mem_v[l].reshape(Bs, N_MEM, C_WIDTH), wa, wb, wc, wo, nf, wr, br)

    flat = lambda t: t.reshape((-1,) + t.shape[2:])
    y_p, y_s = _moe(
        [(flat(x1_p), flat(hp_p), flat(ri_p), flat(rw_p), cnt_p[0, :N_EXPERTS]),
         (flat(x1_s), flat(hp_s), flat(ri_s), flat(rw_s), cnt_s[0, :N_EXPERTS])],
        w_gate_up[l], b_gate_up[l], w_down[l], b_down[l])

    return (y_p.reshape(B, S, D_MODEL), y_s.reshape(Bs, Ts, D_MODEL),
            ak_tail.reshape(1, B, keep, A_HEADS, A_HEAD_DIM), av_tail.reshape(1, B, keep, A_HEADS, A_HEAD_DIM),
            s_prompt.reshape(1, B, G_HEADS, G_DK, G_DV),
            mk.reshape(1, B, N_MEM, C_HEADS, C_HEAD_DIM), mv.reshape(1, B, N_MEM, C_HEADS, C_HEAD_DIM),
            ak_new.reshape(1, Bs, Ts, A_HEADS, A_HEAD_DIM), av_new.reshape(1, Bs, Ts, A_HEADS, A_HEAD_DIM),
            s_sample.reshape(1, Bs, G_HEADS, G_DK, G_DV))
```

```python
import functools

import jax
import jax.numpy as jnp
from jax import lax
from jax.experimental import pallas as pl
from jax.experimental.pallas import tpu as pltpu

F32 = jnp.float32
BF16 = jnp.bfloat16
U32 = jnp.uint32
I32 = jnp.int32

D_MODEL = 1024
CHUNK = 64
BAND_CHUNKS = 8
WINDOW = BAND_CHUNKS * CHUNK
N_MEM = 256
A_HEADS, A_HEAD_DIM = 8, 64
A_WIDTH = A_HEADS * A_HEAD_DIM
REL_MAX = 128
G_HEADS, G_DK, G_DV = 4, 64, 128
G_KW, G_VW = G_HEADS * G_DK, G_HEADS * G_DV
G_RANK = 16
G_TAU = 16.0
G_SUB = 16
C_HEADS, C_HEAD_DIM = 4, 128
C_WIDTH = C_HEADS * C_HEAD_DIM
N_BRANCH = 3
N_EXPERTS = 32
TOP_K = 4
D_EXPERT = 1024
SWIGLU_LIMIT = 7.0
SWIGLU_ALPHA = 1.702
EPS = 1e-6
NEG_INF = -1e30

LANES = 128
HALF = D_MODEL // 2
ROW_TILE = 512
ATTN_SUB = 128
FFN_BLOCK = 256
VMEM_LIMIT = 56 * 1024 * 1024

OFF_AQ, OFF_AK, OFF_AV = 0, 512, 1024
OFF_GQ, OFF_GK, OFF_GV, OFF_GR = 1536, 1792, 2048, 2560
OFF_CQ, OFF_GATE, OFF_LR = 3072, 3584, 6656
IN_COLS = OFF_LR + LANES


def _params(sem):
    return pltpu.CompilerParams(dimension_semantics=sem, vmem_limit_bytes=VMEM_LIMIT)


def _sigmoid(x):
    return 0.5 * jnp.tanh(0.5 * x) + 0.5


def _head_rms(y, head_dim):
    cols = []
    for p in range(y.shape[1] // LANES):
        blk = y[:, p * LANES:(p + 1) * LANES]
        sq = blk * blk
        if head_dim == LANES:
            sc = lax.rsqrt(jnp.sum(sq, -1, keepdims=True) * (1.0 / LANES) + EPS)
        else:
            lo = lax.broadcasted_iota(I32, blk.shape, 1) < head_dim
            s_lo = jnp.sum(jnp.where(lo, sq, 0.0), -1, keepdims=True)
            s_hi = jnp.sum(jnp.where(lo, 0.0, sq), -1, keepdims=True)
            sc = jnp.where(lo, lax.rsqrt(s_lo * (1.0 / head_dim) + EPS),
                           lax.rsqrt(s_hi * (1.0 / head_dim) + EPS))
        cols.append(blk * sc)
    return jnp.concatenate(cols, axis=-1)


def _split_bf16(x):
    hi = x.astype(BF16)
    lo = (x - hi.astype(F32)).astype(BF16)
    return hi, lo


def _pack_rows(x):
    lo = lax.bitcast_convert_type(x[:, :HALF].astype(BF16).astype(F32), U32)
    hi = lax.bitcast_convert_type(x[:, HALF:].astype(BF16).astype(F32), U32)
    return (lo >> 16) | (hi & jnp.uint32(0xFFFF0000))


def _unpack_rows(u):
    lo = lax.bitcast_convert_type(u << 16, F32)
    hi = lax.bitcast_convert_type(u & jnp.uint32(0xFFFF0000), F32)
    return lo, hi


SLABS = HALF // LANES


def _store_slabs(ref, lead, u):
    slab_major = jnp.stack([u[:, j * LANES:(j + 1) * LANES] for j in range(SLABS)], axis=0)
    ref[lead + (slice(None),) * 3] = pltpu.einshape("jtl->tjl", slab_major)


def _load_slabs(ref, lead):
    slab_major = pltpu.einshape("tjl->jtl", ref[lead + (slice(None),) * 3])
    return jnp.concatenate([slab_major[j] for j in range(SLABS)], axis=-1)


def _inproj_kernel(x_ref, nm_ref, w_ref, aqn_ref, akn_ref, cqn_ref, wup_ref, bla_ref, bg_ref,
                   aq_ref, ak_ref, av_ref, gq_ref, gk_ref, gv_ref, gr_ref, la_ref, cq_ref, gt_ref,
                   akt_ref, avt_ref, *, n_tiles, n_tail):
    j = pl.program_id(1)
    x = x_ref[0]
    h = (x * lax.rsqrt(jnp.mean(x * x, -1, keepdims=True) + EPS) * nm_ref[...]).astype(BF16)

    def seg(off, width):
        return jnp.dot(h, w_ref[:, off:off + width], preferred_element_type=F32)

    in_tail = j >= n_tiles - n_tail

    aq = _head_rms(seg(OFF_AQ, A_WIDTH), A_HEAD_DIM) * aqn_ref[...] * (A_HEAD_DIM ** -0.5)
    aq_ref[0] = aq.astype(BF16)

    ak = _head_rms(seg(OFF_AK, A_WIDTH), A_HEAD_DIM) * akn_ref[...]
    ak_ref[0] = ak.astype(BF16)

    @pl.when(in_tail)
    def _():
        akt_ref[0] = ak

    av = seg(OFF_AV, A_WIDTH)
    av_ref[0] = av.astype(BF16)

    @pl.when(in_tail)
    def _():
        avt_ref[0] = av

    gq_ref[0] = (seg(OFF_GQ, G_KW) * (G_DK ** -0.5)).astype(BF16)
    gk_ref[0] = seg(OFF_GK, G_KW).astype(BF16)
    gv_ref[0] = seg(OFF_GV, G_VW).astype(BF16)
    gr = seg(OFF_GR, G_VW)
    gr_ref[0] = (gr * _sigmoid(gr)).astype(BF16)

    lr = seg(OFF_LR, LANES).astype(BF16)
    z = jnp.dot(lr, wup_ref[...], preferred_element_type=F32) + bla_ref[...]
    la_ref[0] = (jnp.minimum(z, 0.0) - jnp.log1p(jnp.exp(-jnp.abs(z)))) * (1.0 / G_TAU)

    cq = _head_rms(seg(OFF_CQ, C_WIDTH), C_HEAD_DIM) * cqn_ref[...]
    cq_ref[0] = cq.astype(BF16)

    gate_chunk = 512
    for c in range(N_BRANCH * D_MODEL // gate_chunk):
        lo = c * gate_chunk
        g = seg(OFF_GATE + lo, gate_chunk) + bg_ref[:, lo:lo + gate_chunk]
        gt_ref[0, :, lo:lo + gate_chunk] = _sigmoid(g).astype(BF16)


def _inproj(x, keep, nm, w_r, aqn, akn, cqn, wup, bla, bg):
    G, R, _ = x.shape
    tm = min(ROW_TILE, R)
    n_tiles = R // tm
    n_tail = keep // tm
    assert R % tm == 0 and keep % tm == 0 and n_tail >= 1

    def row(width, dtype):
        return (jax.ShapeDtypeStruct((G, R, width), dtype),
                pl.BlockSpec((1, tm, width), lambda g, j: (g, j, 0)))

    def tail(width):
        return (jax.ShapeDtypeStruct((G, keep, width), F32),
                pl.BlockSpec((1, tm, width), lambda g, j: (g, jnp.maximum(j - (n_tiles - n_tail), 0), 0)))

    outs = [row(A_WIDTH, BF16), row(A_WIDTH, BF16), row(A_WIDTH, BF16), row(G_KW, BF16),
            row(G_KW, BF16), row(G_VW, BF16), row(G_VW, BF16), row(G_KW, F32), row(C_WIDTH, BF16),
            row(N_BRANCH * D_MODEL, BF16), tail(A_WIDTH), tail(A_WIDTH)]

    def full(a):
        return pl.BlockSpec(a.shape, lambda g, j: (0,) * a.ndim)

    return pl.pallas_call(
        functools.partial(_inproj_kernel, n_tiles=n_tiles, n_tail=n_tail),
        grid=(G, n_tiles),
        in_specs=[pl.BlockSpec((1, tm, D_MODEL), lambda g, j: (g, j, 0)), full(nm), full(w_r),
                  full(aqn), full(akn), full(cqn), full(wup), full(bla), full(bg)],
        out_specs=[o[1] for o in outs],
        out_shape=[o[0] for o in outs],
        compiler_params=_params(("arbitrary", "arbitrary")),
        name="inproj",
    )(x, nm, w_r, aqn, akn, cqn, wup, bla, bg)


def _attend(pairs, lo_mask):
    T = pairs[0][0].shape[0]
    scores = []
    for q, parts in pairs:
        zero = jnp.zeros_like(q)
        q2 = jnp.concatenate([jnp.where(lo_mask, q, zero), jnp.where(lo_mask, zero, q)], axis=0)
        ss = []
        for (k, _, bias2, valid) in parts:
            s = lax.dot_general(q2, k, (((1,), (1,)), ((), ())), preferred_element_type=F32) + bias2
            if valid is not None:
                s = jnp.where(valid, s, NEG_INF)
            ss.append(s)
        scores.append(ss)
    probs, sums = [], []
    for ss in scores:
        m = ss[0].max(-1, keepdims=True)
        for s in ss[1:]:
            m = jnp.maximum(m, s.max(-1, keepdims=True))
        ps = [jnp.exp(s - m) for s in ss]
        l = ps[0].sum(-1, keepdims=True)
        for p in ps[1:]:
            l = l + p.sum(-1, keepdims=True)
        probs.append([p.astype(BF16) for p in ps])
        sums.append(l)
    outs = []
    for (q, parts), ps, l in zip(pairs, probs, sums):
        o = jnp.dot(ps[0], parts[0][1], preferred_element_type=F32)
        for p, part in zip(ps[1:], parts[1:]):
            o = o + jnp.dot(p, part[1], preferred_element_type=F32)
        o = o / l
        outs.append(jnp.where(lo_mask, o[:T], o[T:]))
    return outs


def _attn_prompt_kernel(q_ref, kp_ref, kc_ref, vp_ref, vc_ref, bias_ref, o_ref, *, tb):
    j = pl.program_id(1)
    has_prev = j > 0
    lo_mask = lax.broadcasted_iota(I32, (ATTN_SUB, LANES), 1) < A_HEAD_DIM
    for s in range(tb // ATTN_SUB):
        r0 = s * ATTN_SUB
        len_a = tb - r0
        len_b = r0 + ATTN_SUB
        pairs = []
        for p in range(A_WIDTH // LANES):
            c0 = p * LANES
            q = q_ref[0, r0:r0 + ATTN_SUB, c0:c0 + LANES]
            parts = [
                (kp_ref[0, r0:tb, c0:c0 + LANES], vp_ref[0, r0:tb, c0:c0 + LANES],
                 bias_ref[p, :, 0:len_a], has_prev),
                (kc_ref[0, 0:len_b, c0:c0 + LANES], vc_ref[0, 0:len_b, c0:c0 + LANES],
                 bias_ref[p, :, len_a:len_a + len_b], None),
            ]
            pairs.append((q, parts))
        for p, o in enumerate(_attend(pairs, lo_mask)):
            o_ref[0, r0:r0 + ATTN_SUB, p * LANES:(p + 1) * LANES] = o.astype(BF16)


def _rel_bias(table, n_q, n_k, offset):
    period = n_q + n_k - 1
    m = jnp.arange(period)
    u = table[:, jnp.clip(n_q - 1 + offset - m, -REL_MAX, REL_MAX) + REL_MAX].astype(F32)
    rows = jnp.tile(u, (1, n_q + 1))[:, :n_q * (period + 1)].reshape(-1, n_q, period + 1)[:, :, :n_k]
    return rows[:, ::-1, :]


def _band_bias(table):
    qc = jnp.arange(ATTN_SUB)[:, None] // CHUNK
    kc = jnp.arange(ATTN_SUB + WINDOW)[None, :] // CHUNK
    ok = (kc >= qc) & (kc <= qc + BAND_CHUNKS)
    return jnp.where(ok[None], _rel_bias(table, ATTN_SUB, ATTN_SUB + WINDOW, WINDOW), NEG_INF)


def _attn_prompt(aq, ak, av, table):
    B, S, _ = aq.shape
    tb = WINDOW
    assert S % tb == 0
    bias = _band_bias(table).reshape(A_WIDTH // LANES, 2 * ATTN_SUB, ATTN_SUB + WINDOW)
    cur = pl.BlockSpec((1, tb, A_WIDTH), lambda b, j: (b, j, 0))
    prev = pl.BlockSpec((1, tb, A_WIDTH), lambda b, j: (b, jnp.maximum(j - 1, 0), 0))
    return pl.pallas_call(
        functools.partial(_attn_prompt_kernel, tb=tb),
        grid=(B, S // tb),
        in_specs=[cur, prev, cur, prev, cur, pl.BlockSpec(bias.shape, lambda b, j: (0, 0, 0))],
        out_specs=cur,
        out_shape=jax.ShapeDtypeStruct((B, S, A_WIDTH), BF16),
        compiler_params=_params(("arbitrary", "arbitrary")),
        name="attn_prompt",
    )(aq, ak, ak, av, av, bias)


def _attn_sample_kernel(q_ref, k_ref, v_ref, bias_ref, o_ref):
    T = q_ref.shape[1]
    lo_mask = lax.broadcasted_iota(I32, (T, LANES), 1) < A_HEAD_DIM
    pairs = []
    for p in range(A_WIDTH // LANES):
        c0 = p * LANES
        parts = [(k_ref[0, :, c0:c0 + LANES], v_ref[0, :, c0:c0 + LANES], bias_ref[p], None)]
        pairs.append((q_ref[0, :, c0:c0 + LANES], parts))
    for p, o in enumerate(_attend(pairs, lo_mask)):
        o_ref[0, :, p * LANES:(p + 1) * LANES] = o.astype(BF16)


def _attn_sample(aq, ak, av, cache_k, cache_v, table):
    B, T, _ = aq.shape
    P = cache_k.shape[1]
    L = (P + T + LANES - 1) // LANES * LANES
    pad = jnp.zeros((B, L - P - T, A_WIDTH), BF16)
    kk = jnp.concatenate([cache_k.astype(BF16), ak, pad], axis=1)
    vv = jnp.concatenate([cache_v.astype(BF16), av, pad], axis=1)
    bias = jnp.where((jnp.arange(L) < P + T)[None, None, :], _rel_bias(table, T, L, P), NEG_INF)
    bias = bias.reshape(A_WIDTH // LANES, 2 * T, L)
    new = pl.BlockSpec((1, T, A_WIDTH), lambda b: (b, 0, 0))
    old = pl.BlockSpec((1, L, A_WIDTH), lambda b: (b, 0, 0))
    return pl.pallas_call(
        _attn_sample_kernel,
        grid=(B,),
        in_specs=[new, old, old, pl.BlockSpec(bias.shape, lambda b: (0, 0, 0))],
        out_specs=new,
        out_shape=jax.ShapeDtypeStruct((B, T, A_WIDTH), BF16),
        compiler_params=_params(("arbitrary",)),
        name="attn_sample",
    )(aq, kk, vv, bias)


def _gla_kernel(q_ref, k_ref, v_ref, la_ref, gr_ref, gain_ref, s0_ref, o_ref, sf_ref, s_scr, *, C, n_chunks,
                n_group):
    j = pl.program_id(1)

    @pl.when(j == 0)
    def _():
        s_scr[...] = s0_ref[0]

    n_sub = C // G_SUB
    ri = lax.broadcasted_iota(I32, (C, C), 0)
    ci = lax.broadcasted_iota(I32, (C, C), 1)
    tril = (ci <= ri).astype(BF16)
    lane_kw = lax.broadcasted_iota(I32, (1, G_KW), 1)
    head_of_lane = lane_kw // G_DK
    row_kw = lax.broadcasted_iota(I32, (C, G_KW), 0)
    ur = lax.broadcasted_iota(I32, (2 * C, 4 * C), 0) - C
    uc = lax.broadcasted_iota(I32, (2 * C, 4 * C), 1)
    u_mat = ((ur >= 0) & ((uc >= C) | (ur <= uc))).astype(BF16)

    def heads_on_rows(x):
        return jnp.concatenate([jnp.where(head_of_lane == h, x, 0.0) for h in range(G_HEADS)], axis=0)

    def chunk_group(g, S):
        ns = range(n_group)
        rs = [pl.multiple_of((g * n_group + n) * C, C) for n in ns]
        q = [q_ref[0, pl.ds(r, C), :].astype(F32) for r in rs]
        k = [k_ref[0, pl.ds(r, C), :].astype(F32) for r in rs]
        v = [v_ref[0, pl.ds(r, C), :] for r in rs]
        la = [la_ref[0, pl.ds(r, C), :] for r in rs]

        split = [_split_bf16(x) for x in la]
        b = [jnp.dot(tril, hi, preferred_element_type=F32) + jnp.dot(tril, lo, preferred_element_type=F32)
             for hi, lo in split]
        xt = [jnp.concatenate([k[n], la[n]], axis=0).T for n in ns]
        split_t = [_split_bf16(x) for x in xt]
        xb = [jnp.dot(hi, u_mat, preferred_element_type=F32) + jnp.dot(lo, u_mat, preferred_element_type=F32)
              for hi, lo in split_t]
        b_last = [x[:, LANES:] for x in xb]
        kd = [(xt[n] * jnp.exp(b_last[n] - xb[n][:, :LANES])).astype(BF16) for n in ns]
        zeros_v = jnp.zeros((C, G_VW), BF16)
        kv = [jnp.dot(kd[n], jnp.concatenate([v[n], zeros_v], axis=0), preferred_element_type=F32)
              for n in ns]
        kv_d = [jnp.concatenate([x[h * G_DK:(h + 1) * G_DK, h * G_DV:(h + 1) * G_DV] for h in range(G_HEADS)],
                                axis=0) for x in kv]

        states = [S]
        for n in ns:
            states.append(jnp.exp(b_last[n]) * states[n] + kv_d[n])

        r_inter = [jnp.dot(heads_on_rows(q[n] * jnp.exp(b[n])).astype(BF16), states[n].astype(BF16),
                           preferred_element_type=F32) for n in ns]
        o = [jnp.concatenate([x[h * C:(h + 1) * C] for h in range(G_HEADS)], axis=1) for x in r_inter]

        o_rows = [[] for _ in ns]
        for i in range(n_sub):
            r0, r1 = i * G_SUB, (i + 1) * G_SUB
            atts = []
            for n in ns:
                bs = b[n][r0 - 1:r0] if i > 0 else jnp.zeros((1, G_KW), F32)
                qe = q[n][r0:r1] * jnp.exp(b[n][r0:r1] - bs)
                ke = (k[n] * jnp.exp(jnp.where(row_kw < r1, bs - b[n], -jnp.inf))).astype(BF16)
                att = lax.dot_general(heads_on_rows(qe).astype(BF16), ke, (((1,), (1,)), ((), ())),
                                      preferred_element_type=F32)
                tt = lax.broadcasted_iota(I32, att.shape, 0) % G_SUB + r0
                ss = lax.broadcasted_iota(I32, att.shape, 1)
                atts.append(jnp.where(ss <= tt, att, 0.0).astype(BF16))
            for n in ns:
                ov = jnp.dot(atts[n], v[n], preferred_element_type=F32)
                o_rows[n].append(jnp.concatenate(
                    [ov[h * G_SUB:(h + 1) * G_SUB, h * G_DV:(h + 1) * G_DV] for h in range(G_HEADS)], axis=1))

        for n in ns:
            on = o[n] + jnp.concatenate(o_rows[n], axis=0)
            on = _head_rms(on, G_DV) * gain_ref[...] * gr_ref[0, pl.ds(rs[n], C), :].astype(F32)
            o_ref[0, pl.ds(rs[n], C), :] = on.astype(BF16)
        return states[-1]

    s_scr[...] = lax.fori_loop(0, n_chunks // n_group, chunk_group, s_scr[...])

    @pl.when(j == pl.num_programs(1) - 1)
    def _():
        sf_ref[0] = s_scr[...]


def _gla(gq, gk, gv, la, gr, gain, s0):
    B, T, _ = gq.shape
    C = CHUNK
    tb = min(ROW_TILE, T)
    assert T % tb == 0 and tb % C == 0 and C % G_SUB == 0 and 2 * C == LANES
    kw = pl.BlockSpec((1, tb, G_KW), lambda b, j: (b, j, 0))
    vw = pl.BlockSpec((1, tb, G_VW), lambda b, j: (b, j, 0))
    st = pl.BlockSpec((1, G_KW, G_DV), lambda b, j: (b, 0, 0))
    return pl.pallas_call(
        functools.partial(_gla_kernel, C=C, n_chunks=tb // C, n_group=8 if (tb // C) % 8 == 0 else 1),
        grid=(B, T // tb),
        in_specs=[kw, kw, vw, kw, vw, pl.BlockSpec(gain.shape, lambda b, j: (0, 0)), st],
        out_specs=[vw, st],
        out_shape=[jax.ShapeDtypeStruct((B, T, G_VW), BF16), jax.ShapeDtypeStruct((B, G_KW, G_DV), F32)],
        scratch_shapes=[pltpu.VMEM((G_KW, G_DV), F32)],
        compiler_params=_params(("arbitrary", "arbitrary")),
        name="gla",
    )(gq, gk, gv, la, gr, gain, s0)


def _memkv_kernel(mem_ref, gm_ref, w_ref, gk_ref, k_ref, v_ref):
    x = mem_ref[0]
    h = (x * lax.rsqrt(jnp.mean(x * x, -1, keepdims=True) + EPS) * gm_ref[...]).astype(BF16)
    k = jnp.dot(h, w_ref[:, :C_WIDTH], preferred_element_type=F32)
    k_ref[0] = _head_rms(k, C_HEAD_DIM) * gk_ref[...]
    v_ref[0] = jnp.dot(h, w_ref[:, C_WIDTH:], preferred_element_type=F32)


def _memkv(mem, gm, w_kv, gk):
    B, M, _ = mem.shape
    out = pl.BlockSpec((1, M, C_WIDTH), lambda b: (b, 0, 0))
    return pl.pallas_call(
        _memkv_kernel,
        grid=(B,),
        in_specs=[pl.BlockSpec((1, M, D_MODEL), lambda b: (b, 0, 0)),
                  pl.BlockSpec(gm.shape, lambda b: (0, 0)),
                  pl.BlockSpec(w_kv.shape, lambda b: (0, 0)),
                  pl.BlockSpec(gk.shape, lambda b: (0, 0))],
        out_specs=[out, out],
        out_shape=[jax.ShapeDtypeStruct((B, M, C_WIDTH), F32)] * 2,
        compiler_params=_params(("arbitrary",)),
        name="memkv",
    )(mem, gm, w_kv, gk)


def _merge_kernel(x_ref, ya_ref, yb_ref, cq_ref, gt_ref, mk_ref, mv_ref, wa_ref, wb_ref, wc_ref,
                  wo_ref, nf_ref, wr_ref, br_ref,
                  x1_ref, hp_ref, ri_ref, rw_ref, cnt_ref, run_scr):
    first = jnp.logical_and(pl.program_id(0) == 0, pl.program_id(1) == 0)

    @pl.when(first)
    def _():
        run_scr[...] = jnp.zeros_like(run_scr)

    tm = x_ref.shape[1]
    heads = [slice(h * C_HEAD_DIM, (h + 1) * C_HEAD_DIM) for h in range(C_HEADS)]
    scores = [lax.dot_general(cq_ref[0, :, c], mk_ref[0, :, c].astype(BF16), (((1,), (1,)), ((), ())),
                              preferred_element_type=F32) * (C_HEAD_DIM ** -0.5) for c in heads]
    probs = [jnp.exp(s - s.max(-1, keepdims=True)) for s in scores]
    sums = [p.sum(-1, keepdims=True) for p in probs]
    cols = [jnp.dot(p.astype(BF16), mv_ref[0, :, c].astype(BF16), preferred_element_type=F32) / l
            for p, l, c in zip(probs, sums, heads)]
    yc_in = jnp.concatenate(cols, axis=-1).astype(BF16)

    y_a = jnp.dot(ya_ref[0], wa_ref[...], preferred_element_type=F32)
    y_b = jnp.dot(yb_ref[0], wb_ref[...], preferred_element_type=F32)
    y_c = jnp.dot(yc_in, wc_ref[...], preferred_element_type=F32)
    merged = (gt_ref[0, :, 0:D_MODEL].astype(F32) * y_a
              + gt_ref[0, :, D_MODEL:2 * D_MODEL].astype(F32) * y_b
              + gt_ref[0, :, 2 * D_MODEL:3 * D_MODEL].astype(F32) * y_c)
    x1 = x_ref[0] + jnp.dot(merged.astype(BF16), wo_ref[...], preferred_element_type=F32)
    x1_ref[0] = x1

    h2 = x1 * lax.rsqrt(jnp.mean(x1 * x1, -1, keepdims=True) + EPS) * nf_ref[...]
    _store_slabs(hp_ref, (0,), _pack_rows(h2))

    logits = jnp.dot(h2.astype(BF16), wr_ref[...], preferred_element_type=F32) + br_ref[...]
    lane = lax.broadcasted_iota(I32, (tm, LANES), 1)
    lane_f = lane.astype(F32)
    vals, sels, idxs = [], [], []
    l = logits
    for _ in range(TOP_K):
        m = l.max(-1, keepdims=True)
        idx = jnp.min(jnp.where(l == m, lane_f, float(LANES)), -1, keepdims=True)
        sel = lane_f == idx
        vals.append(m)
        idxs.append(idx)
        sels.append(sel)
        l = jnp.where(sel, -3e38, l)
    es = [jnp.exp(vk - vals[0]) for vk in vals]
    den = es[0] + es[1] + es[2] + es[3]
    cnt = jnp.zeros((tm, LANES), F32)
    for sel in sels:
        cnt = cnt + jnp.where(sel, 1.0, 0.0)
    tp = max(tm, LANES)
    cnt_p = cnt if tp == tm else jnp.concatenate([cnt, jnp.zeros((tp - tm, LANES), F32)], axis=0)
    ri = lax.broadcasted_iota(I32, (tp, tp), 0)
    ci = lax.broadcasted_iota(I32, (tp, tp), 1)
    before = jnp.dot(jnp.where(ci < ri, 1.0, 0.0).astype(BF16), cnt_p.astype(BF16),
                     preferred_element_type=F32)[0:tm] + run_scr[0:1, :]
    r_i = jnp.zeros((tm, LANES), I32)
    r_w = jnp.zeros((tm, LANES), F32)
    for kk in range(TOP_K):
        rank = jnp.sum(jnp.where(sels[kk], before, 0.0), -1, keepdims=True)
        r_i = jnp.where(lane == kk, idxs[kk].astype(I32), r_i)
        r_i = jnp.where(lane == TOP_K + kk, rank.astype(I32), r_i)
        r_w = jnp.where(lane == kk, es[kk] / den, r_w)
    ri_ref[0] = r_i
    rw_ref[0] = r_w
    run_scr[...] = run_scr[...] + jnp.sum(cnt, axis=0, keepdims=True)
    cnt_ref[...] = run_scr[...]


def _merge(x, ya, yb, cq, gt, mk, mv, wa, wb, wc, wo, nf, wr, br):
    B, T, _ = x.shape
    tm = min(ROW_TILE, T)
    assert T % tm == 0

    def row(width):
        return pl.BlockSpec((1, tm, width), lambda b, j: (b, j, 0))

    def full(a):
        return pl.BlockSpec(a.shape, lambda b, j: (0,) * a.ndim)

    mem = pl.BlockSpec((1, N_MEM, C_WIDTH), lambda b, j: (b, 0, 0))
    return pl.pallas_call(
        _merge_kernel,
        grid=(B, T // tm),
        in_specs=[row(D_MODEL), row(A_WIDTH), row(G_VW), row(C_WIDTH), row(N_BRANCH * D_MODEL), mem, mem,
                  full(wa), full(wb), full(wc), full(wo), full(nf), full(wr), full(br)],
        out_specs=[row(D_MODEL), pl.BlockSpec((1, tm, SLABS, LANES), lambda b, j: (b, j, 0, 0)), row(LANES), row(LANES),
                   pl.BlockSpec((8, LANES), lambda b, j: (0, 0))],
        out_shape=[jax.ShapeDtypeStruct((B, T, D_MODEL), F32), jax.ShapeDtypeStruct((B, T, SLABS, LANES), U32),
                   jax.ShapeDtypeStruct((B, T, LANES), I32), jax.ShapeDtypeStruct((B, T, LANES), F32),
                   jax.ShapeDtypeStruct((8, LANES), F32)],
        scratch_shapes=[pltpu.VMEM((8, LANES), F32)],
        compiler_params=_params(("arbitrary", "arbitrary")),
        name="merge",
    )(x, ya, yb, cq, gt, mk, mv, wa, wb, wc, wo, nf, wr, br)


def _dispatch_kernel(dest_ref, dest2_ref, pad_lo_ref, pad_n_ref, hp_ref, hp2_ref, xs_ref, zero_scr, sem, sem2,
                     psem, *, tm, tm2, n_pad):
    def scatter(dref, href, s, n):
        def issue(t, c):
            for kk in range(TOP_K):
                pltpu.make_async_copy(href.at[t], xs_ref.at[dref[0, 0, t * TOP_K + kk]], s).start(priority=kk % 2)
            return c
        lax.fori_loop(0, n, issue, 0, unroll=8)

    def drain(href, s, n):
        for _ in range(TOP_K):
            pltpu.make_async_copy(href, xs_ref.at[pl.ds(0, n)], s).wait()

    scatter(dest_ref, hp_ref, sem, tm)

    @pl.when(pl.program_id(0) == 0)
    def _():
        scatter(dest2_ref, hp2_ref, sem2, tm2)
        zero_scr[...] = jnp.zeros_like(zero_scr)

        def pad_copy(e, n):
            return pltpu.make_async_copy(zero_scr, xs_ref.at[pad_lo_ref[e] + n], psem)

        def fill(e, c):
            lax.fori_loop(0, pad_n_ref[e], lambda n, cc: (pad_copy(e, n).start(), cc)[1], 0)
            return c

        def fill_wait(e, c):
            lax.fori_loop(0, pad_n_ref[e], lambda n, cc: (pad_copy(e, n).wait(), cc)[1], 0)
            return c

        lax.fori_loop(0, n_pad, fill, 0)
        lax.fori_loop(0, n_pad, fill_wait, 0)
        drain(hp2_ref, sem2, tm2)

    drain(hp_ref, sem, tm)


def _dispatch(dest, hp, dest2, hp2, pad_lo, pad_n, rows):
    N, N2 = hp.shape[0], hp2.shape[0]
    tm = min(ROW_TILE, N)
    assert N % tm == 0
    n_steps = N // tm
    smem = functools.partial(pl.BlockSpec, memory_space=pltpu.SMEM)
    n_pad = pad_lo.shape[0]
    return pl.pallas_call(
        functools.partial(_dispatch_kernel, tm=tm, tm2=N2, n_pad=n_pad),
        grid=(n_steps,),
        in_specs=[smem((1, 1, tm * TOP_K), lambda i: (i, 0, 0)),
                  smem((1, 1, N2 * TOP_K), lambda i: (0, 0, 0)),
                  smem((n_pad,), lambda i: (0,)), smem((n_pad,), lambda i: (0,)),
                  pl.BlockSpec((tm, SLABS, LANES), lambda i: (i, 0, 0)),
                  pl.BlockSpec((N2, SLABS, LANES), lambda i: (0, 0, 0))],
        out_specs=pl.BlockSpec(memory_space=pl.ANY),
        out_shape=jax.ShapeDtypeStruct((rows, SLABS, LANES), U32),
        scratch_shapes=[pltpu.VMEM((SLABS, LANES), U32), pltpu.SemaphoreType.DMA(()),
                        pltpu.SemaphoreType.DMA(()), pltpu.SemaphoreType.DMA(())],
        compiler_params=_params(("arbitrary",)),
        name="moe_dispatch",
    )(dest.reshape(n_steps, 1, tm * TOP_K), dest2.reshape(1, 1, N2 * TOP_K), pad_lo, pad_n, hp, hp2)


def _ffn_kernel(be_ref, nu_ref, x_ref, wgu_ref, bgu_ref, wd_ref, bd_ref, y_ref, wgu_bf, wd_bf):
    i = pl.program_id(0)
    prev = be_ref[jnp.maximum(i - 1, 0)]
    new_expert = jnp.logical_or(i == 0, be_ref[i] != prev)

    @pl.when(new_expert)
    def _():
        wgu_bf[...] = wgu_ref[0].astype(BF16)
        wd_bf[...] = wd_ref[0].astype(BF16)

    @pl.when(i < nu_ref[0])
    def _():
        x_lo, x_hi = _unpack_rows(_load_slabs(x_ref, ()))
        gu = (jnp.dot(x_lo.astype(BF16), wgu_bf[0:HALF, :], preferred_element_type=F32)
              + jnp.dot(x_hi.astype(BF16), wgu_bf[HALF:, :], preferred_element_type=F32)
              + bgu_ref[0])
        gate = jnp.minimum(gu[:, :D_EXPERT], SWIGLU_LIMIT)
        up = jnp.clip(gu[:, D_EXPERT:], -SWIGLU_LIMIT, SWIGLU_LIMIT)
        act = (up + 1.0) * (gate * _sigmoid(SWIGLU_ALPHA * gate))
        y = jnp.dot(act.astype(BF16), wd_bf[...], preferred_element_type=F32) + bd_ref[0]
        _store_slabs(y_ref, (), _pack_rows(y))

    @pl.when(i >= nu_ref[0])
    def _():
        y_ref[...] = jnp.zeros_like(y_ref)


def _ffn(blk_expert, n_used, xs, w_gu, b_gu, w_d, b_d):
    P = xs.shape[0]
    nblk = P // FFN_BLOCK
    rows = (FFN_BLOCK, SLABS, LANES)
    grid_spec = pltpu.PrefetchScalarGridSpec(
        num_scalar_prefetch=2,
        grid=(nblk,),
        in_specs=[pl.BlockSpec(rows, lambda i, be, nu: (jnp.minimum(i, nu[0] - 1), 0, 0)),
                  pl.BlockSpec((1, D_MODEL, 2 * D_EXPERT), lambda i, be, nu: (be[i], 0, 0)),
                  pl.BlockSpec((1, 1, 2 * D_EXPERT), lambda i, be, nu: (be[i], 0, 0)),
                  pl.BlockSpec((1, D_EXPERT, D_MODEL), lambda i, be, nu: (be[i], 0, 0)),
                  pl.BlockSpec((1, 1, D_MODEL), lambda i, be, nu: (be[i], 0, 0))],
        out_specs=pl.BlockSpec(rows, lambda i, be, nu: (i, 0, 0)),
        scratch_shapes=[pltpu.VMEM((D_MODEL, 2 * D_EXPERT), BF16), pltpu.VMEM((D_EXPERT, D_MODEL), BF16)],
    )
    return pl.pallas_call(
        _ffn_kernel,
        grid_spec=grid_spec,
        out_shape=jax.ShapeDtypeStruct((P, SLABS, LANES), U32),
        compiler_params=_params(("arbitrary",)),
        name="moe_ffn",
    )(blk_expert, n_used, xs, w_gu, b_gu.reshape(N_EXPERTS, 1, -1), w_d, b_d.reshape(N_EXPERTS, 1, -1))


def _combine_kernel(dcur_ref, dnext_ref, x1_ref, rw_ref, ys_ref, o_ref, buf, sem, *, tm, n_steps, group):
    i = pl.program_id(0)
    tile = 8

    def base(s, kk):
        return (s * TOP_K + kk) * (tm * tile)

    def start_row(dref, s, t, kk):
        row = base(s, kk) + t * tile
        row = row if isinstance(row, int) else pl.multiple_of(row, tile)
        pltpu.make_async_copy(ys_ref.at[dref[0, 0, t * TOP_K + kk]], buf.at[pl.ds(row, SLABS)],
                              sem.at[s]).start(priority=kk % 2)

    def wait_tile(s):
        for _ in range(TOP_K):
            pltpu.make_async_copy(buf.at[pl.ds(0, tm * SLABS)], buf.at[pl.ds(tm * SLABS, tm * SLABS)],
                                  sem.at[s]).wait()

    @pl.when(i == 0)
    def _():
        def body(t, c):
            for kk in range(TOP_K):
                start_row(dcur_ref, 0, t, kk)
            return c
        lax.fori_loop(0, tm, body, 0)

    def step(slot):
        nxt = 1 - slot
        wait_tile(slot)
        for g in range(tm // group):
            rows = slice(g * group, (g + 1) * group)
            for t in range(g * group, (g + 1) * group):
                for kk in range(TOP_K):
                    start_row(dnext_ref, nxt, t, kk)
            w = rw_ref[rows, :]
            acc_lo = x1_ref[rows, :HALF]
            acc_hi = x1_ref[rows, HALF:]
            for kk in range(TOP_K):
                start = base(slot, kk) + g * group * tile
                u = jnp.concatenate([buf[pl.ds(start + j, group, stride=tile), :] for j in range(SLABS)],
                                    axis=-1)
                lo, hi = _unpack_rows(u)
                wk = w[:, kk:kk + 1]
                acc_lo = acc_lo + wk * lo
                acc_hi = acc_hi + wk * hi
            o_ref[rows, :HALF] = acc_lo
            o_ref[rows, HALF:] = acc_hi

        @pl.when(i == n_steps - 1)
        def _():
            wait_tile(nxt)

    for slot in range(2):
        pl.when(i % 2 == slot)(functools.partial(step, slot))


def _combine(dest, x1, rw, ys):
    N = x1.shape[0]
    tm = min(256, N)
    assert N % tm == 0
    n_steps = N // tm
    dest3 = dest.reshape(n_steps, 1, tm * TOP_K)
    smem = functools.partial(pl.BlockSpec, memory_space=pltpu.SMEM)
    return pl.pallas_call(
        functools.partial(_combine_kernel, tm=tm, n_steps=n_steps, group=min(32, tm)),
        grid=(n_steps,),
        in_specs=[smem((1, 1, tm * TOP_K), lambda i: (i, 0, 0)),
                  smem((1, 1, tm * TOP_K), lambda i: (jnp.minimum(i + 1, n_steps - 1), 0, 0)),
                  pl.BlockSpec((tm, D_MODEL), lambda i: (i, 0)),
                  pl.BlockSpec((tm, LANES), lambda i: (i, 0)),
                  pl.BlockSpec(memory_space=pl.ANY)],
        out_specs=pl.BlockSpec((tm, D_MODEL), lambda i: (i, 0)),
        out_shape=jax.ShapeDtypeStruct((N, D_MODEL), F32),
        scratch_shapes=[pltpu.VMEM((2 * TOP_K * tm * 8, LANES), U32), pltpu.SemaphoreType.DMA((2,))],
        compiler_params=_params(("arbitrary",)),
        name="moe_combine",
    )(dest3, dest3, x1, rw, ys)


def _moe(groups, w_gu, b_gu, w_d, b_d):
    n_assign = sum(g[0].shape[0] for g in groups) * TOP_K
    nblk = (n_assign + N_EXPERTS * (FFN_BLOCK - 1) + FFN_BLOCK - 1) // FFN_BLOCK
    counts = [g[4].astype(I32) for g in groups]
    total = sum(counts)
    padded = (total + FFN_BLOCK - 1) // FFN_BLOCK * FFN_BLOCK
    pend = jnp.cumsum(padded)
    pstart = pend - padded
    n_used = pend[-1:] // FFN_BLOCK
    blk = jnp.minimum(jnp.arange(nblk, dtype=I32), n_used[0] - 1) * FFN_BLOCK
    blk_expert = jnp.minimum(jnp.sum(pend[None, :] <= blk[:, None], axis=1), N_EXPERTS - 1).astype(I32)

    experts = jnp.arange(N_EXPERTS, dtype=I32)
    dests = []
    base = pstart
    for g, c in zip(groups, counts):
        idx, rank = g[2][:, :TOP_K], g[2][:, TOP_K:2 * TOP_K]
        dests.append(jnp.sum(jnp.where(idx[..., None] == experts, base, 0), axis=-1) + rank)
        base = base + c

    rows = nblk * FFN_BLOCK
    pad_lo = jnp.concatenate([pstart + total, pend[-1:]])
    pad_n = jnp.concatenate([padded - total, rows - pend[-1:]])
    (g_main, g_small), (d_main, d_small) = groups, dests
    xs = _dispatch(d_main, g_main[1], d_small, g_small[1], pad_lo, pad_n, rows)
    ys = _ffn(blk_expert, n_used.astype(I32), xs, w_gu, b_gu, w_d, b_d)
    return [_combine(d, g[0], g[3], ys) for g, d in zip(groups, dests)]


def _tile_lanes(g, reps):
    return jnp.tile(g.astype(F32), reps)[None, :]


def kernel(x_prompt, x_sample, mem_prompt, cache_attn_k, cache_attn_v, state_gla, cache_mem_k, cache_mem_v, norm_mix, w_in, a_q_norm, a_k_norm, rel_bias_table, w_a_o, w_gla_a_up, b_gla_a, gla_out_norm, w_b_o, c_q_norm, c_k_norm, norm_mem, w_mem_kv, w_c_o, b_gate, w_out, norm_ffn, w_router, b_router, w_gate_up, b_gate_up, w_down, b_down):
    depth = norm_mix.shape[0]
    assert depth == 1
    l = 0
    B, S, _ = x_prompt.shape
    Bs, Ts, _ = x_sample.shape
    keep = min(WINDOW, S)

    w = w_in[l]
    sizes = (A_WIDTH, A_WIDTH, A_WIDTH, G_KW, G_KW, G_VW, G_VW, G_RANK, C_WIDTH, N_BRANCH * D_MODEL)
    offs = [0]
    for s_ in sizes:
        offs.append(offs[-1] + s_)
    seg = [w[:, offs[i]:offs[i + 1]] for i in range(len(sizes))]
    w_r = jnp.concatenate(seg[0:7] + [seg[8], seg[9], seg[7], jnp.zeros((D_MODEL, LANES - G_RANK), F32)],
                          axis=1).astype(BF16)
    nm = norm_mix[l][None, :]
    aqn = _tile_lanes(a_q_norm[l], A_HEADS)
    akn = _tile_lanes(a_k_norm[l], A_HEADS)
    cqn = _tile_lanes(c_q_norm[l], C_HEADS)
    ckn = _tile_lanes(c_k_norm[l], C_HEADS)
    gon = _tile_lanes(gla_out_norm[l], G_HEADS)
    wup = jnp.concatenate([w_gla_a_up[l], jnp.zeros((LANES - G_RANK, G_KW), F32)], axis=0).astype(BF16)
    bla = b_gla_a[l][None, :]
    bg = b_gate[l][None, :]
    wa, wb, wc, wo = (t[l].astype(BF16) for t in (w_a_o, w_b_o, w_c_o, w_out))
    nf = norm_ffn[l][None, :]
    wr = jnp.concatenate([w_router[l], jnp.zeros((D_MODEL, LANES - N_EXPERTS), F32)], axis=1).astype(BF16)
    br = jnp.concatenate([b_router[l], jnp.full((LANES - N_EXPERTS,), NEG_INF, F32)])[None, :]
    table = rel_bias_table[l]

    mk, mv = _memkv(mem_prompt, norm_mem[l][None, :], w_mem_kv[l].astype(BF16), ckn)
    (aq, ak, av, gq, gk, gv, gr, la, cq, gt, ak_tail, av_tail) = _inproj(
        x_prompt, keep, nm, w_r, aqn, akn, cqn, wup, bla, bg)
    ya = _attn_prompt(aq, ak, av, table)
    yb, s_prompt = _gla(gq, gk, gv, la, gr, gon, jnp.zeros((B, G_KW, G_DV), F32))
    x1_p, hp_p, ri_p, rw_p, cnt_p = _merge(x_prompt, ya, yb, cq, gt, mk, mv, wa, wb, wc, wo, nf, wr, br)

    (aq, ak, av, gq, gk, gv, gr, la, cq, gt, ak_new, av_new) = _inproj(
        x_sample.reshape(1, Bs * Ts, D_MODEL), Bs * Ts, nm, w_r, aqn, akn, cqn, wup, bla, bg)
    rs = lambda t: t.reshape(Bs, Ts, t.shape[-1])
    P = cache_attn_k.shape[2]
    ya = _attn_sample(rs(aq), rs(ak), rs(av), cache_attn_k[l].reshape(Bs, P, A_WIDTH),
                      cache_attn_v[l].reshape(Bs, P, A_WIDTH), table)
    t_pad = (Ts + CHUNK - 1) // CHUNK * CHUNK
    zp = lambda t: jnp.pad(rs(t), ((0, 0), (0, t_pad - Ts), (0, 0)))
    yb, s_sample = _gla(zp(gq), zp(gk), zp(gv), zp(la), zp(gr), gon, state_gla[l].reshape(Bs, G_KW, G_DV))
    yb = yb[:, :Ts]
    x1_s, hp_s, ri_s, rw_s, cnt_s = _merge(
        x_sample, ya, yb, rs(cq), rs(gt), cache_mem_k[l].reshape(Bs, N_MEM, C_WIDTH),
        cache_mem_v[l].reshape(Bs, N_MEM, C_WIDTH), wa, wb, wc, wo, nf, wr, br)

    flat = lambda t: t.reshape((-1,) + t.shape[2:])
    y_p, y_s = _moe(
        [(flat(x1_p), flat(hp_p), flat(ri_p), flat(rw_p), cnt_p[0, :N_EXPERTS]),
         (flat(x1_s), flat(hp_s), flat(ri_s), flat(rw_s), cnt_s[0, :N_EXPERTS])],
        w_gate_up[l], b_gate_up[l], w_down[l], b_down[l])

    return (y_p.reshape(B, S, D_MODEL), y_s.reshape(Bs, Ts, D_MODEL),
            ak_tail.reshape(1, B, keep, A_HEADS, A_HEAD_DIM), av_tail.reshape(1, B, keep, A_HEADS, A_HEAD_DIM),
            s_prompt.reshape(1, B, G_HEADS, G_DK, G_DV),
            mk.reshape(1, B, N_MEM, C_HEADS, C_HEAD_DIM), mv.reshape(1, B, N_MEM, C_HEADS, C_HEAD_DIM),
            ak_new.reshape(1, Bs, Ts, A_HEADS, A_HEAD_DIM), av_new.reshape(1, Bs, Ts, A_HEADS, A_HEAD_DIM),
            s_sample.reshape(1, Bs, G_HEADS, G_DK, G_DV))
```

```python
import functools

import jax
import jax.numpy as jnp
from jax import lax
from jax.experimental import pallas as pl
from jax.experimental.pallas import tpu as pltpu

F32 = jnp.float32
BF16 = jnp.bfloat16
U32 = jnp.uint32
I32 = jnp.int32

D_MODEL = 1024
CHUNK = 64
BAND_CHUNKS = 8
WINDOW = BAND_CHUNKS * CHUNK
N_MEM = 256
A_HEADS, A_HEAD_DIM = 8, 64
A_WIDTH = A_HEADS * A_HEAD_DIM
REL_MAX = 128
G_HEADS, G_DK, G_DV = 4, 64, 128
G_KW, G_VW = G_HEADS * G_DK, G_HEADS * G_DV
G_RANK = 16
G_TAU = 16.0
G_SUB = 16
C_HEADS, C_HEAD_DIM = 4, 128
C_WIDTH = C_HEADS * C_HEAD_DIM
N_BRANCH = 3
N_EXPERTS = 32
TOP_K = 4
D_EXPERT = 1024
SWIGLU_LIMIT = 7.0
SWIGLU_ALPHA = 1.702
EPS = 1e-6
NEG_INF = -1e30

LANES = 128
HALF = D_MODEL // 2
ROW_TILE = 512
ATTN_SUB = 128
FFN_BLOCK = 256
VMEM_LIMIT = 56 * 1024 * 1024

OFF_AQ, OFF_AK, OFF_AV = 0, 512, 1024
OFF_GQ, OFF_GK, OFF_GV, OFF_GR = 1536, 1792, 2048, 2560
OFF_CQ, OFF_GATE, OFF_LR = 3072, 3584, 6656
IN_COLS = OFF_LR + LANES


def _params(sem):
    return pltpu.CompilerParams(dimension_semantics=sem, vmem_limit_bytes=VMEM_LIMIT)


def _sigmoid(x):
    return 0.5 * jnp.tanh(0.5 * x) + 0.5


def _head_rms(y, head_dim):
    cols = []
    for p in range(y.shape[1] // LANES):
        blk = y[:, p * LANES:(p + 1) * LANES]
        sq = blk * blk
        if head_dim == LANES:
            sc = lax.rsqrt(jnp.sum(sq, -1, keepdims=True) * (1.0 / LANES) + EPS)
        else:
            lo = lax.broadcasted_iota(I32, blk.shape, 1) < head_dim
            s_lo = jnp.sum(jnp.where(lo, sq, 0.0), -1, keepdims=True)
            s_hi = jnp.sum(jnp.where(lo, 0.0, sq), -1, keepdims=True)
            sc = jnp.where(lo, lax.rsqrt(s_lo * (1.0 / head_dim) + EPS),
                           lax.rsqrt(s_hi * (1.0 / head_dim) + EPS))
        cols.append(blk * sc)
    return jnp.concatenate(cols, axis=-1)


def _split_bf16(x):
    hi = x.astype(BF16)
    lo = (x - hi.astype(F32)).astype(BF16)
    return hi, lo


def _pack_rows(x):
    lo = lax.bitcast_convert_type(x[:, :HALF].astype(BF16).astype(F32), U32)
    hi = lax.bitcast_convert_type(x[:, HALF:].astype(BF16).astype(F32), U32)
    return (lo >> 16) | (hi & jnp.uint32(0xFFFF0000))


def _unpack_rows(u):
    lo = lax.bitcast_convert_type(u << 16, F32)
    hi = lax.bitcast_convert_type(u & jnp.uint32(0xFFFF0000), F32)
    return lo, hi


SLABS = HALF // LANES
ROW_SLABS = 8
META = SLABS


def _store_slabs(ref, lead, u):
    slab_major = jnp.stack([u[:, j * LANES:(j + 1) * LANES] for j in range(SLABS)], axis=0)
    ref[lead + (slice(None),) * 3] = pltpu.einshape("jtl->tjl", slab_major)


def _load_slabs(ref, lead):
    slab_major = pltpu.einshape("tjl->jtl", ref[lead + (slice(None),) * 3])
    return jnp.concatenate([slab_major[j] for j in range(SLABS)], axis=-1)


def _inproj_kernel(x_ref, nm_ref, w_ref, aqn_ref, akn_ref, cqn_ref, wup_ref, bla_ref, bg_ref,
                   aq_ref, ak_ref, av_ref, gq_ref, gk_ref, gv_ref, gr_ref, la_ref, cq_ref, gt_ref,
                   akt_ref, avt_ref, *, n_tiles, n_tail):
    j = pl.program_id(1)
    x = x_ref[0]
    h = (x * lax.rsqrt(jnp.mean(x * x, -1, keepdims=True) + EPS) * nm_ref[...]).astype(BF16)

    def seg(off, width):
        return jnp.dot(h, w_ref[:, off:off + width], preferred_element_type=F32)

    in_tail = j >= n_tiles - n_tail

    aq = _head_rms(seg(OFF_AQ, A_WIDTH), A_HEAD_DIM) * aqn_ref[...] * (A_HEAD_DIM ** -0.5)
    aq_ref[0] = aq.astype(BF16)

    ak = _head_rms(seg(OFF_AK, A_WIDTH), A_HEAD_DIM) * akn_ref[...]
    ak_ref[0] = ak.astype(BF16)

    @pl.when(in_tail)
    def _():
        akt_ref[0] = ak

    av = seg(OFF_AV, A_WIDTH)
    av_ref[0] = av.astype(BF16)

    @pl.when(in_tail)
    def _():
        avt_ref[0] = av

    gq_ref[0] = (seg(OFF_GQ, G_KW) * (G_DK ** -0.5)).astype(BF16)
    gk_ref[0] = seg(OFF_GK, G_KW).astype(BF16)
    gv_ref[0] = seg(OFF_GV, G_VW).astype(BF16)
    gr = seg(OFF_GR, G_VW)
    gr_ref[0] = (gr * _sigmoid(gr)).astype(BF16)

    lr = seg(OFF_LR, LANES).astype(BF16)
    z = jnp.dot(lr, wup_ref[...], preferred_element_type=F32) + bla_ref[...]
    la_ref[0] = (jnp.minimum(z, 0.0) - jnp.log1p(jnp.exp(-jnp.abs(z)))) * (1.0 / G_TAU)

    cq = _head_rms(seg(OFF_CQ, C_WIDTH), C_HEAD_DIM) * cqn_ref[...]
    cq_ref[0] = cq.astype(BF16)

    gate_chunk = 512
    for c in range(N_BRANCH * D_MODEL // gate_chunk):
        lo = c * gate_chunk
        g = seg(OFF_GATE + lo, gate_chunk) + bg_ref[:, lo:lo + gate_chunk]
        gt_ref[0, :, lo:lo + gate_chunk] = _sigmoid(g).astype(BF16)


def _inproj(x, keep, nm, w_r, aqn, akn, cqn, wup, bla, bg):
    G, R, _ = x.shape
    tm = min(ROW_TILE, R)
    n_tiles = R // tm
    n_tail = keep // tm
    assert R % tm == 0 and keep % tm == 0 and n_tail >= 1

    def row(width, dtype):
        return (jax.ShapeDtypeStruct((G, R, width), dtype),
                pl.BlockSpec((1, tm, width), lambda g, j: (g, j, 0)))

    def tail(width):
        return (jax.ShapeDtypeStruct((G, keep, width), F32),
                pl.BlockSpec((1, tm, width), lambda g, j: (g, jnp.maximum(j - (n_tiles - n_tail), 0), 0)))

    outs = [row(A_WIDTH, BF16), row(A_WIDTH, BF16), row(A_WIDTH, BF16), row(G_KW, BF16),
            row(G_KW, BF16), row(G_VW, BF16), row(G_VW, BF16), row(G_KW, F32), row(C_WIDTH, BF16),
            row(N_BRANCH * D_MODEL, BF16), tail(A_WIDTH), tail(A_WIDTH)]

    def full(a):
        return pl.BlockSpec(a.shape, lambda g, j: (0,) * a.ndim)

    return pl.pallas_call(
        functools.partial(_inproj_kernel, n_tiles=n_tiles, n_tail=n_tail),
        grid=(G, n_tiles),
        in_specs=[pl.BlockSpec((1, tm, D_MODEL), lambda g, j: (g, j, 0)), full(nm), full(w_r),
                  full(aqn), full(akn), full(cqn), full(wup), full(bla), full(bg)],
        out_specs=[o[1] for o in outs],
        out_shape=[o[0] for o in outs],
        compiler_params=_params(("arbitrary", "arbitrary")),
        name="inproj",
    )(x, nm, w_r, aqn, akn, cqn, wup, bla, bg)


def _attend(pairs, lo_mask):
    T = pairs[0][0].shape[0]
    scores = []
    for q, parts in pairs:
        zero = jnp.zeros_like(q)
        q2 = jnp.concatenate([jnp.where(lo_mask, q, zero), jnp.where(lo_mask, zero, q)], axis=0)
        ss = []
        for (k, _, bias2, valid) in parts:
            s = lax.dot_general(q2, k, (((1,), (1,)), ((), ())), preferred_element_type=F32) + bias2
            if valid is not None:
                s = jnp.where(valid, s, NEG_INF)
            ss.append(s)
        scores.append(ss)
    probs, sums = [], []
    for ss in scores:
        m = ss[0].max(-1, keepdims=True)
        for s in ss[1:]:
            m = jnp.maximum(m, s.max(-1, keepdims=True))
        ps = [jnp.exp(s - m) for s in ss]
        l = ps[0].sum(-1, keepdims=True)
        for p in ps[1:]:
            l = l + p.sum(-1, keepdims=True)
        probs.append([p.astype(BF16) for p in ps])
        sums.append(l)
    outs = []
    for (q, parts), ps, l in zip(pairs, probs, sums):
        o = jnp.dot(ps[0], parts[0][1], preferred_element_type=F32)
        for p, part in zip(ps[1:], parts[1:]):
            o = o + jnp.dot(p, part[1], preferred_element_type=F32)
        o = o / l
        outs.append(jnp.where(lo_mask, o[:T], o[T:]))
    return outs


def _attn_prompt_kernel(q_ref, kp_ref, kc_ref, vp_ref, vc_ref, bias_ref, o_ref, *, tb):
    j = pl.program_id(1)
    has_prev = j > 0
    lo_mask = lax.broadcasted_iota(I32, (ATTN_SUB, LANES), 1) < A_HEAD_DIM
    for s in range(tb // ATTN_SUB):
        r0 = s * ATTN_SUB
        len_a = tb - r0
        len_b = r0 + ATTN_SUB
        pairs = []
        for p in range(A_WIDTH // LANES):
            c0 = p * LANES
            q = q_ref[0, r0:r0 + ATTN_SUB, c0:c0 + LANES]
            parts = [
                (kp_ref[0, r0:tb, c0:c0 + LANES], vp_ref[0, r0:tb, c0:c0 + LANES],
                 bias_ref[p, :, 0:len_a], has_prev),
                (kc_ref[0, 0:len_b, c0:c0 + LANES], vc_ref[0, 0:len_b, c0:c0 + LANES],
                 bias_ref[p, :, len_a:len_a + len_b], None),
            ]
            pairs.append((q, parts))
        for p, o in enumerate(_attend(pairs, lo_mask)):
            o_ref[0, r0:r0 + ATTN_SUB, p * LANES:(p + 1) * LANES] = o.astype(BF16)


def _rel_bias(table, n_q, n_k, offset):
    period = n_q + n_k - 1
    m = jnp.arange(period)
    u = table[:, jnp.clip(n_q - 1 + offset - m, -REL_MAX, REL_MAX) + REL_MAX].astype(F32)
    rows = jnp.tile(u, (1, n_q + 1))[:, :n_q * (period + 1)].reshape(-1, n_q, period + 1)[:, :, :n_k]
    return rows[:, ::-1, :]


def _band_bias(table):
    qc = jnp.arange(ATTN_SUB)[:, None] // CHUNK
    kc = jnp.arange(ATTN_SUB + WINDOW)[None, :] // CHUNK
    ok = (kc >= qc) & (kc <= qc + BAND_CHUNKS)
    return jnp.where(ok[None], _rel_bias(table, ATTN_SUB, ATTN_SUB + WINDOW, WINDOW), NEG_INF)


def _attn_prompt(aq, ak, av, table):
    B, S, _ = aq.shape
    tb = WINDOW
    assert S % tb == 0
    bias = _band_bias(table).reshape(A_WIDTH // LANES, 2 * ATTN_SUB, ATTN_SUB + WINDOW)
    cur = pl.BlockSpec((1, tb, A_WIDTH), lambda b, j: (b, j, 0))
    prev = pl.BlockSpec((1, tb, A_WIDTH), lambda b, j: (b, jnp.maximum(j - 1, 0), 0))
    return pl.pallas_call(
        functools.partial(_attn_prompt_kernel, tb=tb),
        grid=(B, S // tb),
        in_specs=[cur, prev, cur, prev, cur, pl.BlockSpec(bias.shape, lambda b, j: (0, 0, 0))],
        out_specs=cur,
        out_shape=jax.ShapeDtypeStruct((B, S, A_WIDTH), BF16),
        compiler_params=_params(("arbitrary", "arbitrary")),
        name="attn_prompt",
    )(aq, ak, ak, av, av, bias)


def _attn_sample_kernel(q_ref, k_ref, v_ref, bias_ref, o_ref):
    T = q_ref.shape[1]
    lo_mask = lax.broadcasted_iota(I32, (T, LANES), 1) < A_HEAD_DIM
    pairs = []
    for p in range(A_WIDTH // LANES):
        c0 = p * LANES
        parts = [(k_ref[0, :, c0:c0 + LANES], v_ref[0, :, c0:c0 + LANES], bias_ref[p], None)]
        pairs.append((q_ref[0, :, c0:c0 + LANES], parts))
    for p, o in enumerate(_attend(pairs, lo_mask)):
        o_ref[0, :, p * LANES:(p + 1) * LANES] = o.astype(BF16)


def _attn_sample(aq, ak, av, cache_k, cache_v, table):
    B, T, _ = aq.shape
    P = cache_k.shape[1]
    L = (P + T + LANES - 1) // LANES * LANES
    pad = jnp.zeros((B, L - P - T, A_WIDTH), BF16)
    kk = jnp.concatenate([cache_k.astype(BF16), ak, pad], axis=1)
    vv = jnp.concatenate([cache_v.astype(BF16), av, pad], axis=1)
    bias = jnp.where((jnp.arange(L) < P + T)[None, None, :], _rel_bias(table, T, L, P), NEG_INF)
    bias = bias.reshape(A_WIDTH // LANES, 2 * T, L)
    new = pl.BlockSpec((1, T, A_WIDTH), lambda b: (b, 0, 0))
    old = pl.BlockSpec((1, L, A_WIDTH), lambda b: (b, 0, 0))
    return pl.pallas_call(
        _attn_sample_kernel,
        grid=(B,),
        in_specs=[new, old, old, pl.BlockSpec(bias.shape, lambda b: (0, 0, 0))],
        out_specs=new,
        out_shape=jax.ShapeDtypeStruct((B, T, A_WIDTH), BF16),
        compiler_params=_params(("arbitrary",)),
        name="attn_sample",
    )(aq, kk, vv, bias)


def _gla_kernel(q_ref, k_ref, v_ref, la_ref, gr_ref, gain_ref, s0_ref, o_ref, sf_ref, s_scr, *, C, n_chunks,
                n_group):
    j = pl.program_id(1)

    @pl.when(j == 0)
    def _():
        s_scr[...] = s0_ref[0]

    n_sub = C // G_SUB
    ri = lax.broadcasted_iota(I32, (C, C), 0)
    ci = lax.broadcasted_iota(I32, (C, C), 1)
    tril = (ci <= ri).astype(BF16)
    lane_kw = lax.broadcasted_iota(I32, (1, G_KW), 1)
    head_of_lane = lane_kw // G_DK
    row_kw = lax.broadcasted_iota(I32, (C, G_KW), 0)
    ur = lax.broadcasted_iota(I32, (2 * C, 4 * C), 0) - C
    uc = lax.broadcasted_iota(I32, (2 * C, 4 * C), 1)
    u_mat = ((ur >= 0) & ((uc >= C) | (ur <= uc))).astype(BF16)

    def heads_on_rows(x):
        return jnp.concatenate([jnp.where(head_of_lane == h, x, 0.0) for h in range(G_HEADS)], axis=0)

    def chunk_group(g, S):
        ns = range(n_group)
        rs = [pl.multiple_of((g * n_group + n) * C, C) for n in ns]
        q = [q_ref[0, pl.ds(r, C), :].astype(F32) for r in rs]
        k = [k_ref[0, pl.ds(r, C), :].astype(F32) for r in rs]
        v = [v_ref[0, pl.ds(r, C), :] for r in rs]
        la = [la_ref[0, pl.ds(r, C), :] for r in rs]

        split = [_split_bf16(x) for x in la]
        b = [jnp.dot(tril, hi, preferred_element_type=F32) + jnp.dot(tril, lo, preferred_element_type=F32)
             for hi, lo in split]
        xt = [jnp.concatenate([k[n], la[n]], axis=0).T for n in ns]
        split_t = [_split_bf16(x) for x in xt]
        xb = [jnp.dot(hi, u_mat, preferred_element_type=F32) + jnp.dot(lo, u_mat, preferred_element_type=F32)
              for hi, lo in split_t]
        b_last = [x[:, LANES:] for x in xb]
        kd = [(xt[n] * jnp.exp(b_last[n] - xb[n][:, :LANES])).astype(BF16) for n in ns]
        zeros_v = jnp.zeros((C, G_VW), BF16)
        kv = [jnp.dot(kd[n], jnp.concatenate([v[n], zeros_v], axis=0), preferred_element_type=F32)
              for n in ns]
        kv_d = [jnp.concatenate([x[h * G_DK:(h + 1) * G_DK, h * G_DV:(h + 1) * G_DV] for h in range(G_HEADS)],
                                axis=0) for x in kv]

        states = [S]
        for n in ns:
            states.append(jnp.exp(b_last[n]) * states[n] + kv_d[n])

        r_inter = [jnp.dot(heads_on_rows(q[n] * jnp.exp(b[n])).astype(BF16), states[n].astype(BF16),
                           preferred_element_type=F32) for n in ns]
        o = [jnp.concatenate([x[h * C:(h + 1) * C] for h in range(G_HEADS)], axis=1) for x in r_inter]

        o_rows = [[] for _ in ns]
        for i in range(n_sub):
            r0, r1 = i * G_SUB, (i + 1) * G_SUB
            atts = []
            for n in ns:
                bs = b[n][r0 - 1:r0] if i > 0 else jnp.zeros((1, G_KW), F32)
                qe = q[n][r0:r1] * jnp.exp(b[n][r0:r1] - bs)
                ke = (k[n] * jnp.exp(jnp.where(row_kw < r1, bs - b[n], -jnp.inf))).astype(BF16)
                att = lax.dot_general(heads_on_rows(qe).astype(BF16), ke, (((1,), (1,)), ((), ())),
                                      preferred_element_type=F32)
                tt = lax.broadcasted_iota(I32, att.shape, 0) % G_SUB + r0
                ss = lax.broadcasted_iota(I32, att.shape, 1)
                atts.append(jnp.where(ss <= tt, att, 0.0).astype(BF16))
            for n in ns:
                ov = jnp.dot(atts[n], v[n], preferred_element_type=F32)
                o_rows[n].append(jnp.concatenate(
                    [ov[h * G_SUB:(h + 1) * G_SUB, h * G_DV:(h + 1) * G_DV] for h in range(G_HEADS)], axis=1))

        for n in ns:
            on = o[n] + jnp.concatenate(o_rows[n], axis=0)
            on = _head_rms(on, G_DV) * gain_ref[...] * gr_ref[0, pl.ds(rs[n], C), :].astype(F32)
            o_ref[0, pl.ds(rs[n], C), :] = on.astype(BF16)
        return states[-1]

    s_scr[...] = lax.fori_loop(0, n_chunks // n_group, chunk_group, s_scr[...])

    @pl.when(j == pl.num_programs(1) - 1)
    def _():
        sf_ref[0] = s_scr[...]


def _gla(gq, gk, gv, la, gr, gain, s0):
    B, T, _ = gq.shape
    C = CHUNK
    tb = min(ROW_TILE, T)
    assert T % tb == 0 and tb % C == 0 and C % G_SUB == 0 and 2 * C == LANES
    kw = pl.BlockSpec((1, tb, G_KW), lambda b, j: (b, j, 0))
    vw = pl.BlockSpec((1, tb, G_VW), lambda b, j: (b, j, 0))
    st = pl.BlockSpec((1, G_KW, G_DV), lambda b, j: (b, 0, 0))
    return pl.pallas_call(
        functools.partial(_gla_kernel, C=C, n_chunks=tb // C, n_group=8 if (tb // C) % 8 == 0 else 1),
        grid=(B, T // tb),
        in_specs=[kw, kw, vw, kw, vw, pl.BlockSpec(gain.shape, lambda b, j: (0, 0)), st],
        out_specs=[vw, st],
        out_shape=[jax.ShapeDtypeStruct((B, T, G_VW), BF16), jax.ShapeDtypeStruct((B, G_KW, G_DV), F32)],
        scratch_shapes=[pltpu.VMEM((G_KW, G_DV), F32)],
        compiler_params=_params(("arbitrary", "arbitrary")),
        name="gla",
    )(gq, gk, gv, la, gr, gain, s0)


def _memkv_kernel(mem_ref, gm_ref, w_ref, gk_ref, k_ref, v_ref):
    x = mem_ref[0]
    h = (x * lax.rsqrt(jnp.mean(x * x, -1, keepdims=True) + EPS) * gm_ref[...]).astype(BF16)
    k = jnp.dot(h, w_ref[:, :C_WIDTH], preferred_element_type=F32)
    k_ref[0] = _head_rms(k, C_HEAD_DIM) * gk_ref[...]
    v_ref[0] = jnp.dot(h, w_ref[:, C_WIDTH:], preferred_element_type=F32)


def _memkv(mem, gm, w_kv, gk):
    B, M, _ = mem.shape
    out = pl.BlockSpec((1, M, C_WIDTH), lambda b: (b, 0, 0))
    return pl.pallas_call(
        _memkv_kernel,
        grid=(B,),
        in_specs=[pl.BlockSpec((1, M, D_MODEL), lambda b: (b, 0, 0)),
                  pl.BlockSpec(gm.shape, lambda b: (0, 0)),
                  pl.BlockSpec(w_kv.shape, lambda b: (0, 0)),
                  pl.BlockSpec(gk.shape, lambda b: (0, 0))],
        out_specs=[out, out],
        out_shape=[jax.ShapeDtypeStruct((B, M, C_WIDTH), F32)] * 2,
        compiler_params=_params(("arbitrary",)),
        name="memkv",
    )(mem, gm, w_kv, gk)


def _merge_kernel(x_ref, ya_ref, yb_ref, cq_ref, gt_ref, mk_ref, mv_ref, wa_ref, wb_ref, wc_ref,
                  wo_ref, nf_ref, wr_ref, br_ref,
                  x1_ref, hp_ref, ri_ref, rw_ref, cnt_ref, run_scr, *, tok_base):
    first =jnp.logical_and(pl.program_id(0) == 0, pl.program_id(1) == 0)

    @pl.when(first)
    def _():
        run_scr[...] = jnp.zeros_like(run_scr)

    tm = x_ref.shape[1]
    heads = [slice(h * C_HEAD_DIM, (h + 1) * C_HEAD_DIM) for h in range(C_HEADS)]
    scores = [lax.dot_general(cq_ref[0, :, c], mk_ref[0, :, c].astype(BF16), (((1,), (1,)), ((), ())),
                              preferred_element_type=F32) * (C_HEAD_DIM ** -0.5) for c in heads]
    probs = [jnp.exp(s - s.max(-1, keepdims=True)) for s in scores]
    sums = [p.sum(-1, keepdims=True) for p in probs]
    cols = [jnp.dot(p.astype(BF16), mv_ref[0, :, c].astype(BF16), preferred_element_type=F32) / l
            for p, l, c in zip(probs, sums, heads)]
    yc_in = jnp.concatenate(cols, axis=-1).astype(BF16)

    y_a = jnp.dot(ya_ref[0], wa_ref[...], preferred_element_type=F32)
    y_b = jnp.dot(yb_ref[0], wb_ref[...], preferred_element_type=F32)
    y_c = jnp.dot(yc_in, wc_ref[...], preferred_element_type=F32)
    merged = (gt_ref[0, :, 0:D_MODEL].astype(F32) * y_a
              + gt_ref[0, :, D_MODEL:2 * D_MODEL].astype(F32) * y_b
              + gt_ref[0, :, 2 * D_MODEL:3 * D_MODEL].astype(F32) * y_c)
    x1 = x_ref[0] + jnp.dot(merged.astype(BF16), wo_ref[...], preferred_element_type=F32)
    x1_ref[0] = x1

    h2 = x1 * lax.rsqrt(jnp.mean(x1 * x1, -1, keepdims=True) + EPS) * nf_ref[...]

    logits = jnp.dot(h2.astype(BF16), wr_ref[...], preferred_element_type=F32) + br_ref[...]
    lane = lax.broadcasted_iota(I32, (tm, LANES), 1)
    lane_f = lane.astype(F32)
    vals, sels, idxs = [], [], []
    l = logits
    for _ in range(TOP_K):
        m = l.max(-1, keepdims=True)
        idx = jnp.min(jnp.where(l == m, lane_f, float(LANES)), -1, keepdims=True)
        sel = lane_f == idx
        vals.append(m)
        idxs.append(idx)
        sels.append(sel)
        l = jnp.where(sel, -3e38, l)
    es = [jnp.exp(vk - vals[0]) for vk in vals]
    den = es[0] + es[1] + es[2] + es[3]
    cnt = jnp.zeros((tm, LANES), F32)
    for sel in sels:
        cnt = cnt + jnp.where(sel, 1.0, 0.0)
    tp = max(tm, LANES)
    cnt_p = cnt if tp == tm else jnp.concatenate([cnt, jnp.zeros((tp - tm, LANES), F32)], axis=0)
    ri = lax.broadcasted_iota(I32, (tp, tp), 0)
    ci = lax.broadcasted_iota(I32, (tp, tp), 1)
    before = jnp.dot(jnp.where(ci < ri, 1.0, 0.0).astype(BF16), cnt_p.astype(BF16),
                     preferred_element_type=F32)[0:tm] + run_scr[0:1, :]
    r_i = jnp.zeros((tm, LANES), I32)
    r_w = jnp.zeros((tm, LANES), F32)
    for kk in range(TOP_K):
        rank = jnp.sum(jnp.where(sels[kk], before, 0.0), -1, keepdims=True)
        r_i = jnp.where(lane == kk, idxs[kk].astype(I32), r_i)
        r_i = jnp.where(lane == TOP_K + kk, rank.astype(I32), r_i)
        r_w = jnp.where(lane == kk, es[kk] / den, r_w)
    ri_ref[0] = r_i
    rw_ref[0] = r_w
    run_scr[...] = run_scr[...] + jnp.sum(cnt, axis=0, keepdims=True)
    cnt_ref[...] = run_scr[...]

    tok = tok_base + (pl.program_id(0) * pl.num_programs(1) + pl.program_id(1)) * tm \
        + lax.broadcasted_iota(I32, (tm, LANES), 0)
    meta = jnp.where(lane == 0, tok, 0)
    for kk in range(TOP_K):
        meta = jnp.where(lane == 1 + kk, idxs[kk].astype(I32), meta)
    packed = _pack_rows(h2)
    zero = jnp.zeros((tm, LANES), U32)
    slabs = [packed[:, j * LANES:(j + 1) * LANES] for j in range(SLABS)]
    slabs += [lax.bitcast_convert_type(meta, U32)] + [zero] * (ROW_SLABS - SLABS - 1)
    hp_ref[0] = pltpu.einshape("jtl->tjl", jnp.stack(slabs, axis=0))


def _merge(x, ya, yb, cq, gt, mk, mv, wa, wb, wc, wo, nf, wr, br, tok_base):
    B, T, _ = x.shape
    tm = min(ROW_TILE, T)
    assert T % tm == 0

    def row(width):
        return pl.BlockSpec((1, tm, width), lambda b, j: (b, j, 0))

    def full(a):
        return pl.BlockSpec(a.shape, lambda b, j: (0,) * a.ndim)

    mem = pl.BlockSpec((1, N_MEM, C_WIDTH), lambda b, j: (b, 0, 0))
    return pl.pallas_call(
        functools.partial(_merge_kernel, tok_base=tok_base),
        grid=(B, T // tm),
        in_specs=[row(D_MODEL), row(A_WIDTH), row(G_VW), row(C_WIDTH), row(N_BRANCH * D_MODEL), mem, mem,
                  full(wa), full(wb), full(wc), full(wo), full(nf), full(wr), full(br)],
        out_specs=[row(D_MODEL), pl.BlockSpec((1, tm, ROW_SLABS, LANES), lambda b, j: (b, j, 0, 0)), row(LANES),
                   row(LANES), pl.BlockSpec((8, LANES), lambda b, j: (0, 0))],
        out_shape=[jax.ShapeDtypeStruct((B, T, D_MODEL), F32), jax.ShapeDtypeStruct((B, T, ROW_SLABS, LANES), U32),
                   jax.ShapeDtypeStruct((B, T, LANES), I32), jax.ShapeDtypeStruct((B, T, LANES), F32),
                   jax.ShapeDtypeStruct((8, LANES), F32)],
        scratch_shapes=[pltpu.VMEM((8, LANES), F32)],
        compiler_params=_params(("arbitrary", "arbitrary")),
        name="merge",
    )(x, ya, yb, cq, gt, mk, mv, wa, wb, wc, wo, nf, wr, br)


def _dispatch_kernel(dest_ref, dest2_ref, pad_lo_ref, pad_n_ref, hp_ref, hp2_ref, xs_ref, zero_scr, sem, sem2,
                     psem, *, tm, tm2, n_pad, n_tokens):
    def scatter(dref, href, s, n):
        def issue(t, c):
            for kk in range(TOP_K):
                pltpu.make_async_copy(href.at[t], xs_ref.at[dref[0, 0, t * TOP_K + kk]], s).start(priority=kk % 2)
            return c
        lax.fori_loop(0, n, issue, 0, unroll=8)

    def drain(href, s, n):
        for _ in range(TOP_K):
            pltpu.make_async_copy(href, xs_ref.at[pl.ds(0, n)], s).wait()

    scatter(dest_ref, hp_ref, sem, tm)

    @pl.when(pl.program_id(0) == 0)
    def _():
        scatter(dest2_ref, hp2_ref, sem2, tm2)
        sub = lax.broadcasted_iota(I32, (ROW_SLABS, LANES), 0)
        lane = lax.broadcasted_iota(I32, (ROW_SLABS, LANES), 1)
        filler = jnp.where((sub == META) & (lane == 0), n_tokens,
                           jnp.where((sub == META) & (lane <= TOP_K), N_EXPERTS, 0))
        zero_scr[...] = lax.bitcast_convert_type(filler, U32)

        def pad_copy(e, n):
            return pltpu.make_async_copy(zero_scr, xs_ref.at[pad_lo_ref[e] + n], psem)

        def fill(e, c):
            lax.fori_loop(0, pad_n_ref[e], lambda n, cc: (pad_copy(e, n).start(), cc)[1], 0)
            return c

        def fill_wait(e, c):
            lax.fori_loop(0, pad_n_ref[e], lambda n, cc: (pad_copy(e, n).wait(), cc)[1], 0)
            return c

        lax.fori_loop(0, n_pad, fill, 0)
        lax.fori_loop(0, n_pad, fill_wait, 0)
        drain(hp2_ref, sem2, tm2)

    drain(hp_ref, sem, tm)


def _dispatch(dest, hp, dest2, hp2, pad_lo, pad_n, rows):
    N, N2 = hp.shape[0], hp2.shape[0]
    tm = min(ROW_TILE, N)
    assert N % tm == 0
    n_steps = N // tm
    smem = functools.partial(pl.BlockSpec, memory_space=pltpu.SMEM)
    n_pad = pad_lo.shape[0]
    return pl.pallas_call(
        functools.partial(_dispatch_kernel, tm=tm, tm2=N2, n_pad=n_pad, n_tokens=N + N2),
        grid=(n_steps,),
        in_specs=[smem((1, 1, tm * TOP_K), lambda i: (i, 0, 0)),
                  smem((1, 1, N2 * TOP_K), lambda i: (0, 0, 0)),
                  smem((n_pad,), lambda i: (0,)), smem((n_pad,), lambda i: (0,)),
                  pl.BlockSpec((tm, ROW_SLABS, LANES), lambda i: (i, 0, 0)),
                  pl.BlockSpec((N2, ROW_SLABS, LANES), lambda i: (0, 0, 0))],
        out_specs=pl.BlockSpec(memory_space=pl.ANY),
        out_shape=jax.ShapeDtypeStruct((rows, ROW_SLABS, LANES), U32),
        scratch_shapes=[pltpu.VMEM((ROW_SLABS, LANES), U32), pltpu.SemaphoreType.DMA(()),
                        pltpu.SemaphoreType.DMA(()), pltpu.SemaphoreType.DMA(())],
        compiler_params=_params(("arbitrary",)),
        name="moe_dispatch",
    )(dest.reshape(n_steps, 1, tm * TOP_K), dest2.reshape(1, 1, N2 * TOP_K), pad_lo, pad_n, hp, hp2)


def _ffn_kernel(be_ref, nu_ref, x_ref, wgu_ref, bgu_ref, wd_ref, bd_ref, y4_ref,
                wgu_bf, wd_bf, ybuf, idv, ids_smem, sem_ids, sem_rows, *, n_blk, n_tokens):
    i = pl.program_id(0)
    n_used = nu_ref[0]
    n_assign = n_tokens * TOP_K
    prev = be_ref[jnp.maximum(i - 1, 0)]
    new_expert = jnp.logical_or(i == 0, be_ref[i] != prev)

    def rows_done():
        pltpu.make_async_copy(ybuf.at[0], y4_ref.at[pl.ds(0, FFN_BLOCK)], sem_rows).wait()

    def ids_copy(s):
        return pltpu.make_async_copy(idv, ids_smem.at[s], sem_ids)

    def start_rows(s, lo, hi):
        for r in range(lo, hi):
            pltpu.make_async_copy(ybuf.at[s, r], y4_ref.at[ids_smem[s, 0, r]], sem_rows).start(priority=r % 2)

    @pl.when(i == 0)
    def _():
        ybuf[1] = jnp.zeros(ybuf.shape[1:], U32)
        spare = pltpu.make_async_copy(ybuf.at[1], y4_ref.at[pl.ds(n_assign, FFN_BLOCK)], sem_rows)
        spare.start()
        spare.wait()

    @pl.when(new_expert)
    def _():
        wgu_bf[...] = wgu_ref[0].astype(BF16)
        wd_bf[...] = wd_ref[0].astype(BF16)

    def compute(s, flush):
        q = FFN_BLOCK // 4
        if flush:
            ids_copy(1 - s).wait()
            start_rows(1 - s, 0, q)
        xm = pltpu.einshape("tjl->jtl", x_ref[...])
        x_lo, x_hi = _unpack_rows(jnp.concatenate([xm[j] for j in range(SLABS)], axis=-1))
        meta = lax.bitcast_convert_type(xm[META], I32)
        tok = meta[:, 0:1]
        choice = jnp.zeros_like(tok)
        for kk in range(1, TOP_K):
            choice = jnp.where(meta[:, 1 + kk:2 + kk] == be_ref[i], kk, choice)
        local = lax.broadcasted_iota(I32, tok.shape, 0)
        dest = jnp.where(tok >= n_tokens, n_assign + local, tok * TOP_K + choice)
        dest_t = jnp.broadcast_to(dest.astype(F32), (FFN_BLOCK, LANES)).T
        idv[...] = dest_t[0:8].astype(I32)
        ids_copy(s).start()
        gu = (jnp.dot(x_lo.astype(BF16), wgu_bf[0:HALF, :], preferred_element_type=F32)
              + jnp.dot(x_hi.astype(BF16), wgu_bf[HALF:, :], preferred_element_type=F32)
              + bgu_ref[0])
        if flush:
            start_rows(1 - s, q, 2 * q)
        gate = jnp.minimum(gu[:, :D_EXPERT], SWIGLU_LIMIT)
        up = jnp.clip(gu[:, D_EXPERT:], -SWIGLU_LIMIT, SWIGLU_LIMIT)
        act = (up + 1.0) * (gate * _sigmoid(SWIGLU_ALPHA * gate))
        if flush:
            start_rows(1 - s, 2 * q, 3 * q)
        y = jnp.dot(act.astype(BF16), wd_bf[...], preferred_element_type=F32) + bd_ref[0]
        if flush:
            start_rows(1 - s, 3 * q, 4 * q)
        _store_slabs(ybuf, (s,), _pack_rows(y))
        if flush:
            rows_done()

    def flush_only(s):
        ids_copy(s).wait()
        start_rows(s, 0, FFN_BLOCK)
        rows_done()

    real = i < n_used
    prev_real = jnp.logical_and(i >= 1, i - 1 < n_used)
    for s in range(2):
        mine = i % 2 == s
        pl.when(mine & real & prev_real)(functools.partial(compute, s, True))
        pl.when(mine & real & jnp.logical_not(prev_real))(functools.partial(compute, s, False))
        pl.when(mine & jnp.logical_not(real) & prev_real)(functools.partial(flush_only, 1 - s))
        pl.when(mine & real & (i == n_blk - 1))(functools.partial(flush_only, s))


def _ffn(blk_expert, n_used, xs, n_tokens, w_gu, b_gu, w_d, b_d):
    P = xs.shape[0]
    nblk = P // FFN_BLOCK
    grid_spec = pltpu.PrefetchScalarGridSpec(
        num_scalar_prefetch=2,
        grid=(nblk,),
        in_specs=[pl.BlockSpec((FFN_BLOCK, ROW_SLABS, LANES), lambda i, be, nu: (jnp.minimum(i, nu[0] - 1), 0, 0)),
                  pl.BlockSpec((1, D_MODEL, 2 * D_EXPERT), lambda i, be, nu: (be[i], 0, 0)),
                  pl.BlockSpec((1, 1, 2 * D_EXPERT), lambda i, be, nu: (be[i], 0, 0)),
                  pl.BlockSpec((1, D_EXPERT, D_MODEL), lambda i, be, nu: (be[i], 0, 0)),
                  pl.BlockSpec((1, 1, D_MODEL), lambda i, be, nu: (be[i], 0, 0))],
        out_specs=pl.BlockSpec(memory_space=pl.ANY),
        scratch_shapes=[pltpu.VMEM((D_MODEL, 2 * D_EXPERT), BF16), pltpu.VMEM((D_EXPERT, D_MODEL), BF16),
                        pltpu.VMEM((2, FFN_BLOCK, SLABS, LANES), U32), pltpu.VMEM((8, FFN_BLOCK), I32),
                        pltpu.SMEM((2, 8, FFN_BLOCK), I32), pltpu.SemaphoreType.DMA(()),
                        pltpu.SemaphoreType.DMA(())],
    )
    return pl.pallas_call(
        functools.partial(_ffn_kernel, n_blk=nblk, n_tokens=n_tokens),
        grid_spec=grid_spec,
        out_shape=jax.ShapeDtypeStruct((n_tokens * TOP_K + FFN_BLOCK, SLABS, LANES), U32),
        compiler_params=_params(("arbitrary",)),
        name="moe_ffn",
    )(blk_expert, n_used, xs, w_gu, b_gu.reshape(N_EXPERTS, 1, -1), w_d, b_d.reshape(N_EXPERTS, 1, -1))


def _combine_kernel(x1_ref, rw_ref, y_ref, o_ref, *, tm):
    w = rw_ref[...]
    acc_lo = x1_ref[:, :HALF]
    acc_hi = x1_ref[:, HALF:]
    for kk in range(TOP_K):
        rows = y_ref[pl.ds(kk, tm, stride=TOP_K)]
        slab_major = pltpu.einshape("tjl->jtl", rows)
        lo, hi = _unpack_rows(jnp.concatenate([slab_major[j] for j in range(SLABS)], axis=-1))
        wk = w[:, kk:kk + 1]
        acc_lo = acc_lo + wk * lo
        acc_hi = acc_hi + wk * hi
    o_ref[:, :HALF] = acc_lo
    o_ref[:, HALF:] = acc_hi


def _combine(x1, rw, y4, tok_base):
    N = x1.shape[0]
    tm = min(256, N)
    assert N % tm == 0 and tok_base % tm == 0
    first = tok_base // tm
    return pl.pallas_call(
        functools.partial(_combine_kernel, tm=tm),
        grid=(N // tm,),
        in_specs=[pl.BlockSpec((tm, D_MODEL), lambda i: (i, 0)),
                  pl.BlockSpec((tm, LANES), lambda i: (i, 0)),
                  pl.BlockSpec((tm * TOP_K, SLABS, LANES), lambda i: (first + i, 0, 0))],
        out_specs=pl.BlockSpec((tm, D_MODEL), lambda i: (i, 0)),
        out_shape=jax.ShapeDtypeStruct((N, D_MODEL), F32),
        compiler_params=_params(("arbitrary",)),
        name="moe_combine",
    )(x1, rw, y4)


def _moe(groups, w_gu, b_gu, w_d, b_d):
    n_tokens = sum(g[0].shape[0] for g in groups)
    n_assign = n_tokens * TOP_K
    nblk = (n_assign + N_EXPERTS * (FFN_BLOCK - 1) + FFN_BLOCK - 1) // FFN_BLOCK
    counts = [g[4].astype(I32) for g in groups]
    total = sum(counts)
    padded = (total + FFN_BLOCK - 1) // FFN_BLOCK * FFN_BLOCK
    pend = jnp.cumsum(padded)
    pstart = pend - padded
    n_used = pend[-1:] // FFN_BLOCK
    blk = jnp.minimum(jnp.arange(nblk, dtype=I32), n_used[0] - 1) * FFN_BLOCK
    blk_expert = jnp.minimum(jnp.sum(pend[None, :] <= blk[:, None], axis=1), N_EXPERTS - 1).astype(I32)

    experts = jnp.arange(N_EXPERTS, dtype=I32)
    dests = []
    base = pstart
    for g, c in zip(groups, counts):
        idx, rank = g[2][:, :TOP_K], g[2][:, TOP_K:2 * TOP_K]
        dests.append(jnp.sum(jnp.where(idx[..., None] == experts, base, 0), axis=-1) + rank)
        base = base + c

    rows = nblk * FFN_BLOCK
    pad_lo = jnp.concatenate([pstart + total, pend[-1:]])
    pad_n = jnp.concatenate([padded - total, rows - pend[-1:]])
    (g_main, g_small), (d_main, d_small) = groups, dests
    xs = _dispatch(d_main, g_main[1], d_small, g_small[1], pad_lo, pad_n, rows)
    y4 = _ffn(blk_expert, n_used.astype(I32), xs, n_tokens, w_gu, b_gu, w_d, b_d)
    return [_combine(g_main[0], g_main[3], y4, 0), _combine(g_small[0], g_small[3], y4, g_main[0].shape[0])]


def _tile_lanes(g, reps):
    return jnp.tile(g.astype(F32), reps)[None, :]


def kernel(x_prompt, x_sample, mem_prompt, cache_attn_k, cache_attn_v, state_gla, cache_mem_k, cache_mem_v, norm_mix, w_in, a_q_norm, a_k_norm, rel_bias_table, w_a_o, w_gla_a_up, b_gla_a, gla_out_norm, w_b_o, c_q_norm, c_k_norm, norm_mem, w_mem_kv, w_c_o, b_gate, w_out, norm_ffn, w_router, b_router, w_gate_up, b_gate_up, w_down, b_down):
    depth = norm_mix.shape[0]
    assert depth == 1
    l = 0
    B, S, _ = x_prompt.shape
    Bs, Ts, _ = x_sample.shape
    keep = min(WINDOW, S)

    w = w_in[l]
    sizes = (A_WIDTH, A_WIDTH, A_WIDTH, G_KW, G_KW, G_VW, G_VW, G_RANK, C_WIDTH, N_BRANCH * D_MODEL)
    offs = [0]
    for s_ in sizes:
        offs.append(offs[-1] + s_)
    seg = [w[:, offs[i]:offs[i + 1]] for i in range(len(sizes))]
    w_r = jnp.concatenate(seg[0:7] + [seg[8], seg[9], seg[7], jnp.zeros((D_MODEL, LANES - G_RANK), F32)],
                          axis=1).astype(BF16)
    nm = norm_mix[l][None, :]
    aqn = _tile_lanes(a_q_norm[l], A_HEADS)
    akn = _tile_lanes(a_k_norm[l], A_HEADS)
    cqn = _tile_lanes(c_q_norm[l], C_HEADS)
    ckn = _tile_lanes(c_k_norm[l], C_HEADS)
    gon = _tile_lanes(gla_out_norm[l], G_HEADS)
    wup = jnp.concatenate([w_gla_a_up[l], jnp.zeros((LANES - G_RANK, G_KW), F32)], axis=0).astype(BF16)
    bla = b_gla_a[l][None, :]
    bg = b_gate[l][None, :]
    wa, wb, wc, wo = (t[l].astype(BF16) for t in (w_a_o, w_b_o, w_c_o, w_out))
    nf = norm_ffn[l][None, :]
    wr = jnp.concatenate([w_router[l], jnp.zeros((D_MODEL, LANES - N_EXPERTS), F32)], axis=1).astype(BF16)
    br = jnp.concatenate([b_router[l], jnp.full((LANES - N_EXPERTS,), NEG_INF, F32)])[None, :]
    table = rel_bias_table[l]

    mk, mv = _memkv(mem_prompt, norm_mem[l][None, :], w_mem_kv[l].astype(BF16), ckn)
    (aq, ak, av, gq, gk, gv, gr, la, cq, gt, ak_tail, av_tail) = _inproj(
        x_prompt, keep, nm, w_r, aqn, akn, cqn, wup, bla, bg)
    ya = _attn_prompt(aq, ak, av, table)
    yb, s_prompt = _gla(gq, gk, gv, la, gr, gon, jnp.zeros((B, G_KW, G_DV), F32))
    x1_p, hp_p, ri_p, rw_p, cnt_p = _merge(x_prompt, ya, yb, cq, gt, mk, mv, wa, wb, wc, wo, nf, wr, br, 0)

    (aq, ak, av, gq, gk, gv, gr, la, cq, gt, ak_new, av_new) = _inproj(
        x_sample.reshape(1, Bs * Ts, D_MODEL), Bs * Ts, nm, w_r, aqn, akn, cqn, wup, bla, bg)
    rs = lambda t: t.reshape(Bs, Ts, t.shape[-1])
    P = cache_attn_k.shape[2]
    ya = _attn_sample(rs(aq), rs(ak), rs(av), cache_attn_k[l].reshape(Bs, P, A_WIDTH),
                      cache_attn_v[l].reshape(Bs, P, A_WIDTH), table)
    t_pad = (Ts + CHUNK - 1) // CHUNK * CHUNK
    zp = lambda t: jnp.pad(rs(t), ((0, 0), (0, t_pad - Ts), (0, 0)))
    yb, s_sample = _gla(zp(gq), zp(gk), zp(gv), zp(la), zp(gr), gon, state_gla[l].reshape(Bs, G_KW, G_DV))
    yb = yb[:, :Ts]
    x1_s, hp_s, ri_s, rw_s, cnt_s = _merge(
        x_sample, ya, yb, rs(cq), rs(gt), cache_mem_k[l].reshape(Bs, N_MEM, C_WIDTH),
        cache_mem_v[l].reshape(Bs, N_MEM, C_WIDTH), wa, wb, wc, wo, nf, wr, br, B * S)

    flat = lambda t: t.reshape((-1,) + t.shape[2:])
    y_p, y_s = _moe(
        [(flat(x1_p), flat(hp_p), flat(ri_p), flat(rw_p), cnt_p[0, :N_EXPERTS]),
         (flat(x1_s), flat(hp_s), flat(ri_s), flat(rw_s), cnt_s[0, :N_EXPERTS])],
        w_gate_up[l], b_gate_up[l], w_down[l], b_down[l])

    return (y_p.reshape(B, S, D_MODEL), y_s.reshape(Bs, Ts, D_MODEL),
            ak_tail.reshape(1, B, keep, A_HEADS, A_HEAD_DIM), av_tail.reshape(1, B, keep, A_HEADS, A_HEAD_DIM),
            s_prompt.reshape(1, B, G_HEADS, G_DK, G_DV),
            mk.reshape(1, B, N_MEM, C_HEADS, C_HEAD_DIM), mv.reshape(1, B, N_MEM, C_HEADS, C_HEAD_DIM),
            ak_new.reshape(1, Bs, Ts, A_HEADS, A_HEAD_DIM), av_new.reshape(1, Bs, Ts, A_HEADS, A_HEAD_DIM),
            s_sample.reshape(1, Bs, G_HEADS, G_DK, G_DV))
```

```python
import functools

import jax
import jax.numpy as jnp
from jax import lax
from jax.experimental import pallas as pl
from jax.experimental.pallas import tpu as pltpu

F32 = jnp.float32
BF16 = jnp.bfloat16
U32 = jnp.uint32
I32 = jnp.int32

D_MODEL = 1024
CHUNK = 64
BAND_CHUNKS = 8
WINDOW = BAND_CHUNKS * CHUNK
N_MEM = 256
A_HEADS, A_HEAD_DIM = 8, 64
A_WIDTH = A_HEADS * A_HEAD_DIM
REL_MAX = 128
G_HEADS, G_DK, G_DV = 4, 64, 128
G_KW, G_VW = G_HEADS * G_DK, G_HEADS * G_DV
G_RANK = 16
G_TAU = 16.0
G_SUB = 16
C_HEADS, C_HEAD_DIM = 4, 128
C_WIDTH = C_HEADS * C_HEAD_DIM
N_BRANCH = 3
N_EXPERTS = 32
TOP_K = 4
D_EXPERT = 1024
SWIGLU_LIMIT = 7.0
SWIGLU_ALPHA = 1.702
EPS = 1e-6
NEG_INF = -1e30

LANES = 128
HALF = D_MODEL // 2
ROW_TILE = 512
ATTN_SUB = 128
FFN_BLOCK = 256
VMEM_LIMIT = 56 * 1024 * 1024

OFF_AQ, OFF_AK, OFF_AV = 0, 512, 1024
OFF_GQ, OFF_GK, OFF_GV, OFF_GR = 1536, 1792, 2048, 2560
OFF_CQ, OFF_GATE, OFF_LR = 3072, 3584, 6656
IN_COLS = OFF_LR + LANES


def _params(sem):
    return pltpu.CompilerParams(dimension_semantics=sem, vmem_limit_bytes=VMEM_LIMIT)


def _sigmoid(x):
    return 0.5 * jnp.tanh(0.5 * x) + 0.5


def _head_rms(y, head_dim):
    cols = []
    for p in range(y.shape[1] // LANES):
        blk = y[:, p * LANES:(p + 1) * LANES]
        sq = blk * blk
        if head_dim == LANES:
            sc = lax.rsqrt(jnp.sum(sq, -1, keepdims=True) * (1.0 / LANES) + EPS)
        else:
            lo = lax.broadcasted_iota(I32, blk.shape, 1) < head_dim
            s_lo = jnp.sum(jnp.where(lo, sq, 0.0), -1, keepdims=True)
            s_hi = jnp.sum(jnp.where(lo, 0.0, sq), -1, keepdims=True)
            sc = jnp.where(lo, lax.rsqrt(s_lo * (1.0 / head_dim) + EPS),
                           lax.rsqrt(s_hi * (1.0 / head_dim) + EPS))
        cols.append(blk * sc)
    return jnp.concatenate(cols, axis=-1)


def _split_bf16(x):
    hi = x.astype(BF16)
    lo = (x - hi.astype(F32)).astype(BF16)
    return hi, lo


def _pack_rows(x):
    lo = lax.bitcast_convert_type(x[:, :HALF].astype(BF16).astype(F32), U32)
    hi = lax.bitcast_convert_type(x[:, HALF:].astype(BF16).astype(F32), U32)
    return (lo >> 16) | (hi & jnp.uint32(0xFFFF0000))


def _unpack_rows(u):
    lo = lax.bitcast_convert_type(u << 16, F32)
    hi = lax.bitcast_convert_type(u & jnp.uint32(0xFFFF0000), F32)
    return lo, hi


SLABS = HALF // LANES
ROW_SLABS = 8
META = SLABS


def _store_slabs(ref, lead, u):
    slab_major = jnp.stack([u[:, j * LANES:(j + 1) * LANES] for j in range(SLABS)], axis=0)
    ref[lead + (slice(None),) * 3] = pltpu.einshape("jtl->tjl", slab_major)


def _load_slabs(ref, lead):
    slab_major = pltpu.einshape("tjl->jtl", ref[lead + (slice(None),) * 3])
    return jnp.concatenate([slab_major[j] for j in range(SLABS)], axis=-1)


def _inproj_kernel(x_ref, nm_ref, w_ref, aqn_ref, akn_ref, cqn_ref, wup_ref, bla_ref, bg_ref,
                   aq_ref, ak_ref, av_ref, gq_ref, gk_ref, gv_ref, gr_ref, la_ref, cq_ref, gt_ref,
                   akt_ref, avt_ref, *, n_tiles, n_tail):
    j = pl.program_id(1)
    x = x_ref[0]
    h = (x * lax.rsqrt(jnp.mean(x * x, -1, keepdims=True) + EPS) * nm_ref[...]).astype(BF16)

    def seg(off, width):
        return jnp.dot(h, w_ref[:, off:off + width], preferred_element_type=F32)

    in_tail = j >= n_tiles - n_tail

    aq = _head_rms(seg(OFF_AQ, A_WIDTH), A_HEAD_DIM) * aqn_ref[...] * (A_HEAD_DIM ** -0.5)
    aq_ref[0] = aq.astype(BF16)

    ak = _head_rms(seg(OFF_AK, A_WIDTH), A_HEAD_DIM) * akn_ref[...]
    ak_ref[0] = ak.astype(BF16)

    @pl.when(in_tail)
    def _():
        akt_ref[0] = ak

    av = seg(OFF_AV, A_WIDTH)
    av_ref[0] = av.astype(BF16)

    @pl.when(in_tail)
    def _():
        avt_ref[0] = av

    gq_ref[0] = (seg(OFF_GQ, G_KW) * (G_DK ** -0.5)).astype(BF16)
    gk_ref[0] = seg(OFF_GK, G_KW).astype(BF16)
    gv_ref[0] = seg(OFF_GV, G_VW).astype(BF16)
    gr = seg(OFF_GR, G_VW)
    gr_ref[0] = (gr * _sigmoid(gr)).astype(BF16)

    lr = seg(OFF_LR, LANES).astype(BF16)
    z = jnp.dot(lr, wup_ref[...], preferred_element_type=F32) + bla_ref[...]
    la_ref[0] = (jnp.minimum(z, 0.0) - jnp.log1p(jnp.exp(-jnp.abs(z)))) * (1.0 / G_TAU)

    cq = _head_rms(seg(OFF_CQ, C_WIDTH), C_HEAD_DIM) * cqn_ref[...]
    cq_ref[0] = cq.astype(BF16)

    gate_chunk = 512
    for c in range(N_BRANCH * D_MODEL // gate_chunk):
        lo = c * gate_chunk
        g = seg(OFF_GATE + lo, gate_chunk) + bg_ref[:, lo:lo + gate_chunk]
        gt_ref[0, :, lo:lo + gate_chunk] = _sigmoid(g).astype(BF16)


def _inproj(x, keep, nm, w_r, aqn, akn, cqn, wup, bla, bg):
    G, R, _ = x.shape
    tm = min(ROW_TILE, R)
    n_tiles = R // tm
    n_tail = keep // tm
    assert R % tm == 0 and keep % tm == 0 and n_tail >= 1

    def row(width, dtype):
        return (jax.ShapeDtypeStruct((G, R, width), dtype),
                pl.BlockSpec((1, tm, width), lambda g, j: (g, j, 0)))

    def tail(width):
        return (jax.ShapeDtypeStruct((G, keep, width), F32),
                pl.BlockSpec((1, tm, width), lambda g, j: (g, jnp.maximum(j - (n_tiles - n_tail), 0), 0)))

    outs = [row(A_WIDTH, BF16), row(A_WIDTH, BF16), row(A_WIDTH, BF16), row(G_KW, BF16),
            row(G_KW, BF16), row(G_VW, BF16), row(G_VW, BF16), row(G_KW, F32), row(C_WIDTH, BF16),
            row(N_BRANCH * D_MODEL, BF16), tail(A_WIDTH), tail(A_WIDTH)]

    def full(a):
        return pl.BlockSpec(a.shape, lambda g, j: (0,) * a.ndim)

    return pl.pallas_call(
        functools.partial(_inproj_kernel, n_tiles=n_tiles, n_tail=n_tail),
        grid=(G, n_tiles),
        in_specs=[pl.BlockSpec((1, tm, D_MODEL), lambda g, j: (g, j, 0)), full(nm), full(w_r),
                  full(aqn), full(akn), full(cqn), full(wup), full(bla), full(bg)],
        out_specs=[o[1] for o in outs],
        out_shape=[o[0] for o in outs],
        compiler_params=_params(("arbitrary", "arbitrary")),
        name="inproj",
    )(x, nm, w_r, aqn, akn, cqn, wup, bla, bg)


def _attend(pairs, lo_mask):
    T = pairs[0][0].shape[0]
    scores = []
    for q, parts in pairs:
        zero = jnp.zeros_like(q)
        q2 = jnp.concatenate([jnp.where(lo_mask, q, zero), jnp.where(lo_mask, zero, q)], axis=0)
        ss = []
        for (k, _, bias2, valid) in parts:
            s = lax.dot_general(q2, k, (((1,), (1,)), ((), ())), preferred_element_type=F32) + bias2
            if valid is not None:
                s = jnp.where(valid, s, NEG_INF)
            ss.append(s)
        scores.append(ss)
    probs, sums = [], []
    for ss in scores:
        m = ss[0].max(-1, keepdims=True)
        for s in ss[1:]:
            m = jnp.maximum(m, s.max(-1, keepdims=True))
        ps = [jnp.exp(s - m) for s in ss]
        l = ps[0].sum(-1, keepdims=True)
        for p in ps[1:]:
            l = l + p.sum(-1, keepdims=True)
        probs.append([p.astype(BF16) for p in ps])
        sums.append(l)
    outs = []
    for (q, parts), ps, l in zip(pairs, probs, sums):
        o = jnp.dot(ps[0], parts[0][1], preferred_element_type=F32)
        for p, part in zip(ps[1:], parts[1:]):
            o = o + jnp.dot(p, part[1], preferred_element_type=F32)
        o = o / l
        outs.append(jnp.where(lo_mask, o[:T], o[T:]))
    return outs


def _attn_prompt_kernel(q_ref, kp_ref, kc_ref, vp_ref, vc_ref, bias_ref, o_ref, *, tb):
    j = pl.program_id(1)
    has_prev = j > 0
    lo_mask = lax.broadcasted_iota(I32, (ATTN_SUB, LANES), 1) < A_HEAD_DIM
    for s in range(tb // ATTN_SUB):
        r0 = s * ATTN_SUB
        len_a = tb - r0
        len_b = r0 + ATTN_SUB
        pairs = []
        for p in range(A_WIDTH // LANES):
            c0 = p * LANES
            q = q_ref[0, r0:r0 + ATTN_SUB, c0:c0 + LANES]
            parts = [
                (kp_ref[0, r0:tb, c0:c0 + LANES], vp_ref[0, r0:tb, c0:c0 + LANES],
                 bias_ref[p, :, 0:len_a], has_prev),
                (kc_ref[0, 0:len_b, c0:c0 + LANES], vc_ref[0, 0:len_b, c0:c0 + LANES],
                 bias_ref[p, :, len_a:len_a + len_b], None),
            ]
            pairs.append((q, parts))
        for p, o in enumerate(_attend(pairs, lo_mask)):
            o_ref[0, r0:r0 + ATTN_SUB, p * LANES:(p + 1) * LANES] = o.astype(BF16)


def _rel_bias(table, n_q, n_k, offset):
    period = n_q + n_k - 1
    m = jnp.arange(period)
    u = table[:, jnp.clip(n_q - 1 + offset - m, -REL_MAX, REL_MAX) + REL_MAX].astype(F32)
    rows = jnp.tile(u, (1, n_q + 1))[:, :n_q * (period + 1)].reshape(-1, n_q, period + 1)[:, :, :n_k]
    return rows[:, ::-1, :]


def _band_bias(table):
    qc = jnp.arange(ATTN_SUB)[:, None] // CHUNK
    kc = jnp.arange(ATTN_SUB + WINDOW)[None, :] // CHUNK
    ok = (kc >= qc) & (kc <= qc + BAND_CHUNKS)
    return jnp.where(ok[None], _rel_bias(table, ATTN_SUB, ATTN_SUB + WINDOW, WINDOW), NEG_INF)


def _attn_prompt(aq, ak, av, table):
    B, S, _ = aq.shape
    tb = WINDOW
    assert S % tb == 0
    bias = _band_bias(table).reshape(A_WIDTH // LANES, 2 * ATTN_SUB, ATTN_SUB + WINDOW)
    cur = pl.BlockSpec((1, tb, A_WIDTH), lambda b, j: (b, j, 0))
    prev = pl.BlockSpec((1, tb, A_WIDTH), lambda b, j: (b, jnp.maximum(j - 1, 0), 0))
    return pl.pallas_call(
        functools.partial(_attn_prompt_kernel, tb=tb),
        grid=(B, S // tb),
        in_specs=[cur, prev, cur, prev, cur, pl.BlockSpec(bias.shape, lambda b, j: (0, 0, 0))],
        out_specs=cur,
        out_shape=jax.ShapeDtypeStruct((B, S, A_WIDTH), BF16),
        compiler_params=_params(("arbitrary", "arbitrary")),
        name="attn_prompt",
    )(aq, ak, ak, av, av, bias)


def _attn_sample_kernel(q_ref, k_ref, v_ref, bias_ref, o_ref):
    T = q_ref.shape[1]
    lo_mask = lax.broadcasted_iota(I32, (T, LANES), 1) < A_HEAD_DIM
    pairs = []
    for p in range(A_WIDTH // LANES):
        c0 = p * LANES
        parts = [(k_ref[0, :, c0:c0 + LANES], v_ref[0, :, c0:c0 + LANES], bias_ref[p], None)]
        pairs.append((q_ref[0, :, c0:c0 + LANES], parts))
    for p, o in enumerate(_attend(pairs, lo_mask)):
        o_ref[0, :, p * LANES:(p + 1) * LANES] = o.astype(BF16)


def _attn_sample(aq, ak, av, cache_k, cache_v, table):
    B, T, _ = aq.shape
    P = cache_k.shape[1]
    L = (P + T + LANES - 1) // LANES * LANES
    pad = jnp.zeros((B, L - P - T, A_WIDTH), BF16)
    kk = jnp.concatenate([cache_k.astype(BF16), ak, pad], axis=1)
    vv = jnp.concatenate([cache_v.astype(BF16), av, pad], axis=1)
    bias = jnp.where((jnp.arange(L) < P + T)[None, None, :], _rel_bias(table, T, L, P), NEG_INF)
    bias = bias.reshape(A_WIDTH // LANES, 2 * T, L)
    new = pl.BlockSpec((1, T, A_WIDTH), lambda b: (b, 0, 0))
    old = pl.BlockSpec((1, L, A_WIDTH), lambda b: (b, 0, 0))
    return pl.pallas_call(
        _attn_sample_kernel,
        grid=(B,),
        in_specs=[new, old, old, pl.BlockSpec(bias.shape, lambda b: (0, 0, 0))],
        out_specs=new,
        out_shape=jax.ShapeDtypeStruct((B, T, A_WIDTH), BF16),
        compiler_params=_params(("arbitrary",)),
        name="attn_sample",
    )(aq, kk, vv, bias)


def _gla_kernel(q_ref, k_ref, v_ref, la_ref, gr_ref, gain_ref, s0_ref, o_ref, sf_ref, s_scr, *, C, n_chunks,
                n_group):
    j = pl.program_id(1)

    @pl.when(j == 0)
    def _():
        s_scr[...] = s0_ref[0]

    n_sub = C // G_SUB
    ri = lax.broadcasted_iota(I32, (C, C), 0)
    ci = lax.broadcasted_iota(I32, (C, C), 1)
    tril = (ci <= ri).astype(BF16)
    lane_kw = lax.broadcasted_iota(I32, (1, G_KW), 1)
    head_of_lane = lane_kw // G_DK
    row_kw = lax.broadcasted_iota(I32, (C, G_KW), 0)
    ur = lax.broadcasted_iota(I32, (2 * C, 4 * C), 0) - C
    uc = lax.broadcasted_iota(I32, (2 * C, 4 * C), 1)
    u_mat = ((ur >= 0) & ((uc >= C) | (ur <= uc))).astype(BF16)

    def heads_on_rows(x):
        return jnp.concatenate([jnp.where(head_of_lane == h, x, 0.0) for h in range(G_HEADS)], axis=0)

    def chunk_group(g, S):
        ns = range(n_group)
        rs = [pl.multiple_of((g * n_group + n) * C, C) for n in ns]
        q = [q_ref[0, pl.ds(r, C), :].astype(F32) for r in rs]
        k = [k_ref[0, pl.ds(r, C), :].astype(F32) for r in rs]
        v = [v_ref[0, pl.ds(r, C), :] for r in rs]
        la = [la_ref[0, pl.ds(r, C), :] for r in rs]

        split = [_split_bf16(x) for x in la]
        b = [jnp.dot(tril, hi, preferred_element_type=F32) + jnp.dot(tril, lo, preferred_element_type=F32)
             for hi, lo in split]
        xt = [jnp.concatenate([k[n], la[n]], axis=0).T for n in ns]
        split_t = [_split_bf16(x) for x in xt]
        xb = [jnp.dot(hi, u_mat, preferred_element_type=F32) + jnp.dot(lo, u_mat, preferred_element_type=F32)
              for hi, lo in split_t]
        b_last = [x[:, LANES:] for x in xb]
        kd = [(xt[n] * jnp.exp(b_last[n] - xb[n][:, :LANES])).astype(BF16) for n in ns]
        zeros_v = jnp.zeros((C, G_VW), BF16)
        kv = [jnp.dot(kd[n], jnp.concatenate([v[n], zeros_v], axis=0), preferred_element_type=F32)
              for n in ns]
        kv_d = [jnp.concatenate([x[h * G_DK:(h + 1) * G_DK, h * G_DV:(h + 1) * G_DV] for h in range(G_HEADS)],
                                axis=0) for x in kv]

        states = [S]
        for n in ns:
            states.append(jnp.exp(b_last[n]) * states[n] + kv_d[n])

        r_inter = [jnp.dot(heads_on_rows(q[n] * jnp.exp(b[n])).astype(BF16), states[n].astype(BF16),
                           preferred_element_type=F32) for n in ns]
        o = [jnp.concatenate([x[h * C:(h + 1) * C] for h in range(G_HEADS)], axis=1) for x in r_inter]

        o_rows = [[] for _ in ns]
        for i in range(n_sub):
            r0, r1 = i * G_SUB, (i + 1) * G_SUB
            atts = []
            for n in ns:
                bs = b[n][r0 - 1:r0] if i > 0 else jnp.zeros((1, G_KW), F32)
                qe = q[n][r0:r1] * jnp.exp(b[n][r0:r1] - bs)
                ke = (k[n] * jnp.exp(jnp.where(row_kw < r1, bs - b[n], -jnp.inf))).astype(BF16)
                att = lax.dot_general(heads_on_rows(qe).astype(BF16), ke, (((1,), (1,)), ((), ())),
                                      preferred_element_type=F32)
                tt = lax.broadcasted_iota(I32, att.shape, 0) % G_SUB + r0
                ss = lax.broadcasted_iota(I32, att.shape, 1)
                atts.append(jnp.where(ss <= tt, att, 0.0).astype(BF16))
            for n in ns:
                ov = jnp.dot(atts[n], v[n], preferred_element_type=F32)
                o_rows[n].append(jnp.concatenate(
                    [ov[h * G_SUB:(h + 1) * G_SUB, h * G_DV:(h + 1) * G_DV] for h in range(G_HEADS)], axis=1))

        for n in ns:
            on = o[n] + jnp.concatenate(o_rows[n], axis=0)
            on = _head_rms(on, G_DV) * gain_ref[...] * gr_ref[0, pl.ds(rs[n], C), :].astype(F32)
            o_ref[0, pl.ds(rs[n], C), :] = on.astype(BF16)
        return states[-1]

    s_scr[...] = lax.fori_loop(0, n_chunks // n_group, chunk_group, s_scr[...])

    @pl.when(j == pl.num_programs(1) - 1)
    def _():
        sf_ref[0] = s_scr[...]


def _gla(gq, gk, gv, la, gr, gain, s0):
    B, T, _ = gq.shape
    C = CHUNK
    tb = min(ROW_TILE, T)
    assert T % tb == 0 and tb % C == 0 and C % G_SUB == 0 and 2 * C == LANES
    kw = pl.BlockSpec((1, tb, G_KW), lambda b, j: (b, j, 0))
    vw = pl.BlockSpec((1, tb, G_VW), lambda b, j: (b, j, 0))
    st = pl.BlockSpec((1, G_KW, G_DV), lambda b, j: (b, 0, 0))
    return pl.pallas_call(
        functools.partial(_gla_kernel, C=C, n_chunks=tb // C, n_group=8 if (tb // C) % 8 == 0 else 1),
        grid=(B, T // tb),
        in_specs=[kw, kw, vw, kw, vw, pl.BlockSpec(gain.shape, lambda b, j: (0, 0)), st],
        out_specs=[vw, st],
        out_shape=[jax.ShapeDtypeStruct((B, T, G_VW), BF16), jax.ShapeDtypeStruct((B, G_KW, G_DV), F32)],
        scratch_shapes=[pltpu.VMEM((G_KW, G_DV), F32)],
        compiler_params=_params(("arbitrary", "arbitrary")),
        name="gla",
    )(gq, gk, gv, la, gr, gain, s0)


def _memkv_kernel(mem_ref, gm_ref, w_ref, gk_ref, k_ref, v_ref):
    x = mem_ref[0]
    h = (x * lax.rsqrt(jnp.mean(x * x, -1, keepdims=True) + EPS) * gm_ref[...]).astype(BF16)
    k = jnp.dot(h, w_ref[:, :C_WIDTH], preferred_element_type=F32)
    k_ref[0] = _head_rms(k, C_HEAD_DIM) * gk_ref[...]
    v_ref[0] = jnp.dot(h, w_ref[:, C_WIDTH:], preferred_element_type=F32)


def _memkv(mem, gm, w_kv, gk):
    B, M, _ = mem.shape
    out = pl.BlockSpec((1, M, C_WIDTH), lambda b: (b, 0, 0))
    return pl.pallas_call(
        _memkv_kernel,
        grid=(B,),
        in_specs=[pl.BlockSpec((1, M, D_MODEL), lambda b: (b, 0, 0)),
                  pl.BlockSpec(gm.shape, lambda b: (0, 0)),
                  pl.BlockSpec(w_kv.shape, lambda b: (0, 0)),
                  pl.BlockSpec(gk.shape, lambda b: (0, 0))],
        out_specs=[out, out],
        out_shape=[jax.ShapeDtypeStruct((B, M, C_WIDTH), F32)] * 2,
        compiler_params=_params(("arbitrary",)),
        name="memkv",
    )(mem, gm, w_kv, gk)


def _merge_kernel(x_ref, ya_ref, yb_ref, cq_ref, gt_ref, mk_ref, mv_ref, wa_ref, wb_ref, wc_ref,
                  wo_ref, nf_ref, wr_ref, br_ref,
                  x1_ref, hp_ref, ri_ref, rw_ref, cnt_ref, run_scr, *, tok_base):
    first =jnp.logical_and(pl.program_id(0) == 0, pl.program_id(1) == 0)

    @pl.when(first)
    def _():
        run_scr[...] = jnp.zeros_like(run_scr)

    tm = x_ref.shape[1]
    heads = [slice(h * C_HEAD_DIM, (h + 1) * C_HEAD_DIM) for h in range(C_HEADS)]
    scores = [lax.dot_general(cq_ref[0, :, c], mk_ref[0, :, c].astype(BF16), (((1,), (1,)), ((), ())),
                              preferred_element_type=F32) * (C_HEAD_DIM ** -0.5) for c in heads]
    probs = [jnp.exp(s - s.max(-1, keepdims=True)) for s in scores]
    sums = [p.sum(-1, keepdims=True) for p in probs]
    cols = [jnp.dot(p.astype(BF16), mv_ref[0, :, c].astype(BF16), preferred_element_type=F32) / l
            for p, l, c in zip(probs, sums, heads)]
    yc_in = jnp.concatenate(cols, axis=-1).astype(BF16)

    y_a = jnp.dot(ya_ref[0], wa_ref[...], preferred_element_type=F32)
    y_b = jnp.dot(yb_ref[0], wb_ref[...], preferred_element_type=F32)
    y_c = jnp.dot(yc_in, wc_ref[...], preferred_element_type=F32)
    merged = (gt_ref[0, :, 0:D_MODEL].astype(F32) * y_a
              + gt_ref[0, :, D_MODEL:2 * D_MODEL].astype(F32) * y_b
              + gt_ref[0, :, 2 * D_MODEL:3 * D_MODEL].astype(F32) * y_c)
    x1 = x_ref[0] + jnp.dot(merged.astype(BF16), wo_ref[...], preferred_element_type=F32)
    x1_ref[0] = x1

    h2 = x1 * lax.rsqrt(jnp.mean(x1 * x1, -1, keepdims=True) + EPS) * nf_ref[...]

    logits = jnp.dot(h2.astype(BF16), wr_ref[...], preferred_element_type=F32) + br_ref[...]
    lane = lax.broadcasted_iota(I32, (tm, LANES), 1)
    lane_f = lane.astype(F32)
    vals, sels, idxs = [], [], []
    l = logits
    for _ in range(TOP_K):
        m = l.max(-1, keepdims=True)
        idx = jnp.min(jnp.where(l == m, lane_f, float(LANES)), -1, keepdims=True)
        sel = lane_f == idx
        vals.append(m)
        idxs.append(idx)
        sels.append(sel)
        l = jnp.where(sel, -3e38, l)
    es = [jnp.exp(vk - vals[0]) for vk in vals]
    den = es[0] + es[1] + es[2] + es[3]
    cnt = jnp.zeros((tm, LANES), F32)
    for sel in sels:
        cnt = cnt + jnp.where(sel, 1.0, 0.0)
    tp = max(tm, LANES)
    cnt_p = cnt if tp == tm else jnp.concatenate([cnt, jnp.zeros((tp - tm, LANES), F32)], axis=0)
    ri = lax.broadcasted_iota(I32, (tp, tp), 0)
    ci = lax.broadcasted_iota(I32, (tp, tp), 1)
    before = jnp.dot(jnp.where(ci < ri, 1.0, 0.0).astype(BF16), cnt_p.astype(BF16),
                     preferred_element_type=F32)[0:tm] + run_scr[0:1, :]
    r_i = jnp.zeros((tm, LANES), I32)
    r_w = jnp.zeros((tm, LANES), F32)
    for kk in range(TOP_K):
        rank = jnp.sum(jnp.where(sels[kk], before, 0.0), -1, keepdims=True)
        r_i = jnp.where(lane == kk, idxs[kk].astype(I32), r_i)
        r_i = jnp.where(lane == TOP_K + kk, rank.astype(I32), r_i)
        r_w = jnp.where(lane == kk, es[kk] / den, r_w)
    ri_ref[0] = r_i
    rw_ref[0] = r_w
    run_scr[...] = run_scr[...] + jnp.sum(cnt, axis=0, keepdims=True)
    cnt_ref[...] = run_scr[...]

    tok = tok_base + (pl.program_id(0) * pl.num_programs(1) + pl.program_id(1)) * tm \
        + lax.broadcasted_iota(I32, (tm, LANES), 0)
    meta = jnp.where(lane == 0, tok, 0)
    for kk in range(TOP_K):
        meta = jnp.where(lane == 1 + kk, idxs[kk].astype(I32), meta)
    packed = _pack_rows(h2)
    zero = jnp.zeros((tm, LANES), U32)
    slabs = [packed[:, j * LANES:(j + 1) * LANES] for j in range(SLABS)]
    slabs += [lax.bitcast_convert_type(meta, U32)] + [zero] * (ROW_SLABS - SLABS - 1)
    hp_ref[0] = pltpu.einshape("jtl->tjl", jnp.stack(slabs, axis=0))


def _merge(x, ya, yb, cq, gt, mk, mv, wa, wb, wc, wo, nf, wr, br, tok_base):
    B, T, _ = x.shape
    tm = min(ROW_TILE, T)
    assert T % tm == 0

    def row(width):
        return pl.BlockSpec((1, tm, width), lambda b, j: (b, j, 0))

    def full(a):
        return pl.BlockSpec(a.shape, lambda b, j: (0,) * a.ndim)

    mem = pl.BlockSpec((1, N_MEM, C_WIDTH), lambda b, j: (b, 0, 0))
    return pl.pallas_call(
        functools.partial(_merge_kernel, tok_base=tok_base),
        grid=(B, T // tm),
        in_specs=[row(D_MODEL), row(A_WIDTH), row(G_VW), row(C_WIDTH), row(N_BRANCH * D_MODEL), mem, mem,
                  full(wa), full(wb), full(wc), full(wo), full(nf), full(wr), full(br)],
        out_specs=[row(D_MODEL), pl.BlockSpec((1, tm, ROW_SLABS, LANES), lambda b, j: (b, j, 0, 0)), row(LANES),
                   row(LANES), pl.BlockSpec((8, LANES), lambda b, j: (0, 0))],
        out_shape=[jax.ShapeDtypeStruct((B, T, D_MODEL), F32), jax.ShapeDtypeStruct((B, T, ROW_SLABS, LANES), U32),
                   jax.ShapeDtypeStruct((B, T, LANES), I32), jax.ShapeDtypeStruct((B, T, LANES), F32),
                   jax.ShapeDtypeStruct((8, LANES), F32)],
        scratch_shapes=[pltpu.VMEM((8, LANES), F32)],
        compiler_params=_params(("arbitrary", "arbitrary")),
        name="merge",
    )(x, ya, yb, cq, gt, mk, mv, wa, wb, wc, wo, nf, wr, br)


def _dispatch_kernel(dest_ref, dest2_ref, pad_lo_ref, pad_n_ref, hp_ref, hp2_ref, xs_ref, zero_scr, sem, sem2,
                     psem, *, tm, tm2, n_pad, n_tokens):
    def scatter(dref, href, s, n):
        def issue(t, c):
            for kk in range(TOP_K):
                pltpu.make_async_copy(href.at[t], xs_ref.at[dref[0, 0, t * TOP_K + kk]], s).start(priority=kk % 2)
            return c
        lax.fori_loop(0, n, issue, 0, unroll=8)

    def drain(href, s, n):
        for _ in range(TOP_K):
            pltpu.make_async_copy(href, xs_ref.at[pl.ds(0, n)], s).wait()

    scatter(dest_ref, hp_ref, sem, tm)

    @pl.when(pl.program_id(0) == 0)
    def _():
        scatter(dest2_ref, hp2_ref, sem2, tm2)
        sub = lax.broadcasted_iota(I32, (ROW_SLABS, LANES), 0)
        lane = lax.broadcasted_iota(I32, (ROW_SLABS, LANES), 1)
        filler = jnp.where((sub == META) & (lane == 0), n_tokens,
                           jnp.where((sub == META) & (lane <= TOP_K), N_EXPERTS, 0))
        zero_scr[...] = lax.bitcast_convert_type(filler, U32)

        def pad_copy(e, n):
            return pltpu.make_async_copy(zero_scr, xs_ref.at[pad_lo_ref[e] + n], psem)

        def fill(e, c):
            lax.fori_loop(0, pad_n_ref[e], lambda n, cc: (pad_copy(e, n).start(), cc)[1], 0)
            return c

        def fill_wait(e, c):
            lax.fori_loop(0, pad_n_ref[e], lambda n, cc: (pad_copy(e, n).wait(), cc)[1], 0)
            return c

        lax.fori_loop(0, n_pad, fill, 0)
        lax.fori_loop(0, n_pad, fill_wait, 0)
        drain(hp2_ref, sem2, tm2)

    drain(hp_ref, sem, tm)


def _dispatch(dest, hp, dest2, hp2, pad_lo, pad_n, rows):
    N, N2 = hp.shape[0], hp2.shape[0]
    tm = min(ROW_TILE, N)
    assert N % tm == 0
    n_steps = N // tm
    smem = functools.partial(pl.BlockSpec, memory_space=pltpu.SMEM)
    n_pad = pad_lo.shape[0]
    return pl.pallas_call(
        functools.partial(_dispatch_kernel, tm=tm, tm2=N2, n_pad=n_pad, n_tokens=N + N2),
        grid=(n_steps,),
        in_specs=[smem((1, 1, tm * TOP_K), lambda i: (i, 0, 0)),
                  smem((1, 1, N2 * TOP_K), lambda i: (0, 0, 0)),
                  smem((n_pad,), lambda i: (0,)), smem((n_pad,), lambda i: (0,)),
                  pl.BlockSpec((tm, ROW_SLABS, LANES), lambda i: (i, 0, 0)),
                  pl.BlockSpec((N2, ROW_SLABS, LANES), lambda i: (0, 0, 0))],
        out_specs=pl.BlockSpec(memory_space=pl.ANY),
        out_shape=jax.ShapeDtypeStruct((rows, ROW_SLABS, LANES), U32),
        scratch_shapes=[pltpu.VMEM((ROW_SLABS, LANES), U32), pltpu.SemaphoreType.DMA(()),
                        pltpu.SemaphoreType.DMA(()), pltpu.SemaphoreType.DMA(())],
        compiler_params=_params(("arbitrary",)),
        name="moe_dispatch",
    )(dest.reshape(n_steps, 1, tm * TOP_K), dest2.reshape(1, 1, N2 * TOP_K), pad_lo, pad_n, hp, hp2)


def _ffn_kernel(be_ref, nu_ref, x_ref, wgu_ref, bgu_ref, wd_ref, bd_ref, y4_ref,
                wgu_bf, wd_bf, ybuf, idv, ids_smem, sem_ids, sem_rows, *, n_blk, n_tokens):
    i = pl.program_id(0)
    n_used = nu_ref[0]
    n_assign = n_tokens * TOP_K
    prev = be_ref[jnp.maximum(i - 1, 0)]
    new_expert = jnp.logical_or(i == 0, be_ref[i] != prev)

    def rows_done(s):
        pltpu.make_async_copy(ybuf.at[s], y4_ref.at[pl.ds(0, FFN_BLOCK)], sem_rows.at[s]).wait()

    def ids_copy(s):
        return pltpu.make_async_copy(idv, ids_smem.at[s], sem_ids)

    def start_rows(s, lo, hi):
        for r in range(lo, hi):
            pltpu.make_async_copy(ybuf.at[s, r], y4_ref.at[ids_smem[s, 0, r]],
                                  sem_rows.at[s]).start(priority=r % 2)

    @pl.when(i == 0)
    def _():
        ybuf[1] = jnp.zeros(ybuf.shape[1:], U32)
        spare = pltpu.make_async_copy(ybuf.at[1], y4_ref.at[pl.ds(n_assign, FFN_BLOCK)], sem_rows.at[1])
        spare.start()
        spare.wait()

    @pl.when(new_expert)
    def _():
        wgu_bf[...] = wgu_ref[0].astype(BF16)
        wd_bf[...] = wd_ref[0].astype(BF16)

    def compute(s, flush):
        q = FFN_BLOCK // 4
        if flush:
            ids_copy(1 - s).wait()
            start_rows(1 - s, 0, q)
        xm = pltpu.einshape("tjl->jtl", x_ref[...])
        x_lo, x_hi = _unpack_rows(jnp.concatenate([xm[j] for j in range(SLABS)], axis=-1))
        meta = lax.bitcast_convert_type(xm[META], I32)
        tok = meta[:, 0:1]
        choice = jnp.zeros_like(tok)
        for kk in range(1, TOP_K):
            choice = jnp.where(meta[:, 1 + kk:2 + kk] == be_ref[i], kk, choice)
        local = lax.broadcasted_iota(I32, tok.shape, 0)
        dest = jnp.where(tok >= n_tokens, n_assign + local, tok * TOP_K + choice)
        dest_t = jnp.broadcast_to(dest.astype(F32), (FFN_BLOCK, LANES)).T
        idv[...] = dest_t[0:8].astype(I32)
        ids_copy(s).start()
        gu = (jnp.dot(x_lo.astype(BF16), wgu_bf[0:HALF, :], preferred_element_type=F32)
              + jnp.dot(x_hi.astype(BF16), wgu_bf[HALF:, :], preferred_element_type=F32)
              + bgu_ref[0])
        if flush:
            start_rows(1 - s, q, 2 * q)
        gate = jnp.minimum(gu[:, :D_EXPERT], SWIGLU_LIMIT)
        up = jnp.clip(gu[:, D_EXPERT:], -SWIGLU_LIMIT, SWIGLU_LIMIT)
        act = (up + 1.0) * (gate * _sigmoid(SWIGLU_ALPHA * gate))
        if flush:
            start_rows(1 - s, 2 * q, 3 * q)
        y = jnp.dot(act.astype(BF16), wd_bf[...], preferred_element_type=F32) + bd_ref[0]
        if flush:
            start_rows(1 - s, 3 * q, 4 * q)
        packed = _pack_rows(y)
        pl.when(i >= 2)(functools.partial(rows_done, s))
        _store_slabs(ybuf, (s,), packed)

    def flush_only(s, also):
        ids_copy(s).wait()
        start_rows(s, 0, FFN_BLOCK)
        pl.when(also)(functools.partial(rows_done, 1 - s))
        rows_done(s)

    real = i < n_used
    prev_real = jnp.logical_and(i >= 1, i - 1 < n_used)
    for s in range(2):
        mine = i % 2 == s
        pl.when(mine & real & prev_real)(functools.partial(compute, s, True))
        pl.when(mine & real & jnp.logical_not(prev_real))(functools.partial(compute, s, False))
        pl.when(mine & jnp.logical_not(real) & prev_real)(functools.partial(flush_only, 1 - s, i >= 2))
        pl.when(mine & real & (i == n_blk - 1))(functools.partial(flush_only, s, i >= 1))


def _ffn(blk_expert, n_used, xs, n_tokens, w_gu, b_gu, w_d, b_d):
    P = xs.shape[0]
    nblk = P // FFN_BLOCK
    grid_spec = pltpu.PrefetchScalarGridSpec(
        num_scalar_prefetch=2,
        grid=(nblk,),
        in_specs=[pl.BlockSpec((FFN_BLOCK, ROW_SLABS, LANES), lambda i, be, nu: (jnp.minimum(i, nu[0] - 1), 0, 0)),
                  pl.BlockSpec((1, D_MODEL, 2 * D_EXPERT), lambda i, be, nu: (be[i], 0, 0)),
                  pl.BlockSpec((1, 1, 2 * D_EXPERT), lambda i, be, nu: (be[i], 0, 0)),
                  pl.BlockSpec((1, D_EXPERT, D_MODEL), lambda i, be, nu: (be[i], 0, 0)),
                  pl.BlockSpec((1, 1, D_MODEL), lambda i, be, nu: (be[i], 0, 0))],
        out_specs=pl.BlockSpec(memory_space=pl.ANY),
        scratch_shapes=[pltpu.VMEM((D_MODEL, 2 * D_EXPERT), BF16), pltpu.VMEM((D_EXPERT, D_MODEL), BF16),
                        pltpu.VMEM((2, FFN_BLOCK, SLABS, LANES), U32), pltpu.VMEM((8, FFN_BLOCK), I32),
                        pltpu.SMEM((2, 8, FFN_BLOCK), I32), pltpu.SemaphoreType.DMA(()),
                        pltpu.SemaphoreType.DMA((2,))],
    )
    return pl.pallas_call(
        functools.partial(_ffn_kernel, n_blk=nblk, n_tokens=n_tokens),
        grid_spec=grid_spec,
        out_shape=jax.ShapeDtypeStruct((n_tokens * TOP_K + FFN_BLOCK, SLABS, LANES), U32),
        compiler_params=_params(("arbitrary",)),
        name="moe_ffn",
    )(blk_expert, n_used, xs, w_gu, b_gu.reshape(N_EXPERTS, 1, -1), w_d, b_d.reshape(N_EXPERTS, 1, -1))


def _combine_kernel(x1_ref, rw_ref, y_ref, o_ref, *, tm):
    w = rw_ref[...]
    acc_lo = x1_ref[:, :HALF]
    acc_hi = x1_ref[:, HALF:]
    for kk in range(TOP_K):
        rows = y_ref[pl.ds(kk, tm, stride=TOP_K)]
        slab_major = pltpu.einshape("tjl->jtl", rows)
        lo, hi = _unpack_rows(jnp.concatenate([slab_major[j] for j in range(SLABS)], axis=-1))
        wk = w[:, kk:kk + 1]
        acc_lo = acc_lo + wk * lo
        acc_hi = acc_hi + wk * hi
    o_ref[:, :HALF] = acc_lo
    o_ref[:, HALF:] = acc_hi


def _combine(x1, rw, y4, tok_base):
    N = x1.shape[0]
    tm = min(256, N)
    assert N % tm == 0 and tok_base % tm == 0
    first = tok_base // tm
    return pl.pallas_call(
        functools.partial(_combine_kernel, tm=tm),
        grid=(N // tm,),
        in_specs=[pl.BlockSpec((tm, D_MODEL), lambda i: (i, 0)),
                  pl.BlockSpec((tm, LANES), lambda i: (i, 0)),
                  pl.BlockSpec((tm * TOP_K, SLABS, LANES), lambda i: (first + i, 0, 0))],
        out_specs=pl.BlockSpec((tm, D_MODEL), lambda i: (i, 0)),
        out_shape=jax.ShapeDtypeStruct((N, D_MODEL), F32),
        compiler_params=_params(("arbitrary",)),
        name="moe_combine",
    )(x1, rw, y4)


def _moe(groups, w_gu, b_gu, w_d, b_d):
    n_tokens = sum(g[0].shape[0] for g in groups)
    n_assign = n_tokens * TOP_K
    nblk = (n_assign + N_EXPERTS * (FFN_BLOCK - 1) + FFN_BLOCK - 1) // FFN_BLOCK
    counts = [g[4].astype(I32) for g in groups]
    total = sum(counts)
    padded = (total + FFN_BLOCK - 1) // FFN_BLOCK * FFN_BLOCK
    pend = jnp.cumsum(padded)
    pstart = pend - padded
    n_used = pend[-1:] // FFN_BLOCK
    blk = jnp.minimum(jnp.arange(nblk, dtype=I32), n_used[0] - 1) * FFN_BLOCK
    blk_expert = jnp.minimum(jnp.sum(pend[None, :] <= blk[:, None], axis=1), N_EXPERTS - 1).astype(I32)

    experts = jnp.arange(N_EXPERTS, dtype=I32)
    dests = []
    base = pstart
    for g, c in zip(groups, counts):
        idx, rank = g[2][:, :TOP_K], g[2][:, TOP_K:2 * TOP_K]
        dests.append(jnp.sum(jnp.where(idx[..., None] == experts, base, 0), axis=-1) + rank)
        base = base + c

    rows = nblk * FFN_BLOCK
    pad_lo = jnp.concatenate([pstart + total, pend[-1:]])
    pad_n = jnp.concatenate([padded - total, rows - pend[-1:]])
    (g_main, g_small), (d_main, d_small) = groups, dests
    xs = _dispatch(d_main, g_main[1], d_small, g_small[1], pad_lo, pad_n, rows)
    y4 = _ffn(blk_expert, n_used.astype(I32), xs, n_tokens, w_gu, b_gu, w_d, b_d)
    return [_combine(g_main[0], g_main[3], y4, 0), _combine(g_small[0], g_small[3], y4, g_main[0].shape[0])]


def _tile_lanes(g, reps):
    return jnp.tile(g.astype(F32), reps)[None, :]


def kernel(x_prompt, x_sample, mem_prompt, cache_attn_k, cache_attn_v, state_gla, cache_mem_k, cache_mem_v, norm_mix, w_in, a_q_norm, a_k_norm, rel_bias_table, w_a_o, w_gla_a_up, b_gla_a, gla_out_norm, w_b_o, c_q_norm, c_k_norm, norm_mem, w_mem_kv, w_c_o, b_gate, w_out, norm_ffn, w_router, b_router, w_gate_up, b_gate_up, w_down, b_down):
    depth = norm_mix.shape[0]
    assert depth == 1
    l = 0
    B, S, _ = x_prompt.shape
    Bs, Ts, _ = x_sample.shape
    keep = min(WINDOW, S)

    w = w_in[l]
    sizes = (A_WIDTH, A_WIDTH, A_WIDTH, G_KW, G_KW, G_VW, G_VW, G_RANK, C_WIDTH, N_BRANCH * D_MODEL)
    offs = [0]
    for s_ in sizes:
        offs.append(offs[-1] + s_)
    seg = [w[:, offs[i]:offs[i + 1]] for i in range(len(sizes))]
    w_r = jnp.concatenate(seg[0:7] + [seg[8], seg[9], seg[7], jnp.zeros((D_MODEL, LANES - G_RANK), F32)],
                          axis=1).astype(BF16)
    nm = norm_mix[l][None, :]
    aqn = _tile_lanes(a_q_norm[l], A_HEADS)
    akn = _tile_lanes(a_k_norm[l], A_HEADS)
    cqn = _tile_lanes(c_q_norm[l], C_HEADS)
    ckn = _tile_lanes(c_k_norm[l], C_HEADS)
    gon = _tile_lanes(gla_out_norm[l], G_HEADS)
    wup = jnp.concatenate([w_gla_a_up[l], jnp.zeros((LANES - G_RANK, G_KW), F32)], axis=0).astype(BF16)
    bla = b_gla_a[l][None, :]
    bg = b_gate[l][None, :]
    wa, wb, wc, wo = (t[l].astype(BF16) for t in (w_a_o, w_b_o, w_c_o, w_out))
    nf = norm_ffn[l][None, :]
    wr = jnp.concatenate([w_router[l], jnp.zeros((D_MODEL, LANES - N_EXPERTS), F32)], axis=1).astype(BF16)
    br = jnp.concatenate([b_router[l], jnp.full((LANES - N_EXPERTS,), NEG_INF, F32)])[None, :]
    table = rel_bias_table[l]

    mk, mv = _memkv(mem_prompt, norm_mem[l][None, :], w_mem_kv[l].astype(BF16), ckn)
    (aq, ak, av, gq, gk, gv, gr, la, cq, gt, ak_tail, av_tail) = _inproj(
        x_prompt, keep, nm, w_r, aqn, akn, cqn, wup, bla, bg)
    ya = _attn_prompt(aq, ak, av, table)
    yb, s_prompt = _gla(gq, gk, gv, la, gr, gon, jnp.zeros((B, G_KW, G_DV), F32))
    x1_p, hp_p, ri_p, rw_p, cnt_p = _merge(x_prompt, ya, yb, cq, gt, mk, mv, wa, wb, wc, wo, nf, wr, br, 0)

    (aq, ak, av, gq, gk, gv, gr, la, cq, gt, ak_new, av_new) = _inproj(
        x_sample.reshape(1, Bs * Ts, D_MODEL), Bs * Ts, nm, w_r, aqn, akn, cqn, wup, bla, bg)
    rs = lambda t: t.reshape(Bs, Ts, t.shape[-1])
    P = cache_attn_k.shape[2]
    ya = _attn_sample(rs(aq), rs(ak), rs(av), cache_attn_k[l].reshape(Bs, P, A_WIDTH),
                      cache_attn_v[l].reshape(Bs, P, A_WIDTH), table)
    t_pad = (Ts + CHUNK - 1) // CHUNK * CHUNK
    zp = lambda t: jnp.pad(rs(t), ((0, 0), (0, t_pad - Ts), (0, 0)))
    yb, s_sample = _gla(zp(gq), zp(gk), zp(gv), zp(la), zp(gr), gon, state_gla[l].reshape(Bs, G_KW, G_DV))
    yb = yb[:, :Ts]
    x1_s, hp_s, ri_s, rw_s, cnt_s = _merge(
        x_sample, ya, yb, rs(cq), rs(gt), cache_mem_k[l].reshape(Bs, N_MEM, C_WIDTH),
        cache_mem_v[l].reshape(Bs, N_MEM, C_WIDTH), wa, wb, wc, wo, nf, wr, br, B * S)

    flat = lambda t: t.reshape((-1,) + t.shape[2:])
    y_p, y_s = _moe(
        [(flat(x1_p), flat(hp_p), flat(ri_p), flat(rw_p), cnt_p[0, :N_EXPERTS]),
         (flat(x1_s), flat(hp_s), flat(ri_s), flat(rw_s), cnt_s[0, :N_EXPERTS])],
        w_gate_up[l], b_gate_up[l], w_down[l], b_down[l])

    return (y_p.reshape(B, S, D_MODEL), y_s.reshape(Bs, Ts, D_MODEL),
            ak_tail.reshape(1, B, keep, A_HEADS, A_HEAD_DIM), av_tail.reshape(1, B, keep, A_HEADS, A_HEAD_DIM),
            s_prompt.reshape(1, B, G_HEADS, G_DK, G_DV),
            mk.reshape(1, B, N_MEM, C_HEADS, C_HEAD_DIM), mv.reshape(1, B, N_MEM, C_HEADS, C_HEAD_DIM),
            ak_new.reshape(1, Bs, Ts, A_HEADS, A_HEAD_DIM), av_new.reshape(1, Bs, Ts, A_HEADS, A_HEAD_DIM),
            s_sample.reshape(1, Bs, G_HEADS, G_DK, G_DV))
```

```python
import functools

import jax
import jax.numpy as jnp
from jax import lax
from jax.experimental import pallas as pl
from jax.experimental.pallas import tpu as pltpu

F32 = jnp.float32
BF16 = jnp.bfloat16
U32 = jnp.uint32
I32 = jnp.int32

D_MODEL = 1024
CHUNK = 64
BAND_CHUNKS = 8
WINDOW = BAND_CHUNKS * CHUNK
N_MEM = 256
A_HEADS, A_HEAD_DIM = 8, 64
A_WIDTH = A_HEADS * A_HEAD_DIM
REL_MAX = 128
G_HEADS, G_DK, G_DV = 4, 64, 128
G_KW, G_VW = G_HEADS * G_DK, G_HEADS * G_DV
G_RANK = 16
G_TAU = 16.0
G_SUB = 16
C_HEADS, C_HEAD_DIM = 4, 128
C_WIDTH = C_HEADS * C_HEAD_DIM
N_BRANCH = 3
N_EXPERTS = 32
TOP_K = 4
D_EXPERT = 1024
SWIGLU_LIMIT = 7.0
SWIGLU_ALPHA = 1.702
EPS = 1e-6
NEG_INF = -1e30

LANES = 128
HALF = D_MODEL // 2
ROW_TILE = 512
ATTN_SUB = 128
FFN_BLOCK = 512
VMEM_LIMIT = 56 * 1024 * 1024

OFF_AQ, OFF_AK, OFF_AV = 0, 512, 1024
OFF_GQ, OFF_GK, OFF_GV, OFF_GR = 1536, 1792, 2048, 2560
OFF_CQ, OFF_GATE, OFF_LR = 3072, 3584, 6656
IN_COLS = OFF_LR + LANES


def _params(sem):
    return pltpu.CompilerParams(dimension_semantics=sem, vmem_limit_bytes=VMEM_LIMIT)


def _sigmoid(x):
    return 0.5 * jnp.tanh(0.5 * x) + 0.5


def _head_rms(y, head_dim):
    cols = []
    for p in range(y.shape[1] // LANES):
        blk = y[:, p * LANES:(p + 1) * LANES]
        sq = blk * blk
        if head_dim == LANES:
            sc = lax.rsqrt(jnp.sum(sq, -1, keepdims=True) * (1.0 / LANES) + EPS)
        else:
            lo = lax.broadcasted_iota(I32, blk.shape, 1) < head_dim
            s_lo = jnp.sum(jnp.where(lo, sq, 0.0), -1, keepdims=True)
            s_hi = jnp.sum(jnp.where(lo, 0.0, sq), -1, keepdims=True)
            sc = jnp.where(lo, lax.rsqrt(s_lo * (1.0 / head_dim) + EPS),
                           lax.rsqrt(s_hi * (1.0 / head_dim) + EPS))
        cols.append(blk * sc)
    return jnp.concatenate(cols, axis=-1)


def _split_bf16(x):
    hi = x.astype(BF16)
    lo = (x - hi.astype(F32)).astype(BF16)
    return hi, lo


def _pack_rows(x):
    lo = lax.bitcast_convert_type(x[:, :HALF].astype(BF16).astype(F32), U32)
    hi = lax.bitcast_convert_type(x[:, HALF:].astype(BF16).astype(F32), U32)
    return (lo >> 16) | (hi & jnp.uint32(0xFFFF0000))


def _unpack_rows(u):
    lo = lax.bitcast_convert_type(u << 16, F32)
    hi = lax.bitcast_convert_type(u & jnp.uint32(0xFFFF0000), F32)
    return lo, hi


SLABS = HALF // LANES
ROW_SLABS = 8
META = SLABS


def _store_slabs(ref, lead, u):
    slab_major = jnp.stack([u[:, j * LANES:(j + 1) * LANES] for j in range(SLABS)], axis=0)
    ref[lead + (slice(None),) * 3] = pltpu.einshape("jtl->tjl", slab_major)


def _load_slabs(ref, lead):
    slab_major = pltpu.einshape("tjl->jtl", ref[lead + (slice(None),) * 3])
    return jnp.concatenate([slab_major[j] for j in range(SLABS)], axis=-1)


def _inproj_kernel(x_ref, nm_ref, w_ref, aqn_ref, akn_ref, cqn_ref, wup_ref, bla_ref, bg_ref,
                   aq_ref, ak_ref, av_ref, gq_ref, gk_ref, gv_ref, gr_ref, la_ref, cq_ref, gt_ref,
                   akt_ref, avt_ref, *, n_tiles, n_tail):
    j = pl.program_id(1)
    x = x_ref[0]
    h = (x * lax.rsqrt(jnp.mean(x * x, -1, keepdims=True) + EPS) * nm_ref[...]).astype(BF16)

    def seg(off, width):
        return jnp.dot(h, w_ref[:, off:off + width], preferred_element_type=F32)

    in_tail = j >= n_tiles - n_tail

    aq = _head_rms(seg(OFF_AQ, A_WIDTH), A_HEAD_DIM) * aqn_ref[...] * (A_HEAD_DIM ** -0.5)
    aq_ref[0] = aq.astype(BF16)

    ak = _head_rms(seg(OFF_AK, A_WIDTH), A_HEAD_DIM) * akn_ref[...]
    ak_ref[0] = ak.astype(BF16)

    @pl.when(in_tail)
    def _():
        akt_ref[0] = ak

    av = seg(OFF_AV, A_WIDTH)
    av_ref[0] = av.astype(BF16)

    @pl.when(in_tail)
    def _():
        avt_ref[0] = av

    gq_ref[0] = (seg(OFF_GQ, G_KW) * (G_DK ** -0.5)).astype(BF16)
    gk_ref[0] = seg(OFF_GK, G_KW).astype(BF16)
    gv_ref[0] = seg(OFF_GV, G_VW).astype(BF16)
    gr = seg(OFF_GR, G_VW)
    gr_ref[0] = (gr * _sigmoid(gr)).astype(BF16)

    lr = seg(OFF_LR, LANES).astype(BF16)
    z = jnp.dot(lr, wup_ref[...], preferred_element_type=F32) + bla_ref[...]
    la_ref[0] = (jnp.minimum(z, 0.0) - jnp.log1p(jnp.exp(-jnp.abs(z)))) * (1.0 / G_TAU)

    cq = _head_rms(seg(OFF_CQ, C_WIDTH), C_HEAD_DIM) * cqn_ref[...]
    cq_ref[0] = cq.astype(BF16)

    gate_chunk = 512
    for c in range(N_BRANCH * D_MODEL // gate_chunk):
        lo = c * gate_chunk
        g = seg(OFF_GATE + lo, gate_chunk) + bg_ref[:, lo:lo + gate_chunk]
        gt_ref[0, :, lo:lo + gate_chunk] = _sigmoid(g).astype(BF16)


def _inproj(x, keep, nm, w_r, aqn, akn, cqn, wup, bla, bg):
    G, R, _ = x.shape
    tm = min(ROW_TILE, R)
    n_tiles = R // tm
    n_tail = keep // tm
    assert R % tm == 0 and keep % tm == 0 and n_tail >= 1

    def row(width, dtype):
        return (jax.ShapeDtypeStruct((G, R, width), dtype),
                pl.BlockSpec((1, tm, width), lambda g, j: (g, j, 0)))

    def tail(width):
        return (jax.ShapeDtypeStruct((G, keep, width), F32),
                pl.BlockSpec((1, tm, width), lambda g, j: (g, jnp.maximum(j - (n_tiles - n_tail), 0), 0)))

    outs = [row(A_WIDTH, BF16), row(A_WIDTH, BF16), row(A_WIDTH, BF16), row(G_KW, BF16),
            row(G_KW, BF16), row(G_VW, BF16), row(G_VW, BF16), row(G_KW, F32), row(C_WIDTH, BF16),
            row(N_BRANCH * D_MODEL, BF16), tail(A_WIDTH), tail(A_WIDTH)]

    def full(a):
        return pl.BlockSpec(a.shape, lambda g, j: (0,) * a.ndim)

    return pl.pallas_call(
        functools.partial(_inproj_kernel, n_tiles=n_tiles, n_tail=n_tail),
        grid=(G, n_tiles),
        in_specs=[pl.BlockSpec((1, tm, D_MODEL), lambda g, j: (g, j, 0)), full(nm), full(w_r),
                  full(aqn), full(akn), full(cqn), full(wup), full(bla), full(bg)],
        out_specs=[o[1] for o in outs],
        out_shape=[o[0] for o in outs],
        compiler_params=_params(("arbitrary", "arbitrary")),
        name="inproj",
    )(x, nm, w_r, aqn, akn, cqn, wup, bla, bg)


def _attend(pairs, lo_mask):
    T = pairs[0][0].shape[0]
    scores = []
    for q, parts in pairs:
        zero = jnp.zeros_like(q)
        q2 = jnp.concatenate([jnp.where(lo_mask, q, zero), jnp.where(lo_mask, zero, q)], axis=0)
        ss = []
        for (k, _, bias2, valid) in parts:
            s = lax.dot_general(q2, k, (((1,), (1,)), ((), ())), preferred_element_type=F32) + bias2
            if valid is not None:
                s = jnp.where(valid, s, NEG_INF)
            ss.append(s)
        scores.append(ss)
    probs, sums = [], []
    for ss in scores:
        m = ss[0].max(-1, keepdims=True)
        for s in ss[1:]:
            m = jnp.maximum(m, s.max(-1, keepdims=True))
        ps = [jnp.exp(s - m) for s in ss]
        l = ps[0].sum(-1, keepdims=True)
        for p in ps[1:]:
            l = l + p.sum(-1, keepdims=True)
        probs.append([p.astype(BF16) for p in ps])
        sums.append(l)
    outs = []
    for (q, parts), ps, l in zip(pairs, probs, sums):
        o = jnp.dot(ps[0], parts[0][1], preferred_element_type=F32)
        for p, part in zip(ps[1:], parts[1:]):
            o = o + jnp.dot(p, part[1], preferred_element_type=F32)
        o = o / l
        outs.append(jnp.where(lo_mask, o[:T], o[T:]))
    return outs


def _attn_prompt_kernel(q_ref, kp_ref, kc_ref, vp_ref, vc_ref, bias_ref, o_ref, *, tb):
    j = pl.program_id(1)
    has_prev = j > 0
    lo_mask = lax.broadcasted_iota(I32, (ATTN_SUB, LANES), 1) < A_HEAD_DIM
    for s in range(tb // ATTN_SUB):
        r0 = s * ATTN_SUB
        len_a = tb - r0
        len_b = r0 + ATTN_SUB
        pairs = []
        for p in range(A_WIDTH // LANES):
            c0 = p * LANES
            q = q_ref[0, r0:r0 + ATTN_SUB, c0:c0 + LANES]
            parts = [
                (kp_ref[0, r0:tb, c0:c0 + LANES], vp_ref[0, r0:tb, c0:c0 + LANES],
                 bias_ref[p, :, 0:len_a], has_prev),
                (kc_ref[0, 0:len_b, c0:c0 + LANES], vc_ref[0, 0:len_b, c0:c0 + LANES],
                 bias_ref[p, :, len_a:len_a + len_b], None),
            ]
            pairs.append((q, parts))
        for p, o in enumerate(_attend(pairs, lo_mask)):
            o_ref[0, r0:r0 + ATTN_SUB, p * LANES:(p + 1) * LANES] = o.astype(BF16)


def _rel_bias(table, n_q, n_k, offset):
    period = n_q + n_k - 1
    m = jnp.arange(period)
    u = table[:, jnp.clip(n_q - 1 + offset - m, -REL_MAX, REL_MAX) + REL_MAX].astype(F32)
    rows = jnp.tile(u, (1, n_q + 1))[:, :n_q * (period + 1)].reshape(-1, n_q, period + 1)[:, :, :n_k]
    return rows[:, ::-1, :]


def _band_bias(table):
    qc = jnp.arange(ATTN_SUB)[:, None] // CHUNK
    kc = jnp.arange(ATTN_SUB + WINDOW)[None, :] // CHUNK
    ok = (kc >= qc) & (kc <= qc + BAND_CHUNKS)
    return jnp.where(ok[None], _rel_bias(table, ATTN_SUB, ATTN_SUB + WINDOW, WINDOW), NEG_INF)


def _attn_prompt(aq, ak, av, table):
    B, S, _ = aq.shape
    tb = WINDOW
    assert S % tb == 0
    bias = _band_bias(table).reshape(A_WIDTH // LANES, 2 * ATTN_SUB, ATTN_SUB + WINDOW)
    cur = pl.BlockSpec((1, tb, A_WIDTH), lambda b, j: (b, j, 0))
    prev = pl.BlockSpec((1, tb, A_WIDTH), lambda b, j: (b, jnp.maximum(j - 1, 0), 0))
    return pl.pallas_call(
        functools.partial(_attn_prompt_kernel, tb=tb),
        grid=(B, S // tb),
        in_specs=[cur, prev, cur, prev, cur, pl.BlockSpec(bias.shape, lambda b, j: (0, 0, 0))],
        out_specs=cur,
        out_shape=jax.ShapeDtypeStruct((B, S, A_WIDTH), BF16),
        compiler_params=_params(("arbitrary", "arbitrary")),
        name="attn_prompt",
    )(aq, ak, ak, av, av, bias)


def _attn_sample_kernel(q_ref, k_ref, v_ref, bias_ref, o_ref):
    T = q_ref.shape[1]
    lo_mask = lax.broadcasted_iota(I32, (T, LANES), 1) < A_HEAD_DIM
    pairs = []
    for p in range(A_WIDTH // LANES):
        c0 = p * LANES
        parts = [(k_ref[0, :, c0:c0 + LANES], v_ref[0, :, c0:c0 + LANES], bias_ref[p], None)]
        pairs.append((q_ref[0, :, c0:c0 + LANES], parts))
    for p, o in enumerate(_attend(pairs, lo_mask)):
        o_ref[0, :, p * LANES:(p + 1) * LANES] = o.astype(BF16)


def _attn_sample(aq, ak, av, cache_k, cache_v, table):
    B, T, _ = aq.shape
    P = cache_k.shape[1]
    L = (P + T + LANES - 1) // LANES * LANES
    pad = jnp.zeros((B, L - P - T, A_WIDTH), BF16)
    kk = jnp.concatenate([cache_k.astype(BF16), ak, pad], axis=1)
    vv = jnp.concatenate([cache_v.astype(BF16), av, pad], axis=1)
    bias = jnp.where((jnp.arange(L) < P + T)[None, None, :], _rel_bias(table, T, L, P), NEG_INF)
    bias = bias.reshape(A_WIDTH // LANES, 2 * T, L)
    new = pl.BlockSpec((1, T, A_WIDTH), lambda b: (b, 0, 0))
    old = pl.BlockSpec((1, L, A_WIDTH), lambda b: (b, 0, 0))
    return pl.pallas_call(
        _attn_sample_kernel,
        grid=(B,),
        in_specs=[new, old, old, pl.BlockSpec(bias.shape, lambda b: (0, 0, 0))],
        out_specs=new,
        out_shape=jax.ShapeDtypeStruct((B, T, A_WIDTH), BF16),
        compiler_params=_params(("arbitrary",)),
        name="attn_sample",
    )(aq, kk, vv, bias)


def _gla_kernel(q_ref, k_ref, v_ref, la_ref, gr_ref, gain_ref, s0_ref, o_ref, sf_ref, s_scr, *, C, n_chunks,
                n_group):
    j = pl.program_id(1)

    @pl.when(j == 0)
    def _():
        s_scr[...] = s0_ref[0]

    n_sub = C // G_SUB
    ri = lax.broadcasted_iota(I32, (C, C), 0)
    ci = lax.broadcasted_iota(I32, (C, C), 1)
    tril = (ci <= ri).astype(BF16)
    lane_kw = lax.broadcasted_iota(I32, (1, G_KW), 1)
    head_of_lane = lane_kw // G_DK
    row_kw = lax.broadcasted_iota(I32, (C, G_KW), 0)
    ur = lax.broadcasted_iota(I32, (2 * C, 4 * C), 0) - C
    uc = lax.broadcasted_iota(I32, (2 * C, 4 * C), 1)
    u_mat = ((ur >= 0) & ((uc >= C) | (ur <= uc))).astype(BF16)

    def heads_on_rows(x):
        return jnp.concatenate([jnp.where(head_of_lane == h, x, 0.0) for h in range(G_HEADS)], axis=0)

    def chunk_group(g, S):
        ns = range(n_group)
        rs = [pl.multiple_of((g * n_group + n) * C, C) for n in ns]
        q = [q_ref[0, pl.ds(r, C), :].astype(F32) for r in rs]
        k = [k_ref[0, pl.ds(r, C), :].astype(F32) for r in rs]
        v = [v_ref[0, pl.ds(r, C), :] for r in rs]
        la = [la_ref[0, pl.ds(r, C), :] for r in rs]

        split = [_split_bf16(x) for x in la]
        b = [jnp.dot(tril, hi, preferred_element_type=F32) + jnp.dot(tril, lo, preferred_element_type=F32)
             for hi, lo in split]
        xt = [jnp.concatenate([k[n], la[n]], axis=0).T for n in ns]
        split_t = [_split_bf16(x) for x in xt]
        xb = [jnp.dot(hi, u_mat, preferred_element_type=F32) + jnp.dot(lo, u_mat, preferred_element_type=F32)
              for hi, lo in split_t]
        b_last = [x[:, LANES:] for x in xb]
        kd = [(xt[n] * jnp.exp(b_last[n] - xb[n][:, :LANES])).astype(BF16) for n in ns]
        zeros_v = jnp.zeros((C, G_VW), BF16)
        kv = [jnp.dot(kd[n], jnp.concatenate([v[n], zeros_v], axis=0), preferred_element_type=F32)
              for n in ns]
        kv_d = [jnp.concatenate([x[h * G_DK:(h + 1) * G_DK, h * G_DV:(h + 1) * G_DV] for h in range(G_HEADS)],
                                axis=0) for x in kv]

        states = [S]
        for n in ns:
            states.append(jnp.exp(b_last[n]) * states[n] + kv_d[n])

        r_inter = [jnp.dot(heads_on_rows(q[n] * jnp.exp(b[n])).astype(BF16), states[n].astype(BF16),
                           preferred_element_type=F32) for n in ns]
        o = [jnp.concatenate([x[h * C:(h + 1) * C] for h in range(G_HEADS)], axis=1) for x in r_inter]

        o_rows = [[] for _ in ns]
        for i in range(n_sub):
            r0, r1 = i * G_SUB, (i + 1) * G_SUB
            atts = []
            for n in ns:
                bs = b[n][r0 - 1:r0] if i > 0 else jnp.zeros((1, G_KW), F32)
                qe = q[n][r0:r1] * jnp.exp(b[n][r0:r1] - bs)
                ke = (k[n] * jnp.exp(jnp.where(row_kw < r1, bs - b[n], -jnp.inf))).astype(BF16)
                att = lax.dot_general(heads_on_rows(qe).astype(BF16), ke, (((1,), (1,)), ((), ())),
                                      preferred_element_type=F32)
                tt = lax.broadcasted_iota(I32, att.shape, 0) % G_SUB + r0
                ss = lax.broadcasted_iota(I32, att.shape, 1)
                atts.append(jnp.where(ss <= tt, att, 0.0).astype(BF16))
            for n in ns:
                ov = jnp.dot(atts[n], v[n], preferred_element_type=F32)
                o_rows[n].append(jnp.concatenate(
                    [ov[h * G_SUB:(h + 1) * G_SUB, h * G_DV:(h + 1) * G_DV] for h in range(G_HEADS)], axis=1))

        for n in ns:
            on = o[n] + jnp.concatenate(o_rows[n], axis=0)
            on = _head_rms(on, G_DV) * gain_ref[...] * gr_ref[0, pl.ds(rs[n], C), :].astype(F32)
            o_ref[0, pl.ds(rs[n], C), :] = on.astype(BF16)
        return states[-1]

    s_scr[...] = lax.fori_loop(0, n_chunks // n_group, chunk_group, s_scr[...])

    @pl.when(j == pl.num_programs(1) - 1)
    def _():
        sf_ref[0] = s_scr[...]


def _gla(gq, gk, gv, la, gr, gain, s0):
    B, T, _ = gq.shape
    C = CHUNK
    tb = min(ROW_TILE, T)
    assert T % tb == 0 and tb % C == 0 and C % G_SUB == 0 and 2 * C == LANES
    kw = pl.BlockSpec((1, tb, G_KW), lambda b, j: (b, j, 0))
    vw = pl.BlockSpec((1, tb, G_VW), lambda b, j: (b, j, 0))
    st = pl.BlockSpec((1, G_KW, G_DV), lambda b, j: (b, 0, 0))
    return pl.pallas_call(
        functools.partial(_gla_kernel, C=C, n_chunks=tb // C, n_group=8 if (tb // C) % 8 == 0 else 1),
        grid=(B, T // tb),
        in_specs=[kw, kw, vw, kw, vw, pl.BlockSpec(gain.shape, lambda b, j: (0, 0)), st],
        out_specs=[vw, st],
        out_shape=[jax.ShapeDtypeStruct((B, T, G_VW), BF16), jax.ShapeDtypeStruct((B, G_KW, G_DV), F32)],
        scratch_shapes=[pltpu.VMEM((G_KW, G_DV), F32)],
        compiler_params=_params(("arbitrary", "arbitrary")),
        name="gla",
    )(gq, gk, gv, la, gr, gain, s0)


def _memkv_kernel(mem_ref, gm_ref, w_ref, gk_ref, k_ref, v_ref):
    x = mem_ref[0]
    h = (x * lax.rsqrt(jnp.mean(x * x, -1, keepdims=True) + EPS) * gm_ref[...]).astype(BF16)
    k = jnp.dot(h, w_ref[:, :C_WIDTH], preferred_element_type=F32)
    k_ref[0] = _head_rms(k, C_HEAD_DIM) * gk_ref[...]
    v_ref[0] = jnp.dot(h, w_ref[:, C_WIDTH:], preferred_element_type=F32)


def _memkv(mem, gm, w_kv, gk):
    B, M, _ = mem.shape
    out = pl.BlockSpec((1, M, C_WIDTH), lambda b: (b, 0, 0))
    return pl.pallas_call(
        _memkv_kernel,
        grid=(B,),
        in_specs=[pl.BlockSpec((1, M, D_MODEL), lambda b: (b, 0, 0)),
                  pl.BlockSpec(gm.shape, lambda b: (0, 0)),
                  pl.BlockSpec(w_kv.shape, lambda b: (0, 0)),
                  pl.BlockSpec(gk.shape, lambda b: (0, 0))],
        out_specs=[out, out],
        out_shape=[jax.ShapeDtypeStruct((B, M, C_WIDTH), F32)] * 2,
        compiler_params=_params(("arbitrary",)),
        name="memkv",
    )(mem, gm, w_kv, gk)


def _merge_kernel(x_ref, ya_ref, yb_ref, cq_ref, gt_ref, mk_ref, mv_ref, wa_ref, wb_ref, wc_ref,
                  wo_ref, nf_ref, wr_ref, br_ref,
                  x1_ref, hp_ref, ri_ref, rw_ref, cnt_ref, run_scr, *, tok_base):
    first =jnp.logical_and(pl.program_id(0) == 0, pl.program_id(1) == 0)

    @pl.when(first)
    def _():
        run_scr[...] = jnp.zeros_like(run_scr)

    tm = x_ref.shape[1]
    n_part = 2 if tm % 256 == 0 else 1
    parts = [slice(n * (tm // n_part), (n + 1) * (tm // n_part)) for n in range(n_part)]
    heads = [slice(h * C_HEAD_DIM, (h + 1) * C_HEAD_DIM) for h in range(C_HEADS)]
    mk = [mk_ref[0, :, c].astype(BF16) for c in heads]
    mv = [mv_ref[0, :, c].astype(BF16) for c in heads]
    scores = [[lax.dot_general(cq_ref[0, r, c], k, (((1,), (1,)), ((), ())), preferred_element_type=F32)
               * (C_HEAD_DIM ** -0.5) for c, k in zip(heads, mk)] for r in parts]
    probs = [[jnp.exp(s - s.max(-1, keepdims=True)) for s in ss] for ss in scores]
    sums = [[p.sum(-1, keepdims=True) for p in ps] for ps in probs]
    yc_in = [jnp.concatenate([jnp.dot(p.astype(BF16), v, preferred_element_type=F32) / l
                              for p, l, v in zip(ps, ls, mv)], axis=-1).astype(BF16)
             for ps, ls in zip(probs, sums)]
    y_a = [jnp.dot(ya_ref[0, r, :], wa_ref[...], preferred_element_type=F32) for r in parts]
    y_b = [jnp.dot(yb_ref[0, r, :], wb_ref[...], preferred_element_type=F32) for r in parts]
    y_c = [jnp.dot(y, wc_ref[...], preferred_element_type=F32) for y in yc_in]
    merged = [(gt_ref[0, r, 0:D_MODEL].astype(F32) * a
               + gt_ref[0, r, D_MODEL:2 * D_MODEL].astype(F32) * b
               + gt_ref[0, r, 2 * D_MODEL:3 * D_MODEL].astype(F32) * c).astype(BF16)
              for r, a, b, c in zip(parts, y_a, y_b, y_c)]
    x1s = [x_ref[0, r, :] + jnp.dot(m, wo_ref[...], preferred_element_type=F32) for r, m in zip(parts, merged)]
    for r, v in zip(parts, x1s):
        x1_ref[0, r, :] = v
    h2s = [v * lax.rsqrt(jnp.mean(v * v, -1, keepdims=True) + EPS) * nf_ref[...] for v in x1s]

    logits = jnp.concatenate([jnp.dot(h.astype(BF16), wr_ref[...], preferred_element_type=F32) for h in h2s],
                             axis=0) + br_ref[...]
    h2 = jnp.concatenate(h2s, axis=0)
    lane = lax.broadcasted_iota(I32, (tm, LANES), 1)
    lane_f = lane.astype(F32)
    vals, sels, idxs = [], [], []
    l = logits
    for _ in range(TOP_K):
        m = l.max(-1, keepdims=True)
        idx = jnp.min(jnp.where(l == m, lane_f, float(LANES)), -1, keepdims=True)
        sel = lane_f == idx
        vals.append(m)
        idxs.append(idx)
        sels.append(sel)
        l = jnp.where(sel, -3e38, l)
    es = [jnp.exp(vk - vals[0]) for vk in vals]
    den = es[0] + es[1] + es[2] + es[3]
    cnt = jnp.zeros((tm, LANES), F32)
    for sel in sels:
        cnt = cnt + jnp.where(sel, 1.0, 0.0)
    tp = max(tm, LANES)
    cnt_p = cnt if tp == tm else jnp.concatenate([cnt, jnp.zeros((tp - tm, LANES), F32)], axis=0)
    ri = lax.broadcasted_iota(I32, (tp, tp), 0)
    ci = lax.broadcasted_iota(I32, (tp, tp), 1)
    before = jnp.dot(jnp.where(ci < ri, 1.0, 0.0).astype(BF16), cnt_p.astype(BF16),
                     preferred_element_type=F32)[0:tm] + run_scr[0:1, :]
    r_i = jnp.zeros((tm, LANES), I32)
    r_w = jnp.zeros((tm, LANES), F32)
    for kk in range(TOP_K):
        rank = jnp.sum(jnp.where(sels[kk], before, 0.0), -1, keepdims=True)
        r_i = jnp.where(lane == kk, idxs[kk].astype(I32), r_i)
        r_i = jnp.where(lane == TOP_K + kk, rank.astype(I32), r_i)
        r_w = jnp.where(lane == kk, es[kk] / den, r_w)
    ri_ref[0] = r_i
    rw_ref[0] = r_w
    run_scr[...] = run_scr[...] + jnp.sum(cnt, axis=0, keepdims=True)
    cnt_ref[...] = run_scr[...]

    tok = tok_base + (pl.program_id(0) * pl.num_programs(1) + pl.program_id(1)) * tm \
        + lax.broadcasted_iota(I32, (tm, LANES), 0)
    meta = jnp.where(lane == 0, tok, 0)
    for kk in range(TOP_K):
        meta = jnp.where(lane == 1 + kk, idxs[kk].astype(I32), meta)
    packed = _pack_rows(h2)
    zero = jnp.zeros((tm, LANES), U32)
    slabs = [packed[:, j * LANES:(j + 1) * LANES] for j in range(SLABS)]
    slabs += [lax.bitcast_convert_type(meta, U32)] + [zero] * (ROW_SLABS - SLABS - 1)
    hp_ref[0] = pltpu.einshape("jtl->tjl", jnp.stack(slabs, axis=0))


def _merge(x, ya, yb, cq, gt, mk, mv, wa, wb, wc, wo, nf, wr, br, tok_base):
    B, T, _ = x.shape
    tm = min(ROW_TILE, T)
    assert T % tm == 0

    def row(width):
        return pl.BlockSpec((1, tm, width), lambda b, j: (b, j, 0))

    def full(a):
        return pl.BlockSpec(a.shape, lambda b, j: (0,) * a.ndim)

    mem = pl.BlockSpec((1, N_MEM, C_WIDTH), lambda b, j: (b, 0, 0))
    return pl.pallas_call(
        functools.partial(_merge_kernel, tok_base=tok_base),
        grid=(B, T // tm),
        in_specs=[row(D_MODEL), row(A_WIDTH), row(G_VW), row(C_WIDTH), row(N_BRANCH * D_MODEL), mem, mem,
                  full(wa), full(wb), full(wc), full(wo), full(nf), full(wr), full(br)],
        out_specs=[row(D_MODEL), pl.BlockSpec((1, tm, ROW_SLABS, LANES), lambda b, j: (b, j, 0, 0)), row(LANES),
                   row(LANES), pl.BlockSpec((8, LANES), lambda b, j: (0, 0))],
        out_shape=[jax.ShapeDtypeStruct((B, T, D_MODEL), F32), jax.ShapeDtypeStruct((B, T, ROW_SLABS, LANES), U32),
                   jax.ShapeDtypeStruct((B, T, LANES), I32), jax.ShapeDtypeStruct((B, T, LANES), F32),
                   jax.ShapeDtypeStruct((8, LANES), F32)],
        scratch_shapes=[pltpu.VMEM((8, LANES), F32)],
        compiler_params=_params(("arbitrary", "arbitrary")),
        name="merge",
    )(x, ya, yb, cq, gt, mk, mv, wa, wb, wc, wo, nf, wr, br)


def _dispatch_kernel(dest_ref, dest2_ref, pad_lo_ref, pad_n_ref, hp_ref, hp2_ref, xs_ref, zero_scr, sem, sem2,
                     psem, *, tm, tm2, n_pad, n_tokens):
    def scatter(dref, href, s, n):
        def issue(t, c):
            for kk in range(TOP_K):
                pltpu.make_async_copy(href.at[t], xs_ref.at[dref[0, 0, t * TOP_K + kk]], s).start(priority=kk % 2)
            return c
        lax.fori_loop(0, n, issue, 0, unroll=8)

    def drain(href, s, n):
        for _ in range(TOP_K):
            pltpu.make_async_copy(href, xs_ref.at[pl.ds(0, n)], s).wait()

    scatter(dest_ref, hp_ref, sem, tm)

    @pl.when(pl.program_id(0) == 0)
    def _():
        scatter(dest2_ref, hp2_ref, sem2, tm2)
        sub = lax.broadcasted_iota(I32, (ROW_SLABS, LANES), 0)
        lane = lax.broadcasted_iota(I32, (ROW_SLABS, LANES), 1)
        filler = jnp.where((sub == META) & (lane == 0), n_tokens,
                           jnp.where((sub == META) & (lane <= TOP_K), N_EXPERTS, 0))
        zero_scr[...] = lax.bitcast_convert_type(filler, U32)

        def pad_copy(e, n):
            return pltpu.make_async_copy(zero_scr, xs_ref.at[pad_lo_ref[e] + n], psem)

        def fill(e, c):
            lax.fori_loop(0, pad_n_ref[e], lambda n, cc: (pad_copy(e, n).start(), cc)[1], 0)
            return c

        def fill_wait(e, c):
            lax.fori_loop(0, pad_n_ref[e], lambda n, cc: (pad_copy(e, n).wait(), cc)[1], 0)
            return c

        lax.fori_loop(0, n_pad, fill, 0)
        lax.fori_loop(0, n_pad, fill_wait, 0)
        drain(hp2_ref, sem2, tm2)

    drain(hp_ref, sem, tm)


def _dispatch(dest, hp, dest2, hp2, pad_lo, pad_n, rows):
    N, N2 = hp.shape[0], hp2.shape[0]
    tm = min(ROW_TILE, N)
    assert N % tm == 0
    n_steps = N // tm
    smem = functools.partial(pl.BlockSpec, memory_space=pltpu.SMEM)
    n_pad = pad_lo.shape[0]
    return pl.pallas_call(
        functools.partial(_dispatch_kernel, tm=tm, tm2=N2, n_pad=n_pad, n_tokens=N + N2),
        grid=(n_steps,),
        in_specs=[smem((1, 1, tm * TOP_K), lambda i: (i, 0, 0)),
                  smem((1, 1, N2 * TOP_K), lambda i: (0, 0, 0)),
                  smem((n_pad,), lambda i: (0,)), smem((n_pad,), lambda i: (0,)),
                  pl.BlockSpec((tm, ROW_SLABS, LANES), lambda i: (i, 0, 0)),
                  pl.BlockSpec((N2, ROW_SLABS, LANES), lambda i: (0, 0, 0))],
        out_specs=pl.BlockSpec(memory_space=pl.ANY),
        out_shape=jax.ShapeDtypeStruct((rows, ROW_SLABS, LANES), U32),
        scratch_shapes=[pltpu.VMEM((ROW_SLABS, LANES), U32), pltpu.SemaphoreType.DMA(()),
                        pltpu.SemaphoreType.DMA(()), pltpu.SemaphoreType.DMA(())],
        compiler_params=_params(("arbitrary",)),
        name="moe_dispatch",
    )(dest.reshape(n_steps, 1, tm * TOP_K), dest2.reshape(1, 1, N2 * TOP_K), pad_lo, pad_n, hp, hp2)


def _ffn_kernel(be_ref, nu_ref, x_ref, wgu_ref, bgu_ref, wd_ref, bd_ref, y4_ref,
                wgu_bf, wd_bf, ybuf, idv, ids_smem, sem_ids, sem_rows, *, n_blk, n_tokens):
    i = pl.program_id(0)
    n_used = nu_ref[0]
    n_assign = n_tokens * TOP_K
    prev = be_ref[jnp.maximum(i - 1, 0)]
    new_expert = jnp.logical_or(i == 0, be_ref[i] != prev)

    def rows_done(s):
        pltpu.make_async_copy(ybuf.at[s], y4_ref.at[pl.ds(0, FFN_BLOCK)], sem_rows.at[s]).wait()

    def ids_copy(s):
        return pltpu.make_async_copy(idv, ids_smem.at[s], sem_ids)

    def start_rows(s, lo, hi):
        for r in range(lo, hi):
            pltpu.make_async_copy(ybuf.at[s, r], y4_ref.at[ids_smem[s, 0, r]],
                                  sem_rows.at[s]).start(priority=r % 2)

    @pl.when(i == 0)
    def _():
        ybuf[1] = jnp.zeros(ybuf.shape[1:], U32)
        spare = pltpu.make_async_copy(ybuf.at[1], y4_ref.at[pl.ds(n_assign, FFN_BLOCK)], sem_rows.at[1])
        spare.start()
        spare.wait()

    @pl.when(new_expert)
    def _():
        wgu_bf[...] = wgu_ref[0].astype(BF16)
        wd_bf[...] = wd_ref[0].astype(BF16)

    def compute(s, flush):
        q = FFN_BLOCK // 4
        if flush:
            ids_copy(1 - s).wait()
            start_rows(1 - s, 0, q)
        xm = pltpu.einshape("tjl->jtl", x_ref[...])
        x_lo, x_hi = _unpack_rows(jnp.concatenate([xm[j] for j in range(SLABS)], axis=-1))
        meta = lax.bitcast_convert_type(xm[META], I32)
        tok = meta[:, 0:1]
        choice = jnp.zeros_like(tok)
        for kk in range(1, TOP_K):
            choice = jnp.where(meta[:, 1 + kk:2 + kk] == be_ref[i], kk, choice)
        local = lax.broadcasted_iota(I32, tok.shape, 0)
        dest = jnp.where(tok >= n_tokens, n_assign + local, tok * TOP_K + choice)
        dest_t = jnp.broadcast_to(dest.astype(F32), (FFN_BLOCK, LANES)).T
        idv[...] = dest_t[0:8].astype(I32)
        ids_copy(s).start()
        gu = (jnp.dot(x_lo.astype(BF16), wgu_bf[0:HALF, :], preferred_element_type=F32)
              + jnp.dot(x_hi.astype(BF16), wgu_bf[HALF:, :], preferred_element_type=F32)
              + bgu_ref[0])
        if flush:
            start_rows(1 - s, q, 2 * q)
        gate = jnp.minimum(gu[:, :D_EXPERT], SWIGLU_LIMIT)
        up = jnp.clip(gu[:, D_EXPERT:], -SWIGLU_LIMIT, SWIGLU_LIMIT)
        act = (up + 1.0) * (gate * _sigmoid(SWIGLU_ALPHA * gate))
        if flush:
            start_rows(1 - s, 2 * q, 3 * q)
        y = jnp.dot(act.astype(BF16), wd_bf[...], preferred_element_type=F32) + bd_ref[0]
        if flush:
            start_rows(1 - s, 3 * q, 4 * q)
        packed = _pack_rows(y)
        pl.when(i >= 2)(functools.partial(rows_done, s))
        _store_slabs(ybuf, (s,), packed)

    def flush_only(s, also):
        ids_copy(s).wait()
        start_rows(s, 0, FFN_BLOCK)
        pl.when(also)(functools.partial(rows_done, 1 - s))
        rows_done(s)

    real = i < n_used
    prev_real = jnp.logical_and(i >= 1, i - 1 < n_used)
    for s in range(2):
        mine = i % 2 == s
        pl.when(mine & real & prev_real)(functools.partial(compute, s, True))
        pl.when(mine & real & jnp.logical_not(prev_real))(functools.partial(compute, s, False))
        pl.when(mine & jnp.logical_not(real) & prev_real)(functools.partial(flush_only, 1 - s, i >= 2))
        pl.when(mine & real & (i == n_blk - 1))(functools.partial(flush_only, s, i >= 1))


def _ffn(blk_expert, n_used, xs, n_tokens, w_gu, b_gu, w_d, b_d):
    P = xs.shape[0]
    nblk = P // FFN_BLOCK
    grid_spec = pltpu.PrefetchScalarGridSpec(
        num_scalar_prefetch=2,
        grid=(nblk,),
        in_specs=[pl.BlockSpec((FFN_BLOCK, ROW_SLABS, LANES), lambda i, be, nu: (jnp.minimum(i, nu[0] - 1), 0, 0)),
                  pl.BlockSpec((1, D_MODEL, 2 * D_EXPERT), lambda i, be, nu: (be[i], 0, 0)),
                  pl.BlockSpec((1, 1, 2 * D_EXPERT), lambda i, be, nu: (be[i], 0, 0)),
                  pl.BlockSpec((1, D_EXPERT, D_MODEL), lambda i, be, nu: (be[i], 0, 0)),
                  pl.BlockSpec((1, 1, D_MODEL), lambda i, be, nu: (be[i], 0, 0))],
        out_specs=pl.BlockSpec(memory_space=pl.ANY),
        scratch_shapes=[pltpu.VMEM((D_MODEL, 2 * D_EXPERT), BF16), pltpu.VMEM((D_EXPERT, D_MODEL), BF16),
                        pltpu.VMEM((2, FFN_BLOCK, SLABS, LANES), U32), pltpu.VMEM((8, FFN_BLOCK), I32),
                        pltpu.SMEM((2, 8, FFN_BLOCK), I32), pltpu.SemaphoreType.DMA(()),
                        pltpu.SemaphoreType.DMA((2,))],
    )
    return pl.pallas_call(
        functools.partial(_ffn_kernel, n_blk=nblk, n_tokens=n_tokens),
        grid_spec=grid_spec,
        out_shape=jax.ShapeDtypeStruct((n_tokens * TOP_K + FFN_BLOCK, SLABS, LANES), U32),
        compiler_params=_params(("arbitrary",)),
        name="moe_ffn",
    )(blk_expert, n_used, xs, w_gu, b_gu.reshape(N_EXPERTS, 1, -1), w_d, b_d.reshape(N_EXPERTS, 1, -1))


def _combine_kernel(x1_ref, rw_ref, y_ref, o_ref, *, tm):
    w = rw_ref[...]
    acc_lo = x1_ref[:, :HALF]
    acc_hi = x1_ref[:, HALF:]
    for kk in range(TOP_K):
        rows = y_ref[pl.ds(kk, tm, stride=TOP_K)]
        slab_major = pltpu.einshape("tjl->jtl", rows)
        lo, hi = _unpack_rows(jnp.concatenate([slab_major[j] for j in range(SLABS)], axis=-1))
        wk = w[:, kk:kk + 1]
        acc_lo = acc_lo + wk * lo
        acc_hi = acc_hi + wk * hi
    o_ref[:, :HALF] = acc_lo
    o_ref[:, HALF:] = acc_hi


def _combine(x1, rw, y4, tok_base):
    N = x1.shape[0]
    tm = min(256, N)
    assert N % tm == 0 and tok_base % tm == 0
    first = tok_base // tm
    return pl.pallas_call(
        functools.partial(_combine_kernel, tm=tm),
        grid=(N // tm,),
        in_specs=[pl.BlockSpec((tm, D_MODEL), lambda i: (i, 0)),
                  pl.BlockSpec((tm, LANES), lambda i: (i, 0)),
                  pl.BlockSpec((tm * TOP_K, SLABS, LANES), lambda i: (first + i, 0, 0))],
        out_specs=pl.BlockSpec((tm, D_MODEL), lambda i: (i, 0)),
        out_shape=jax.ShapeDtypeStruct((N, D_MODEL), F32),
        compiler_params=_params(("arbitrary",)),
        name="moe_combine",
    )(x1, rw, y4)


def _moe(groups, w_gu, b_gu, w_d, b_d):
    n_tokens = sum(g[0].shape[0] for g in groups)
    n_assign = n_tokens * TOP_K
    nblk = (n_assign + N_EXPERTS * (FFN_BLOCK - 1) + FFN_BLOCK - 1) // FFN_BLOCK
    counts = [g[4].astype(I32) for g in groups]
    total = sum(counts)
    padded = (total + FFN_BLOCK - 1) // FFN_BLOCK * FFN_BLOCK
    pend = jnp.cumsum(padded)
    pstart = pend - padded
    n_used = pend[-1:] // FFN_BLOCK
    blk = jnp.minimum(jnp.arange(nblk, dtype=I32), n_used[0] - 1) * FFN_BLOCK
    blk_expert = jnp.minimum(jnp.sum(pend[None, :] <= blk[:, None], axis=1), N_EXPERTS - 1).astype(I32)

    experts = jnp.arange(N_EXPERTS, dtype=I32)
    dests = []
    base = pstart
    for g, c in zip(groups, counts):
        idx, rank = g[2][:, :TOP_K], g[2][:, TOP_K:2 * TOP_K]
        dests.append(jnp.sum(jnp.where(idx[..., None] == experts, base, 0), axis=-1) + rank)
        base = base + c

    rows = nblk * FFN_BLOCK
    pad_lo = jnp.concatenate([pstart + total, pend[-1:]])
    pad_n = jnp.concatenate([padded - total, rows - pend[-1:]])
    (g_main, g_small), (d_main, d_small) = groups, dests
    xs = _dispatch(d_main, g_main[1], d_small, g_small[1], pad_lo, pad_n, rows)
    y4 = _ffn(blk_expert, n_used.astype(I32), xs, n_tokens, w_gu, b_gu, w_d, b_d)
    return [_combine(g_main[0], g_main[3], y4, 0), _combine(g_small[0], g_small[3], y4, g_main[0].shape[0])]


def _tile_lanes(g, reps):
    return jnp.tile(g.astype(F32), reps)[None, :]


def kernel(x_prompt, x_sample, mem_prompt, cache_attn_k, cache_attn_v, state_gla, cache_mem_k, cache_mem_v, norm_mix, w_in, a_q_norm, a_k_norm, rel_bias_table, w_a_o, w_gla_a_up, b_gla_a, gla_out_norm, w_b_o, c_q_norm, c_k_norm, norm_mem, w_mem_kv, w_c_o, b_gate, w_out, norm_ffn, w_router, b_router, w_gate_up, b_gate_up, w_down, b_down):
    depth = norm_mix.shape[0]
    assert depth == 1
    l = 0
    B, S, _ = x_prompt.shape
    Bs, Ts, _ = x_sample.shape
    keep = min(WINDOW, S)

    w = w_in[l]
    sizes = (A_WIDTH, A_WIDTH, A_WIDTH, G_KW, G_KW, G_VW, G_VW, G_RANK, C_WIDTH, N_BRANCH * D_MODEL)
    offs = [0]
    for s_ in sizes:
        offs.append(offs[-1] + s_)
    seg = [w[:, offs[i]:offs[i + 1]] for i in range(len(sizes))]
    w_r = jnp.concatenate(seg[0:7] + [seg[8], seg[9], seg[7], jnp.zeros((D_MODEL, LANES - G_RANK), F32)],
                          axis=1).astype(BF16)
    nm = norm_mix[l][None, :]
    aqn = _tile_lanes(a_q_norm[l], A_HEADS)
    akn = _tile_lanes(a_k_norm[l], A_HEADS)
    cqn = _tile_lanes(c_q_norm[l], C_HEADS)
    ckn = _tile_lanes(c_k_norm[l], C_HEADS)
    gon = _tile_lanes(gla_out_norm[l], G_HEADS)
    wup = jnp.concatenate([w_gla_a_up[l], jnp.zeros((LANES - G_RANK, G_KW), F32)], axis=0).astype(BF16)
    bla = b_gla_a[l][None, :]
    bg = b_gate[l][None, :]
    wa, wb, wc, wo = (t[l].astype(BF16) for t in (w_a_o, w_b_o, w_c_o, w_out))
    nf = norm_ffn[l][None, :]
    wr = jnp.concatenate([w_router[l], jnp.zeros((D_MODEL, LANES - N_EXPERTS), F32)], axis=1).astype(BF16)
    br = jnp.concatenate([b_router[l], jnp.full((LANES - N_EXPERTS,), NEG_INF, F32)])[None, :]
    table = rel_bias_table[l]

    mk, mv = _memkv(mem_prompt, norm_mem[l][None, :], w_mem_kv[l].astype(BF16), ckn)
    (aq, ak, av, gq, gk, gv, gr, la, cq, gt, ak_tail, av_tail) = _inproj(
        x_prompt, keep, nm, w_r, aqn, akn, cqn, wup, bla, bg)
    ya = _attn_prompt(aq, ak, av, table)
    yb, s_prompt = _gla(gq, gk, gv, la, gr, gon, jnp.zeros((B, G_KW, G_DV), F32))
    x1_p, hp_p, ri_p, rw_p, cnt_p = _merge(x_prompt, ya, yb, cq, gt, mk, mv, wa, wb, wc, wo, nf, wr, br, 0)

    (aq, ak, av, gq, gk, gv, gr, la, cq, gt, ak_new, av_new) = _inproj(
        x_sample.reshape(1, Bs * Ts, D_MODEL), Bs * Ts, nm, w_r, aqn, akn, cqn, wup, bla, bg)
    rs = lambda t: t.reshape(Bs, Ts, t.shape[-1])
    P = cache_attn_k.shape[2]
    ya = _attn_sample(rs(aq), rs(ak), rs(av), cache_attn_k[l].reshape(Bs, P, A_WIDTH),
                      cache_attn_v[l].reshape(Bs, P, A_WIDTH), table)
    t_pad = (Ts + CHUNK - 1) // CHUNK * CHUNK
    zp = lambda t: jnp.pad(rs(t), ((0, 0), (0, t_pad - Ts), (0, 0)))
    yb, s_sample = _gla(zp(gq), zp(gk), zp(gv), zp(la), zp(gr), gon, state_gla[l].reshape(Bs, G_KW, G_DV))
    yb = yb[:, :Ts]
    x1_s, hp_s, ri_s, rw_s, cnt_s = _merge(
        x_sample, ya, yb, rs(cq), rs(gt), cache_mem_k[l].reshape(Bs, N_MEM, C_WIDTH),
        cache_mem_v[l].reshape(Bs, N_MEM, C_WIDTH), wa, wb, wc, wo, nf, wr, br, B * S)

    flat = lambda t: t.reshape((-1,) + t.shape[2:])
    y_p, y_s = _moe(
        [(flat(x1_p), flat(hp_p), flat(ri_p), flat(rw_p), cnt_p[0, :N_EXPERTS]),
         (flat(x1_s), flat(hp_s), flat(ri_s), flat(rw_s), cnt_s[0, :N_EXPERTS])],
        w_gate_up[l], b_gate_up[l], w_down[l], b_down[l])

    return (y_p.reshape(B, S, D_MODEL), y_s.reshape(Bs, Ts, D_MODEL),
            ak_tail.reshape(1, B, keep, A_HEADS, A_HEAD_DIM), av_tail.reshape(1, B, keep, A_HEADS, A_HEAD_DIM),
            s_prompt.reshape(1, B, G_HEADS, G_DK, G_DV),
            mk.reshape(1, B, N_MEM, C_HEADS, C_HEAD_DIM), mv.reshape(1, B, N_MEM, C_HEADS, C_HEAD_DIM),
            ak_new.reshape(1, Bs, Ts, A_HEADS, A_HEAD_DIM), av_new.reshape(1, Bs, Ts, A_HEADS, A_HEAD_DIM),
            s_sample.reshape(1, Bs, G_HEADS, G_DK, G_DV))
```

```python
import functools

import jax
import jax.numpy as jnp
from jax import lax
from jax.experimental import pallas as pl
from jax.experimental.pallas import tpu as pltpu

F32 = jnp.float32
BF16 = jnp.bfloat16
U32 = jnp.uint32
I32 = jnp.int32

D_MODEL = 1024
CHUNK = 64
BAND_CHUNKS = 8
WINDOW = BAND_CHUNKS * CHUNK
N_MEM = 256
A_HEADS, A_HEAD_DIM = 8, 64
A_WIDTH = A_HEADS * A_HEAD_DIM
REL_MAX = 128
G_HEADS, G_DK, G_DV = 4, 64, 128
G_KW, G_VW = G_HEADS * G_DK, G_HEADS * G_DV
G_RANK = 16
G_TAU = 16.0
G_SUB = 16
C_HEADS, C_HEAD_DIM = 4, 128
C_WIDTH = C_HEADS * C_HEAD_DIM
N_BRANCH = 3
N_EXPERTS = 32
TOP_K = 4
D_EXPERT = 1024
SWIGLU_LIMIT = 7.0
SWIGLU_ALPHA = 1.702
EPS = 1e-6
NEG_INF = -1e30

LANES = 128
HALF = D_MODEL // 2
ROW_TILE = 512
ATTN_SUB = 128
FFN_BLOCK = 512
VMEM_LIMIT = 56 * 1024 * 1024

OFF_AQ, OFF_AK, OFF_AV = 0, 512, 1024
OFF_GQ, OFF_GK, OFF_GV, OFF_GR = 1536, 1792, 2048, 2560
OFF_CQ, OFF_GATE, OFF_LR = 3072, 3584, 6656
IN_COLS = OFF_LR + LANES


def _params(sem):
    return pltpu.CompilerParams(dimension_semantics=sem, vmem_limit_bytes=VMEM_LIMIT)


def _sigmoid(x):
    return 0.5 * jnp.tanh(0.5 * x) + 0.5


def _head_rms(y, head_dim):
    cols = []
    for p in range(y.shape[1] // LANES):
        blk = y[:, p * LANES:(p + 1) * LANES]
        sq = blk * blk
        if head_dim == LANES:
            sc = lax.rsqrt(jnp.sum(sq, -1, keepdims=True) * (1.0 / LANES) + EPS)
        else:
            lo = lax.broadcasted_iota(I32, blk.shape, 1) < head_dim
            s_lo = jnp.sum(jnp.where(lo, sq, 0.0), -1, keepdims=True)
            s_hi = jnp.sum(jnp.where(lo, 0.0, sq), -1, keepdims=True)
            sc = jnp.where(lo, lax.rsqrt(s_lo * (1.0 / head_dim) + EPS),
                           lax.rsqrt(s_hi * (1.0 / head_dim) + EPS))
        cols.append(blk * sc)
    return jnp.concatenate(cols, axis=-1)


def _split_bf16(x):
    hi = x.astype(BF16)
    lo = (x - hi.astype(F32)).astype(BF16)
    return hi, lo


def _pack_rows(x):
    lo = lax.bitcast_convert_type(x[:, :HALF].astype(BF16).astype(F32), U32)
    hi = lax.bitcast_convert_type(x[:, HALF:].astype(BF16).astype(F32), U32)
    return (lo >> 16) | (hi & jnp.uint32(0xFFFF0000))


def _unpack_rows(u):
    lo = lax.bitcast_convert_type(u << 16, F32)
    hi = lax.bitcast_convert_type(u & jnp.uint32(0xFFFF0000), F32)
    return lo, hi


SLABS = HALF // LANES
ROW_SLABS = 8
META = SLABS


def _store_slabs(ref, lead, u):
    slab_major = jnp.stack([u[:, j * LANES:(j + 1) * LANES] for j in range(SLABS)], axis=0)
    ref[lead + (slice(None),) * 3] = pltpu.einshape("jtl->tjl", slab_major)


def _inproj_kernel(x_ref, nm_ref, w_ref, aqn_ref, akn_ref, cqn_ref, wup_ref, bla_ref, bg_ref,
                   aq_ref, ak_ref, av_ref, gq_ref, gk_ref, gv_ref, gr_ref, la_ref, cq_ref, gt_ref,
                   akt_ref, avt_ref, *, n_tiles, n_tail):
    j = pl.program_id(1)
    x = x_ref[0]
    h = (x * lax.rsqrt(jnp.mean(x * x, -1, keepdims=True) + EPS) * nm_ref[...]).astype(BF16)

    def seg(off, width):
        return jnp.dot(h, w_ref[:, off:off + width], preferred_element_type=F32)

    in_tail = j >= n_tiles - n_tail

    aq = _head_rms(seg(OFF_AQ, A_WIDTH), A_HEAD_DIM) * aqn_ref[...] * (A_HEAD_DIM ** -0.5)
    aq_ref[0] = aq.astype(BF16)

    ak = _head_rms(seg(OFF_AK, A_WIDTH), A_HEAD_DIM) * akn_ref[...]
    ak_ref[0] = ak.astype(BF16)

    @pl.when(in_tail)
    def _():
        akt_ref[0] = ak

    av = seg(OFF_AV, A_WIDTH)
    av_ref[0] = av.astype(BF16)

    @pl.when(in_tail)
    def _():
        avt_ref[0] = av

    gq_ref[0] = (seg(OFF_GQ, G_KW) * (G_DK ** -0.5)).astype(BF16)
    gk_ref[0] = seg(OFF_GK, G_KW).astype(BF16)
    gv_ref[0] = seg(OFF_GV, G_VW).astype(BF16)
    gr = seg(OFF_GR, G_VW)
    gr_ref[0] = (gr * _sigmoid(gr)).astype(BF16)

    lr = seg(OFF_LR, LANES).astype(BF16)
    z = jnp.dot(lr, wup_ref[...], preferred_element_type=F32) + bla_ref[...]
    la_ref[0] = (jnp.minimum(z, 0.0) - jnp.log1p(jnp.exp(-jnp.abs(z)))) * (1.0 / G_TAU)

    cq = _head_rms(seg(OFF_CQ, C_WIDTH), C_HEAD_DIM) * cqn_ref[...]
    cq_ref[0] = cq.astype(BF16)

    gate_chunk = 512
    for c in range(N_BRANCH * D_MODEL // gate_chunk):
        lo = c * gate_chunk
        g = seg(OFF_GATE + lo, gate_chunk) + bg_ref[:, lo:lo + gate_chunk]
        gt_ref[0, :, lo:lo + gate_chunk] = _sigmoid(g).astype(BF16)


def _inproj(x, keep, nm, w_r, aqn, akn, cqn, wup, bla, bg):
    G, R, _ = x.shape
    tm = min(ROW_TILE, R)
    n_tiles = R // tm
    n_tail = keep // tm
    assert R % tm == 0 and keep % tm == 0 and n_tail >= 1

    def row(width, dtype):
        return (jax.ShapeDtypeStruct((G, R, width), dtype),
                pl.BlockSpec((1, tm, width), lambda g, j: (g, j, 0)))

    def tail(width):
        return (jax.ShapeDtypeStruct((G, keep, width), F32),
                pl.BlockSpec((1, tm, width), lambda g, j: (g, jnp.maximum(j - (n_tiles - n_tail), 0), 0)))

    outs = [row(A_WIDTH, BF16), row(A_WIDTH, BF16), row(A_WIDTH, BF16), row(G_KW, BF16),
            row(G_KW, BF16), row(G_VW, BF16), row(G_VW, BF16), row(G_KW, F32), row(C_WIDTH, BF16),
            row(N_BRANCH * D_MODEL, BF16), tail(A_WIDTH), tail(A_WIDTH)]

    def full(a):
        return pl.BlockSpec(a.shape, lambda g, j: (0,) * a.ndim)

    return pl.pallas_call(
        functools.partial(_inproj_kernel, n_tiles=n_tiles, n_tail=n_tail),
        grid=(G, n_tiles),
        in_specs=[pl.BlockSpec((1, tm, D_MODEL), lambda g, j: (g, j, 0)), full(nm), full(w_r),
                  full(aqn), full(akn), full(cqn), full(wup), full(bla), full(bg)],
        out_specs=[o[1] for o in outs],
        out_shape=[o[0] for o in outs],
        compiler_params=_params(("arbitrary", "arbitrary")),
        name="inproj",
    )(x, nm, w_r, aqn, akn, cqn, wup, bla, bg)


def _attend(pairs, lo_mask):
    T = pairs[0][0].shape[0]
    scores = []
    for q, parts in pairs:
        zero = jnp.zeros_like(q)
        q2 = jnp.concatenate([jnp.where(lo_mask, q, zero), jnp.where(lo_mask, zero, q)], axis=0)
        ss = []
        for (k, _, bias2, valid) in parts:
            s = lax.dot_general(q2, k, (((1,), (1,)), ((), ())), preferred_element_type=F32) + bias2
            if valid is not None:
                s = jnp.where(valid, s, NEG_INF)
            ss.append(s)
        scores.append(ss)
    probs, sums = [], []
    for ss in scores:
        m = ss[0].max(-1, keepdims=True)
        for s in ss[1:]:
            m = jnp.maximum(m, s.max(-1, keepdims=True))
        ps = [jnp.exp(s - m) for s in ss]
        l = ps[0].sum(-1, keepdims=True)
        for p in ps[1:]:
            l = l + p.sum(-1, keepdims=True)
        probs.append([p.astype(BF16) for p in ps])
        sums.append(l)
    outs = []
    for (q, parts), ps, l in zip(pairs, probs, sums):
        o = jnp.dot(ps[0], parts[0][1], preferred_element_type=F32)
        for p, part in zip(ps[1:], parts[1:]):
            o = o + jnp.dot(p, part[1], preferred_element_type=F32)
        o = o / l
        outs.append(jnp.where(lo_mask, o[:T], o[T:]))
    return outs


def _attn_prompt_kernel(q_ref, kp_ref, kc_ref, vp_ref, vc_ref, bias_ref, o_ref, *, tb):
    j = pl.program_id(1)
    has_prev = j > 0
    lo_mask = lax.broadcasted_iota(I32, (ATTN_SUB, LANES), 1) < A_HEAD_DIM
    for s in range(tb // ATTN_SUB):
        r0 = s * ATTN_SUB
        len_a = tb - r0
        len_b = r0 + ATTN_SUB
        pairs = []
        for p in range(A_WIDTH // LANES):
            c0 = p * LANES
            q = q_ref[0, r0:r0 + ATTN_SUB, c0:c0 + LANES]
            parts = [
                (kp_ref[0, r0:tb, c0:c0 + LANES], vp_ref[0, r0:tb, c0:c0 + LANES],
                 bias_ref[p, :, 0:len_a], has_prev),
                (kc_ref[0, 0:len_b, c0:c0 + LANES], vc_ref[0, 0:len_b, c0:c0 + LANES],
                 bias_ref[p, :, len_a:len_a + len_b], None),
            ]
            pairs.append((q, parts))
        for p, o in enumerate(_attend(pairs, lo_mask)):
            o_ref[0, r0:r0 + ATTN_SUB, p * LANES:(p + 1) * LANES] = o.astype(BF16)


def _rel_bias(table, n_q, n_k, offset):
    period = n_q + n_k - 1
    m = jnp.arange(period)
    u = table[:, jnp.clip(n_q - 1 + offset - m, -REL_MAX, REL_MAX) + REL_MAX].astype(F32)
    rows = jnp.tile(u, (1, n_q + 1))[:, :n_q * (period + 1)].reshape(-1, n_q, period + 1)[:, :, :n_k]
    return rows[:, ::-1, :]


def _band_bias(table):
    qc = jnp.arange(ATTN_SUB)[:, None] // CHUNK
    kc = jnp.arange(ATTN_SUB + WINDOW)[None, :] // CHUNK
    ok = (kc >= qc) & (kc <= qc + BAND_CHUNKS)
    return jnp.where(ok[None], _rel_bias(table, ATTN_SUB, ATTN_SUB + WINDOW, WINDOW), NEG_INF)


def _attn_prompt(aq, ak, av, table):
    B, S, _ = aq.shape
    tb = WINDOW
    assert S % tb == 0
    bias = _band_bias(table).reshape(A_WIDTH // LANES, 2 * ATTN_SUB, ATTN_SUB + WINDOW)
    cur = pl.BlockSpec((1, tb, A_WIDTH), lambda b, j: (b, j, 0))
    prev = pl.BlockSpec((1, tb, A_WIDTH), lambda b, j: (b, jnp.maximum(j - 1, 0), 0))
    return pl.pallas_call(
        functools.partial(_attn_prompt_kernel, tb=tb),
        grid=(B, S // tb),
        in_specs=[cur, prev, cur, prev, cur, pl.BlockSpec(bias.shape, lambda b, j: (0, 0, 0))],
        out_specs=cur,
        out_shape=jax.ShapeDtypeStruct((B, S, A_WIDTH), BF16),
        compiler_params=_params(("arbitrary", "arbitrary")),
        name="attn_prompt",
    )(aq, ak, ak, av, av, bias)


def _attn_sample_kernel(q_ref, k_ref, v_ref, bias_ref, o_ref):
    T = q_ref.shape[1]
    lo_mask = lax.broadcasted_iota(I32, (T, LANES), 1) < A_HEAD_DIM
    pairs = []
    for p in range(A_WIDTH // LANES):
        c0 = p * LANES
        parts = [(k_ref[0, :, c0:c0 + LANES], v_ref[0, :, c0:c0 + LANES], bias_ref[p], None)]
        pairs.append((q_ref[0, :, c0:c0 + LANES], parts))
    for p, o in enumerate(_attend(pairs, lo_mask)):
        o_ref[0, :, p * LANES:(p + 1) * LANES] = o.astype(BF16)


def _attn_sample(aq, ak, av, cache_k, cache_v, table):
    B, T, _ = aq.shape
    P = cache_k.shape[1]
    L = (P + T + LANES - 1) // LANES * LANES
    pad = jnp.zeros((B, L - P - T, A_WIDTH), BF16)
    kk = jnp.concatenate([cache_k.astype(BF16), ak, pad], axis=1)
    vv = jnp.concatenate([cache_v.astype(BF16), av, pad], axis=1)
    bias = jnp.where((jnp.arange(L) < P + T)[None, None, :], _rel_bias(table, T, L, P), NEG_INF)
    bias = bias.reshape(A_WIDTH // LANES, 2 * T, L)
    new = pl.BlockSpec((1, T, A_WIDTH), lambda b: (b, 0, 0))
    old = pl.BlockSpec((1, L, A_WIDTH), lambda b: (b, 0, 0))
    return pl.pallas_call(
        _attn_sample_kernel,
        grid=(B,),
        in_specs=[new, old, old, pl.BlockSpec(bias.shape, lambda b: (0, 0, 0))],
        out_specs=new,
        out_shape=jax.ShapeDtypeStruct((B, T, A_WIDTH), BF16),
        compiler_params=_params(("arbitrary",)),
        name="attn_sample",
    )(aq, kk, vv, bias)


def _gla_kernel(q_ref, k_ref, v_ref, la_ref, gr_ref, gain_ref, s0_ref, o_ref, sf_ref, s_scr, *, C, n_chunks,
                n_group):
    j = pl.program_id(1)

    @pl.when(j == 0)
    def _():
        s_scr[...] = s0_ref[0]

    n_sub = C // G_SUB
    ri = lax.broadcasted_iota(I32, (C, C), 0)
    ci = lax.broadcasted_iota(I32, (C, C), 1)
    tril = (ci <= ri).astype(BF16)
    lane_kw = lax.broadcasted_iota(I32, (1, G_KW), 1)
    head_of_lane = lane_kw // G_DK
    row_kw = lax.broadcasted_iota(I32, (C, G_KW), 0)
    ur = lax.broadcasted_iota(I32, (2 * C, 4 * C), 0) - C
    uc = lax.broadcasted_iota(I32, (2 * C, 4 * C), 1)
    u_mat = ((ur >= 0) & ((uc >= C) | (ur <= uc))).astype(BF16)

    def heads_on_rows(x):
        return jnp.concatenate([jnp.where(head_of_lane == h, x, 0.0) for h in range(G_HEADS)], axis=0)

    def chunk_group(g, S):
        ns = range(n_group)
        rs = [pl.multiple_of((g * n_group + n) * C, C) for n in ns]
        q = [q_ref[0, pl.ds(r, C), :].astype(F32) for r in rs]
        k = [k_ref[0, pl.ds(r, C), :].astype(F32) for r in rs]
        v = [v_ref[0, pl.ds(r, C), :] for r in rs]
        la = [la_ref[0, pl.ds(r, C), :] for r in rs]

        split = [_split_bf16(x) for x in la]
        b = [jnp.dot(tril, hi, preferred_element_type=F32) + jnp.dot(tril, lo, preferred_element_type=F32)
             for hi, lo in split]
        xt = [jnp.concatenate([k[n], la[n]], axis=0).T for n in ns]
        split_t = [_split_bf16(x) for x in xt]
        xb = [jnp.dot(hi, u_mat, preferred_element_type=F32) + jnp.dot(lo, u_mat, preferred_element_type=F32)
              for hi, lo in split_t]
        b_last = [x[:, LANES:] for x in xb]
        kd = [(xt[n] * jnp.exp(b_last[n] - xb[n][:, :LANES])).astype(BF16) for n in ns]
        zeros_v = jnp.zeros((C, G_VW), BF16)
        kv = [jnp.dot(kd[n], jnp.concatenate([v[n], zeros_v], axis=0), preferred_element_type=F32)
              for n in ns]
        kv_d = [jnp.concatenate([x[h * G_DK:(h + 1) * G_DK, h * G_DV:(h + 1) * G_DV] for h in range(G_HEADS)],
                                axis=0) for x in kv]

        states = [S]
        for n in ns:
            states.append(jnp.exp(b_last[n]) * states[n] + kv_d[n])

        r_inter = [jnp.dot(heads_on_rows(q[n] * jnp.exp(b[n])).astype(BF16), states[n].astype(BF16),
                           preferred_element_type=F32) for n in ns]
        o = [jnp.concatenate([x[h * C:(h + 1) * C] for h in range(G_HEADS)], axis=1) for x in r_inter]

        o_rows = [[] for _ in ns]
        for i in range(n_sub):
            r0, r1 = i * G_SUB, (i + 1) * G_SUB
            atts = []
            for n in ns:
                bs = b[n][r0 - 1:r0] if i > 0 else jnp.zeros((1, G_KW), F32)
                qe = q[n][r0:r1] * jnp.exp(b[n][r0:r1] - bs)
                ke = (k[n] * jnp.exp(jnp.where(row_kw < r1, bs - b[n], -jnp.inf))).astype(BF16)
                att = lax.dot_general(heads_on_rows(qe).astype(BF16), ke, (((1,), (1,)), ((), ())),
                                      preferred_element_type=F32)
                tt = lax.broadcasted_iota(I32, att.shape, 0) % G_SUB + r0
                ss = lax.broadcasted_iota(I32, att.shape, 1)
                atts.append(jnp.where(ss <= tt, att, 0.0).astype(BF16))
            for n in ns:
                ov = jnp.dot(atts[n], v[n], preferred_element_type=F32)
                o_rows[n].append(jnp.concatenate(
                    [ov[h * G_SUB:(h + 1) * G_SUB, h * G_DV:(h + 1) * G_DV] for h in range(G_HEADS)], axis=1))

        for n in ns:
            on = o[n] + jnp.concatenate(o_rows[n], axis=0)
            on = _head_rms(on, G_DV) * gain_ref[...] * gr_ref[0, pl.ds(rs[n], C), :].astype(F32)
            o_ref[0, pl.ds(rs[n], C), :] = on.astype(BF16)
        return states[-1]

    s_scr[...] = lax.fori_loop(0, n_chunks // n_group, chunk_group, s_scr[...])

    @pl.when(j == pl.num_programs(1) - 1)
    def _():
        sf_ref[0] = s_scr[...]


def _gla(gq, gk, gv, la, gr, gain, s0):
    B, T, _ = gq.shape
    C = CHUNK
    tb = min(ROW_TILE, T)
    assert T % tb == 0 and tb % C == 0 and C % G_SUB == 0 and 2 * C == LANES
    kw = pl.BlockSpec((1, tb, G_KW), lambda b, j: (b, j, 0))
    vw = pl.BlockSpec((1, tb, G_VW), lambda b, j: (b, j, 0))
    st = pl.BlockSpec((1, G_KW, G_DV), lambda b, j: (b, 0, 0))
    return pl.pallas_call(
        functools.partial(_gla_kernel, C=C, n_chunks=tb // C, n_group=8 if (tb // C) % 8 == 0 else 1),
        grid=(B, T // tb),
        in_specs=[kw, kw, vw, kw, vw, pl.BlockSpec(gain.shape, lambda b, j: (0, 0)), st],
        out_specs=[vw, st],
        out_shape=[jax.ShapeDtypeStruct((B, T, G_VW), BF16), jax.ShapeDtypeStruct((B, G_KW, G_DV), F32)],
        scratch_shapes=[pltpu.VMEM((G_KW, G_DV), F32)],
        compiler_params=_params(("arbitrary", "arbitrary")),
        name="gla",
    )(gq, gk, gv, la, gr, gain, s0)


def _memkv_kernel(mem_ref, gm_ref, w_ref, gk_ref, k_ref, v_ref):
    x = mem_ref[0]
    h = (x * lax.rsqrt(jnp.mean(x * x, -1, keepdims=True) + EPS) * gm_ref[...]).astype(BF16)
    k = jnp.dot(h, w_ref[:, :C_WIDTH], preferred_element_type=F32)
    k_ref[0] = _head_rms(k, C_HEAD_DIM) * gk_ref[...]
    v_ref[0] = jnp.dot(h, w_ref[:, C_WIDTH:], preferred_element_type=F32)


def _memkv(mem, gm, w_kv, gk):
    B, M, _ = mem.shape
    out = pl.BlockSpec((1, M, C_WIDTH), lambda b: (b, 0, 0))
    return pl.pallas_call(
        _memkv_kernel,
        grid=(B,),
        in_specs=[pl.BlockSpec((1, M, D_MODEL), lambda b: (b, 0, 0)),
                  pl.BlockSpec(gm.shape, lambda b: (0, 0)),
                  pl.BlockSpec(w_kv.shape, lambda b: (0, 0)),
                  pl.BlockSpec(gk.shape, lambda b: (0, 0))],
        out_specs=[out, out],
        out_shape=[jax.ShapeDtypeStruct((B, M, C_WIDTH), F32)] * 2,
        compiler_params=_params(("arbitrary",)),
        name="memkv",
    )(mem, gm, w_kv, gk)


def _merge_kernel(x_ref, ya_ref, yb_ref, cq_ref, gt_ref, mk_ref, mv_ref, wa_ref, wb_ref, wc_ref,
                  wo_ref, nf_ref, wr_ref, br_ref,
                  x1_ref, hp_ref, ri_ref, rw_ref, cnt_ref, run_scr, *, tok_base):
    first =jnp.logical_and(pl.program_id(0) == 0, pl.program_id(1) == 0)

    @pl.when(first)
    def _():
        run_scr[...] = jnp.zeros_like(run_scr)

    tm = x_ref.shape[1]
    n_part = 2 if tm % 256 == 0 else 1
    parts = [slice(n * (tm // n_part), (n + 1) * (tm // n_part)) for n in range(n_part)]
    heads = [slice(h * C_HEAD_DIM, (h + 1) * C_HEAD_DIM) for h in range(C_HEADS)]
    mk = [mk_ref[0, :, c].astype(BF16) for c in heads]
    mv = [mv_ref[0, :, c].astype(BF16) for c in heads]
    scores = [[lax.dot_general(cq_ref[0, r, c], k, (((1,), (1,)), ((), ())), preferred_element_type=F32)
               * (C_HEAD_DIM ** -0.5) for c, k in zip(heads, mk)] for r in parts]
    probs = [[jnp.exp(s - s.max(-1, keepdims=True)) for s in ss] for ss in scores]
    sums = [[p.sum(-1, keepdims=True) for p in ps] for ps in probs]
    yc_in = [jnp.concatenate([jnp.dot(p.astype(BF16), v, preferred_element_type=F32) / l
                              for p, l, v in zip(ps, ls, mv)], axis=-1).astype(BF16)
             for ps, ls in zip(probs, sums)]
    y_a = [jnp.dot(ya_ref[0, r, :], wa_ref[...], preferred_element_type=F32) for r in parts]
    y_b = [jnp.dot(yb_ref[0, r, :], wb_ref[...], preferred_element_type=F32) for r in parts]
    y_c = [jnp.dot(y, wc_ref[...], preferred_element_type=F32) for y in yc_in]
    merged = [(gt_ref[0, r, 0:D_MODEL].astype(F32) * a
               + gt_ref[0, r, D_MODEL:2 * D_MODEL].astype(F32) * b
               + gt_ref[0, r, 2 * D_MODEL:3 * D_MODEL].astype(F32) * c).astype(BF16)
              for r, a, b, c in zip(parts, y_a, y_b, y_c)]
    x1s = [x_ref[0, r, :] + jnp.dot(m, wo_ref[...], preferred_element_type=F32) for r, m in zip(parts, merged)]
    for r, v in zip(parts, x1s):
        x1_ref[0, r, :] = v
    h2s = [v * lax.rsqrt(jnp.mean(v * v, -1, keepdims=True) + EPS) * nf_ref[...] for v in x1s]

    logits = jnp.concatenate([jnp.dot(h.astype(BF16), wr_ref[...], preferred_element_type=F32) for h in h2s],
                             axis=0) + br_ref[...]
    h2 = jnp.concatenate(h2s, axis=0)
    lane = lax.broadcasted_iota(I32, (tm, LANES), 1)
    lane_f = lane.astype(F32)
    vals, sels, idxs = [], [], []
    l = logits
    for _ in range(TOP_K):
        m = l.max(-1, keepdims=True)
        idx = jnp.min(jnp.where(l == m, lane_f, float(LANES)), -1, keepdims=True)
        sel = lane_f == idx
        vals.append(m)
        idxs.append(idx)
        sels.append(sel)
        l = jnp.where(sel, -3e38, l)
    es = [jnp.exp(vk - vals[0]) for vk in vals]
    den = es[0] + es[1] + es[2] + es[3]
    cnt = jnp.zeros((tm, LANES), F32)
    for sel in sels:
        cnt = cnt + jnp.where(sel, 1.0, 0.0)
    tp = max(tm, LANES)
    cnt_p = cnt if tp == tm else jnp.concatenate([cnt, jnp.zeros((tp - tm, LANES), F32)], axis=0)
    ri = lax.broadcasted_iota(I32, (tp, tp), 0)
    ci = lax.broadcasted_iota(I32, (tp, tp), 1)
    before = jnp.dot(jnp.where(ci < ri, 1.0, 0.0).astype(BF16), cnt_p.astype(BF16),
                     preferred_element_type=F32)[0:tm] + run_scr[0:1, :]
    r_i = jnp.zeros((tm, LANES), I32)
    r_w = jnp.zeros((tm, LANES), F32)
    for kk in range(TOP_K):
        rank = jnp.sum(jnp.where(sels[kk], before, 0.0), -1, keepdims=True)
        r_i = jnp.where(lane == kk, idxs[kk].astype(I32), r_i)
        r_i = jnp.where(lane == TOP_K + kk, rank.astype(I32), r_i)
        r_w = jnp.where(lane == kk, es[kk] / den, r_w)
    ri_ref[0] = r_i
    rw_ref[0] = r_w
    run_scr[...] = run_scr[...] + jnp.sum(cnt, axis=0, keepdims=True)
    cnt_ref[...] = run_scr[...]

    tok = tok_base + (pl.program_id(0) * pl.num_programs(1) + pl.program_id(1)) * tm \
        + lax.broadcasted_iota(I32, (tm, LANES), 0)
    meta = jnp.where(lane == 0, tok, 0)
    for kk in range(TOP_K):
        meta = jnp.where(lane == 1 + kk, idxs[kk].astype(I32), meta)
    packed = _pack_rows(h2)
    zero = jnp.zeros((tm, LANES), U32)
    slabs = [packed[:, j * LANES:(j + 1) * LANES] for j in range(SLABS)]
    slabs += [lax.bitcast_convert_type(meta, U32)] + [zero] * (ROW_SLABS - SLABS - 1)
    hp_ref[0] = pltpu.einshape("jtl->tjl", jnp.stack(slabs, axis=0))


def _merge(x, ya, yb, cq, gt, mk, mv, wa, wb, wc, wo, nf, wr, br, tok_base):
    B, T, _ = x.shape
    tm = min(ROW_TILE, T)
    assert T % tm == 0

    def row(width):
        return pl.BlockSpec((1, tm, width), lambda b, j: (b, j, 0))

    def full(a):
        return pl.BlockSpec(a.shape, lambda b, j: (0,) * a.ndim)

    mem = pl.BlockSpec((1, N_MEM, C_WIDTH), lambda b, j: (b, 0, 0))
    return pl.pallas_call(
        functools.partial(_merge_kernel, tok_base=tok_base),
        grid=(B, T // tm),
        in_specs=[row(D_MODEL), row(A_WIDTH), row(G_VW), row(C_WIDTH), row(N_BRANCH * D_MODEL), mem, mem,
                  full(wa), full(wb), full(wc), full(wo), full(nf), full(wr), full(br)],
        out_specs=[row(D_MODEL), pl.BlockSpec((1, tm, ROW_SLABS, LANES), lambda b, j: (b, j, 0, 0)), row(LANES),
                   row(LANES), pl.BlockSpec((8, LANES), lambda b, j: (0, 0))],
        out_shape=[jax.ShapeDtypeStruct((B, T, D_MODEL), F32), jax.ShapeDtypeStruct((B, T, ROW_SLABS, LANES), U32),
                   jax.ShapeDtypeStruct((B, T, LANES), I32), jax.ShapeDtypeStruct((B, T, LANES), F32),
                   jax.ShapeDtypeStruct((8, LANES), F32)],
        scratch_shapes=[pltpu.VMEM((8, LANES), F32)],
        compiler_params=_params(("arbitrary", "arbitrary")),
        name="merge",
    )(x, ya, yb, cq, gt, mk, mv, wa, wb, wc, wo, nf, wr, br)


def _dispatch_kernel(dest_ref, dest2_ref, pad_lo_ref, pad_n_ref, hp_ref, hp2_ref, xs_ref, zero_scr, sem, sem2,
                     psem, *, tm, tm2, n_pad, n_tokens):
    def scatter(dref, href, s, n):
        def issue(t, c):
            for kk in range(TOP_K):
                pltpu.make_async_copy(href.at[t], xs_ref.at[dref[0, 0, t * TOP_K + kk]], s).start(priority=kk % 2)
            return c
        lax.fori_loop(0, n, issue, 0, unroll=8)

    def drain(href, s, n):
        for _ in range(TOP_K):
            pltpu.make_async_copy(href, xs_ref.at[pl.ds(0, n)], s).wait()

    scatter(dest_ref, hp_ref, sem, tm)

    @pl.when(pl.program_id(0) == 0)
    def _():
        scatter(dest2_ref, hp2_ref, sem2, tm2)
        shape = (FFN_BLOCK, ROW_SLABS, LANES)
        sub = lax.broadcasted_iota(I32, shape, 1)
        lane = lax.broadcasted_iota(I32, shape, 2)
        filler = jnp.where((sub == META) & (lane == 0), n_tokens,
                           jnp.where((sub == META) & (lane <= TOP_K), N_EXPERTS, 0))
        zero_scr[...] = lax.bitcast_convert_type(filler, U32)

        log_blk = FFN_BLOCK.bit_length() - 1

        def span(e, wait):
            lo, n = pad_lo_ref[e], pad_n_ref[e]
            whole = n >> log_blk

            def go(cp):
                cp.wait() if wait else cp.start()

            def block(j, c):
                go(pltpu.make_async_copy(zero_scr, xs_ref.at[pl.ds(lo + j * FFN_BLOCK, FFN_BLOCK)], psem))
                return c

            lax.fori_loop(0, whole, block, 0)
            rem = n - (whole << log_blk)
            for b in reversed(range(log_blk)):
                off = lo + (whole << log_blk) + ((rem >> (b + 1)) << (b + 1))

                @pl.when((rem >> b) & 1 == 1)
                def _():
                    go(pltpu.make_async_copy(zero_scr.at[pl.ds(0, 1 << b)], xs_ref.at[pl.ds(off, 1 << b)], psem))

        def fill(e, c):
            span(e, False)
            return c

        def fill_wait(e, c):
            span(e, True)
            return c

        lax.fori_loop(0, n_pad, fill, 0)
        lax.fori_loop(0, n_pad, fill_wait, 0)
        drain(hp2_ref, sem2, tm2)

    drain(hp_ref, sem, tm)


def _dispatch(dest, hp, dest2, hp2, pad_lo, pad_n, rows):
    N, N2 = hp.shape[0], hp2.shape[0]
    tm = min(ROW_TILE, N)
    assert N % tm == 0
    n_steps = N // tm
    smem = functools.partial(pl.BlockSpec, memory_space=pltpu.SMEM)
    n_pad = pad_lo.shape[0]
    return pl.pallas_call(
        functools.partial(_dispatch_kernel, tm=tm, tm2=N2, n_pad=n_pad, n_tokens=N + N2),
        grid=(n_steps,),
        in_specs=[smem((1, 1, tm * TOP_K), lambda i: (i, 0, 0)),
                  smem((1, 1, N2 * TOP_K), lambda i: (0, 0, 0)),
                  smem((n_pad,), lambda i: (0,)), smem((n_pad,), lambda i: (0,)),
                  pl.BlockSpec((tm, ROW_SLABS, LANES), lambda i: (i, 0, 0)),
                  pl.BlockSpec((N2, ROW_SLABS, LANES), lambda i: (0, 0, 0))],
        out_specs=pl.BlockSpec(memory_space=pl.ANY),
        out_shape=jax.ShapeDtypeStruct((rows, ROW_SLABS, LANES), U32),
        scratch_shapes=[pltpu.VMEM((FFN_BLOCK, ROW_SLABS, LANES), U32), pltpu.SemaphoreType.DMA(()),
                        pltpu.SemaphoreType.DMA(()), pltpu.SemaphoreType.DMA(())],
        compiler_params=_params(("arbitrary",)),
        name="moe_dispatch",
    )(dest.reshape(n_steps, 1, tm * TOP_K), dest2.reshape(1, 1, N2 * TOP_K), pad_lo, pad_n, hp, hp2)


def _ffn_kernel(be_ref, nu_ref, x_ref, wgu_ref, bgu_ref, wd_ref, bd_ref, y4_ref,
                wgu_bf, wd_bf, ybuf, idv, ids_smem, sem_ids, sem_rows, *, n_blk, n_tokens):
    i = pl.program_id(0)
    n_used = nu_ref[0]
    n_assign = n_tokens * TOP_K
    prev = be_ref[jnp.maximum(i - 1, 0)]
    new_expert = jnp.logical_or(i == 0, be_ref[i] != prev)

    def rows_done(s):
        pltpu.make_async_copy(ybuf.at[s], y4_ref.at[pl.ds(0, FFN_BLOCK)], sem_rows.at[s]).wait()

    def ids_copy(s):
        return pltpu.make_async_copy(idv, ids_smem.at[s], sem_ids)

    def start_rows(s, lo, hi):
        for r in range(lo, hi):
            pltpu.make_async_copy(ybuf.at[s, r], y4_ref.at[ids_smem[s, 0, r]],
                                  sem_rows.at[s]).start(priority=r % 2)

    @pl.when(i == 0)
    def _():
        ybuf[1] = jnp.zeros(ybuf.shape[1:], U32)
        spare = pltpu.make_async_copy(ybuf.at[1], y4_ref.at[pl.ds(n_assign, FFN_BLOCK)], sem_rows.at[1])
        spare.start()
        spare.wait()

    @pl.when(new_expert)
    def _():
        wgu_bf[...] = wgu_ref[0].astype(BF16)
        wd_bf[...] = wd_ref[0].astype(BF16)

    def compute(s, flush):
        q = FFN_BLOCK // 4
        if flush:
            ids_copy(1 - s).wait()
            start_rows(1 - s, 0, q)
        xm = pltpu.einshape("tjl->jtl", x_ref[...])
        x_lo, x_hi = _unpack_rows(jnp.concatenate([xm[j] for j in range(SLABS)], axis=-1))
        meta = lax.bitcast_convert_type(xm[META], I32)
        tok = meta[:, 0:1]
        choice = jnp.zeros_like(tok)
        for kk in range(1, TOP_K):
            choice = jnp.where(meta[:, 1 + kk:2 + kk] == be_ref[i], kk, choice)
        local = lax.broadcasted_iota(I32, tok.shape, 0)
        dest = jnp.where(tok >= n_tokens, n_assign + local, tok * TOP_K + choice)
        dest_t = jnp.broadcast_to(dest.astype(F32), (FFN_BLOCK, LANES)).T
        idv[...] = dest_t[0:8].astype(I32)
        ids_copy(s).start()
        gu = (jnp.dot(x_lo.astype(BF16), wgu_bf[0:HALF, :], preferred_element_type=F32)
              + jnp.dot(x_hi.astype(BF16), wgu_bf[HALF:, :], preferred_element_type=F32)
              + bgu_ref[0])
        if flush:
            start_rows(1 - s, q, 2 * q)
        gate = jnp.minimum(gu[:, :D_EXPERT], SWIGLU_LIMIT)
        up = jnp.clip(gu[:, D_EXPERT:], -SWIGLU_LIMIT, SWIGLU_LIMIT)
        act = (up + 1.0) * (gate * _sigmoid(SWIGLU_ALPHA * gate))
        if flush:
            start_rows(1 - s, 2 * q, 3 * q)
        y = jnp.dot(act.astype(BF16), wd_bf[...], preferred_element_type=F32) + bd_ref[0]
        if flush:
            start_rows(1 - s, 3 * q, 4 * q)
        packed = _pack_rows(y)
        pl.when(i >= 2)(functools.partial(rows_done, s))
        _store_slabs(ybuf, (s,), packed)

    def flush_only(s, also):
        ids_copy(s).wait()
        start_rows(s, 0, FFN_BLOCK)
        pl.when(also)(functools.partial(rows_done, 1 - s))
        rows_done(s)

    real = i < n_used
    prev_real = jnp.logical_and(i >= 1, i - 1 < n_used)
    for s in range(2):
        mine = i % 2 == s
        pl.when(mine & real & prev_real)(functools.partial(compute, s, True))
        pl.when(mine & real & jnp.logical_not(prev_real))(functools.partial(compute, s, False))
        pl.when(mine & jnp.logical_not(real) & prev_real)(functools.partial(flush_only, 1 - s, i >= 2))
        pl.when(mine & real & (i == n_blk - 1))(functools.partial(flush_only, s, i >= 1))


def _ffn(blk_expert, n_used, xs, n_tokens, w_gu, b_gu, w_d, b_d):
    P = xs.shape[0]
    nblk = P // FFN_BLOCK
    grid_spec = pltpu.PrefetchScalarGridSpec(
        num_scalar_prefetch=2,
        grid=(nblk,),
        in_specs=[pl.BlockSpec((FFN_BLOCK, ROW_SLABS, LANES), lambda i, be, nu: (jnp.minimum(i, nu[0] - 1), 0, 0)),
                  pl.BlockSpec((1, D_MODEL, 2 * D_EXPERT), lambda i, be, nu: (be[i], 0, 0)),
                  pl.BlockSpec((1, 1, 2 * D_EXPERT), lambda i, be, nu: (be[i], 0, 0)),
                  pl.BlockSpec((1, D_EXPERT, D_MODEL), lambda i, be, nu: (be[i], 0, 0)),
                  pl.BlockSpec((1, 1, D_MODEL), lambda i, be, nu: (be[i], 0, 0))],
        out_specs=pl.BlockSpec(memory_space=pl.ANY),
        scratch_shapes=[pltpu.VMEM((D_MODEL, 2 * D_EXPERT), BF16), pltpu.VMEM((D_EXPERT, D_MODEL), BF16),
                        pltpu.VMEM((2, FFN_BLOCK, SLABS, LANES), U32), pltpu.VMEM((8, FFN_BLOCK), I32),
                        pltpu.SMEM((2, 8, FFN_BLOCK), I32), pltpu.SemaphoreType.DMA(()),
                        pltpu.SemaphoreType.DMA((2,))],
    )
    return pl.pallas_call(
        functools.partial(_ffn_kernel, n_blk=nblk, n_tokens=n_tokens),
        grid_spec=grid_spec,
        out_shape=jax.ShapeDtypeStruct((n_tokens * TOP_K + FFN_BLOCK, SLABS, LANES), U32),
        compiler_params=_params(("arbitrary",)),
        name="moe_ffn",
    )(blk_expert, n_used, xs, w_gu, b_gu.reshape(N_EXPERTS, 1, -1), w_d, b_d.reshape(N_EXPERTS, 1, -1))


def _combine_kernel(x1_ref, rw_ref, y_ref, o_ref, *, tm):
    w = rw_ref[...]
    acc_lo = x1_ref[:, :HALF]
    acc_hi = x1_ref[:, HALF:]
    for kk in range(TOP_K):
        rows = y_ref[pl.ds(kk, tm, stride=TOP_K)]
        slab_major = pltpu.einshape("tjl->jtl", rows)
        lo, hi = _unpack_rows(jnp.concatenate([slab_major[j] for j in range(SLABS)], axis=-1))
        wk = w[:, kk:kk + 1]
        acc_lo = acc_lo + wk * lo
        acc_hi = acc_hi + wk * hi
    o_ref[:, :HALF] = acc_lo
    o_ref[:, HALF:] = acc_hi


def _combine(x1, rw, y4, tok_base):
    N = x1.shape[0]
    tm = min(256, N)
    assert N % tm == 0 and tok_base % tm == 0
    first = tok_base // tm
    return pl.pallas_call(
        functools.partial(_combine_kernel, tm=tm),
        grid=(N // tm,),
        in_specs=[pl.BlockSpec((tm, D_MODEL), lambda i: (i, 0)),
                  pl.BlockSpec((tm, LANES), lambda i: (i, 0)),
                  pl.BlockSpec((tm * TOP_K, SLABS, LANES), lambda i: (first + i, 0, 0))],
        out_specs=pl.BlockSpec((tm, D_MODEL), lambda i: (i, 0)),
        out_shape=jax.ShapeDtypeStruct((N, D_MODEL), F32),
        compiler_params=_params(("arbitrary",)),
        name="moe_combine",
    )(x1, rw, y4)


def _moe(groups, w_gu, b_gu, w_d, b_d):
    n_tokens = sum(g[0].shape[0] for g in groups)
    n_assign = n_tokens * TOP_K
    nblk = (n_assign + N_EXPERTS * (FFN_BLOCK - 1) + FFN_BLOCK - 1) // FFN_BLOCK
    counts = [g[4].astype(I32) for g in groups]
    total = sum(counts)
    padded = (total + FFN_BLOCK - 1) // FFN_BLOCK * FFN_BLOCK
    pend = jnp.cumsum(padded)
    pstart = pend - padded
    n_used = pend[-1:] // FFN_BLOCK
    blk = jnp.minimum(jnp.arange(nblk, dtype=I32), n_used[0] - 1) * FFN_BLOCK
    blk_expert = jnp.minimum(jnp.sum(pend[None, :] <= blk[:, None], axis=1), N_EXPERTS - 1).astype(I32)

    experts = jnp.arange(N_EXPERTS, dtype=I32)
    dests = []
    base = pstart
    for g, c in zip(groups, counts):
        idx, rank = g[2][:, :TOP_K], g[2][:, TOP_K:2 * TOP_K]
        dests.append(jnp.sum(jnp.where(idx[..., None] == experts, base, 0), axis=-1) + rank)
        base = base + c

    rows = nblk * FFN_BLOCK
    pad_lo = jnp.concatenate([pstart + total, pend[-1:]])
    pad_n = jnp.concatenate([padded - total, rows - pend[-1:]])
    (g_main, g_small), (d_main, d_small) = groups, dests
    xs = _dispatch(d_main, g_main[1], d_small, g_small[1], pad_lo, pad_n, rows)
    y4 = _ffn(blk_expert, n_used.astype(I32), xs, n_tokens, w_gu, b_gu, w_d, b_d)
    return [_combine(g_main[0], g_main[3], y4, 0), _combine(g_small[0], g_small[3], y4, g_main[0].shape[0])]


def _tile_lanes(g, reps):
    return jnp.tile(g.astype(F32), reps)[None, :]


def kernel(x_prompt, x_sample, mem_prompt, cache_attn_k, cache_attn_v, state_gla, cache_mem_k, cache_mem_v, norm_mix, w_in, a_q_norm, a_k_norm, rel_bias_table, w_a_o, w_gla_a_up, b_gla_a, gla_out_norm, w_b_o, c_q_norm, c_k_norm, norm_mem, w_mem_kv, w_c_o, b_gate, w_out, norm_ffn, w_router, b_router, w_gate_up, b_gate_up, w_down, b_down):
    depth = norm_mix.shape[0]
    assert depth == 1
    l = 0
    B, S, _ = x_prompt.shape
    Bs, Ts, _ = x_sample.shape
    keep = min(WINDOW, S)

    w = w_in[l]
    sizes = (A_WIDTH, A_WIDTH, A_WIDTH, G_KW, G_KW, G_VW, G_VW, G_RANK, C_WIDTH, N_BRANCH * D_MODEL)
    offs = [0]
    for s_ in sizes:
        offs.append(offs[-1] + s_)
    seg = [w[:, offs[i]:offs[i + 1]] for i in range(len(sizes))]
    w_r = jnp.concatenate(seg[0:7] + [seg[8], seg[9], seg[7], jnp.zeros((D_MODEL, LANES - G_RANK), F32)],
                          axis=1).astype(BF16)
    nm = norm_mix[l][None, :]
    aqn = _tile_lanes(a_q_norm[l], A_HEADS)
    akn = _tile_lanes(a_k_norm[l], A_HEADS)
    cqn = _tile_lanes(c_q_norm[l], C_HEADS)
    ckn = _tile_lanes(c_k_norm[l], C_HEADS)
    gon = _tile_lanes(gla_out_norm[l], G_HEADS)
    wup = jnp.concatenate([w_gla_a_up[l], jnp.zeros((LANES - G_RANK, G_KW), F32)], axis=0).astype(BF16)
    bla = b_gla_a[l][None, :]
    bg = b_gate[l][None, :]
    wa, wb, wc, wo = (t[l].astype(BF16) for t in (w_a_o, w_b_o, w_c_o, w_out))
    nf = norm_ffn[l][None, :]
    wr = jnp.concatenate([w_router[l], jnp.zeros((D_MODEL, LANES - N_EXPERTS), F32)], axis=1).astype(BF16)
    br = jnp.concatenate([b_router[l], jnp.full((LANES - N_EXPERTS,), NEG_INF, F32)])[None, :]
    table = rel_bias_table[l]

    mk, mv = _memkv(mem_prompt, norm_mem[l][None, :], w_mem_kv[l].astype(BF16), ckn)
    (aq, ak, av, gq, gk, gv, gr, la, cq, gt, ak_tail, av_tail) = _inproj(
        x_prompt, keep, nm, w_r, aqn, akn, cqn, wup, bla, bg)
    ya = _attn_prompt(aq, ak, av, table)
    yb, s_prompt = _gla(gq, gk, gv, la, gr, gon, jnp.zeros((B, G_KW, G_DV), F32))
    x1_p, hp_p, ri_p, rw_p, cnt_p = _merge(x_prompt, ya, yb, cq, gt, mk, mv, wa, wb, wc, wo, nf, wr, br, 0)

    (aq, ak, av, gq, gk, gv, gr, la, cq, gt, ak_new, av_new) = _inproj(
        x_sample.reshape(1, Bs * Ts, D_MODEL), Bs * Ts, nm, w_r, aqn, akn, cqn, wup, bla, bg)
    rs = lambda t: t.reshape(Bs, Ts, t.shape[-1])
    P = cache_attn_k.shape[2]
    ya = _attn_sample(rs(aq), rs(ak), rs(av), cache_attn_k[l].reshape(Bs, P, A_WIDTH),
                      cache_attn_v[l].reshape(Bs, P, A_WIDTH), table)
    t_pad = (Ts + CHUNK - 1) // CHUNK * CHUNK
    zp = lambda t: jnp.pad(rs(t), ((0, 0), (0, t_pad - Ts), (0, 0)))
    yb, s_sample = _gla(zp(gq), zp(gk), zp(gv), zp(la), zp(gr), gon, state_gla[l].reshape(Bs, G_KW, G_DV))
    yb = yb[:, :Ts]
    x1_s, hp_s, ri_s, rw_s, cnt_s = _merge(
        x_sample, ya, yb, rs(cq), rs(gt), cache_mem_k[l].reshape(Bs, N_MEM, C_WIDTH),
        cache_mem_v[l].reshape(Bs, N_MEM, C_WIDTH), wa, wb, wc, wo, nf, wr, br, B * S)

    flat = lambda t: t.reshape((-1,) + t.shape[2:])
    y_p, y_s = _moe(
        [(flat(x1_p), flat(hp_p), flat(ri_p), flat(rw_p), cnt_p[0, :N_EXPERTS]),
         (flat(x1_s), flat(hp_s), flat(ri_s), flat(rw_s), cnt_s[0, :N_EXPERTS])],
        w_gate_up[l], b_gate_up[l], w_down[l], b_down[l])

    return (y_p.reshape(B, S, D_MODEL), y_s.reshape(Bs, Ts, D_MODEL),
            ak_tail.reshape(1, B, keep, A_HEADS, A_HEAD_DIM), av_tail.reshape(1, B, keep, A_HEADS, A_HEAD_DIM),
            s_prompt.reshape(1, B, G_HEADS, G_DK, G_DV),
            mk.reshape(1, B, N_MEM, C_HEADS, C_HEAD_DIM), mv.reshape(1, B, N_MEM, C_HEADS, C_HEAD_DIM),
            ak_new.reshape(1, Bs, Ts, A_HEADS, A_HEAD_DIM), av_new.reshape(1, Bs, Ts, A_HEADS, A_HEAD_DIM),
            s_sample.reshape(1, Bs, G_HEADS, G_DK, G_DV))
```

```python
import functools

import jax
import jax.numpy as jnp
from jax import lax
from jax.experimental import pallas as pl
from jax.experimental.pallas import tpu as pltpu

F32 = jnp.float32
BF16 = jnp.bfloat16
U32 = jnp.uint32
I32 = jnp.int32

D_MODEL = 1024
CHUNK = 64
BAND_CHUNKS = 8
WINDOW = BAND_CHUNKS * CHUNK
N_MEM = 256
A_HEADS, A_HEAD_DIM = 8, 64
A_WIDTH = A_HEADS * A_HEAD_DIM
REL_MAX = 128
G_HEADS, G_DK, G_DV = 4, 64, 128
G_KW, G_VW = G_HEADS * G_DK, G_HEADS * G_DV
G_RANK = 16
G_TAU = 16.0
G_SUB = 16
C_HEADS, C_HEAD_DIM = 4, 128
C_WIDTH = C_HEADS * C_HEAD_DIM
N_BRANCH = 3
N_EXPERTS = 32
TOP_K = 4
D_EXPERT = 1024
SWIGLU_LIMIT = 7.0
SWIGLU_ALPHA = 1.702
EPS = 1e-6
NEG_INF = -1e30

LANES = 128
HALF = D_MODEL // 2
ROW_TILE = 512
ATTN_SUB = 128
FFN_BLOCK = 512
VMEM_LIMIT = 56 * 1024 * 1024

OFF_AQ, OFF_AK, OFF_AV = 0, 512, 1024
OFF_GQ, OFF_GK, OFF_GV, OFF_GR = 1536, 1792, 2048, 2560
OFF_CQ, OFF_GATE, OFF_LR = 3072, 3584, 6656
IN_COLS = OFF_LR + LANES


def _params(sem):
    return pltpu.CompilerParams(dimension_semantics=sem, vmem_limit_bytes=VMEM_LIMIT)


def _sigmoid(x):
    return 0.5 * jnp.tanh(0.5 * x) + 0.5


def _head_rms(y, head_dim):
    cols = []
    for p in range(y.shape[1] // LANES):
        blk = y[:, p * LANES:(p + 1) * LANES]
        sq = blk * blk
        if head_dim == LANES:
            sc = lax.rsqrt(jnp.sum(sq, -1, keepdims=True) * (1.0 / LANES) + EPS)
        else:
            lo = lax.broadcasted_iota(I32, blk.shape, 1) < head_dim
            s_lo = jnp.sum(jnp.where(lo, sq, 0.0), -1, keepdims=True)
            s_hi = jnp.sum(jnp.where(lo, 0.0, sq), -1, keepdims=True)
            sc = jnp.where(lo, lax.rsqrt(s_lo * (1.0 / head_dim) + EPS),
                           lax.rsqrt(s_hi * (1.0 / head_dim) + EPS))
        cols.append(blk * sc)
    return jnp.concatenate(cols, axis=-1)


def _split_bf16(x):
    hi = x.astype(BF16)
    lo = (x - hi.astype(F32)).astype(BF16)
    return hi, lo


def _pack_rows(x):
    lo = lax.bitcast_convert_type(x[:, :HALF].astype(BF16).astype(F32), U32)
    hi = lax.bitcast_convert_type(x[:, HALF:].astype(BF16).astype(F32), U32)
    return (lo >> 16) | (hi & jnp.uint32(0xFFFF0000))


def _unpack_rows(u):
    lo = lax.bitcast_convert_type(u << 16, F32)
    hi = lax.bitcast_convert_type(u & jnp.uint32(0xFFFF0000), F32)
    return lo, hi


SLABS = HALF // LANES
ROW_SLABS = 8
META = SLABS


def _store_slabs(ref, lead, u):
    slab_major = jnp.stack([u[:, j * LANES:(j + 1) * LANES] for j in range(SLABS)], axis=0)
    ref[lead + (slice(None),) * 3] = pltpu.einshape("jtl->tjl", slab_major)


def _inproj_kernel(x_ref, nm_ref, w_ref, aqn_ref, akn_ref, cqn_ref, wup_ref, bla_ref, bg_ref,
                   aq_ref, ak_ref, av_ref, gq_ref, gk_ref, gv_ref, gr_ref, la_ref, cq_ref, gt_ref,
                   akt_ref, avt_ref, *, n_tiles, n_tail):
    j = pl.program_id(1)
    x = x_ref[0]
    h = (x * lax.rsqrt(jnp.mean(x * x, -1, keepdims=True) + EPS) * nm_ref[...]).astype(BF16)

    def seg(off, width):
        return jnp.dot(h, w_ref[:, off:off + width], preferred_element_type=F32)

    in_tail = j >= n_tiles - n_tail

    lr = seg(OFF_LR, LANES).astype(BF16)

    aq = _head_rms(seg(OFF_AQ, A_WIDTH), A_HEAD_DIM) * aqn_ref[...] * (A_HEAD_DIM ** -0.5)
    aq_ref[0] = aq.astype(BF16)

    ak = _head_rms(seg(OFF_AK, A_WIDTH), A_HEAD_DIM) * akn_ref[...]
    ak_ref[0] = ak.astype(BF16)

    @pl.when(in_tail)
    def _():
        akt_ref[0] = ak

    z = jnp.dot(lr, wup_ref[...], preferred_element_type=F32) + bla_ref[...]
    la_ref[0] = (jnp.minimum(z, 0.0) - jnp.log1p(jnp.exp(-jnp.abs(z)))) * (1.0 / G_TAU)

    gq_ref[0] = (seg(OFF_GQ, G_KW) * (G_DK ** -0.5)).astype(BF16)
    gk_ref[0] = seg(OFF_GK, G_KW).astype(BF16)
    gv_ref[0] = seg(OFF_GV, G_VW).astype(BF16)
    gr = seg(OFF_GR, G_VW)
    gr_ref[0] = (gr * _sigmoid(gr)).astype(BF16)

    cq = _head_rms(seg(OFF_CQ, C_WIDTH), C_HEAD_DIM) * cqn_ref[...]
    cq_ref[0] = cq.astype(BF16)

    gate_chunk = 512
    for c in range(N_BRANCH * D_MODEL // gate_chunk):
        lo = c * gate_chunk
        g = seg(OFF_GATE + lo, gate_chunk) + bg_ref[:, lo:lo + gate_chunk]
        gt_ref[0, :, lo:lo + gate_chunk] = _sigmoid(g).astype(BF16)

    av = seg(OFF_AV, A_WIDTH)
    av_ref[0] = av.astype(BF16)

    @pl.when(in_tail)
    def _():
        avt_ref[0] = av


def _inproj(x, keep, nm, w_r, aqn, akn, cqn, wup, bla, bg):
    G, R, _ = x.shape
    tm = min(ROW_TILE, R)
    n_tiles = R // tm
    n_tail = keep // tm
    assert R % tm == 0 and keep % tm == 0 and n_tail >= 1

    def row(width, dtype):
        return (jax.ShapeDtypeStruct((G, R, width), dtype),
                pl.BlockSpec((1, tm, width), lambda g, j: (g, j, 0)))

    def tail(width):
        return (jax.ShapeDtypeStruct((G, keep, width), F32),
                pl.BlockSpec((1, tm, width), lambda g, j: (g, jnp.maximum(j - (n_tiles - n_tail), 0), 0)))

    outs = [row(A_WIDTH, BF16), row(A_WIDTH, BF16), row(A_WIDTH, BF16), row(G_KW, BF16),
            row(G_KW, BF16), row(G_VW, BF16), row(G_VW, BF16), row(G_KW, F32), row(C_WIDTH, BF16),
            row(N_BRANCH * D_MODEL, BF16), tail(A_WIDTH), tail(A_WIDTH)]

    def full(a):
        return pl.BlockSpec(a.shape, lambda g, j: (0,) * a.ndim)

    return pl.pallas_call(
        functools.partial(_inproj_kernel, n_tiles=n_tiles, n_tail=n_tail),
        grid=(G, n_tiles),
        in_specs=[pl.BlockSpec((1, tm, D_MODEL), lambda g, j: (g, j, 0)), full(nm), full(w_r),
                  full(aqn), full(akn), full(cqn), full(wup), full(bla), full(bg)],
        out_specs=[o[1] for o in outs],
        out_shape=[o[0] for o in outs],
        compiler_params=_params(("arbitrary", "arbitrary")),
        name="inproj",
    )(x, nm, w_r, aqn, akn, cqn, wup, bla, bg)


def _attend(pairs, lo_mask):
    T = pairs[0][0].shape[0]
    scores = []
    for q, parts in pairs:
        zero = jnp.zeros_like(q)
        q2 = jnp.concatenate([jnp.where(lo_mask, q, zero), jnp.where(lo_mask, zero, q)], axis=0)
        ss = []
        for (k, _, bias2, valid) in parts:
            s = lax.dot_general(q2, k, (((1,), (1,)), ((), ())), preferred_element_type=F32) + bias2
            if valid is not None:
                s = jnp.where(valid, s, NEG_INF)
            ss.append(s)
        scores.append(ss)
    probs, sums = [], []
    for ss in scores:
        m = ss[0].max(-1, keepdims=True)
        for s in ss[1:]:
            m = jnp.maximum(m, s.max(-1, keepdims=True))
        ps = [jnp.exp(s - m) for s in ss]
        l = ps[0].sum(-1, keepdims=True)
        for p in ps[1:]:
            l = l + p.sum(-1, keepdims=True)
        probs.append([p.astype(BF16) for p in ps])
        sums.append(l)
    outs = []
    for (q, parts), ps, l in zip(pairs, probs, sums):
        o = jnp.dot(ps[0], parts[0][1], preferred_element_type=F32)
        for p, part in zip(ps[1:], parts[1:]):
            o = o + jnp.dot(p, part[1], preferred_element_type=F32)
        o = o / l
        outs.append(jnp.where(lo_mask, o[:T], o[T:]))
    return outs


def _attn_prompt_kernel(q_ref, kp_ref, kc_ref, vp_ref, vc_ref, bias_ref, o_ref, *, tb):
    j = pl.program_id(1)
    has_prev = j > 0
    lo_mask = lax.broadcasted_iota(I32, (ATTN_SUB, LANES), 1) < A_HEAD_DIM
    for s in range(tb // ATTN_SUB):
        r0 = s * ATTN_SUB
        len_a = tb - r0
        len_b = r0 + ATTN_SUB
        pairs = []
        for p in range(A_WIDTH // LANES):
            c0 = p * LANES
            q = q_ref[0, r0:r0 + ATTN_SUB, c0:c0 + LANES]
            parts = [
                (kp_ref[0, r0:tb, c0:c0 + LANES], vp_ref[0, r0:tb, c0:c0 + LANES],
                 bias_ref[p, :, 0:len_a], has_prev),
                (kc_ref[0, 0:len_b, c0:c0 + LANES], vc_ref[0, 0:len_b, c0:c0 + LANES],
                 bias_ref[p, :, len_a:len_a + len_b], None),
            ]
            pairs.append((q, parts))
        outs = _attend(pairs[:2], lo_mask) + _attend(pairs[2:], lo_mask)
        for p, o in enumerate(outs):
            o_ref[0, r0:r0 + ATTN_SUB, p * LANES:(p + 1) * LANES] = o.astype(BF16)


def _rel_bias(table, n_q, n_k, offset):
    period = n_q + n_k - 1
    m = jnp.arange(period)
    u = table[:, jnp.clip(n_q - 1 + offset - m, -REL_MAX, REL_MAX) + REL_MAX].astype(F32)
    rows = jnp.tile(u, (1, n_q + 1))[:, :n_q * (period + 1)].reshape(-1, n_q, period + 1)[:, :, :n_k]
    return rows[:, ::-1, :]


def _band_bias(table):
    qc = jnp.arange(ATTN_SUB)[:, None] // CHUNK
    kc = jnp.arange(ATTN_SUB + WINDOW)[None, :] // CHUNK
    ok = (kc >= qc) & (kc <= qc + BAND_CHUNKS)
    return jnp.where(ok[None], _rel_bias(table, ATTN_SUB, ATTN_SUB + WINDOW, WINDOW), NEG_INF)


def _attn_prompt(aq, ak, av, table):
    B, S, _ = aq.shape
    tb = WINDOW
    assert S % tb == 0
    bias = _band_bias(table).reshape(A_WIDTH // LANES, 2 * ATTN_SUB, ATTN_SUB + WINDOW)
    cur = pl.BlockSpec((1, tb, A_WIDTH), lambda b, j: (b, j, 0))
    prev = pl.BlockSpec((1, tb, A_WIDTH), lambda b, j: (b, jnp.maximum(j - 1, 0), 0))
    return pl.pallas_call(
        functools.partial(_attn_prompt_kernel, tb=tb),
        grid=(B, S // tb),
        in_specs=[cur, prev, cur, prev, cur, pl.BlockSpec(bias.shape, lambda b, j: (0, 0, 0))],
        out_specs=cur,
        out_shape=jax.ShapeDtypeStruct((B, S, A_WIDTH), BF16),
        compiler_params=_params(("arbitrary", "arbitrary")),
        name="attn_prompt",
    )(aq, ak, ak, av, av, bias)


def _attn_sample_kernel(q_ref, k_ref, v_ref, bias_ref, o_ref):
    T = q_ref.shape[1]
    lo_mask = lax.broadcasted_iota(I32, (T, LANES), 1) < A_HEAD_DIM
    pairs = []
    for p in range(A_WIDTH // LANES):
        c0 = p * LANES
        parts = [(k_ref[0, :, c0:c0 + LANES], v_ref[0, :, c0:c0 + LANES], bias_ref[p], None)]
        pairs.append((q_ref[0, :, c0:c0 + LANES], parts))
    for p, o in enumerate(_attend(pairs, lo_mask)):
        o_ref[0, :, p * LANES:(p + 1) * LANES] = o.astype(BF16)


def _attn_sample(aq, ak, av, cache_k, cache_v, table):
    B, T, _ = aq.shape
    P = cache_k.shape[1]
    L = (P + T + LANES - 1) // LANES * LANES
    pad = jnp.zeros((B, L - P - T, A_WIDTH), BF16)
    kk = jnp.concatenate([cache_k.astype(BF16), ak, pad], axis=1)
    vv = jnp.concatenate([cache_v.astype(BF16), av, pad], axis=1)
    bias = jnp.where((jnp.arange(L) < P + T)[None, None, :], _rel_bias(table, T, L, P), NEG_INF)
    bias = bias.reshape(A_WIDTH // LANES, 2 * T, L)
    new = pl.BlockSpec((1, T, A_WIDTH), lambda b: (b, 0, 0))
    old = pl.BlockSpec((1, L, A_WIDTH), lambda b: (b, 0, 0))
    return pl.pallas_call(
        _attn_sample_kernel,
        grid=(B,),
        in_specs=[new, old, old, pl.BlockSpec(bias.shape, lambda b: (0, 0, 0))],
        out_specs=new,
        out_shape=jax.ShapeDtypeStruct((B, T, A_WIDTH), BF16),
        compiler_params=_params(("arbitrary",)),
        name="attn_sample",
    )(aq, kk, vv, bias)


def _gla_kernel(q_ref, k_ref, v_ref, la_ref, gr_ref, gain_ref, s0_ref, o_ref, sf_ref, s_scr, *, C, n_chunks,
                n_group):
    j = pl.program_id(1)

    @pl.when(j == 0)
    def _():
        s_scr[...] = s0_ref[0]

    n_sub = C // G_SUB
    ri = lax.broadcasted_iota(I32, (C, C), 0)
    ci = lax.broadcasted_iota(I32, (C, C), 1)
    tril = (ci <= ri).astype(BF16)
    lane_kw = lax.broadcasted_iota(I32, (1, G_KW), 1)
    head_of_lane = lane_kw // G_DK
    row_kw = lax.broadcasted_iota(I32, (C, G_KW), 0)
    ur = lax.broadcasted_iota(I32, (2 * C, 4 * C), 0) - C
    uc = lax.broadcasted_iota(I32, (2 * C, 4 * C), 1)
    u_mat = ((ur >= 0) & ((uc >= C) | (ur <= uc))).astype(BF16)

    def heads_on_rows(x):
        return jnp.concatenate([jnp.where(head_of_lane == h, x, 0.0) for h in range(G_HEADS)], axis=0)

    def chunk_group(g, S):
        ns = range(n_group)
        rs = [pl.multiple_of((g * n_group + n) * C, C) for n in ns]
        q = [q_ref[0, pl.ds(r, C), :].astype(F32) for r in rs]
        k = [k_ref[0, pl.ds(r, C), :].astype(F32) for r in rs]
        v = [v_ref[0, pl.ds(r, C), :] for r in rs]
        la = [la_ref[0, pl.ds(r, C), :] for r in rs]

        split = [_split_bf16(x) for x in la]
        b = [jnp.dot(tril, hi, preferred_element_type=F32) + jnp.dot(tril, lo, preferred_element_type=F32)
             for hi, lo in split]
        xt = [jnp.concatenate([k[n], la[n]], axis=0).T for n in ns]
        split_t = [_split_bf16(x) for x in xt]
        xb = [jnp.dot(hi, u_mat, preferred_element_type=F32) + jnp.dot(lo, u_mat, preferred_element_type=F32)
              for hi, lo in split_t]
        b_last = [x[:, LANES:] for x in xb]
        kd = [(xt[n] * jnp.exp(b_last[n] - xb[n][:, :LANES])).astype(BF16) for n in ns]
        zeros_v = jnp.zeros((C, G_VW), BF16)
        kv = [jnp.dot(kd[n], jnp.concatenate([v[n], zeros_v], axis=0), preferred_element_type=F32)
              for n in ns]
        kv_d = [jnp.concatenate([x[h * G_DK:(h + 1) * G_DK, h * G_DV:(h + 1) * G_DV] for h in range(G_HEADS)],
                                axis=0) for x in kv]

        states = [S]
        for n in ns:
            states.append(jnp.exp(b_last[n]) * states[n] + kv_d[n])

        r_inter = [jnp.dot(heads_on_rows(q[n] * jnp.exp(b[n])).astype(BF16), states[n].astype(BF16),
                           preferred_element_type=F32) for n in ns]
        o = [jnp.concatenate([x[h * C:(h + 1) * C] for h in range(G_HEADS)], axis=1) for x in r_inter]

        o_rows = [[] for _ in ns]
        for i in range(n_sub):
            r0, r1 = i * G_SUB, (i + 1) * G_SUB
            atts = []
            for n in ns:
                bs = b[n][r0 - 1:r0] if i > 0 else jnp.zeros((1, G_KW), F32)
                qe = q[n][r0:r1] * jnp.exp(b[n][r0:r1] - bs)
                ke = (k[n] * jnp.exp(jnp.where(row_kw < r1, bs - b[n], -jnp.inf))).astype(BF16)
                att = lax.dot_general(heads_on_rows(qe).astype(BF16), ke, (((1,), (1,)), ((), ())),
                                      preferred_element_type=F32)
                tt = lax.broadcasted_iota(I32, att.shape, 0) % G_SUB + r0
                ss = lax.broadcasted_iota(I32, att.shape, 1)
                atts.append(jnp.where(ss <= tt, att, 0.0).astype(BF16))
            for n in ns:
                ov = jnp.dot(atts[n], v[n], preferred_element_type=F32)
                o_rows[n].append(jnp.concatenate(
                    [ov[h * G_SUB:(h + 1) * G_SUB, h * G_DV:(h + 1) * G_DV] for h in range(G_HEADS)], axis=1))

        for n in ns:
            on = o[n] + jnp.concatenate(o_rows[n], axis=0)
            on = _head_rms(on, G_DV) * gain_ref[...] * gr_ref[0, pl.ds(rs[n], C), :].astype(F32)
            o_ref[0, pl.ds(rs[n], C), :] = on.astype(BF16)
        return states[-1]

    s_scr[...] = lax.fori_loop(0, n_chunks // n_group, chunk_group, s_scr[...])

    @pl.when(j == pl.num_programs(1) - 1)
    def _():
        sf_ref[0] = s_scr[...]


def _gla(gq, gk, gv, la, gr, gain, s0):
    B, T, _ = gq.shape
    C = CHUNK
    tb = min(ROW_TILE, T)
    assert T % tb == 0 and tb % C == 0 and C % G_SUB == 0 and 2 * C == LANES
    kw = pl.BlockSpec((1, tb, G_KW), lambda b, j: (b, j, 0))
    vw = pl.BlockSpec((1, tb, G_VW), lambda b, j: (b, j, 0))
    st = pl.BlockSpec((1, G_KW, G_DV), lambda b, j: (b, 0, 0))
    return pl.pallas_call(
        functools.partial(_gla_kernel, C=C, n_chunks=tb // C, n_group=8 if (tb // C) % 8 == 0 else 1),
        grid=(B, T // tb),
        in_specs=[kw, kw, vw, kw, vw, pl.BlockSpec(gain.shape, lambda b, j: (0, 0)), st],
        out_specs=[vw, st],
        out_shape=[jax.ShapeDtypeStruct((B, T, G_VW), BF16), jax.ShapeDtypeStruct((B, G_KW, G_DV), F32)],
        scratch_shapes=[pltpu.VMEM((G_KW, G_DV), F32)],
        compiler_params=_params(("arbitrary", "arbitrary")),
        name="gla",
    )(gq, gk, gv, la, gr, gain, s0)


def _memkv_kernel(mem_ref, gm_ref, w_ref, gk_ref, k_ref, v_ref):
    x = mem_ref[0]
    h = (x * lax.rsqrt(jnp.mean(x * x, -1, keepdims=True) + EPS) * gm_ref[...]).astype(BF16)
    k = jnp.dot(h, w_ref[:, :C_WIDTH], preferred_element_type=F32)
    k_ref[0] = _head_rms(k, C_HEAD_DIM) * gk_ref[...]
    v_ref[0] = jnp.dot(h, w_ref[:, C_WIDTH:], preferred_element_type=F32)


def _memkv(mem, gm, w_kv, gk):
    B, M, _ = mem.shape
    out = pl.BlockSpec((1, M, C_WIDTH), lambda b: (b, 0, 0))
    return pl.pallas_call(
        _memkv_kernel,
        grid=(B,),
        in_specs=[pl.BlockSpec((1, M, D_MODEL), lambda b: (b, 0, 0)),
                  pl.BlockSpec(gm.shape, lambda b: (0, 0)),
                  pl.BlockSpec(w_kv.shape, lambda b: (0, 0)),
                  pl.BlockSpec(gk.shape, lambda b: (0, 0))],
        out_specs=[out, out],
        out_shape=[jax.ShapeDtypeStruct((B, M, C_WIDTH), F32)] * 2,
        compiler_params=_params(("arbitrary",)),
        name="memkv",
    )(mem, gm, w_kv, gk)


def _merge_kernel(x_ref, ya_ref, yb_ref, cq_ref, gt_ref, mk_ref, mv_ref, wa_ref, wb_ref, wc_ref,
                  wo_ref, nf_ref, wr_ref, br_ref,
                  x1_ref, hp_ref, ri_ref, rw_ref, cnt_ref, run_scr, *, tok_base):
    first =jnp.logical_and(pl.program_id(0) == 0, pl.program_id(1) == 0)

    @pl.when(first)
    def _():
        run_scr[...] = jnp.zeros_like(run_scr)

    tm = x_ref.shape[1]
    n_part = 2 if tm % 256 == 0 else 1
    parts = [slice(n * (tm // n_part), (n + 1) * (tm // n_part)) for n in range(n_part)]
    heads = [slice(h * C_HEAD_DIM, (h + 1) * C_HEAD_DIM) for h in range(C_HEADS)]
    mk = [mk_ref[0, :, c].astype(BF16) for c in heads]
    mv = [mv_ref[0, :, c].astype(BF16) for c in heads]
    scores = [[lax.dot_general(cq_ref[0, r, c], k, (((1,), (1,)), ((), ())), preferred_element_type=F32)
               * (C_HEAD_DIM ** -0.5) for c, k in zip(heads, mk)] for r in parts]
    probs = [[jnp.exp(s - s.max(-1, keepdims=True)) for s in ss] for ss in scores]
    sums = [[p.sum(-1, keepdims=True) for p in ps] for ps in probs]
    yc_in = [jnp.concatenate([jnp.dot(p.astype(BF16), v, preferred_element_type=F32) / l
                              for p, l, v in zip(ps, ls, mv)], axis=-1).astype(BF16)
             for ps, ls in zip(probs, sums)]
    y_a = [jnp.dot(ya_ref[0, r, :], wa_ref[...], preferred_element_type=F32) for r in parts]
    y_b = [jnp.dot(yb_ref[0, r, :], wb_ref[...], preferred_element_type=F32) for r in parts]
    y_c = [jnp.dot(y, wc_ref[...], preferred_element_type=F32) for y in yc_in]
    merged = [(gt_ref[0, r, 0:D_MODEL].astype(F32) * a
               + gt_ref[0, r, D_MODEL:2 * D_MODEL].astype(F32) * b
               + gt_ref[0, r, 2 * D_MODEL:3 * D_MODEL].astype(F32) * c).astype(BF16)
              for r, a, b, c in zip(parts, y_a, y_b, y_c)]
    x1s = [x_ref[0, r, :] + jnp.dot(m, wo_ref[...], preferred_element_type=F32) for r, m in zip(parts, merged)]
    for r, v in zip(parts, x1s):
        x1_ref[0, r, :] = v
    h2s = [v * lax.rsqrt(jnp.mean(v * v, -1, keepdims=True) + EPS) * nf_ref[...] for v in x1s]

    logits = jnp.concatenate([jnp.dot(h.astype(BF16), wr_ref[...], preferred_element_type=F32) for h in h2s],
                             axis=0) + br_ref[...]
    h2 = jnp.concatenate(h2s, axis=0)
    lane = lax.broadcasted_iota(I32, (tm, LANES), 1)
    lane_f = lane.astype(F32)
    vals, sels, idxs = [], [], []
    l = logits
    for _ in range(TOP_K):
        m = l.max(-1, keepdims=True)
        idx = jnp.min(jnp.where(l == m, lane_f, float(LANES)), -1, keepdims=True)
        sel = lane_f == idx
        vals.append(m)
        idxs.append(idx)
        sels.append(sel)
        l = jnp.where(sel, -3e38, l)
    es = [jnp.exp(vk - vals[0]) for vk in vals]
    den = es[0] + es[1] + es[2] + es[3]
    cnt = jnp.zeros((tm, LANES), F32)
    for sel in sels:
        cnt = cnt + jnp.where(sel, 1.0, 0.0)
    tp = max(tm, LANES)
    cnt_p = cnt if tp == tm else jnp.concatenate([cnt, jnp.zeros((tp - tm, LANES), F32)], axis=0)
    ri = lax.broadcasted_iota(I32, (tp, tp), 0)
    ci = lax.broadcasted_iota(I32, (tp, tp), 1)
    before = jnp.dot(jnp.where(ci < ri, 1.0, 0.0).astype(BF16), cnt_p.astype(BF16),
                     preferred_element_type=F32)[0:tm] + run_scr[0:1, :]
    r_i = jnp.zeros((tm, LANES), I32)
    r_w = jnp.zeros((tm, LANES), F32)
    for kk in range(TOP_K):
        rank = jnp.sum(jnp.where(sels[kk], before, 0.0), -1, keepdims=True)
        r_i = jnp.where(lane == kk, idxs[kk].astype(I32), r_i)
        r_i = jnp.where(lane == TOP_K + kk, rank.astype(I32), r_i)
        r_w = jnp.where(lane == kk, es[kk] / den, r_w)
    ri_ref[0] = r_i
    rw_ref[0] = r_w
    run_scr[...] = run_scr[...] + jnp.sum(cnt, axis=0, keepdims=True)
    cnt_ref[...] = run_scr[...]

    tok = tok_base + (pl.program_id(0) * pl.num_programs(1) + pl.program_id(1)) * tm \
        + lax.broadcasted_iota(I32, (tm, LANES), 0)
    meta = jnp.where(lane == 0, tok, 0)
    for kk in range(TOP_K):
        meta = jnp.where(lane == 1 + kk, idxs[kk].astype(I32), meta)
    packed = _pack_rows(h2)
    zero = jnp.zeros((tm, LANES), U32)
    slabs = [packed[:, j * LANES:(j + 1) * LANES] for j in range(SLABS)]
    slabs += [lax.bitcast_convert_type(meta, U32)] + [zero] * (ROW_SLABS - SLABS - 1)
    hp_ref[0] = pltpu.einshape("jtl->tjl", jnp.stack(slabs, axis=0))


def _merge(x, ya, yb, cq, gt, mk, mv, wa, wb, wc, wo, nf, wr, br, tok_base):
    B, T, _ = x.shape
    tm = min(ROW_TILE, T)
    assert T % tm == 0

    def row(width):
        return pl.BlockSpec((1, tm, width), lambda b, j: (b, j, 0))

    def full(a):
        return pl.BlockSpec(a.shape, lambda b, j: (0,) * a.ndim)

    mem = pl.BlockSpec((1, N_MEM, C_WIDTH), lambda b, j: (b, 0, 0))
    return pl.pallas_call(
        functools.partial(_merge_kernel, tok_base=tok_base),
        grid=(B, T // tm),
        in_specs=[row(D_MODEL), row(A_WIDTH), row(G_VW), row(C_WIDTH), row(N_BRANCH * D_MODEL), mem, mem,
                  full(wa), full(wb), full(wc), full(wo), full(nf), full(wr), full(br)],
        out_specs=[row(D_MODEL), pl.BlockSpec((1, tm, ROW_SLABS, LANES), lambda b, j: (b, j, 0, 0)), row(LANES),
                   row(LANES), pl.BlockSpec((8, LANES), lambda b, j: (0, 0))],
        out_shape=[jax.ShapeDtypeStruct((B, T, D_MODEL), F32), jax.ShapeDtypeStruct((B, T, ROW_SLABS, LANES), U32),
                   jax.ShapeDtypeStruct((B, T, LANES), I32), jax.ShapeDtypeStruct((B, T, LANES), F32),
                   jax.ShapeDtypeStruct((8, LANES), F32)],
        scratch_shapes=[pltpu.VMEM((8, LANES), F32)],
        compiler_params=_params(("arbitrary", "arbitrary")),
        name="merge",
    )(x, ya, yb, cq, gt, mk, mv, wa, wb, wc, wo, nf, wr, br)


def _dispatch_kernel(dest_ref, dest2_ref, pad_lo_ref, pad_n_ref, hp_ref, hp2_ref, xs_ref, zero_scr, sem, sem2,
                     psem, *, tm, tm2, n_pad, n_tokens):
    def scatter(dref, href, s, n):
        def issue(t, c):
            for kk in range(TOP_K):
                pltpu.make_async_copy(href.at[t], xs_ref.at[dref[0, 0, t * TOP_K + kk]], s).start(priority=kk % 2)
            return c
        lax.fori_loop(0, n, issue, 0, unroll=8)

    def drain(href, s, n):
        for _ in range(TOP_K):
            pltpu.make_async_copy(href, xs_ref.at[pl.ds(0, n)], s).wait()

    scatter(dest_ref, hp_ref, sem, tm)

    @pl.when(pl.program_id(0) == 0)
    def _():
        scatter(dest2_ref, hp2_ref, sem2, tm2)
        shape = (FFN_BLOCK, ROW_SLABS, LANES)
        sub = lax.broadcasted_iota(I32, shape, 1)
        lane = lax.broadcasted_iota(I32, shape, 2)
        filler = jnp.where((sub == META) & (lane == 0), n_tokens,
                           jnp.where((sub == META) & (lane <= TOP_K), N_EXPERTS, 0))
        zero_scr[...] = lax.bitcast_convert_type(filler, U32)

        log_blk = FFN_BLOCK.bit_length() - 1

        def span(e, wait):
            lo, n = pad_lo_ref[e], pad_n_ref[e]
            whole = n >> log_blk

            def go(cp):
                cp.wait() if wait else cp.start()

            def block(j, c):
                go(pltpu.make_async_copy(zero_scr, xs_ref.at[pl.ds(lo + j * FFN_BLOCK, FFN_BLOCK)], psem))
                return c

            lax.fori_loop(0, whole, block, 0)
            rem = n - (whole << log_blk)
            for b in reversed(range(log_blk)):
                off = lo + (whole << log_blk) + ((rem >> (b + 1)) << (b + 1))

                @pl.when((rem >> b) & 1 == 1)
                def _():
                    go(pltpu.make_async_copy(zero_scr.at[pl.ds(0, 1 << b)], xs_ref.at[pl.ds(off, 1 << b)], psem))

        def fill(e, c):
            span(e, False)
            return c

        def fill_wait(e, c):
            span(e, True)
            return c

        lax.fori_loop(0, n_pad, fill, 0)
        lax.fori_loop(0, n_pad, fill_wait, 0)
        drain(hp2_ref, sem2, tm2)

    drain(hp_ref, sem, tm)


def _dispatch(dest, hp, dest2, hp2, pad_lo, pad_n, rows):
    N, N2 = hp.shape[0], hp2.shape[0]
    tm = min(ROW_TILE, N)
    assert N % tm == 0
    n_steps = N // tm
    smem = functools.partial(pl.BlockSpec, memory_space=pltpu.SMEM)
    n_pad = pad_lo.shape[0]
    return pl.pallas_call(
        functools.partial(_dispatch_kernel, tm=tm, tm2=N2, n_pad=n_pad, n_tokens=N + N2),
        grid=(n_steps,),
        in_specs=[smem((1, 1, tm * TOP_K), lambda i: (i, 0, 0)),
                  smem((1, 1, N2 * TOP_K), lambda i: (0, 0, 0)),
                  smem((n_pad,), lambda i: (0,)), smem((n_pad,), lambda i: (0,)),
                  pl.BlockSpec((tm, ROW_SLABS, LANES), lambda i: (i, 0, 0)),
                  pl.BlockSpec((N2, ROW_SLABS, LANES), lambda i: (0, 0, 0))],
        out_specs=pl.BlockSpec(memory_space=pl.ANY),
        out_shape=jax.ShapeDtypeStruct((rows, ROW_SLABS, LANES), U32),
        scratch_shapes=[pltpu.VMEM((FFN_BLOCK, ROW_SLABS, LANES), U32), pltpu.SemaphoreType.DMA(()),
                        pltpu.SemaphoreType.DMA(()), pltpu.SemaphoreType.DMA(())],
        compiler_params=_params(("arbitrary",)),
        name="moe_dispatch",
    )(dest.reshape(n_steps, 1, tm * TOP_K), dest2.reshape(1, 1, N2 * TOP_K), pad_lo, pad_n, hp, hp2)


def _ffn_kernel(be_ref, nu_ref, x_ref, wgu_ref, bgu_ref, wd_ref, bd_ref, y4_ref,
                wgu_bf, wd_bf, ybuf, idv, ids_smem, sem_ids, sem_rows, *, n_blk, n_tokens):
    i = pl.program_id(0)
    n_used = nu_ref[0]
    n_assign = n_tokens * TOP_K
    prev = be_ref[jnp.maximum(i - 1, 0)]
    new_expert = jnp.logical_or(i == 0, be_ref[i] != prev)

    def rows_done(s):
        pltpu.make_async_copy(ybuf.at[s], y4_ref.at[pl.ds(0, FFN_BLOCK)], sem_rows.at[s]).wait()

    def ids_copy(s):
        return pltpu.make_async_copy(idv, ids_smem.at[s], sem_ids)

    def start_rows(s, lo, hi):
        for r in range(lo, hi):
            pltpu.make_async_copy(ybuf.at[s, r], y4_ref.at[ids_smem[s, 0, r]],
                                  sem_rows.at[s]).start(priority=r % 2)

    @pl.when(i == 0)
    def _():
        ybuf[1] = jnp.zeros(ybuf.shape[1:], U32)
        spare = pltpu.make_async_copy(ybuf.at[1], y4_ref.at[pl.ds(n_assign, FFN_BLOCK)], sem_rows.at[1])
        spare.start()
        spare.wait()

    @pl.when(new_expert)
    def _():
        wgu_bf[...] = wgu_ref[0].astype(BF16)
        wd_bf[...] = wd_ref[0].astype(BF16)

    def compute(s, flush):
        q = FFN_BLOCK // 4
        if flush:
            ids_copy(1 - s).wait()
            start_rows(1 - s, 0, q)
        xm = pltpu.einshape("tjl->jtl", x_ref[...])
        x_lo, x_hi = _unpack_rows(jnp.concatenate([xm[j] for j in range(SLABS)], axis=-1))
        meta = lax.bitcast_convert_type(xm[META], I32)
        tok = meta[:, 0:1]
        choice = jnp.zeros_like(tok)
        for kk in range(1, TOP_K):
            choice = jnp.where(meta[:, 1 + kk:2 + kk] == be_ref[i], kk, choice)
        local = lax.broadcasted_iota(I32, tok.shape, 0)
        dest = jnp.where(tok >= n_tokens, n_assign + local, tok * TOP_K + choice)
        dest_t = jnp.broadcast_to(dest.astype(F32), (FFN_BLOCK, LANES)).T
        idv[...] = dest_t[0:8].astype(I32)
        ids_copy(s).start()
        gu = (jnp.dot(x_lo.astype(BF16), wgu_bf[0:HALF, :], preferred_element_type=F32)
              + jnp.dot(x_hi.astype(BF16), wgu_bf[HALF:, :], preferred_element_type=F32)
              + bgu_ref[0])
        if flush:
            start_rows(1 - s, q, 2 * q)
        gate = jnp.minimum(gu[:, :D_EXPERT], SWIGLU_LIMIT)
        up = jnp.clip(gu[:, D_EXPERT:], -SWIGLU_LIMIT, SWIGLU_LIMIT)
        act = (up + 1.0) * (gate * _sigmoid(SWIGLU_ALPHA * gate))
        if flush:
            start_rows(1 - s, 2 * q, 3 * q)
        y = jnp.dot(act.astype(BF16), wd_bf[...], preferred_element_type=F32) + bd_ref[0]
        if flush:
            start_rows(1 - s, 3 * q, 4 * q)
        packed = _pack_rows(y)
        pl.when(i >= 2)(functools.partial(rows_done, s))
        _store_slabs(ybuf, (s,), packed)

    def flush_only(s, also):
        ids_copy(s).wait()
        start_rows(s, 0, FFN_BLOCK)
        pl.when(also)(functools.partial(rows_done, 1 - s))
        rows_done(s)

    real = i < n_used
    prev_real = jnp.logical_and(i >= 1, i - 1 < n_used)
    for s in range(2):
        mine = i % 2 == s
        pl.when(mine & real & prev_real)(functools.partial(compute, s, True))
        pl.when(mine & real & jnp.logical_not(prev_real))(functools.partial(compute, s, False))
        pl.when(mine & jnp.logical_not(real) & prev_real)(functools.partial(flush_only, 1 - s, i >= 2))
        pl.when(mine & real & (i == n_blk - 1))(functools.partial(flush_only, s, i >= 1))


def _ffn(blk_expert, n_used, xs, n_tokens, w_gu, b_gu, w_d, b_d):
    P = xs.shape[0]
    nblk = P // FFN_BLOCK
    grid_spec = pltpu.PrefetchScalarGridSpec(
        num_scalar_prefetch=2,
        grid=(nblk,),
        in_specs=[pl.BlockSpec((FFN_BLOCK, ROW_SLABS, LANES), lambda i, be, nu: (jnp.minimum(i, nu[0] - 1), 0, 0)),
                  pl.BlockSpec((1, D_MODEL, 2 * D_EXPERT), lambda i, be, nu: (be[i], 0, 0)),
                  pl.BlockSpec((1, 1, 2 * D_EXPERT), lambda i, be, nu: (be[i], 0, 0)),
                  pl.BlockSpec((1, D_EXPERT, D_MODEL), lambda i, be, nu: (be[i], 0, 0)),
                  pl.BlockSpec((1, 1, D_MODEL), lambda i, be, nu: (be[i], 0, 0))],
        out_specs=pl.BlockSpec(memory_space=pl.ANY),
        scratch_shapes=[pltpu.VMEM((D_MODEL, 2 * D_EXPERT), BF16), pltpu.VMEM((D_EXPERT, D_MODEL), BF16),
                        pltpu.VMEM((2, FFN_BLOCK, SLABS, LANES), U32), pltpu.VMEM((8, FFN_BLOCK), I32),
                        pltpu.SMEM((2, 8, FFN_BLOCK), I32), pltpu.SemaphoreType.DMA(()),
                        pltpu.SemaphoreType.DMA((2,))],
    )
    return pl.pallas_call(
        functools.partial(_ffn_kernel, n_blk=nblk, n_tokens=n_tokens),
        grid_spec=grid_spec,
        out_shape=jax.ShapeDtypeStruct((n_tokens * TOP_K + FFN_BLOCK, SLABS, LANES), U32),
        compiler_params=_params(("arbitrary",)),
        name="moe_ffn",
    )(blk_expert, n_used, xs, w_gu, b_gu.reshape(N_EXPERTS, 1, -1), w_d, b_d.reshape(N_EXPERTS, 1, -1))


def _combine_kernel(x1_ref, rw_ref, y_ref, o_ref, *, tm):
    w = rw_ref[...]
    acc_lo = x1_ref[:, :HALF]
    acc_hi = x1_ref[:, HALF:]
    for kk in range(TOP_K):
        rows = y_ref[pl.ds(kk, tm, stride=TOP_K)]
        slab_major = pltpu.einshape("tjl->jtl", rows)
        lo, hi = _unpack_rows(jnp.concatenate([slab_major[j] for j in range(SLABS)], axis=-1))
        wk = w[:, kk:kk + 1]
        acc_lo = acc_lo + wk * lo
        acc_hi = acc_hi + wk * hi
    o_ref[:, :HALF] = acc_lo
    o_ref[:, HALF:] = acc_hi


def _combine(x1, rw, y4, tok_base):
    N = x1.shape[0]
    tm = min(256, N)
    assert N % tm == 0 and tok_base % tm == 0
    first = tok_base // tm
    return pl.pallas_call(
        functools.partial(_combine_kernel, tm=tm),
        grid=(N // tm,),
        in_specs=[pl.BlockSpec((tm, D_MODEL), lambda i: (i, 0)),
                  pl.BlockSpec((tm, LANES), lambda i: (i, 0)),
                  pl.BlockSpec((tm * TOP_K, SLABS, LANES), lambda i: (first + i, 0, 0))],
        out_specs=pl.BlockSpec((tm, D_MODEL), lambda i: (i, 0)),
        out_shape=jax.ShapeDtypeStruct((N, D_MODEL), F32),
        compiler_params=_params(("arbitrary",)),
        name="moe_combine",
    )(x1, rw, y4)


def _moe(groups, w_gu, b_gu, w_d, b_d):
    n_tokens = sum(g[0].shape[0] for g in groups)
    n_assign = n_tokens * TOP_K
    nblk = (n_assign + N_EXPERTS * (FFN_BLOCK - 1) + FFN_BLOCK - 1) // FFN_BLOCK
    counts = [g[4].astype(I32) for g in groups]
    total = sum(counts)
    padded = (total + FFN_BLOCK - 1) // FFN_BLOCK * FFN_BLOCK
    pend = jnp.cumsum(padded)
    pstart = pend - padded
    n_used = pend[-1:] // FFN_BLOCK
    blk = jnp.minimum(jnp.arange(nblk, dtype=I32), n_used[0] - 1) * FFN_BLOCK
    blk_expert = jnp.minimum(jnp.sum(pend[None, :] <= blk[:, None], axis=1), N_EXPERTS - 1).astype(I32)

    experts = jnp.arange(N_EXPERTS, dtype=I32)
    dests = []
    base = pstart
    for g, c in zip(groups, counts):
        idx, rank = g[2][:, :TOP_K], g[2][:, TOP_K:2 * TOP_K]
        dests.append(jnp.sum(jnp.where(idx[..., None] == experts, base, 0), axis=-1) + rank)
        base = base + c

    rows = nblk * FFN_BLOCK
    pad_lo = jnp.concatenate([pstart + total, pend[-1:]])
    pad_n = jnp.concatenate([padded - total, rows - pend[-1:]])
    (g_main, g_small), (d_main, d_small) = groups, dests
    xs = _dispatch(d_main, g_main[1], d_small, g_small[1], pad_lo, pad_n, rows)
    y4 = _ffn(blk_expert, n_used.astype(I32), xs, n_tokens, w_gu, b_gu, w_d, b_d)
    return [_combine(g_main[0], g_main[3], y4, 0), _combine(g_small[0], g_small[3], y4, g_main[0].shape[0])]


def _tile_lanes(g, reps):
    return jnp.tile(g.astype(F32), reps)[None, :]


def kernel(x_prompt, x_sample, mem_prompt, cache_attn_k, cache_attn_v, state_gla, cache_mem_k, cache_mem_v, norm_mix, w_in, a_q_norm, a_k_norm, rel_bias_table, w_a_o, w_gla_a_up, b_gla_a, gla_out_norm, w_b_o, c_q_norm, c_k_norm, norm_mem, w_mem_kv, w_c_o, b_gate, w_out, norm_ffn, w_router, b_router, w_gate_up, b_gate_up, w_down, b_down):
    depth = norm_mix.shape[0]
    assert depth == 1
    l = 0
    B, S, _ = x_prompt.shape
    Bs, Ts, _ = x_sample.shape
    keep = min(WINDOW, S)

    w = w_in[l]
    sizes = (A_WIDTH, A_WIDTH, A_WIDTH, G_KW, G_KW, G_VW, G_VW, G_RANK, C_WIDTH, N_BRANCH * D_MODEL)
    offs = [0]
    for s_ in sizes:
        offs.append(offs[-1] + s_)
    seg = [w[:, offs[i]:offs[i + 1]] for i in range(len(sizes))]
    w_r = jnp.concatenate(seg[0:7] + [seg[8], seg[9], seg[7], jnp.zeros((D_MODEL, LANES - G_RANK), F32)],
                          axis=1).astype(BF16)
    nm = norm_mix[l][None, :]
    aqn = _tile_lanes(a_q_norm[l], A_HEADS)
    akn = _tile_lanes(a_k_norm[l], A_HEADS)
    cqn = _tile_lanes(c_q_norm[l], C_HEADS)
    ckn = _tile_lanes(c_k_norm[l], C_HEADS)
    gon = _tile_lanes(gla_out_norm[l], G_HEADS)
    wup = jnp.concatenate([w_gla_a_up[l], jnp.zeros((LANES - G_RANK, G_KW), F32)], axis=0).astype(BF16)
    bla = b_gla_a[l][None, :]
    bg = b_gate[l][None, :]
    wa, wb, wc, wo = (t[l].astype(BF16) for t in (w_a_o, w_b_o, w_c_o, w_out))
    nf = norm_ffn[l][None, :]
    wr = jnp.concatenate([w_router[l], jnp.zeros((D_MODEL, LANES - N_EXPERTS), F32)], axis=1).astype(BF16)
    br = jnp.concatenate([b_router[l], jnp.full((LANES - N_EXPERTS,), NEG_INF, F32)])[None, :]
    table = rel_bias_table[l]

    mk, mv = _memkv(mem_prompt, norm_mem[l][None, :], w_mem_kv[l].astype(BF16), ckn)
    (aq, ak, av, gq, gk, gv, gr, la, cq, gt, ak_tail, av_tail) = _inproj(
        x_prompt, keep, nm, w_r, aqn, akn, cqn, wup, bla, bg)
    ya = _attn_prompt(aq, ak, av, table)
    yb, s_prompt = _gla(gq, gk, gv, la, gr, gon, jnp.zeros((B, G_KW, G_DV), F32))
    x1_p, hp_p, ri_p, rw_p, cnt_p = _merge(x_prompt, ya, yb, cq, gt, mk, mv, wa, wb, wc, wo, nf, wr, br, 0)

    (aq, ak, av, gq, gk, gv, gr, la, cq, gt, ak_new, av_new) = _inproj(
        x_sample.reshape(1, Bs * Ts, D_MODEL), Bs * Ts, nm, w_r, aqn, akn, cqn, wup, bla, bg)
    rs = lambda t: t.reshape(Bs, Ts, t.shape[-1])
    P = cache_attn_k.shape[2]
    ya = _attn_sample(rs(aq), rs(ak), rs(av), cache_attn_k[l].reshape(Bs, P, A_WIDTH),
                      cache_attn_v[l].reshape(Bs, P, A_WIDTH), table)
    t_pad = (Ts + CHUNK - 1) // CHUNK * CHUNK
    zp = lambda t: jnp.pad(rs(t), ((0, 0), (0, t_pad - Ts), (0, 0)))
    yb, s_sample = _gla(zp(gq), zp(gk), zp(gv), zp(la), zp(gr), gon, state_gla[l].reshape(Bs, G_KW, G_DV))
    yb = yb[:, :Ts]
    x1_s, hp_s, ri_s, rw_s, cnt_s = _merge(
        x_sample, ya, yb, rs(cq), rs(gt), cache_mem_k[l].reshape(Bs, N_MEM, C_WIDTH),
        cache_mem_v[l].reshape(Bs, N_MEM, C_WIDTH), wa, wb, wc, wo, nf, wr, br, B * S)

    flat = lambda t: t.reshape((-1,) + t.shape[2:])
    y_p, y_s = _moe(
        [(flat(x1_p), flat(hp_p), flat(ri_p), flat(rw_p), cnt_p[0, :N_EXPERTS]),
         (flat(x1_s), flat(hp_s), flat(ri_s), flat(rw_s), cnt_s[0, :N_EXPERTS])],
        w_gate_up[l], b_gate_up[l], w_down[l], b_down[l])

    return (y_p.reshape(B, S, D_MODEL), y_s.reshape(Bs, Ts, D_MODEL),
            ak_tail.reshape(1, B, keep, A_HEADS, A_HEAD_DIM), av_tail.reshape(1, B, keep, A_HEADS, A_HEAD_DIM),
            s_prompt.reshape(1, B, G_HEADS, G_DK, G_DV),
            mk.reshape(1, B, N_MEM, C_HEADS, C_HEAD_DIM), mv.reshape(1, B, N_MEM, C_HEADS, C_HEAD_DIM),
            ak_new.reshape(1, Bs, Ts, A_HEADS, A_HEAD_DIM), av_new.reshape(1, Bs, Ts, A_HEADS, A_HEAD_DIM),
            s_sample.reshape(1, Bs, G_HEADS, G_DK, G_DV))
```

```python
import functools

import jax
import jax.numpy as jnp
from jax import lax
from jax.experimental import pallas as pl
from jax.experimental.pallas import tpu as pltpu

F32 = jnp.float32
BF16 = jnp.bfloat16
U32 = jnp.uint32
I32 = jnp.int32

D_MODEL = 1024
CHUNK = 64
BAND_CHUNKS = 8
WINDOW = BAND_CHUNKS * CHUNK
N_MEM = 256
A_HEADS, A_HEAD_DIM = 8, 64
A_WIDTH = A_HEADS * A_HEAD_DIM
REL_MAX = 128
G_HEADS, G_DK, G_DV = 4, 64, 128
G_KW, G_VW = G_HEADS * G_DK, G_HEADS * G_DV
G_RANK = 16
G_TAU = 16.0
G_SUB = 16
C_HEADS, C_HEAD_DIM = 4, 128
C_WIDTH = C_HEADS * C_HEAD_DIM
N_BRANCH = 3
N_EXPERTS = 32
TOP_K = 4
D_EXPERT = 1024
SWIGLU_LIMIT = 7.0
SWIGLU_ALPHA = 1.702
EPS = 1e-6
NEG_INF = -1e30

LANES = 128
SUBLANES = 8
HALF = D_MODEL // 2
ROW_TILE = 512
ATTN_SUB = 128
FFN_BLOCK = 512
VMEM_LIMIT = 56 * 1024 * 1024

OFF_AQ, OFF_AK, OFF_AV = 0, 512, 1024
OFF_GQ, OFF_GK, OFF_GV, OFF_GR = 1536, 1792, 2048, 2560
OFF_CQ, OFF_GATE, OFF_LR = 3072, 3584, 6656
IN_COLS = OFF_LR + LANES


def _params(sem):
    return pltpu.CompilerParams(dimension_semantics=sem, vmem_limit_bytes=VMEM_LIMIT)


def _sigmoid(x):
    return 0.5 * jnp.tanh(0.5 * x) + 0.5


def _head_rms(y, head_dim):
    cols = []
    for p in range(y.shape[1] // LANES):
        blk = y[:, p * LANES:(p + 1) * LANES]
        sq = blk * blk
        if head_dim == LANES:
            sc = lax.rsqrt(jnp.sum(sq, -1, keepdims=True) * (1.0 / LANES) + EPS)
        else:
            lo = lax.broadcasted_iota(I32, blk.shape, 1) < head_dim
            s_lo = jnp.sum(jnp.where(lo, sq, 0.0), -1, keepdims=True)
            s_hi = jnp.sum(jnp.where(lo, 0.0, sq), -1, keepdims=True)
            sc = jnp.where(lo, lax.rsqrt(s_lo * (1.0 / head_dim) + EPS),
                           lax.rsqrt(s_hi * (1.0 / head_dim) + EPS))
        cols.append(blk * sc)
    return jnp.concatenate(cols, axis=-1)


def _split_bf16(x):
    hi = x.astype(BF16)
    lo = (x - hi.astype(F32)).astype(BF16)
    return hi, lo


def _pack_rows(x):
    lo = lax.bitcast_convert_type(x[:, :HALF].astype(BF16).astype(F32), U32)
    hi = lax.bitcast_convert_type(x[:, HALF:].astype(BF16).astype(F32), U32)
    return (lo >> 16) | (hi & jnp.uint32(0xFFFF0000))


def _unpack_rows(u):
    lo = lax.bitcast_convert_type(u << 16, F32)
    hi = lax.bitcast_convert_type(u & jnp.uint32(0xFFFF0000), F32)
    return lo, hi


SLABS = HALF // LANES
ROW_SLABS = SUBLANES
META = SLABS


def _store_slabs(ref, lead, u):
    slab_major = jnp.stack([u[:, j * LANES:(j + 1) * LANES] for j in range(SLABS)], axis=0)
    ref[lead + (slice(None),) * 3] = pltpu.einshape("jtl->tjl", slab_major)


def _inproj_kernel(x_ref, nm_ref, w_ref, aqn_ref, akn_ref, cqn_ref, wup_ref, bla_ref, bg_ref,
                   aq_ref, ak_ref, av_ref, gq_ref, gk_ref, gv_ref, gr_ref, la_ref, cq_ref, gt_ref,
                   akt_ref, avt_ref, *, n_tiles, n_tail):
    j = pl.program_id(1)
    x = x_ref[0]
    h = (x * lax.rsqrt(jnp.mean(x * x, -1, keepdims=True) + EPS) * nm_ref[...]).astype(BF16)

    def seg(off, width):
        return jnp.dot(h, w_ref[:, off:off + width], preferred_element_type=F32)

    in_tail = j >= n_tiles - n_tail

    lr = seg(OFF_LR, LANES).astype(BF16)

    aq = _head_rms(seg(OFF_AQ, A_WIDTH), A_HEAD_DIM) * aqn_ref[...] * (A_HEAD_DIM ** -0.5)
    aq_ref[0] = aq.astype(BF16)

    ak = _head_rms(seg(OFF_AK, A_WIDTH), A_HEAD_DIM) * akn_ref[...]
    ak_ref[0] = ak.astype(BF16)

    @pl.when(in_tail)
    def _():
        akt_ref[0] = ak

    z = jnp.dot(lr, wup_ref[...], preferred_element_type=F32) + bla_ref[...]
    la_ref[0] = (jnp.minimum(z, 0.0) - jnp.log1p(jnp.exp(-jnp.abs(z)))) * (1.0 / G_TAU)

    gq_ref[0] = (seg(OFF_GQ, G_KW) * (G_DK ** -0.5)).astype(BF16)
    gk_ref[0] = seg(OFF_GK, G_KW).astype(BF16)
    gv_ref[0] = seg(OFF_GV, G_VW).astype(BF16)
    gr = seg(OFF_GR, G_VW)
    gr_ref[0] = (gr * _sigmoid(gr)).astype(BF16)

    cq = _head_rms(seg(OFF_CQ, C_WIDTH), C_HEAD_DIM) * cqn_ref[...]
    cq_ref[0] = cq.astype(BF16)

    gate_chunk = 512
    for c in range(N_BRANCH * D_MODEL // gate_chunk):
        lo = c * gate_chunk
        g = seg(OFF_GATE + lo, gate_chunk) + bg_ref[:, lo:lo + gate_chunk]
        gt_ref[0, :, lo:lo + gate_chunk] = _sigmoid(g).astype(BF16)

    av = seg(OFF_AV, A_WIDTH)
    av_ref[0] = av.astype(BF16)

    @pl.when(in_tail)
    def _():
        avt_ref[0] = av


def _inproj(x, keep, nm, w_r, aqn, akn, cqn, wup, bla, bg):
    G, R, _ = x.shape
    tm = min(ROW_TILE, R)
    n_tiles = R // tm
    n_tail = keep // tm
    assert R % tm == 0 and keep % tm == 0 and n_tail >= 1

    def row(width, dtype):
        return (jax.ShapeDtypeStruct((G, R, width), dtype),
                pl.BlockSpec((1, tm, width), lambda g, j: (g, j, 0)))

    def tail(width):
        return (jax.ShapeDtypeStruct((G, keep, width), F32),
                pl.BlockSpec((1, tm, width), lambda g, j: (g, jnp.maximum(j - (n_tiles - n_tail), 0), 0)))

    outs = [row(A_WIDTH, BF16), row(A_WIDTH, BF16), row(A_WIDTH, BF16), row(G_KW, BF16),
            row(G_KW, BF16), row(G_VW, BF16), row(G_VW, BF16), row(G_KW, F32), row(C_WIDTH, BF16),
            row(N_BRANCH * D_MODEL, BF16), tail(A_WIDTH), tail(A_WIDTH)]

    def full(a):
        return pl.BlockSpec(a.shape, lambda g, j: (0,) * a.ndim)

    return pl.pallas_call(
        functools.partial(_inproj_kernel, n_tiles=n_tiles, n_tail=n_tail),
        grid=(G, n_tiles),
        in_specs=[pl.BlockSpec((1, tm, D_MODEL), lambda g, j: (g, j, 0)), full(nm), full(w_r),
                  full(aqn), full(akn), full(cqn), full(wup), full(bla), full(bg)],
        out_specs=[o[1] for o in outs],
        out_shape=[o[0] for o in outs],
        compiler_params=_params(("arbitrary", "arbitrary")),
        name="inproj",
    )(x, nm, w_r, aqn, akn, cqn, wup, bla, bg)


def _attend(pairs, lo_mask):
    T = pairs[0][0].shape[0]
    scores = []
    for q, parts in pairs:
        zero = jnp.zeros_like(q)
        q2 = jnp.concatenate([jnp.where(lo_mask, q, zero), jnp.where(lo_mask, zero, q)], axis=0)
        ss = []
        for (k, _, bias2, valid) in parts:
            s = lax.dot_general(q2, k, (((1,), (1,)), ((), ())), preferred_element_type=F32) + bias2
            if valid is not None:
                s = jnp.where(valid, s, NEG_INF)
            ss.append(s)
        scores.append(ss)
    probs, sums = [], []
    for ss in scores:
        m = ss[0].max(-1, keepdims=True)
        for s in ss[1:]:
            m = jnp.maximum(m, s.max(-1, keepdims=True))
        ps = [jnp.exp(s - m) for s in ss]
        l = ps[0].sum(-1, keepdims=True)
        for p in ps[1:]:
            l = l + p.sum(-1, keepdims=True)
        probs.append([p.astype(BF16) for p in ps])
        sums.append(l)
    outs = []
    for (q, parts), ps, l in zip(pairs, probs, sums):
        o = jnp.dot(ps[0], parts[0][1], preferred_element_type=F32)
        for p, part in zip(ps[1:], parts[1:]):
            o = o + jnp.dot(p, part[1], preferred_element_type=F32)
        o = o / l
        outs.append(jnp.where(lo_mask, o[:T], o[T:]))
    return outs


def _attn_prompt_kernel(q_ref, kp_ref, kc_ref, vp_ref, vc_ref, bias_ref, o_ref, *, tb):
    j = pl.program_id(1)
    has_prev = j > 0
    lo_mask = lax.broadcasted_iota(I32, (ATTN_SUB, LANES), 1) < A_HEAD_DIM
    for s in range(tb // ATTN_SUB):
        r0 = s * ATTN_SUB
        len_a = tb - r0
        len_b = r0 + ATTN_SUB
        pairs = []
        for p in range(A_WIDTH // LANES):
            c0 = p * LANES
            q = q_ref[0, r0:r0 + ATTN_SUB, c0:c0 + LANES]
            parts = [
                (kp_ref[0, r0:tb, c0:c0 + LANES], vp_ref[0, r0:tb, c0:c0 + LANES],
                 bias_ref[p, :, 0:len_a], has_prev),
                (kc_ref[0, 0:len_b, c0:c0 + LANES], vc_ref[0, 0:len_b, c0:c0 + LANES],
                 bias_ref[p, :, len_a:len_a + len_b], None),
            ]
            pairs.append((q, parts))
        outs = _attend(pairs[:2], lo_mask) + _attend(pairs[2:], lo_mask)
        for p, o in enumerate(outs):
            o_ref[0, r0:r0 + ATTN_SUB, p * LANES:(p + 1) * LANES] = o.astype(BF16)


def _rel_bias(table, n_q, n_k, offset):
    period = n_q + n_k - 1
    m = jnp.arange(period)
    u = table[:, jnp.clip(n_q - 1 + offset - m, -REL_MAX, REL_MAX) + REL_MAX].astype(F32)
    rows = jnp.tile(u, (1, n_q + 1))[:, :n_q * (period + 1)].reshape(-1, n_q, period + 1)[:, :, :n_k]
    return rows[:, ::-1, :]


def _band_bias(table):
    qc = jnp.arange(ATTN_SUB)[:, None] // CHUNK
    kc = jnp.arange(ATTN_SUB + WINDOW)[None, :] // CHUNK
    ok = (kc >= qc) & (kc <= qc + BAND_CHUNKS)
    return jnp.where(ok[None], _rel_bias(table, ATTN_SUB, ATTN_SUB + WINDOW, WINDOW), NEG_INF)


def _attn_prompt(aq, ak, av, table):
    B, S, _ = aq.shape
    tb = WINDOW
    assert S % tb == 0
    bias = _band_bias(table).reshape(A_WIDTH // LANES, 2 * ATTN_SUB, ATTN_SUB + WINDOW)
    cur = pl.BlockSpec((1, tb, A_WIDTH), lambda b, j: (b, j, 0))
    prev = pl.BlockSpec((1, tb, A_WIDTH), lambda b, j: (b, jnp.maximum(j - 1, 0), 0))
    return pl.pallas_call(
        functools.partial(_attn_prompt_kernel, tb=tb),
        grid=(B, S // tb),
        in_specs=[cur, prev, cur, prev, cur, pl.BlockSpec(bias.shape, lambda b, j: (0, 0, 0))],
        out_specs=cur,
        out_shape=jax.ShapeDtypeStruct((B, S, A_WIDTH), BF16),
        compiler_params=_params(("arbitrary", "arbitrary")),
        name="attn_prompt",
    )(aq, ak, ak, av, av, bias)


def _attn_sample_kernel(q_ref, k_ref, v_ref, bias_ref, o_ref):
    T = q_ref.shape[1]
    lo_mask = lax.broadcasted_iota(I32, (T, LANES), 1) < A_HEAD_DIM
    pairs = []
    for p in range(A_WIDTH // LANES):
        c0 = p * LANES
        parts = [(k_ref[0, :, c0:c0 + LANES], v_ref[0, :, c0:c0 + LANES], bias_ref[p], None)]
        pairs.append((q_ref[0, :, c0:c0 + LANES], parts))
    for p, o in enumerate(_attend(pairs, lo_mask)):
        o_ref[0, :, p * LANES:(p + 1) * LANES] = o.astype(BF16)


def _attn_sample(aq, ak, av, cache_k, cache_v, table):
    B, T, _ = aq.shape
    P = cache_k.shape[1]
    L = (P + T + LANES - 1) // LANES * LANES
    pad = jnp.zeros((B, L - P - T, A_WIDTH), BF16)
    kk = jnp.concatenate([cache_k.astype(BF16), ak, pad], axis=1)
    vv = jnp.concatenate([cache_v.astype(BF16), av, pad], axis=1)
    bias = jnp.where((jnp.arange(L) < P + T)[None, None, :], _rel_bias(table, T, L, P), NEG_INF)
    bias = bias.reshape(A_WIDTH // LANES, 2 * T, L)
    new = pl.BlockSpec((1, T, A_WIDTH), lambda b: (b, 0, 0))
    old = pl.BlockSpec((1, L, A_WIDTH), lambda b: (b, 0, 0))
    return pl.pallas_call(
        _attn_sample_kernel,
        grid=(B,),
        in_specs=[new, old, old, pl.BlockSpec(bias.shape, lambda b: (0, 0, 0))],
        out_specs=new,
        out_shape=jax.ShapeDtypeStruct((B, T, A_WIDTH), BF16),
        compiler_params=_params(("arbitrary",)),
        name="attn_sample",
    )(aq, kk, vv, bias)


def _gla_kernel(q_ref, k_ref, v_ref, la_ref, gr_ref, gain_ref, s0_ref, o_ref, sf_ref, s_scr, *, C, n_chunks,
                n_group):
    j = pl.program_id(1)

    @pl.when(j == 0)
    def _():
        s_scr[...] = s0_ref[0]

    n_sub = C // G_SUB
    ri = lax.broadcasted_iota(I32, (C, C), 0)
    ci = lax.broadcasted_iota(I32, (C, C), 1)
    tril = (ci <= ri).astype(BF16)
    lane_kw = lax.broadcasted_iota(I32, (1, G_KW), 1)
    head_of_lane = lane_kw // G_DK
    row_kw = lax.broadcasted_iota(I32, (C, G_KW), 0)
    ur = lax.broadcasted_iota(I32, (2 * C, 4 * C), 0) - C
    uc = lax.broadcasted_iota(I32, (2 * C, 4 * C), 1)
    u_mat = ((ur >= 0) & ((uc >= C) | (ur <= uc))).astype(BF16)

    def heads_on_rows(x):
        return jnp.concatenate([jnp.where(head_of_lane == h, x, 0.0) for h in range(G_HEADS)], axis=0)

    def chunk_group(g, S):
        ns = range(n_group)
        rs = [pl.multiple_of((g * n_group + n) * C, C) for n in ns]
        q = [q_ref[0, pl.ds(r, C), :].astype(F32) for r in rs]
        k = [k_ref[0, pl.ds(r, C), :].astype(F32) for r in rs]
        v = [v_ref[0, pl.ds(r, C), :] for r in rs]
        la = [la_ref[0, pl.ds(r, C), :] for r in rs]

        split = [_split_bf16(x) for x in la]
        b = [jnp.dot(tril, hi, preferred_element_type=F32) + jnp.dot(tril, lo, preferred_element_type=F32)
             for hi, lo in split]
        xt = [jnp.concatenate([k[n], la[n]], axis=0).T for n in ns]
        split_t = [_split_bf16(x) for x in xt]
        xb = [jnp.dot(hi, u_mat, preferred_element_type=F32) + jnp.dot(lo, u_mat, preferred_element_type=F32)
              for hi, lo in split_t]
        b_last = [x[:, LANES:] for x in xb]
        kd = [(xt[n] * jnp.exp(b_last[n] - xb[n][:, :LANES])).astype(BF16) for n in ns]
        zeros_v = jnp.zeros((C, G_VW), BF16)
        kv = [jnp.dot(kd[n], jnp.concatenate([v[n], zeros_v], axis=0), preferred_element_type=F32)
              for n in ns]
        kv_d = [jnp.concatenate([x[h * G_DK:(h + 1) * G_DK, h * G_DV:(h + 1) * G_DV] for h in range(G_HEADS)],
                                axis=0) for x in kv]

        states = [S]
        for n in ns:
            states.append(jnp.exp(b_last[n]) * states[n] + kv_d[n])

        r_inter = [jnp.dot(heads_on_rows(q[n] * jnp.exp(b[n])).astype(BF16), states[n].astype(BF16),
                           preferred_element_type=F32) for n in ns]
        o = [jnp.concatenate([x[h * C:(h + 1) * C] for h in range(G_HEADS)], axis=1) for x in r_inter]

        o_rows = [[] for _ in ns]
        for i in range(n_sub):
            r0, r1 = i * G_SUB, (i + 1) * G_SUB
            atts = []
            for n in ns:
                bs = b[n][r0 - 1:r0] if i > 0 else jnp.zeros((1, G_KW), F32)
                qe = q[n][r0:r1] * jnp.exp(b[n][r0:r1] - bs)
                ke = (k[n] * jnp.exp(jnp.where(row_kw < r1, bs - b[n], -jnp.inf))).astype(BF16)
                att = lax.dot_general(heads_on_rows(qe).astype(BF16), ke, (((1,), (1,)), ((), ())),
                                      preferred_element_type=F32)
                tt = lax.broadcasted_iota(I32, att.shape, 0) % G_SUB + r0
                ss = lax.broadcasted_iota(I32, att.shape, 1)
                atts.append(jnp.where(ss <= tt, att, 0.0).astype(BF16))
            for n in ns:
                ov = jnp.dot(atts[n], v[n], preferred_element_type=F32)
                o_rows[n].append(jnp.concatenate(
                    [ov[h * G_SUB:(h + 1) * G_SUB, h * G_DV:(h + 1) * G_DV] for h in range(G_HEADS)], axis=1))

        for n in ns:
            on = o[n] + jnp.concatenate(o_rows[n], axis=0)
            on = _head_rms(on, G_DV) * gain_ref[...] * gr_ref[0, pl.ds(rs[n], C), :].astype(F32)
            o_ref[0, pl.ds(rs[n], C), :] = on.astype(BF16)
        return states[-1]

    s_scr[...] = lax.fori_loop(0, n_chunks // n_group, chunk_group, s_scr[...])

    @pl.when(j == pl.num_programs(1) - 1)
    def _():
        sf_ref[0] = s_scr[...]


def _gla(gq, gk, gv, la, gr, gain, s0):
    B, T, _ = gq.shape
    C = CHUNK
    tb = min(ROW_TILE, T)
    assert T % tb == 0 and tb % C == 0 and C % G_SUB == 0 and 2 * C == LANES
    kw = pl.BlockSpec((1, tb, G_KW), lambda b, j: (b, j, 0))
    vw = pl.BlockSpec((1, tb, G_VW), lambda b, j: (b, j, 0))
    st = pl.BlockSpec((1, G_KW, G_DV), lambda b, j: (b, 0, 0))
    return pl.pallas_call(
        functools.partial(_gla_kernel, C=C, n_chunks=tb // C, n_group=8 if (tb // C) % 8 == 0 else 1),
        grid=(B, T // tb),
        in_specs=[kw, kw, vw, kw, vw, pl.BlockSpec(gain.shape, lambda b, j: (0, 0)), st],
        out_specs=[vw, st],
        out_shape=[jax.ShapeDtypeStruct((B, T, G_VW), BF16), jax.ShapeDtypeStruct((B, G_KW, G_DV), F32)],
        scratch_shapes=[pltpu.VMEM((G_KW, G_DV), F32)],
        compiler_params=_params(("arbitrary", "arbitrary")),
        name="gla",
    )(gq, gk, gv, la, gr, gain, s0)


def _memkv_kernel(mem_ref, gm_ref, w_ref, gk_ref, k_ref, v_ref):
    x = mem_ref[0]
    h = (x * lax.rsqrt(jnp.mean(x * x, -1, keepdims=True) + EPS) * gm_ref[...]).astype(BF16)
    k = jnp.dot(h, w_ref[:, :C_WIDTH], preferred_element_type=F32)
    k_ref[0] = _head_rms(k, C_HEAD_DIM) * gk_ref[...]
    v_ref[0] = jnp.dot(h, w_ref[:, C_WIDTH:], preferred_element_type=F32)


def _memkv(mem, gm, w_kv, gk):
    B, M, _ = mem.shape
    out = pl.BlockSpec((1, M, C_WIDTH), lambda b: (b, 0, 0))
    return pl.pallas_call(
        _memkv_kernel,
        grid=(B,),
        in_specs=[pl.BlockSpec((1, M, D_MODEL), lambda b: (b, 0, 0)),
                  pl.BlockSpec(gm.shape, lambda b: (0, 0)),
                  pl.BlockSpec(w_kv.shape, lambda b: (0, 0)),
                  pl.BlockSpec(gk.shape, lambda b: (0, 0))],
        out_specs=[out, out],
        out_shape=[jax.ShapeDtypeStruct((B, M, C_WIDTH), F32)] * 2,
        compiler_params=_params(("arbitrary",)),
        name="memkv",
    )(mem, gm, w_kv, gk)


def _merge_kernel(x_ref, ya_ref, yb_ref, cq_ref, gt_ref, mk_ref, mv_ref, wa_ref, wb_ref, wc_ref,
                  wo_ref, nf_ref, wr_ref, br_ref,
                  x1_ref, hp_ref, ri_ref, rw_ref, cnt_ref, run_scr, *, tok_base):
    first = jnp.logical_and(pl.program_id(0) == 0, pl.program_id(1) == 0)

    @pl.when(first)
    def _():
        run_scr[...] = jnp.zeros_like(run_scr)

    tm = x_ref.shape[1]
    n_part = 2 if tm % 256 == 0 else 1
    parts = [slice(n * (tm // n_part), (n + 1) * (tm // n_part)) for n in range(n_part)]
    heads = [slice(h * C_HEAD_DIM, (h + 1) * C_HEAD_DIM) for h in range(C_HEADS)]
    mk = [mk_ref[0, :, c].astype(BF16) for c in heads]
    mv = [mv_ref[0, :, c].astype(BF16) for c in heads]
    scores = [[lax.dot_general(cq_ref[0, r, c], k, (((1,), (1,)), ((), ())), preferred_element_type=F32)
               * (C_HEAD_DIM ** -0.5) for c, k in zip(heads, mk)] for r in parts]
    probs = [[jnp.exp(s - s.max(-1, keepdims=True)) for s in ss] for ss in scores]
    sums = [[p.sum(-1, keepdims=True) for p in ps] for ps in probs]
    yc_in = [jnp.concatenate([jnp.dot(p.astype(BF16), v, preferred_element_type=F32) / l
                              for p, l, v in zip(ps, ls, mv)], axis=-1).astype(BF16)
             for ps, ls in zip(probs, sums)]
    y_a = [jnp.dot(ya_ref[0, r, :], wa_ref[...], preferred_element_type=F32) for r in parts]
    y_b = [jnp.dot(yb_ref[0, r, :], wb_ref[...], preferred_element_type=F32) for r in parts]
    y_c = [jnp.dot(y, wc_ref[...], preferred_element_type=F32) for y in yc_in]
    merged = [(gt_ref[0, r, 0:D_MODEL].astype(F32) * a
               + gt_ref[0, r, D_MODEL:2 * D_MODEL].astype(F32) * b
               + gt_ref[0, r, 2 * D_MODEL:3 * D_MODEL].astype(F32) * c).astype(BF16)
              for r, a, b, c in zip(parts, y_a, y_b, y_c)]
    x1s = [x_ref[0, r, :] + jnp.dot(m, wo_ref[...], preferred_element_type=F32) for r, m in zip(parts, merged)]
    for r, v in zip(parts, x1s):
        x1_ref[0, r, :] = v
    h2s = [v * lax.rsqrt(jnp.mean(v * v, -1, keepdims=True) + EPS) * nf_ref[...] for v in x1s]

    logits = jnp.concatenate([jnp.dot(h.astype(BF16), wr_ref[...], preferred_element_type=F32) for h in h2s],
                             axis=0) + br_ref[...]
    h2 = jnp.concatenate(h2s, axis=0)
    lane = lax.broadcasted_iota(I32, (tm, LANES), 1)
    lane_f = lane.astype(F32)
    vals, sels, idxs = [], [], []
    l = logits
    for _ in range(TOP_K):
        m = l.max(-1, keepdims=True)
        idx = jnp.min(jnp.where(l == m, lane_f, float(LANES)), -1, keepdims=True)
        sel = lane_f == idx
        vals.append(m)
        idxs.append(idx)
        sels.append(sel)
        l = jnp.where(sel, -3e38, l)
    es = [jnp.exp(vk - vals[0]) for vk in vals]
    den = es[0] + es[1] + es[2] + es[3]
    cnt = jnp.zeros((tm, LANES), F32)
    for sel in sels:
        cnt = cnt + jnp.where(sel, 1.0, 0.0)
    tp = max(tm, LANES)
    cnt_p = cnt if tp == tm else jnp.concatenate([cnt, jnp.zeros((tp - tm, LANES), F32)], axis=0)
    ri = lax.broadcasted_iota(I32, (tp, tp), 0)
    ci = lax.broadcasted_iota(I32, (tp, tp), 1)
    before = jnp.dot(jnp.where(ci < ri, 1.0, 0.0).astype(BF16), cnt_p.astype(BF16),
                     preferred_element_type=F32)[0:tm] + run_scr[0:1, :]
    r_i = jnp.zeros((tm, LANES), I32)
    r_w = jnp.zeros((tm, LANES), F32)
    for kk in range(TOP_K):
        rank = jnp.sum(jnp.where(sels[kk], before, 0.0), -1, keepdims=True)
        r_i = jnp.where(lane == kk, idxs[kk].astype(I32), r_i)
        r_i = jnp.where(lane == TOP_K + kk, rank.astype(I32), r_i)
        r_w = jnp.where(lane == kk, es[kk] / den, r_w)
    ri_ref[0] = r_i
    rw_ref[0] = r_w
    run_scr[...] = run_scr[...] + jnp.sum(cnt, axis=0, keepdims=True)
    cnt_ref[...] = run_scr[...]

    tok = tok_base + (pl.program_id(0) * pl.num_programs(1) + pl.program_id(1)) * tm \
        + lax.broadcasted_iota(I32, (tm, LANES), 0)
    meta = jnp.where(lane == 0, tok, 0)
    for kk in range(TOP_K):
        meta = jnp.where(lane == 1 + kk, idxs[kk].astype(I32), meta)
    packed = _pack_rows(h2)
    zero = jnp.zeros((tm, LANES), U32)
    slabs = [packed[:, j * LANES:(j + 1) * LANES] for j in range(SLABS)]
    slabs += [lax.bitcast_convert_type(meta, U32)] + [zero] * (ROW_SLABS - SLABS - 1)
    hp_ref[0] = pltpu.einshape("jtl->tjl", jnp.stack(slabs, axis=0))


def _merge(x, ya, yb, cq, gt, mk, mv, wa, wb, wc, wo, nf, wr, br, tok_base):
    B, T, _ = x.shape
    tm = min(ROW_TILE, T)
    assert T % tm == 0

    def row(width):
        return pl.BlockSpec((1, tm, width), lambda b, j: (b, j, 0))

    def full(a):
        return pl.BlockSpec(a.shape, lambda b, j: (0,) * a.ndim)

    mem = pl.BlockSpec((1, N_MEM, C_WIDTH), lambda b, j: (b, 0, 0))
    return pl.pallas_call(
        functools.partial(_merge_kernel, tok_base=tok_base),
        grid=(B, T // tm),
        in_specs=[row(D_MODEL), row(A_WIDTH), row(G_VW), row(C_WIDTH), row(N_BRANCH * D_MODEL), mem, mem,
                  full(wa), full(wb), full(wc), full(wo), full(nf), full(wr), full(br)],
        out_specs=[row(D_MODEL), pl.BlockSpec((1, tm, ROW_SLABS, LANES), lambda b, j: (b, j, 0, 0)), row(LANES),
                   row(LANES), pl.BlockSpec((SUBLANES, LANES), lambda b, j: (0, 0))],
        out_shape=[jax.ShapeDtypeStruct((B, T, D_MODEL), F32), jax.ShapeDtypeStruct((B, T, ROW_SLABS, LANES), U32),
                   jax.ShapeDtypeStruct((B, T, LANES), I32), jax.ShapeDtypeStruct((B, T, LANES), F32),
                   jax.ShapeDtypeStruct((SUBLANES, LANES), F32)],
        scratch_shapes=[pltpu.VMEM((SUBLANES, LANES), F32)],
        compiler_params=_params(("arbitrary", "arbitrary")),
        name="merge",
    )(x, ya, yb, cq, gt, mk, mv, wa, wb, wc, wo, nf, wr, br)


def _dispatch_kernel(dest_ref, dest2_ref, pad_lo_ref, pad_n_ref, hp_ref, hp2_ref, xs_ref, filler_scr, sem, sem2,
                     psem, *, tm, tm2, n_pad, n_tokens):
    def scatter(dref, href, s, n):
        def issue(t, c):
            for kk in range(TOP_K):
                pltpu.make_async_copy(href.at[t], xs_ref.at[dref[0, 0, t * TOP_K + kk]], s).start(priority=kk % 2)
            return c
        lax.fori_loop(0, n, issue, 0, unroll=8)

    def drain(href, s, n):
        for _ in range(TOP_K):
            pltpu.make_async_copy(href, xs_ref.at[pl.ds(0, n)], s).wait()

    scatter(dest_ref, hp_ref, sem, tm)

    @pl.when(pl.program_id(0) == 0)
    def _():
        scatter(dest2_ref, hp2_ref, sem2, tm2)
        shape = (FFN_BLOCK, ROW_SLABS, LANES)
        sub = lax.broadcasted_iota(I32, shape, 1)
        lane = lax.broadcasted_iota(I32, shape, 2)
        filler = jnp.where((sub == META) & (lane == 0), n_tokens,
                           jnp.where((sub == META) & (lane <= TOP_K), N_EXPERTS, 0))
        filler_scr[...] = lax.bitcast_convert_type(filler, U32)

        log_blk = FFN_BLOCK.bit_length() - 1

        def span(e, wait):
            lo, n = pad_lo_ref[e], pad_n_ref[e]
            whole = n >> log_blk

            def go(cp):
                cp.wait() if wait else cp.start()

            def block(j, c):
                go(pltpu.make_async_copy(filler_scr, xs_ref.at[pl.ds(lo + j * FFN_BLOCK, FFN_BLOCK)], psem))
                return c

            lax.fori_loop(0, whole, block, 0)
            rem = n - (whole << log_blk)
            for b in reversed(range(log_blk)):
                off = lo + (whole << log_blk) + ((rem >> (b + 1)) << (b + 1))

                @pl.when((rem >> b) & 1 == 1)
                def _():
                    go(pltpu.make_async_copy(filler_scr.at[pl.ds(0, 1 << b)], xs_ref.at[pl.ds(off, 1 << b)], psem))

        def fill(e, c):
            span(e, False)
            return c

        def fill_wait(e, c):
            span(e, True)
            return c

        lax.fori_loop(0, n_pad, fill, 0)
        lax.fori_loop(0, n_pad, fill_wait, 0)
        drain(hp2_ref, sem2, tm2)

    drain(hp_ref, sem, tm)


def _dispatch(dest, hp, dest2, hp2, pad_lo, pad_n, rows):
    N, N2 = hp.shape[0], hp2.shape[0]
    tm = min(ROW_TILE, N)
    assert N % tm == 0
    n_steps = N // tm
    smem = functools.partial(pl.BlockSpec, memory_space=pltpu.SMEM)
    n_pad = pad_lo.shape[0]
    return pl.pallas_call(
        functools.partial(_dispatch_kernel, tm=tm, tm2=N2, n_pad=n_pad, n_tokens=N + N2),
        grid=(n_steps,),
        in_specs=[smem((1, 1, tm * TOP_K), lambda i: (i, 0, 0)),
                  smem((1, 1, N2 * TOP_K), lambda i: (0, 0, 0)),
                  smem((n_pad,), lambda i: (0,)), smem((n_pad,), lambda i: (0,)),
                  pl.BlockSpec((tm, ROW_SLABS, LANES), lambda i: (i, 0, 0)),
                  pl.BlockSpec((N2, ROW_SLABS, LANES), lambda i: (0, 0, 0))],
        out_specs=pl.BlockSpec(memory_space=pl.ANY),
        out_shape=jax.ShapeDtypeStruct((rows, ROW_SLABS, LANES), U32),
        scratch_shapes=[pltpu.VMEM((FFN_BLOCK, ROW_SLABS, LANES), U32), pltpu.SemaphoreType.DMA(()),
                        pltpu.SemaphoreType.DMA(()), pltpu.SemaphoreType.DMA(())],
        compiler_params=_params(("arbitrary",)),
        name="moe_dispatch",
    )(dest.reshape(n_steps, 1, tm * TOP_K), dest2.reshape(1, 1, N2 * TOP_K), pad_lo, pad_n, hp, hp2)


def _ffn_kernel(be_ref, nu_ref, x_ref, wgu_ref, bgu_ref, wd_ref, bd_ref, y4_ref,
                wgu_bf, wd_bf, ybuf, idv, ids_smem, sem_ids, sem_rows, *, n_blk, n_tokens):
    i = pl.program_id(0)
    n_used = nu_ref[0]
    n_assign = n_tokens * TOP_K
    prev = be_ref[jnp.maximum(i - 1, 0)]
    new_expert = jnp.logical_or(i == 0, be_ref[i] != prev)

    def rows_done(s):
        pltpu.make_async_copy(ybuf.at[s], y4_ref.at[pl.ds(0, FFN_BLOCK)], sem_rows.at[s]).wait()

    def ids_copy(s):
        return pltpu.make_async_copy(idv, ids_smem.at[s], sem_ids)

    def start_rows(s, lo, hi):
        for r in range(lo, hi):
            pltpu.make_async_copy(ybuf.at[s, r], y4_ref.at[ids_smem[s, 0, r]],
                                  sem_rows.at[s]).start(priority=r % 2)

    @pl.when(i == 0)
    def _():
        ybuf[1] = jnp.zeros(ybuf.shape[1:], U32)
        spare = pltpu.make_async_copy(ybuf.at[1], y4_ref.at[pl.ds(n_assign, FFN_BLOCK)], sem_rows.at[1])
        spare.start()
        spare.wait()

    @pl.when(new_expert)
    def _():
        wgu_bf[...] = wgu_ref[0].astype(BF16)
        wd_bf[...] = wd_ref[0].astype(BF16)

    def compute(s, flush):
        q = FFN_BLOCK // 4
        if flush:
            ids_copy(1 - s).wait()
            start_rows(1 - s, 0, q)
        xm = pltpu.einshape("tjl->jtl", x_ref[...])
        x_lo, x_hi = _unpack_rows(jnp.concatenate([xm[j] for j in range(SLABS)], axis=-1))
        meta = lax.bitcast_convert_type(xm[META], I32)
        tok = meta[:, 0:1]
        choice = jnp.zeros_like(tok)
        for kk in range(1, TOP_K):
            choice = jnp.where(meta[:, 1 + kk:2 + kk] == be_ref[i], kk, choice)
        local = lax.broadcasted_iota(I32, tok.shape, 0)
        dest = jnp.where(tok >= n_tokens, n_assign + local, tok * TOP_K + choice)
        dest_t = jnp.broadcast_to(dest.astype(F32), (FFN_BLOCK, LANES)).T
        idv[...] = dest_t[0:SUBLANES].astype(I32)
        ids_copy(s).start()
        gu = (jnp.dot(x_lo.astype(BF16), wgu_bf[0:HALF, :], preferred_element_type=F32)
              + jnp.dot(x_hi.astype(BF16), wgu_bf[HALF:, :], preferred_element_type=F32)
              + bgu_ref[0])
        if flush:
            start_rows(1 - s, q, 2 * q)
        gate = jnp.minimum(gu[:, :D_EXPERT], SWIGLU_LIMIT)
        up = jnp.clip(gu[:, D_EXPERT:], -SWIGLU_LIMIT, SWIGLU_LIMIT)
        act = (up + 1.0) * (gate * _sigmoid(SWIGLU_ALPHA * gate))
        if flush:
            start_rows(1 - s, 2 * q, 3 * q)
        y = jnp.dot(act.astype(BF16), wd_bf[...], preferred_element_type=F32) + bd_ref[0]
        if flush:
            start_rows(1 - s, 3 * q, 4 * q)
        packed = _pack_rows(y)
        pl.when(i >= 2)(functools.partial(rows_done, s))
        _store_slabs(ybuf, (s,), packed)

    def flush_only(s, also):
        ids_copy(s).wait()
        start_rows(s, 0, FFN_BLOCK)
        pl.when(also)(functools.partial(rows_done, 1 - s))
        rows_done(s)

    real = i < n_used
    prev_real = jnp.logical_and(i >= 1, i - 1 < n_used)
    for s in range(2):
        mine = i % 2 == s
        pl.when(mine & real & prev_real)(functools.partial(compute, s, True))
        pl.when(mine & real & jnp.logical_not(prev_real))(functools.partial(compute, s, False))
        pl.when(mine & jnp.logical_not(real) & prev_real)(functools.partial(flush_only, 1 - s, i >= 2))
        pl.when(mine & real & (i == n_blk - 1))(functools.partial(flush_only, s, i >= 1))


def _ffn(blk_expert, n_used, xs, n_tokens, w_gu, b_gu, w_d, b_d):
    P = xs.shape[0]
    nblk = P // FFN_BLOCK
    grid_spec = pltpu.PrefetchScalarGridSpec(
        num_scalar_prefetch=2,
        grid=(nblk,),
        in_specs=[pl.BlockSpec((FFN_BLOCK, ROW_SLABS, LANES), lambda i, be, nu: (jnp.minimum(i, nu[0] - 1), 0, 0)),
                  pl.BlockSpec((1, D_MODEL, 2 * D_EXPERT), lambda i, be, nu: (be[i], 0, 0)),
                  pl.BlockSpec((1, 1, 2 * D_EXPERT), lambda i, be, nu: (be[i], 0, 0)),
                  pl.BlockSpec((1, D_EXPERT, D_MODEL), lambda i, be, nu: (be[i], 0, 0)),
                  pl.BlockSpec((1, 1, D_MODEL), lambda i, be, nu: (be[i], 0, 0))],
        out_specs=pl.BlockSpec(memory_space=pl.ANY),
        scratch_shapes=[pltpu.VMEM((D_MODEL, 2 * D_EXPERT), BF16), pltpu.VMEM((D_EXPERT, D_MODEL), BF16),
                        pltpu.VMEM((2, FFN_BLOCK, SLABS, LANES), U32), pltpu.VMEM((SUBLANES, FFN_BLOCK), I32),
                        pltpu.SMEM((2, SUBLANES, FFN_BLOCK), I32), pltpu.SemaphoreType.DMA(()),
                        pltpu.SemaphoreType.DMA((2,))],
    )
    return pl.pallas_call(
        functools.partial(_ffn_kernel, n_blk=nblk, n_tokens=n_tokens),
        grid_spec=grid_spec,
        out_shape=jax.ShapeDtypeStruct((n_tokens * TOP_K + FFN_BLOCK, SLABS, LANES), U32),
        compiler_params=_params(("arbitrary",)),
        name="moe_ffn",
    )(blk_expert, n_used, xs, w_gu, b_gu.reshape(N_EXPERTS, 1, -1), w_d, b_d.reshape(N_EXPERTS, 1, -1))


def _combine_kernel(x1_ref, rw_ref, y_ref, o_ref, *, tm):
    w = rw_ref[...]
    acc_lo = x1_ref[:, :HALF]
    acc_hi = x1_ref[:, HALF:]
    for kk in range(TOP_K):
        rows = y_ref[pl.ds(kk, tm, stride=TOP_K)]
        slab_major = pltpu.einshape("tjl->jtl", rows)
        lo, hi = _unpack_rows(jnp.concatenate([slab_major[j] for j in range(SLABS)], axis=-1))
        wk = w[:, kk:kk + 1]
        acc_lo = acc_lo + wk * lo
        acc_hi = acc_hi + wk * hi
    o_ref[:, :HALF] = acc_lo
    o_ref[:, HALF:] = acc_hi


def _combine(x1, rw, y4, tok_base):
    N = x1.shape[0]
    tm = min(256, N)
    assert N % tm == 0 and tok_base % tm == 0
    first = tok_base // tm
    return pl.pallas_call(
        functools.partial(_combine_kernel, tm=tm),
        grid=(N // tm,),
        in_specs=[pl.BlockSpec((tm, D_MODEL), lambda i: (i, 0)),
                  pl.BlockSpec((tm, LANES), lambda i: (i, 0)),
                  pl.BlockSpec((tm * TOP_K, SLABS, LANES), lambda i: (first + i, 0, 0))],
        out_specs=pl.BlockSpec((tm, D_MODEL), lambda i: (i, 0)),
        out_shape=jax.ShapeDtypeStruct((N, D_MODEL), F32),
        compiler_params=_params(("arbitrary",)),
        name="moe_combine",
    )(x1, rw, y4)


def _moe(groups, w_gu, b_gu, w_d, b_d):
    n_tokens = sum(g[0].shape[0] for g in groups)
    n_assign = n_tokens * TOP_K
    nblk = (n_assign + N_EXPERTS * (FFN_BLOCK - 1) + FFN_BLOCK - 1) // FFN_BLOCK
    counts = [g[4].astype(I32) for g in groups]
    total = sum(counts)
    padded = (total + FFN_BLOCK - 1) // FFN_BLOCK * FFN_BLOCK
    pend = jnp.cumsum(padded)
    pstart = pend - padded
    n_used = pend[-1:] // FFN_BLOCK
    blk = jnp.minimum(jnp.arange(nblk, dtype=I32), n_used[0] - 1) * FFN_BLOCK
    blk_expert = jnp.minimum(jnp.sum(pend[None, :] <= blk[:, None], axis=1), N_EXPERTS - 1).astype(I32)

    experts = jnp.arange(N_EXPERTS, dtype=I32)
    dests = []
    base = pstart
    for g, c in zip(groups, counts):
        idx, rank = g[2][:, :TOP_K], g[2][:, TOP_K:2 * TOP_K]
        dests.append(jnp.sum(jnp.where(idx[..., None] == experts, base, 0), axis=-1) + rank)
        base = base + c

    rows = nblk * FFN_BLOCK
    pad_lo = jnp.concatenate([pstart + total, pend[-1:]])
    pad_n = jnp.concatenate([padded - total, rows - pend[-1:]])
    (g_main, g_small), (d_main, d_small) = groups, dests
    xs = _dispatch(d_main, g_main[1], d_small, g_small[1], pad_lo, pad_n, rows)
    y4 = _ffn(blk_expert, n_used.astype(I32), xs, n_tokens, w_gu, b_gu, w_d, b_d)
    return [_combine(g_main[0], g_main[3], y4, 0), _combine(g_small[0], g_small[3], y4, g_main[0].shape[0])]


def _tile_lanes(g, reps):
    return jnp.tile(g.astype(F32), reps)[None, :]


def kernel(x_prompt, x_sample, mem_prompt, cache_attn_k, cache_attn_v, state_gla, cache_mem_k, cache_mem_v, norm_mix, w_in, a_q_norm, a_k_norm, rel_bias_table, w_a_o, w_gla_a_up, b_gla_a, gla_out_norm, w_b_o, c_q_norm, c_k_norm, norm_mem, w_mem_kv, w_c_o, b_gate, w_out, norm_ffn, w_router, b_router, w_gate_up, b_gate_up, w_down, b_down):
    depth = norm_mix.shape[0]
    assert depth == 1
    l = 0
    B, S, _ = x_prompt.shape
    Bs, Ts, _ = x_sample.shape
    keep = min(WINDOW, S)

    w = w_in[l]
    sizes = (A_WIDTH, A_WIDTH, A_WIDTH, G_KW, G_KW, G_VW, G_VW, G_RANK, C_WIDTH, N_BRANCH * D_MODEL)
    offs = [0]
    for s_ in sizes:
        offs.append(offs[-1] + s_)
    seg = [w[:, offs[i]:offs[i + 1]] for i in range(len(sizes))]
    w_r = jnp.concatenate(seg[0:7] + [seg[8], seg[9], seg[7], jnp.zeros((D_MODEL, LANES - G_RANK), F32)],
                          axis=1).astype(BF16)
    nm = norm_mix[l][None, :]
    aqn = _tile_lanes(a_q_norm[l], A_HEADS)
    akn = _tile_lanes(a_k_norm[l], A_HEADS)
    cqn = _tile_lanes(c_q_norm[l], C_HEADS)
    ckn = _tile_lanes(c_k_norm[l], C_HEADS)
    gon = _tile_lanes(gla_out_norm[l], G_HEADS)
    wup = jnp.concatenate([w_gla_a_up[l], jnp.zeros((LANES - G_RANK, G_KW), F32)], axis=0).astype(BF16)
    bla = b_gla_a[l][None, :]
    bg = b_gate[l][None, :]
    wa, wb, wc, wo = (t[l].astype(BF16) for t in (w_a_o, w_b_o, w_c_o, w_out))
    nf = norm_ffn[l][None, :]
    wr = jnp.concatenate([w_router[l], jnp.zeros((D_MODEL, LANES - N_EXPERTS), F32)], axis=1).astype(BF16)
    br = jnp.concatenate([b_router[l], jnp.full((LANES - N_EXPERTS,), NEG_INF, F32)])[None, :]
    table = rel_bias_table[l]

    mk, mv = _memkv(mem_prompt, norm_mem[l][None, :], w_mem_kv[l].astype(BF16), ckn)
    (aq, ak, av, gq, gk, gv, gr, la, cq, gt, ak_tail, av_tail) = _inproj(
        x_prompt, keep, nm, w_r, aqn, akn, cqn, wup, bla, bg)
    ya = _attn_prompt(aq, ak, av, table)
    yb, s_prompt = _gla(gq, gk, gv, la, gr, gon, jnp.zeros((B, G_KW, G_DV), F32))
    x1_p, hp_p, ri_p, rw_p, cnt_p = _merge(x_prompt, ya, yb, cq, gt, mk, mv, wa, wb, wc, wo, nf, wr, br, 0)

    (aq, ak, av, gq, gk, gv, gr, la, cq, gt, ak_new, av_new) = _inproj(
        x_sample.reshape(1, Bs * Ts, D_MODEL), Bs * Ts, nm, w_r, aqn, akn, cqn, wup, bla, bg)
    rs = lambda t: t.reshape(Bs, Ts, t.shape[-1])
    P = cache_attn_k.shape[2]
    ya = _attn_sample(rs(aq), rs(ak), rs(av), cache_attn_k[l].reshape(Bs, P, A_WIDTH),
                      cache_attn_v[l].reshape(Bs, P, A_WIDTH), table)
    t_pad = (Ts + CHUNK - 1) // CHUNK * CHUNK
    zp = lambda t: jnp.pad(rs(t), ((0, 0), (0, t_pad - Ts), (0, 0)))
    yb, s_sample = _gla(zp(gq), zp(gk), zp(gv), zp(la), zp(gr), gon, state_gla[l].reshape(Bs, G_KW, G_DV))
    yb = yb[:, :Ts]
    x1_s, hp_s, ri_s, rw_s, cnt_s = _merge(
        x_sample, ya, yb, rs(cq), rs(gt), cache_mem_k[l].reshape(Bs, N_MEM, C_WIDTH),
        cache_mem_v[l].reshape(Bs, N_MEM, C_WIDTH), wa, wb, wc, wo, nf, wr, br, B * S)

    flat = lambda t: t.reshape((-1,) + t.shape[2:])
    y_p, y_s = _moe(
        [(flat(x1_p), flat(hp_p), flat(ri_p), flat(rw_p), cnt_p[0, :N_EXPERTS]),
         (flat(x1_s), flat(hp_s), flat(ri_s), flat(rw_s), cnt_s[0, :N_EXPERTS])],
        w_gate_up[l], b_gate_up[l], w_down[l], b_down[l])

    return (y_p.reshape(B, S, D_MODEL), y_s.reshape(Bs, Ts, D_MODEL),
            ak_tail.reshape(1, B, keep, A_HEADS, A_HEAD_DIM), av_tail.reshape(1, B, keep, A_HEADS, A_HEAD_DIM),
            s_prompt.reshape(1, B, G_HEADS, G_DK, G_DV),
            mk.reshape(1, B, N_MEM, C_HEADS, C_HEAD_DIM), mv.reshape(1, B, N_MEM, C_HEADS, C_HEAD_DIM),
            ak_new.reshape(1, Bs, Ts, A_HEADS, A_HEAD_DIM), av_new.reshape(1, Bs, Ts, A_HEADS, A_HEAD_DIM),
            s_sample.reshape(1, Bs, G_HEADS, G_DK, G_DV))
```

```python
import functools

import jax
import jax.numpy as jnp
from jax import lax
from jax.experimental import pallas as pl
from jax.experimental.pallas import tpu as pltpu

F32 = jnp.float32
BF16 = jnp.bfloat16
U32 = jnp.uint32
I32 = jnp.int32

D_MODEL = 1024
CHUNK = 64
BAND_CHUNKS = 8
WINDOW = BAND_CHUNKS * CHUNK
N_MEM = 256
A_HEADS, A_HEAD_DIM = 8, 64
A_WIDTH = A_HEADS * A_HEAD_DIM
REL_MAX = 128
G_HEADS, G_DK, G_DV = 4, 64, 128
G_KW, G_VW = G_HEADS * G_DK, G_HEADS * G_DV
G_RANK = 16
G_TAU = 16.0
G_SUB = 16
C_HEADS, C_HEAD_DIM = 4, 128
C_WIDTH = C_HEADS * C_HEAD_DIM
N_BRANCH = 3
N_EXPERTS = 32
TOP_K = 4
D_EXPERT = 1024
SWIGLU_LIMIT = 7.0
SWIGLU_ALPHA = 1.702
EPS = 1e-6
NEG_INF = -1e30

LANES = 128
SUBLANES = 8
HALF = D_MODEL // 2
ROW_TILE = 512
ATTN_SUB = 128
FFN_BLOCK = 512
VMEM_LIMIT = 56 * 1024 * 1024

OFF_AQ, OFF_AK, OFF_AV = 0, 512, 1024
OFF_GQ, OFF_GK, OFF_GV, OFF_GR = 1536, 1792, 2048, 2560
OFF_CQ, OFF_GATE, OFF_LR = 3072, 3584, 6656
IN_COLS = OFF_LR + LANES


def _params(sem):
    return pltpu.CompilerParams(dimension_semantics=sem, vmem_limit_bytes=VMEM_LIMIT)


def _sigmoid(x):
    return 0.5 * jnp.tanh(0.5 * x) + 0.5


def _head_rms(y, head_dim):
    cols = []
    for p in range(y.shape[1] // LANES):
        blk = y[:, p * LANES:(p + 1) * LANES]
        sq = blk * blk
        if head_dim == LANES:
            sc = lax.rsqrt(jnp.sum(sq, -1, keepdims=True) * (1.0 / LANES) + EPS)
        else:
            lo = lax.broadcasted_iota(I32, blk.shape, 1) < head_dim
            s_lo = jnp.sum(jnp.where(lo, sq, 0.0), -1, keepdims=True)
            s_hi = jnp.sum(jnp.where(lo, 0.0, sq), -1, keepdims=True)
            sc = jnp.where(lo, lax.rsqrt(s_lo * (1.0 / head_dim) + EPS),
                           lax.rsqrt(s_hi * (1.0 / head_dim) + EPS))
        cols.append(blk * sc)
    return jnp.concatenate(cols, axis=-1)


def _split_bf16(x):
    hi = x.astype(BF16)
    lo = (x - hi.astype(F32)).astype(BF16)
    return hi, lo


def _pack_rows(x):
    lo = lax.bitcast_convert_type(x[:, :HALF].astype(BF16).astype(F32), U32)
    hi = lax.bitcast_convert_type(x[:, HALF:].astype(BF16).astype(F32), U32)
    return (lo >> 16) | (hi & jnp.uint32(0xFFFF0000))


def _unpack_rows(u):
    lo = lax.bitcast_convert_type(u << 16, F32)
    hi = lax.bitcast_convert_type(u & jnp.uint32(0xFFFF0000), F32)
    return lo, hi


SLABS = HALF // LANES
ROW_SLABS = SUBLANES
META = SLABS


def _store_slabs(ref, lead, u):
    slab_major = jnp.stack([u[:, j * LANES:(j + 1) * LANES] for j in range(SLABS)], axis=0)
    ref[lead + (slice(None),) * 3] = pltpu.einshape("jtl->tjl", slab_major)


def _inproj_kernel(x_ref, nm_ref, w_ref, aqn_ref, akn_ref, cqn_ref, wup_ref, bla_ref, bg_ref,
                   aq_ref, ak_ref, av_ref, gq_ref, gk_ref, gv_ref, gr_ref, la_ref, cq_ref, gt_ref,
                   akt_ref, avt_ref, *, n_tiles, n_tail):
    j = pl.program_id(1)
    x = x_ref[0]
    h = (x * lax.rsqrt(jnp.mean(x * x, -1, keepdims=True) + EPS) * nm_ref[...]).astype(BF16)

    def seg(off, width):
        return jnp.dot(h, w_ref[:, off:off + width], preferred_element_type=F32)

    in_tail = j >= n_tiles - n_tail

    lr = seg(OFF_LR, LANES).astype(BF16)

    aq = _head_rms(seg(OFF_AQ, A_WIDTH), A_HEAD_DIM) * aqn_ref[...] * (A_HEAD_DIM ** -0.5)
    aq_ref[0] = aq.astype(BF16)

    ak = _head_rms(seg(OFF_AK, A_WIDTH), A_HEAD_DIM) * akn_ref[...]
    ak_ref[0] = ak.astype(BF16)

    @pl.when(in_tail)
    def _():
        akt_ref[0] = ak

    z = jnp.dot(lr, wup_ref[...], preferred_element_type=F32) + bla_ref[...]
    la_ref[0] = (jnp.minimum(z, 0.0) - jnp.log1p(jnp.exp(-jnp.abs(z)))) * (1.0 / G_TAU)

    gq_ref[0] = (seg(OFF_GQ, G_KW) * (G_DK ** -0.5)).astype(BF16)
    gk_ref[0] = seg(OFF_GK, G_KW).astype(BF16)
    gv_ref[0] = seg(OFF_GV, G_VW).astype(BF16)
    gr = seg(OFF_GR, G_VW)
    gr_ref[0] = (gr * _sigmoid(gr)).astype(BF16)

    cq = _head_rms(seg(OFF_CQ, C_WIDTH), C_HEAD_DIM) * cqn_ref[...]
    cq_ref[0] = cq.astype(BF16)

    gate_chunk = 512
    for c in range(N_BRANCH * D_MODEL // gate_chunk):
        lo = c * gate_chunk
        g = seg(OFF_GATE + lo, gate_chunk) + bg_ref[:, lo:lo + gate_chunk]
        gt_ref[0, :, lo:lo + gate_chunk] = _sigmoid(g).astype(BF16)

    av = seg(OFF_AV, A_WIDTH)
    av_ref[0] = av.astype(BF16)

    @pl.when(in_tail)
    def _():
        avt_ref[0] = av


def _inproj(x, keep, nm, w_r, aqn, akn, cqn, wup, bla, bg):
    G, R, _ = x.shape
    tm = min(ROW_TILE, R)
    n_tiles = R // tm
    n_tail = keep // tm
    assert R % tm == 0 and keep % tm == 0 and n_tail >= 1

    def row(width, dtype):
        return (jax.ShapeDtypeStruct((G, R, width), dtype),
                pl.BlockSpec((1, tm, width), lambda g, j: (g, j, 0)))

    def tail(width):
        return (jax.ShapeDtypeStruct((G, keep, width), F32),
                pl.BlockSpec((1, tm, width), lambda g, j: (g, jnp.maximum(j - (n_tiles - n_tail), 0), 0)))

    outs = [row(A_WIDTH, BF16), row(A_WIDTH, BF16), row(A_WIDTH, BF16), row(G_KW, BF16),
            row(G_KW, BF16), row(G_VW, BF16), row(G_VW, BF16), row(G_KW, F32), row(C_WIDTH, BF16),
            row(N_BRANCH * D_MODEL, BF16), tail(A_WIDTH), tail(A_WIDTH)]

    def full(a):
        return pl.BlockSpec(a.shape, lambda g, j: (0,) * a.ndim)

    return pl.pallas_call(
        functools.partial(_inproj_kernel, n_tiles=n_tiles, n_tail=n_tail),
        grid=(G, n_tiles),
        in_specs=[pl.BlockSpec((1, tm, D_MODEL), lambda g, j: (g, j, 0)), full(nm), full(w_r),
                  full(aqn), full(akn), full(cqn), full(wup), full(bla), full(bg)],
        out_specs=[o[1] for o in outs],
        out_shape=[o[0] for o in outs],
        compiler_params=_params(("arbitrary", "arbitrary")),
        name="inproj",
    )(x, nm, w_r, aqn, akn, cqn, wup, bla, bg)


def _attend(pairs, lo_mask):
    T = pairs[0][0].shape[0]
    scores = []
    for q, parts in pairs:
        zero = jnp.zeros_like(q)
        q2 = jnp.concatenate([jnp.where(lo_mask, q, zero), jnp.where(lo_mask, zero, q)], axis=0)
        ss = []
        for (k, _, bias2, valid) in parts:
            s = lax.dot_general(q2, k, (((1,), (1,)), ((), ())), preferred_element_type=F32) + bias2
            if valid is not None:
                s = jnp.where(valid, s, NEG_INF)
            ss.append(s)
        scores.append(ss)
    probs, sums = [], []
    for ss in scores:
        m = ss[0].max(-1, keepdims=True)
        for s in ss[1:]:
            m = jnp.maximum(m, s.max(-1, keepdims=True))
        ps = [jnp.exp(s - m) for s in ss]
        l = ps[0].sum(-1, keepdims=True)
        for p in ps[1:]:
            l = l + p.sum(-1, keepdims=True)
        probs.append([p.astype(BF16) for p in ps])
        sums.append(l)
    outs = []
    for (q, parts), ps, l in zip(pairs, probs, sums):
        o = jnp.dot(ps[0], parts[0][1], preferred_element_type=F32)
        for p, part in zip(ps[1:], parts[1:]):
            o = o + jnp.dot(p, part[1], preferred_element_type=F32)
        o = o / l
        outs.append(jnp.where(lo_mask, o[:T], o[T:]))
    return outs


def _attn_prompt_kernel(q_ref, kp_ref, kc_ref, vp_ref, vc_ref, bias_ref, o_ref, *, tb):
    j = pl.program_id(1)
    lo_mask = lax.broadcasted_iota(I32, (ATTN_SUB, LANES), 1) < A_HEAD_DIM

    def block(with_prev):
        for s in range(tb // ATTN_SUB):
            r0 = s * ATTN_SUB
            len_a = tb - r0
            len_b = r0 + ATTN_SUB
            pairs = []
            for p in range(A_WIDTH // LANES):
                c0 = p * LANES
                q = q_ref[0, r0:r0 + ATTN_SUB, c0:c0 + LANES]
                parts = [(kc_ref[0, 0:len_b, c0:c0 + LANES], vc_ref[0, 0:len_b, c0:c0 + LANES],
                          bias_ref[p, :, len_a:len_a + len_b], None)]
                if with_prev:
                    parts.insert(0, (kp_ref[0, r0:tb, c0:c0 + LANES], vp_ref[0, r0:tb, c0:c0 + LANES],
                                     bias_ref[p, :, 0:len_a], None))
                pairs.append((q, parts))
            outs = _attend(pairs[:2], lo_mask) + _attend(pairs[2:], lo_mask)
            for p, o in enumerate(outs):
                o_ref[0, r0:r0 + ATTN_SUB, p * LANES:(p + 1) * LANES] = o.astype(BF16)

    pl.when(j > 0)(functools.partial(block, True))
    pl.when(j == 0)(functools.partial(block, False))


def _rel_bias(table, n_q, n_k, offset):
    period = n_q + n_k - 1
    m = jnp.arange(period)
    u = table[:, jnp.clip(n_q - 1 + offset - m, -REL_MAX, REL_MAX) + REL_MAX].astype(F32)
    rows = jnp.tile(u, (1, n_q + 1))[:, :n_q * (period + 1)].reshape(-1, n_q, period + 1)[:, :, :n_k]
    return rows[:, ::-1, :]


def _band_bias(table):
    qc = jnp.arange(ATTN_SUB)[:, None] // CHUNK
    kc = jnp.arange(ATTN_SUB + WINDOW)[None, :] // CHUNK
    ok = (kc >= qc) & (kc <= qc + BAND_CHUNKS)
    return jnp.where(ok[None], _rel_bias(table, ATTN_SUB, ATTN_SUB + WINDOW, WINDOW), NEG_INF)


def _attn_prompt(aq, ak, av, table):
    B, S, _ = aq.shape
    tb = WINDOW
    assert S % tb == 0
    bias = _band_bias(table).reshape(A_WIDTH // LANES, 2 * ATTN_SUB, ATTN_SUB + WINDOW)
    cur = pl.BlockSpec((1, tb, A_WIDTH), lambda b, j: (b, j, 0))
    prev = pl.BlockSpec((1, tb, A_WIDTH), lambda b, j: (b, jnp.maximum(j - 1, 0), 0))
    return pl.pallas_call(
        functools.partial(_attn_prompt_kernel, tb=tb),
        grid=(B, S // tb),
        in_specs=[cur, prev, cur, prev, cur, pl.BlockSpec(bias.shape, lambda b, j: (0, 0, 0))],
        out_specs=cur,
        out_shape=jax.ShapeDtypeStruct((B, S, A_WIDTH), BF16),
        compiler_params=_params(("arbitrary", "arbitrary")),
        name="attn_prompt",
    )(aq, ak, ak, av, av, bias)


def _attn_sample_kernel(q_ref, k_ref, v_ref, bias_ref, o_ref):
    T = q_ref.shape[1]
    lo_mask = lax.broadcasted_iota(I32, (T, LANES), 1) < A_HEAD_DIM
    pairs = []
    for p in range(A_WIDTH // LANES):
        c0 = p * LANES
        parts = [(k_ref[0, :, c0:c0 + LANES], v_ref[0, :, c0:c0 + LANES], bias_ref[p], None)]
        pairs.append((q_ref[0, :, c0:c0 + LANES], parts))
    for p, o in enumerate(_attend(pairs, lo_mask)):
        o_ref[0, :, p * LANES:(p + 1) * LANES] = o.astype(BF16)


def _attn_sample(aq, ak, av, cache_k, cache_v, table):
    B, T, _ = aq.shape
    P = cache_k.shape[1]
    L = (P + T + LANES - 1) // LANES * LANES
    pad = jnp.zeros((B, L - P - T, A_WIDTH), BF16)
    kk = jnp.concatenate([cache_k.astype(BF16), ak, pad], axis=1)
    vv = jnp.concatenate([cache_v.astype(BF16), av, pad], axis=1)
    bias = jnp.where((jnp.arange(L) < P + T)[None, None, :], _rel_bias(table, T, L, P), NEG_INF)
    bias = bias.reshape(A_WIDTH // LANES, 2 * T, L)
    new = pl.BlockSpec((1, T, A_WIDTH), lambda b: (b, 0, 0))
    old = pl.BlockSpec((1, L, A_WIDTH), lambda b: (b, 0, 0))
    return pl.pallas_call(
        _attn_sample_kernel,
        grid=(B,),
        in_specs=[new, old, old, pl.BlockSpec(bias.shape, lambda b: (0, 0, 0))],
        out_specs=new,
        out_shape=jax.ShapeDtypeStruct((B, T, A_WIDTH), BF16),
        compiler_params=_params(("arbitrary",)),
        name="attn_sample",
    )(aq, kk, vv, bias)


def _gla_kernel(q_ref, k_ref, v_ref, la_ref, gr_ref, gain_ref, s0_ref, o_ref, sf_ref, s_scr, *, C, n_chunks,
                n_group):
    j = pl.program_id(1)

    @pl.when(j == 0)
    def _():
        s_scr[...] = s0_ref[0]

    n_sub = C // G_SUB
    ri = lax.broadcasted_iota(I32, (C, C), 0)
    ci = lax.broadcasted_iota(I32, (C, C), 1)
    tril = (ci <= ri).astype(BF16)
    tril2 = jnp.concatenate([tril, tril], axis=1)
    lane_kw = lax.broadcasted_iota(I32, (1, G_KW), 1)
    head_of_lane = lane_kw // G_DK
    row_kw = lax.broadcasted_iota(I32, (C, G_KW), 0)
    ur = lax.broadcasted_iota(I32, (2 * C, 4 * C), 0) - C
    uc = lax.broadcasted_iota(I32, (2 * C, 4 * C), 1)
    u_mat = ((ur >= 0) & ((uc >= C) | (ur <= uc))).astype(BF16)
    u_mat2 = jnp.concatenate([u_mat, u_mat], axis=0)

    def heads_on_rows(x):
        return jnp.concatenate([jnp.where(head_of_lane == h, x, 0.0) for h in range(G_HEADS)], axis=0)

    def chunk_group(g, S):
        ns = range(n_group)
        rs = [pl.multiple_of((g * n_group + n) * C, C) for n in ns]
        q = [q_ref[0, pl.ds(r, C), :].astype(F32) for r in rs]
        k = [k_ref[0, pl.ds(r, C), :].astype(F32) for r in rs]
        v = [v_ref[0, pl.ds(r, C), :] for r in rs]
        la = [la_ref[0, pl.ds(r, C), :] for r in rs]

        b = [jnp.dot(tril2, jnp.concatenate(_split_bf16(x), axis=0), preferred_element_type=F32)
             for x in la]
        xt = [jnp.concatenate([k[n], la[n]], axis=0).T for n in ns]
        xb = [jnp.dot(jnp.concatenate(_split_bf16(x), axis=1), u_mat2, preferred_element_type=F32)
              for x in xt]
        b_last = [x[:, LANES:] for x in xb]
        kd = [(xt[n] * jnp.exp(b_last[n] - xb[n][:, :LANES])).astype(BF16) for n in ns]
        zeros_v = jnp.zeros((C, G_VW), BF16)
        kv = [jnp.dot(kd[n], jnp.concatenate([v[n], zeros_v], axis=0), preferred_element_type=F32)
              for n in ns]
        kv_d = [jnp.concatenate([x[h * G_DK:(h + 1) * G_DK, h * G_DV:(h + 1) * G_DV] for h in range(G_HEADS)],
                                axis=0) for x in kv]

        states = [S]
        for n in ns:
            states.append(jnp.exp(b_last[n]) * states[n] + kv_d[n])

        r_inter = [jnp.dot(heads_on_rows(q[n] * jnp.exp(b[n])).astype(BF16), states[n].astype(BF16),
                           preferred_element_type=F32) for n in ns]
        o = [jnp.concatenate([x[h * C:(h + 1) * C] for h in range(G_HEADS)], axis=1) for x in r_inter]

        o_rows = [[] for _ in ns]
        for i in range(n_sub):
            r0, r1 = i * G_SUB, (i + 1) * G_SUB
            atts = []
            for n in ns:
                bs = b[n][r0 - 1:r0] if i > 0 else jnp.zeros((1, G_KW), F32)
                qe = q[n][r0:r1] * jnp.exp(b[n][r0:r1] - bs)
                ke = (k[n] * jnp.exp(jnp.where(row_kw < r1, bs - b[n], -jnp.inf))).astype(BF16)
                att = lax.dot_general(heads_on_rows(qe).astype(BF16), ke, (((1,), (1,)), ((), ())),
                                      preferred_element_type=F32)
                tt = lax.broadcasted_iota(I32, att.shape, 0) % G_SUB + r0
                ss = lax.broadcasted_iota(I32, att.shape, 1)
                atts.append(jnp.where(ss <= tt, att, 0.0).astype(BF16))
            for n in ns:
                ov = jnp.dot(atts[n], v[n], preferred_element_type=F32)
                o_rows[n].append(jnp.concatenate(
                    [ov[h * G_SUB:(h + 1) * G_SUB, h * G_DV:(h + 1) * G_DV] for h in range(G_HEADS)], axis=1))

        for n in ns:
            on = o[n] + jnp.concatenate(o_rows[n], axis=0)
            on = _head_rms(on, G_DV) * gain_ref[...] * gr_ref[0, pl.ds(rs[n], C), :].astype(F32)
            o_ref[0, pl.ds(rs[n], C), :] = on.astype(BF16)
        return states[-1]

    s_scr[...] = lax.fori_loop(0, n_chunks // n_group, chunk_group, s_scr[...])

    @pl.when(j == pl.num_programs(1) - 1)
    def _():
        sf_ref[0] = s_scr[...]


def _gla(gq, gk, gv, la, gr, gain, s0):
    B, T, _ = gq.shape
    C = CHUNK
    tb = min(ROW_TILE, T)
    assert T % tb == 0 and tb % C == 0 and C % G_SUB == 0 and 2 * C == LANES
    kw = pl.BlockSpec((1, tb, G_KW), lambda b, j: (b, j, 0))
    vw = pl.BlockSpec((1, tb, G_VW), lambda b, j: (b, j, 0))
    st = pl.BlockSpec((1, G_KW, G_DV), lambda b, j: (b, 0, 0))
    return pl.pallas_call(
        functools.partial(_gla_kernel, C=C, n_chunks=tb // C, n_group=8 if (tb // C) % 8 == 0 else 1),
        grid=(B, T // tb),
        in_specs=[kw, kw, vw, kw, vw, pl.BlockSpec(gain.shape, lambda b, j: (0, 0)), st],
        out_specs=[vw, st],
        out_shape=[jax.ShapeDtypeStruct((B, T, G_VW), BF16), jax.ShapeDtypeStruct((B, G_KW, G_DV), F32)],
        scratch_shapes=[pltpu.VMEM((G_KW, G_DV), F32)],
        compiler_params=_params(("arbitrary", "arbitrary")),
        name="gla",
    )(gq, gk, gv, la, gr, gain, s0)


def _memkv_kernel(mem_ref, gm_ref, w_ref, gk_ref, k_ref, v_ref):
    x = mem_ref[0]
    h = (x * lax.rsqrt(jnp.mean(x * x, -1, keepdims=True) + EPS) * gm_ref[...]).astype(BF16)
    k = jnp.dot(h, w_ref[:, :C_WIDTH], preferred_element_type=F32)
    k_ref[0] = _head_rms(k, C_HEAD_DIM) * gk_ref[...]
    v_ref[0] = jnp.dot(h, w_ref[:, C_WIDTH:], preferred_element_type=F32)


def _memkv(mem, gm, w_kv, gk):
    B, M, _ = mem.shape
    out = pl.BlockSpec((1, M, C_WIDTH), lambda b: (b, 0, 0))
    return pl.pallas_call(
        _memkv_kernel,
        grid=(B,),
        in_specs=[pl.BlockSpec((1, M, D_MODEL), lambda b: (b, 0, 0)),
                  pl.BlockSpec(gm.shape, lambda b: (0, 0)),
                  pl.BlockSpec(w_kv.shape, lambda b: (0, 0)),
                  pl.BlockSpec(gk.shape, lambda b: (0, 0))],
        out_specs=[out, out],
        out_shape=[jax.ShapeDtypeStruct((B, M, C_WIDTH), F32)] * 2,
        compiler_params=_params(("arbitrary",)),
        name="memkv",
    )(mem, gm, w_kv, gk)


def _merge_kernel(x_ref, ya_ref, yb_ref, cq_ref, gt_ref, mk_ref, mv_ref, wa_ref, wb_ref, wc_ref,
                  wo_ref, nf_ref, wr_ref, br_ref,
                  x1_ref, hp_ref, ri_ref, rw_ref, cnt_ref, run_scr, *, tok_base):
    first = jnp.logical_and(pl.program_id(0) == 0, pl.program_id(1) == 0)

    @pl.when(first)
    def _():
        run_scr[...] = jnp.zeros_like(run_scr)

    tm = x_ref.shape[1]
    n_part = 2 if tm % 256 == 0 else 1
    parts = [slice(n * (tm // n_part), (n + 1) * (tm // n_part)) for n in range(n_part)]
    heads = [slice(h * C_HEAD_DIM, (h + 1) * C_HEAD_DIM) for h in range(C_HEADS)]
    mk = [mk_ref[0, :, c].astype(BF16) for c in heads]
    mv = [mv_ref[0, :, c].astype(BF16) for c in heads]
    scores = [[lax.dot_general(cq_ref[0, r, c], k, (((1,), (1,)), ((), ())), preferred_element_type=F32)
               * (C_HEAD_DIM ** -0.5) for c, k in zip(heads, mk)] for r in parts]
    probs = [[jnp.exp(s - s.max(-1, keepdims=True)) for s in ss] for ss in scores]
    sums = [[p.sum(-1, keepdims=True) for p in ps] for ps in probs]
    yc_in = [jnp.concatenate([jnp.dot(p.astype(BF16), v, preferred_element_type=F32) / l
                              for p, l, v in zip(ps, ls, mv)], axis=-1).astype(BF16)
             for ps, ls in zip(probs, sums)]
    y_a = [jnp.dot(ya_ref[0, r, :], wa_ref[...], preferred_element_type=F32) for r in parts]
    y_b = [jnp.dot(yb_ref[0, r, :], wb_ref[...], preferred_element_type=F32) for r in parts]
    y_c = [jnp.dot(y, wc_ref[...], preferred_element_type=F32) for y in yc_in]
    merged = [(gt_ref[0, r, 0:D_MODEL].astype(F32) * a
               + gt_ref[0, r, D_MODEL:2 * D_MODEL].astype(F32) * b
               + gt_ref[0, r, 2 * D_MODEL:3 * D_MODEL].astype(F32) * c).astype(BF16)
              for r, a, b, c in zip(parts, y_a, y_b, y_c)]
    x1s = [x_ref[0, r, :] + jnp.dot(m, wo_ref[...], preferred_element_type=F32) for r, m in zip(parts, merged)]
    for r, v in zip(parts, x1s):
        x1_ref[0, r, :] = v
    h2s = [v * lax.rsqrt(jnp.mean(v * v, -1, keepdims=True) + EPS) * nf_ref[...] for v in x1s]

    logits = jnp.concatenate([jnp.dot(h.astype(BF16), wr_ref[...], preferred_element_type=F32) for h in h2s],
                             axis=0) + br_ref[...]
    h2 = jnp.concatenate(h2s, axis=0)
    lane = lax.broadcasted_iota(I32, (tm, LANES), 1)
    lane_f = lane.astype(F32)
    vals, sels, idxs = [], [], []
    l = logits
    for _ in range(TOP_K):
        m = l.max(-1, keepdims=True)
        idx = jnp.min(jnp.where(l == m, lane_f, float(LANES)), -1, keepdims=True)
        sel = lane_f == idx
        vals.append(m)
        idxs.append(idx)
        sels.append(sel)
        l = jnp.where(sel, -3e38, l)
    es = [jnp.exp(vk - vals[0]) for vk in vals]
    den = es[0] + es[1] + es[2] + es[3]
    cnt = jnp.zeros((tm, LANES), F32)
    for sel in sels:
        cnt = cnt + jnp.where(sel, 1.0, 0.0)
    tp = max(tm, LANES)
    cnt_p = cnt if tp == tm else jnp.concatenate([cnt, jnp.zeros((tp - tm, LANES), F32)], axis=0)
    ri = lax.broadcasted_iota(I32, (tp, tp), 0)
    ci = lax.broadcasted_iota(I32, (tp, tp), 1)
    before = jnp.dot(jnp.where(ci < ri, 1.0, 0.0).astype(BF16), cnt_p.astype(BF16),
                     preferred_element_type=F32)[0:tm] + run_scr[0:1, :]
    r_i = jnp.zeros((tm, LANES), I32)
    r_w = jnp.zeros((tm, LANES), F32)
    for kk in range(TOP_K):
        rank = jnp.sum(jnp.where(sels[kk], before, 0.0), -1, keepdims=True)
        r_i = jnp.where(lane == kk, idxs[kk].astype(I32), r_i)
        r_i = jnp.where(lane == TOP_K + kk, rank.astype(I32), r_i)
        r_w = jnp.where(lane == kk, es[kk] / den, r_w)
    ri_ref[0] = r_i
    rw_ref[0] = r_w
    run_scr[...] = run_scr[...] + jnp.sum(cnt, axis=0, keepdims=True)
    cnt_ref[...] = run_scr[...]

    tok = tok_base + (pl.program_id(0) * pl.num_programs(1) + pl.program_id(1)) * tm \
        + lax.broadcasted_iota(I32, (tm, LANES), 0)
    meta = jnp.where(lane == 0, tok, 0)
    for kk in range(TOP_K):
        meta = jnp.where(lane == 1 + kk, idxs[kk].astype(I32), meta)
    packed = _pack_rows(h2)
    zero = jnp.zeros((tm, LANES), U32)
    slabs = [packed[:, j * LANES:(j + 1) * LANES] for j in range(SLABS)]
    slabs += [lax.bitcast_convert_type(meta, U32)] + [zero] * (ROW_SLABS - SLABS - 1)
    hp_ref[0] = pltpu.einshape("jtl->tjl", jnp.stack(slabs, axis=0))


def _merge(x, ya, yb, cq, gt, mk, mv, wa, wb, wc, wo, nf, wr, br, tok_base):
    B, T, _ = x.shape
    tm = min(ROW_TILE, T)
    assert T % tm == 0

    def row(width):
        return pl.BlockSpec((1, tm, width), lambda b, j: (b, j, 0))

    def full(a):
        return pl.BlockSpec(a.shape, lambda b, j: (0,) * a.ndim)

    mem = pl.BlockSpec((1, N_MEM, C_WIDTH), lambda b, j: (b, 0, 0))
    return pl.pallas_call(
        functools.partial(_merge_kernel, tok_base=tok_base),
        grid=(B, T // tm),
        in_specs=[row(D_MODEL), row(A_WIDTH), row(G_VW), row(C_WIDTH), row(N_BRANCH * D_MODEL), mem, mem,
                  full(wa), full(wb), full(wc), full(wo), full(nf), full(wr), full(br)],
        out_specs=[row(D_MODEL), pl.BlockSpec((1, tm, ROW_SLABS, LANES), lambda b, j: (b, j, 0, 0)), row(LANES),
                   row(LANES), pl.BlockSpec((SUBLANES, LANES), lambda b, j: (0, 0))],
        out_shape=[jax.ShapeDtypeStruct((B, T, D_MODEL), F32), jax.ShapeDtypeStruct((B, T, ROW_SLABS, LANES), U32),
                   jax.ShapeDtypeStruct((B, T, LANES), I32), jax.ShapeDtypeStruct((B, T, LANES), F32),
                   jax.ShapeDtypeStruct((SUBLANES, LANES), F32)],
        scratch_shapes=[pltpu.VMEM((SUBLANES, LANES), F32)],
        compiler_params=_params(("arbitrary", "arbitrary")),
        name="merge",
    )(x, ya, yb, cq, gt, mk, mv, wa, wb, wc, wo, nf, wr, br)


def _dispatch_kernel(dest_ref, dest2_ref, pad_lo_ref, pad_n_ref, hp_ref, hp2_ref, xs_ref, filler_scr, sem, sem2,
                     psem, *, tm, tm2, n_pad, n_tokens):
    def scatter(dref, href, s, n):
        def issue(t, c):
            for kk in range(TOP_K):
                pltpu.make_async_copy(href.at[t], xs_ref.at[dref[0, 0, t * TOP_K + kk]], s).start(priority=kk % 2)
            return c
        lax.fori_loop(0, n, issue, 0, unroll=8)

    def drain(href, s, n):
        for _ in range(TOP_K):
            pltpu.make_async_copy(href, xs_ref.at[pl.ds(0, n)], s).wait()

    scatter(dest_ref, hp_ref, sem, tm)

    @pl.when(pl.program_id(0) == 0)
    def _():
        scatter(dest2_ref, hp2_ref, sem2, tm2)
        shape = (FFN_BLOCK, ROW_SLABS, LANES)
        sub = lax.broadcasted_iota(I32, shape, 1)
        lane = lax.broadcasted_iota(I32, shape, 2)
        filler = jnp.where((sub == META) & (lane == 0), n_tokens,
                           jnp.where((sub == META) & (lane <= TOP_K), N_EXPERTS, 0))
        filler_scr[...] = lax.bitcast_convert_type(filler, U32)

        log_blk = FFN_BLOCK.bit_length() - 1

        def span(e, wait):
            lo, n = pad_lo_ref[e], pad_n_ref[e]
            whole = n >> log_blk

            def go(cp):
                cp.wait() if wait else cp.start()

            def block(j, c):
                go(pltpu.make_async_copy(filler_scr, xs_ref.at[pl.ds(lo + j * FFN_BLOCK, FFN_BLOCK)], psem))
                return c

            lax.fori_loop(0, whole, block, 0)
            rem = n - (whole << log_blk)
            for b in reversed(range(log_blk)):
                off = lo + (whole << log_blk) + ((rem >> (b + 1)) << (b + 1))

                @pl.when((rem >> b) & 1 == 1)
                def _():
                    go(pltpu.make_async_copy(filler_scr.at[pl.ds(0, 1 << b)], xs_ref.at[pl.ds(off, 1 << b)], psem))

        def fill(e, c):
            span(e, False)
            return c

        def fill_wait(e, c):
            span(e, True)
            return c

        lax.fori_loop(0, n_pad, fill, 0)
        lax.fori_loop(0, n_pad, fill_wait, 0)
        drain(hp2_ref, sem2, tm2)

    drain(hp_ref, sem, tm)


def _dispatch(dest, hp, dest2, hp2, pad_lo, pad_n, rows):
    N, N2 = hp.shape[0], hp2.shape[0]
    tm = min(ROW_TILE, N)
    assert N % tm == 0
    n_steps = N // tm
    smem = functools.partial(pl.BlockSpec, memory_space=pltpu.SMEM)
    n_pad = pad_lo.shape[0]
    return pl.pallas_call(
        functools.partial(_dispatch_kernel, tm=tm, tm2=N2, n_pad=n_pad, n_tokens=N + N2),
        grid=(n_steps,),
        in_specs=[smem((1, 1, tm * TOP_K), lambda i: (i, 0, 0)),
                  smem((1, 1, N2 * TOP_K), lambda i: (0, 0, 0)),
                  smem((n_pad,), lambda i: (0,)), smem((n_pad,), lambda i: (0,)),
                  pl.BlockSpec((tm, ROW_SLABS, LANES), lambda i: (i, 0, 0)),
                  pl.BlockSpec((N2, ROW_SLABS, LANES), lambda i: (0, 0, 0))],
        out_specs=pl.BlockSpec(memory_space=pl.ANY),
        out_shape=jax.ShapeDtypeStruct((rows, ROW_SLABS, LANES), U32),
        scratch_shapes=[pltpu.VMEM((FFN_BLOCK, ROW_SLABS, LANES), U32), pltpu.SemaphoreType.DMA(()),
                        pltpu.SemaphoreType.DMA(()), pltpu.SemaphoreType.DMA(())],
        compiler_params=_params(("arbitrary",)),
        name="moe_dispatch",
    )(dest.reshape(n_steps, 1, tm * TOP_K), dest2.reshape(1, 1, N2 * TOP_K), pad_lo, pad_n, hp, hp2)


def _ffn_kernel(be_ref, nu_ref, x_ref, wgu_ref, bgu_ref, wd_ref, bd_ref, y4_ref,
                wgu_bf, wd_bf, ybuf, idv, ids_smem, sem_ids, sem_rows, *, n_blk, n_tokens):
    i = pl.program_id(0)
    n_used = nu_ref[0]
    n_assign = n_tokens * TOP_K
    prev = be_ref[jnp.maximum(i - 1, 0)]
    new_expert = jnp.logical_or(i == 0, be_ref[i] != prev)

    def rows_done(s):
        pltpu.make_async_copy(ybuf.at[s], y4_ref.at[pl.ds(0, FFN_BLOCK)], sem_rows.at[s]).wait()

    def ids_copy(s):
        return pltpu.make_async_copy(idv, ids_smem.at[s], sem_ids)

    def start_rows(s, lo, hi):
        for r in range(lo, hi):
            pltpu.make_async_copy(ybuf.at[s, r], y4_ref.at[ids_smem[s, 0, r]],
                                  sem_rows.at[s]).start(priority=r % 2)

    @pl.when(i == 0)
    def _():
        ybuf[1] = jnp.zeros(ybuf.shape[1:], U32)
        spare = pltpu.make_async_copy(ybuf.at[1], y4_ref.at[pl.ds(n_assign, FFN_BLOCK)], sem_rows.at[1])
        spare.start()
        spare.wait()

    @pl.when(new_expert)
    def _():
        wgu_bf[...] = wgu_ref[0].astype(BF16)
        wd_bf[...] = wd_ref[0].astype(BF16)

    def compute(s, flush):
        q = FFN_BLOCK // 4
        if flush:
            ids_copy(1 - s).wait()
            start_rows(1 - s, 0, q)
        xm = pltpu.einshape("tjl->jtl", x_ref[...])
        x_lo, x_hi = _unpack_rows(jnp.concatenate([xm[j] for j in range(SLABS)], axis=-1))
        meta = lax.bitcast_convert_type(xm[META], I32)
        tok = meta[:, 0:1]
        choice = jnp.zeros_like(tok)
        for kk in range(1, TOP_K):
            choice = jnp.where(meta[:, 1 + kk:2 + kk] == be_ref[i], kk, choice)
        local = lax.broadcasted_iota(I32, tok.shape, 0)
        dest = jnp.where(tok >= n_tokens, n_assign + local, tok * TOP_K + choice)
        dest_t = jnp.broadcast_to(dest.astype(F32), (FFN_BLOCK, LANES)).T
        idv[...] = dest_t[0:SUBLANES].astype(I32)
        ids_copy(s).start()
        gu = (jnp.dot(x_lo.astype(BF16), wgu_bf[0:HALF, :], preferred_element_type=F32)
              + jnp.dot(x_hi.astype(BF16), wgu_bf[HALF:, :], preferred_element_type=F32)
              + bgu_ref[0])
        if flush:
            start_rows(1 - s, q, 2 * q)
        gate = jnp.minimum(gu[:, :D_EXPERT], SWIGLU_LIMIT)
        up = jnp.clip(gu[:, D_EXPERT:], -SWIGLU_LIMIT, SWIGLU_LIMIT)
        act = (up + 1.0) * (gate * _sigmoid(SWIGLU_ALPHA * gate))
        if flush:
            start_rows(1 - s, 2 * q, 3 * q)
        y = jnp.dot(act.astype(BF16), wd_bf[...], preferred_element_type=F32) + bd_ref[0]
        if flush:
            start_rows(1 - s, 3 * q, 4 * q)
        packed = _pack_rows(y)
        pl.when(i >= 2)(functools.partial(rows_done, s))
        _store_slabs(ybuf, (s,), packed)

    def flush_only(s, also):
        ids_copy(s).wait()
        start_rows(s, 0, FFN_BLOCK)
        pl.when(also)(functools.partial(rows_done, 1 - s))
        rows_done(s)

    real = i < n_used
    prev_real = jnp.logical_and(i >= 1, i - 1 < n_used)
    for s in range(2):
        mine = i % 2 == s
        pl.when(mine & real & prev_real)(functools.partial(compute, s, True))
        pl.when(mine & real & jnp.logical_not(prev_real))(functools.partial(compute, s, False))
        pl.when(mine & jnp.logical_not(real) & prev_real)(functools.partial(flush_only, 1 - s, i >= 2))
        pl.when(mine & real & (i == n_blk - 1))(functools.partial(flush_only, s, i >= 1))


def _ffn(blk_expert, n_used, xs, n_tokens, w_gu, b_gu, w_d, b_d):
    P = xs.shape[0]
    nblk = P // FFN_BLOCK
    grid_spec = pltpu.PrefetchScalarGridSpec(
        num_scalar_prefetch=2,
        grid=(nblk,),
        in_specs=[pl.BlockSpec((FFN_BLOCK, ROW_SLABS, LANES), lambda i, be, nu: (jnp.minimum(i, nu[0] - 1), 0, 0)),
                  pl.BlockSpec((1, D_MODEL, 2 * D_EXPERT), lambda i, be, nu: (be[i], 0, 0)),
                  pl.BlockSpec((1, 1, 2 * D_EXPERT), lambda i, be, nu: (be[i], 0, 0)),
                  pl.BlockSpec((1, D_EXPERT, D_MODEL), lambda i, be, nu: (be[i], 0, 0)),
                  pl.BlockSpec((1, 1, D_MODEL), lambda i, be, nu: (be[i], 0, 0))],
        out_specs=pl.BlockSpec(memory_space=pl.ANY),
        scratch_shapes=[pltpu.VMEM((D_MODEL, 2 * D_EXPERT), BF16), pltpu.VMEM((D_EXPERT, D_MODEL), BF16),
                        pltpu.VMEM((2, FFN_BLOCK, SLABS, LANES), U32), pltpu.VMEM((SUBLANES, FFN_BLOCK), I32),
                        pltpu.SMEM((2, SUBLANES, FFN_BLOCK), I32), pltpu.SemaphoreType.DMA(()),
                        pltpu.SemaphoreType.DMA((2,))],
    )
    return pl.pallas_call(
        functools.partial(_ffn_kernel, n_blk=nblk, n_tokens=n_tokens),
        grid_spec=grid_spec,
        out_shape=jax.ShapeDtypeStruct((n_tokens * TOP_K + FFN_BLOCK, SLABS, LANES), U32),
        compiler_params=_params(("arbitrary",)),
        name="moe_ffn",
    )(blk_expert, n_used, xs, w_gu, b_gu.reshape(N_EXPERTS, 1, -1), w_d, b_d.reshape(N_EXPERTS, 1, -1))


def _combine_kernel(x1_ref, rw_ref, y_ref, o_ref, *, tm):
    w = rw_ref[...]
    acc_lo = x1_ref[:, :HALF]
    acc_hi = x1_ref[:, HALF:]
    for kk in range(TOP_K):
        rows = y_ref[pl.ds(kk, tm, stride=TOP_K)]
        slab_major = pltpu.einshape("tjl->jtl", rows)
        lo, hi = _unpack_rows(jnp.concatenate([slab_major[j] for j in range(SLABS)], axis=-1))
        wk = w[:, kk:kk + 1]
        acc_lo = acc_lo + wk * lo
        acc_hi = acc_hi + wk * hi
    o_ref[:, :HALF] = acc_lo
    o_ref[:, HALF:] = acc_hi


def _combine(x1, rw, y4, tok_base):
    N = x1.shape[0]
    tm = min(256, N)
    assert N % tm == 0 and tok_base % tm == 0
    first = tok_base // tm
    return pl.pallas_call(
        functools.partial(_combine_kernel, tm=tm),
        grid=(N // tm,),
        in_specs=[pl.BlockSpec((tm, D_MODEL), lambda i: (i, 0)),
                  pl.BlockSpec((tm, LANES), lambda i: (i, 0)),
                  pl.BlockSpec((tm * TOP_K, SLABS, LANES), lambda i: (first + i, 0, 0))],
        out_specs=pl.BlockSpec((tm, D_MODEL), lambda i: (i, 0)),
        out_shape=jax.ShapeDtypeStruct((N, D_MODEL), F32),
        compiler_params=_params(("arbitrary",)),
        name="moe_combine",
    )(x1, rw, y4)


def _moe(groups, w_gu, b_gu, w_d, b_d):
    n_tokens = sum(g[0].shape[0] for g in groups)
    n_assign = n_tokens * TOP_K
    nblk = (n_assign + N_EXPERTS * (FFN_BLOCK - 1) + FFN_BLOCK - 1) // FFN_BLOCK
    counts = [g[4].astype(I32) for g in groups]
    total = sum(counts)
    padded = (total + FFN_BLOCK - 1) // FFN_BLOCK * FFN_BLOCK
    pend = jnp.cumsum(padded)
    pstart = pend - padded
    n_used = pend[-1:] // FFN_BLOCK
    blk = jnp.minimum(jnp.arange(nblk, dtype=I32), n_used[0] - 1) * FFN_BLOCK
    blk_expert = jnp.minimum(jnp.sum(pend[None, :] <= blk[:, None], axis=1), N_EXPERTS - 1).astype(I32)

    experts = jnp.arange(N_EXPERTS, dtype=I32)
    dests = []
    base = pstart
    for g, c in zip(groups, counts):
        idx, rank = g[2][:, :TOP_K], g[2][:, TOP_K:2 * TOP_K]
        dests.append(jnp.sum(jnp.where(idx[..., None] == experts, base, 0), axis=-1) + rank)
        base = base + c

    rows = nblk * FFN_BLOCK
    pad_lo = jnp.concatenate([pstart + total, pend[-1:]])
    pad_n = jnp.concatenate([padded - total, rows - pend[-1:]])
    (g_main, g_small), (d_main, d_small) = groups, dests
    xs = _dispatch(d_main, g_main[1], d_small, g_small[1], pad_lo, pad_n, rows)
    y4 = _ffn(blk_expert, n_used.astype(I32), xs, n_tokens, w_gu, b_gu, w_d, b_d)
    return [_combine(g_main[0], g_main[3], y4, 0), _combine(g_small[0], g_small[3], y4, g_main[0].shape[0])]


def _tile_lanes(g, reps):
    return jnp.tile(g.astype(F32), reps)[None, :]


def kernel(x_prompt, x_sample, mem_prompt, cache_attn_k, cache_attn_v, state_gla, cache_mem_k, cache_mem_v, norm_mix, w_in, a_q_norm, a_k_norm, rel_bias_table, w_a_o, w_gla_a_up, b_gla_a, gla_out_norm, w_b_o, c_q_norm, c_k_norm, norm_mem, w_mem_kv, w_c_o, b_gate, w_out, norm_ffn, w_router, b_router, w_gate_up, b_gate_up, w_down, b_down):
    depth = norm_mix.shape[0]
    assert depth == 1
    l = 0
    B, S, _ = x_prompt.shape
    Bs, Ts, _ = x_sample.shape
    keep = min(WINDOW, S)

    w = w_in[l]
    sizes = (A_WIDTH, A_WIDTH, A_WIDTH, G_KW, G_KW, G_VW, G_VW, G_RANK, C_WIDTH, N_BRANCH * D_MODEL)
    offs = [0]
    for s_ in sizes:
        offs.append(offs[-1] + s_)
    seg = [w[:, offs[i]:offs[i + 1]] for i in range(len(sizes))]
    w_r = jnp.concatenate(seg[0:7] + [seg[8], seg[9], seg[7], jnp.zeros((D_MODEL, LANES - G_RANK), F32)],
                          axis=1).astype(BF16)
    nm = norm_mix[l][None, :]
    aqn = _tile_lanes(a_q_norm[l], A_HEADS)
    akn = _tile_lanes(a_k_norm[l], A_HEADS)
    cqn = _tile_lanes(c_q_norm[l], C_HEADS)
    ckn = _tile_lanes(c_k_norm[l], C_HEADS)
    gon = _tile_lanes(gla_out_norm[l], G_HEADS)
    wup = jnp.concatenate([w_gla_a_up[l], jnp.zeros((LANES - G_RANK, G_KW), F32)], axis=0).astype(BF16)
    bla = b_gla_a[l][None, :]
    bg = b_gate[l][None, :]
    wa, wb, wc, wo = (t[l].astype(BF16) for t in (w_a_o, w_b_o, w_c_o, w_out))
    nf = norm_ffn[l][None, :]
    wr = jnp.concatenate([w_router[l], jnp.zeros((D_MODEL, LANES - N_EXPERTS), F32)], axis=1).astype(BF16)
    br = jnp.concatenate([b_router[l], jnp.full((LANES - N_EXPERTS,), NEG_INF, F32)])[None, :]
    table = rel_bias_table[l]

    mk, mv = _memkv(mem_prompt, norm_mem[l][None, :], w_mem_kv[l].astype(BF16), ckn)
    (aq, ak, av, gq, gk, gv, gr, la, cq, gt, ak_tail, av_tail) = _inproj(
        x_prompt, keep, nm, w_r, aqn, akn, cqn, wup, bla, bg)
    ya = _attn_prompt(aq, ak, av, table)
    yb, s_prompt = _gla(gq, gk, gv, la, gr, gon, jnp.zeros((B, G_KW, G_DV), F32))
    x1_p, hp_p, ri_p, rw_p, cnt_p = _merge(x_prompt, ya, yb, cq, gt, mk, mv, wa, wb, wc, wo, nf, wr, br, 0)

    (aq, ak, av, gq, gk, gv, gr, la, cq, gt, ak_new, av_new) = _inproj(
        x_sample.reshape(1, Bs * Ts, D_MODEL), Bs * Ts, nm, w_r, aqn, akn, cqn, wup, bla, bg)
    rs = lambda t: t.reshape(Bs, Ts, t.shape[-1])
    P = cache_attn_k.shape[2]
    ya = _attn_sample(rs(aq), rs(ak), rs(av), cache_attn_k[l].reshape(Bs, P, A_WIDTH),
                      cache_attn_v[l].reshape(Bs, P, A_WIDTH), table)
    t_pad = (Ts + CHUNK - 1) // CHUNK * CHUNK
    zp = lambda t: jnp.pad(rs(t), ((0, 0), (0, t_pad - Ts), (0, 0)))
    yb, s_sample = _gla(zp(gq), zp(gk), zp(gv), zp(la), zp(gr), gon, state_gla[l].reshape(Bs, G_KW, G_DV))
    yb = yb[:, :Ts]
    x1_s, hp_s, ri_s, rw_s, cnt_s = _merge(
        x_sample, ya, yb, rs(cq), rs(gt), cache_mem_k[l].reshape(Bs, N_MEM, C_WIDTH),
        cache_mem_v[l].reshape(Bs, N_MEM, C_WIDTH), wa, wb, wc, wo, nf, wr, br, B * S)

    flat = lambda t: t.reshape((-1,) + t.shape[2:])
    y_p, y_s = _moe(
        [(flat(x1_p), flat(hp_p), flat(ri_p), flat(rw_p), cnt_p[0, :N_EXPERTS]),
         (flat(x1_s), flat(hp_s), flat(ri_s), flat(rw_s), cnt_s[0, :N_EXPERTS])],
        w_gate_up[l], b_gate_up[l], w_down[l], b_down[l])

    return (y_p.reshape(B, S, D_MODEL), y_s.reshape(Bs, Ts, D_MODEL),
            ak_tail.reshape(1, B, keep, A_HEADS, A_HEAD_DIM), av_tail.reshape(1, B, keep, A_HEADS, A_HEAD_DIM),
            s_prompt.reshape(1, B, G_HEADS, G_DK, G_DV),
            mk.reshape(1, B, N_MEM, C_HEADS, C_HEAD_DIM), mv.reshape(1, B, N_MEM, C_HEADS, C_HEAD_DIM),
            ak_new.reshape(1, Bs, Ts, A_HEADS, A_HEAD_DIM), av_new.reshape(1, Bs, Ts, A_HEADS, A_HEAD_DIM),
            s_sample.reshape(1, Bs, G_HEADS, G_DK, G_DV))
```

```python
import functools

import jax
import jax.numpy as jnp
from jax import lax
from jax.experimental import pallas as pl
from jax.experimental.pallas import tpu as pltpu

F32 = jnp.float32
BF16 = jnp.bfloat16
U32 = jnp.uint32
I32 = jnp.int32

D_MODEL = 1024
CHUNK = 64
BAND_CHUNKS = 8
WINDOW = BAND_CHUNKS * CHUNK
N_MEM = 256
A_HEADS, A_HEAD_DIM = 8, 64
A_WIDTH = A_HEADS * A_HEAD_DIM
REL_MAX = 128
G_HEADS, G_DK, G_DV = 4, 64, 128
G_KW, G_VW = G_HEADS * G_DK, G_HEADS * G_DV
G_RANK = 16
G_TAU = 16.0
G_SUB = 16
C_HEADS, C_HEAD_DIM = 4, 128
C_WIDTH = C_HEADS * C_HEAD_DIM
N_BRANCH = 3
N_EXPERTS = 32
TOP_K = 4
D_EXPERT = 1024
SWIGLU_LIMIT = 7.0
SWIGLU_ALPHA = 1.702
EPS = 1e-6
NEG_INF = -1e30

LANES = 128
SUBLANES = 8
HALF = D_MODEL // 2
ROW_TILE = 512
ATTN_SUB = 128
FFN_BLOCK = 512
VMEM_LIMIT = 56 * 1024 * 1024

OFF_AQ, OFF_AK, OFF_AV = 0, 512, 1024
OFF_GQ, OFF_GK, OFF_GV, OFF_GR = 1536, 1792, 2048, 2560
OFF_CQ, OFF_GATE, OFF_LR = 3072, 3584, 6656
IN_COLS = OFF_LR + LANES


def _params(sem):
    return pltpu.CompilerParams(dimension_semantics=sem, vmem_limit_bytes=VMEM_LIMIT)


def _sigmoid(x):
    return 0.5 * jnp.tanh(0.5 * x) + 0.5


def _head_rms(y, head_dim):
    cols = []
    for p in range(y.shape[1] // LANES):
        blk = y[:, p * LANES:(p + 1) * LANES]
        sq = blk * blk
        if head_dim == LANES:
            sc = lax.rsqrt(jnp.sum(sq, -1, keepdims=True) * (1.0 / LANES) + EPS)
        else:
            lo = lax.broadcasted_iota(I32, blk.shape, 1) < head_dim
            s_lo = jnp.sum(jnp.where(lo, sq, 0.0), -1, keepdims=True)
            s_hi = jnp.sum(jnp.where(lo, 0.0, sq), -1, keepdims=True)
            sc = jnp.where(lo, lax.rsqrt(s_lo * (1.0 / head_dim) + EPS),
                           lax.rsqrt(s_hi * (1.0 / head_dim) + EPS))
        cols.append(blk * sc)
    return jnp.concatenate(cols, axis=-1)


def _split_bf16(x):
    hi = x.astype(BF16)
    lo = (x - hi.astype(F32)).astype(BF16)
    return hi, lo


def _pack_rows(x):
    lo = lax.bitcast_convert_type(x[:, :HALF].astype(BF16).astype(F32), U32)
    hi = lax.bitcast_convert_type(x[:, HALF:].astype(BF16).astype(F32), U32)
    return (lo >> 16) | (hi & jnp.uint32(0xFFFF0000))


def _unpack_rows(u):
    lo = lax.bitcast_convert_type(u << 16, F32)
    hi = lax.bitcast_convert_type(u & jnp.uint32(0xFFFF0000), F32)
    return lo, hi


SLABS = HALF // LANES
ROW_SLABS = SUBLANES
INFO_SLAB = SLABS


def _store_slabs(ref, lead, u):
    slab_major = jnp.stack([u[:, j * LANES:(j + 1) * LANES] for j in range(SLABS)], axis=0)
    ref[lead + (slice(None),) * 3] = pltpu.einshape("jtl->tjl", slab_major)


def _inproj_kernel(x_ref, nm_ref, w_ref, aqn_ref, akn_ref, cqn_ref, wup_ref, bla_ref, bg_ref,
                   aq_ref, ak_ref, av_ref, gq_ref, gk_ref, gv_ref, gr_ref, la_ref, cq_ref, gt_ref,
                   akt_ref, avt_ref, *, n_tiles, n_tail):
    j = pl.program_id(1)
    x = x_ref[0]
    h = (x * lax.rsqrt(jnp.mean(x * x, -1, keepdims=True) + EPS) * nm_ref[...]).astype(BF16)

    def seg(off, width):
        return jnp.dot(h, w_ref[:, off:off + width], preferred_element_type=F32)

    in_tail = j >= n_tiles - n_tail

    lr = seg(OFF_LR, LANES).astype(BF16)

    aq = _head_rms(seg(OFF_AQ, A_WIDTH), A_HEAD_DIM) * aqn_ref[...] * (A_HEAD_DIM ** -0.5)
    aq_ref[0] = aq.astype(BF16)

    ak = _head_rms(seg(OFF_AK, A_WIDTH), A_HEAD_DIM) * akn_ref[...]
    ak_ref[0] = ak.astype(BF16)

    @pl.when(in_tail)
    def _():
        akt_ref[0] = ak

    z = jnp.dot(lr, wup_ref[...], preferred_element_type=F32) + bla_ref[...]
    la_ref[0] = (jnp.minimum(z, 0.0) - jnp.log1p(jnp.exp(-jnp.abs(z)))) * (1.0 / G_TAU)

    gq_ref[0] = (seg(OFF_GQ, G_KW) * (G_DK ** -0.5)).astype(BF16)
    gk_ref[0] = seg(OFF_GK, G_KW).astype(BF16)
    gv_ref[0] = seg(OFF_GV, G_VW).astype(BF16)
    gr = seg(OFF_GR, G_VW)
    gr_ref[0] = (gr * _sigmoid(gr)).astype(BF16)

    cq = _head_rms(seg(OFF_CQ, C_WIDTH), C_HEAD_DIM) * cqn_ref[...]
    cq_ref[0] = cq.astype(BF16)

    gate_chunk = 512
    for c in range(N_BRANCH * D_MODEL // gate_chunk):
        lo = c * gate_chunk
        g = seg(OFF_GATE + lo, gate_chunk) + bg_ref[:, lo:lo + gate_chunk]
        gt_ref[0, :, lo:lo + gate_chunk] = _sigmoid(g).astype(BF16)

    av = seg(OFF_AV, A_WIDTH)
    av_ref[0] = av.astype(BF16)

    @pl.when(in_tail)
    def _():
        avt_ref[0] = av


def _inproj(x, keep, nm, w_r, aqn, akn, cqn, wup, bla, bg):
    G, R, _ = x.shape
    tm = min(ROW_TILE, R)
    n_tiles = R // tm
    n_tail = keep // tm
    assert R % tm == 0 and keep % tm == 0 and n_tail >= 1

    def row(width, dtype):
        return (jax.ShapeDtypeStruct((G, R, width), dtype),
                pl.BlockSpec((1, tm, width), lambda g, j: (g, j, 0)))

    def tail(width):
        return (jax.ShapeDtypeStruct((G, keep, width), F32),
                pl.BlockSpec((1, tm, width), lambda g, j: (g, jnp.maximum(j - (n_tiles - n_tail), 0), 0)))

    outs = [row(A_WIDTH, BF16), row(A_WIDTH, BF16), row(A_WIDTH, BF16), row(G_KW, BF16),
            row(G_KW, BF16), row(G_VW, BF16), row(G_VW, BF16), row(G_KW, F32), row(C_WIDTH, BF16),
            row(N_BRANCH * D_MODEL, BF16), tail(A_WIDTH), tail(A_WIDTH)]

    def full(a):
        return pl.BlockSpec(a.shape, lambda g, j: (0,) * a.ndim)

    return pl.pallas_call(
        functools.partial(_inproj_kernel, n_tiles=n_tiles, n_tail=n_tail),
        grid=(G, n_tiles),
        in_specs=[pl.BlockSpec((1, tm, D_MODEL), lambda g, j: (g, j, 0)), full(nm), full(w_r),
                  full(aqn), full(akn), full(cqn), full(wup), full(bla), full(bg)],
        out_specs=[o[1] for o in outs],
        out_shape=[o[0] for o in outs],
        compiler_params=_params(("arbitrary", "arbitrary")),
        name="inproj",
    )(x, nm, w_r, aqn, akn, cqn, wup, bla, bg)


def _attend(pairs, lo_mask):
    T = pairs[0][0].shape[0]
    scores = []
    for q, parts in pairs:
        zero = jnp.zeros_like(q)
        q2 = jnp.concatenate([jnp.where(lo_mask, q, zero), jnp.where(lo_mask, zero, q)], axis=0)
        ss = []
        for (k, _, bias2, valid) in parts:
            s = lax.dot_general(q2, k, (((1,), (1,)), ((), ())), preferred_element_type=F32) + bias2
            if valid is not None:
                s = jnp.where(valid, s, NEG_INF)
            ss.append(s)
        scores.append(ss)
    probs, sums = [], []
    for ss in scores:
        m = ss[0].max(-1, keepdims=True)
        for s in ss[1:]:
            m = jnp.maximum(m, s.max(-1, keepdims=True))
        ps = [jnp.exp(s - m) for s in ss]
        l = ps[0].sum(-1, keepdims=True)
        for p in ps[1:]:
            l = l + p.sum(-1, keepdims=True)
        probs.append([p.astype(BF16) for p in ps])
        sums.append(l)
    outs = []
    for (q, parts), ps, l in zip(pairs, probs, sums):
        o = jnp.dot(ps[0], parts[0][1], preferred_element_type=F32)
        for p, part in zip(ps[1:], parts[1:]):
            o = o + jnp.dot(p, part[1], preferred_element_type=F32)
        o = o / l
        outs.append(jnp.where(lo_mask, o[:T], o[T:]))
    return outs


def _attn_prompt_kernel(q_ref, kp_ref, kc_ref, vp_ref, vc_ref, bias_ref, o_ref, *, tb):
    j = pl.program_id(1)
    lo_mask = lax.broadcasted_iota(I32, (ATTN_SUB, LANES), 1) < A_HEAD_DIM

    def block(with_prev):
        for s in range(tb // ATTN_SUB):
            r0 = s * ATTN_SUB
            len_a = tb - r0
            len_b = r0 + ATTN_SUB
            pairs = []
            for p in range(A_WIDTH // LANES):
                c0 = p * LANES
                q = q_ref[0, r0:r0 + ATTN_SUB, c0:c0 + LANES]
                parts = [(kc_ref[0, 0:len_b, c0:c0 + LANES], vc_ref[0, 0:len_b, c0:c0 + LANES],
                          bias_ref[p, :, len_a:len_a + len_b], None)]
                if with_prev:
                    parts.insert(0, (kp_ref[0, r0:tb, c0:c0 + LANES], vp_ref[0, r0:tb, c0:c0 + LANES],
                                     bias_ref[p, :, 0:len_a], None))
                pairs.append((q, parts))
            outs = _attend(pairs[:2], lo_mask) + _attend(pairs[2:], lo_mask)
            for p, o in enumerate(outs):
                o_ref[0, r0:r0 + ATTN_SUB, p * LANES:(p + 1) * LANES] = o.astype(BF16)

    pl.when(j > 0)(functools.partial(block, True))
    pl.when(j == 0)(functools.partial(block, False))


def _rel_bias(table, n_q, n_k, offset):
    period = n_q + n_k - 1
    m = jnp.arange(period)
    u = table[:, jnp.clip(n_q - 1 + offset - m, -REL_MAX, REL_MAX) + REL_MAX].astype(F32)
    rows = jnp.tile(u, (1, n_q + 1))[:, :n_q * (period + 1)].reshape(-1, n_q, period + 1)[:, :, :n_k]
    return rows[:, ::-1, :]


def _band_bias(table):
    qc = jnp.arange(ATTN_SUB)[:, None] // CHUNK
    kc = jnp.arange(ATTN_SUB + WINDOW)[None, :] // CHUNK
    ok = (kc >= qc) & (kc <= qc + BAND_CHUNKS)
    return jnp.where(ok[None], _rel_bias(table, ATTN_SUB, ATTN_SUB + WINDOW, WINDOW), NEG_INF)


def _attn_prompt(aq, ak, av, table):
    B, S, _ = aq.shape
    tb = WINDOW
    assert S % tb == 0
    bias = _band_bias(table).reshape(A_WIDTH // LANES, 2 * ATTN_SUB, ATTN_SUB + WINDOW)
    cur = pl.BlockSpec((1, tb, A_WIDTH), lambda b, j: (b, j, 0))
    prev = pl.BlockSpec((1, tb, A_WIDTH), lambda b, j: (b, jnp.maximum(j - 1, 0), 0))
    return pl.pallas_call(
        functools.partial(_attn_prompt_kernel, tb=tb),
        grid=(B, S // tb),
        in_specs=[cur, prev, cur, prev, cur, pl.BlockSpec(bias.shape, lambda b, j: (0, 0, 0))],
        out_specs=cur,
        out_shape=jax.ShapeDtypeStruct((B, S, A_WIDTH), BF16),
        compiler_params=_params(("arbitrary", "arbitrary")),
        name="attn_prompt",
    )(aq, ak, ak, av, av, bias)


def _attn_sample_kernel(q_ref, k_ref, v_ref, bias_ref, o_ref):
    T = q_ref.shape[1]
    lo_mask = lax.broadcasted_iota(I32, (T, LANES), 1) < A_HEAD_DIM
    pairs = []
    for p in range(A_WIDTH // LANES):
        c0 = p * LANES
        parts = [(k_ref[0, :, c0:c0 + LANES], v_ref[0, :, c0:c0 + LANES], bias_ref[p], None)]
        pairs.append((q_ref[0, :, c0:c0 + LANES], parts))
    for p, o in enumerate(_attend(pairs, lo_mask)):
        o_ref[0, :, p * LANES:(p + 1) * LANES] = o.astype(BF16)


def _attn_sample(aq, ak, av, cache_k, cache_v, table):
    B, T, _ = aq.shape
    P = cache_k.shape[1]
    L = (P + T + LANES - 1) // LANES * LANES
    pad = jnp.zeros((B, L - P - T, A_WIDTH), BF16)
    kk = jnp.concatenate([cache_k.astype(BF16), ak, pad], axis=1)
    vv = jnp.concatenate([cache_v.astype(BF16), av, pad], axis=1)
    bias = jnp.where((jnp.arange(L) < P + T)[None, None, :], _rel_bias(table, T, L, P), NEG_INF)
    bias = bias.reshape(A_WIDTH // LANES, 2 * T, L)
    new = pl.BlockSpec((1, T, A_WIDTH), lambda b: (b, 0, 0))
    old = pl.BlockSpec((1, L, A_WIDTH), lambda b: (b, 0, 0))
    return pl.pallas_call(
        _attn_sample_kernel,
        grid=(B,),
        in_specs=[new, old, old, pl.BlockSpec(bias.shape, lambda b: (0, 0, 0))],
        out_specs=new,
        out_shape=jax.ShapeDtypeStruct((B, T, A_WIDTH), BF16),
        compiler_params=_params(("arbitrary",)),
        name="attn_sample",
    )(aq, kk, vv, bias)


def _gla_kernel(q_ref, k_ref, v_ref, la_ref, gr_ref, gain_ref, s0_ref, o_ref, sf_ref, s_scr, *, C, n_chunks,
                n_group):
    j = pl.program_id(1)

    @pl.when(j == 0)
    def _():
        s_scr[...] = s0_ref[0]

    n_sub = C // G_SUB
    ri = lax.broadcasted_iota(I32, (C, C), 0)
    ci = lax.broadcasted_iota(I32, (C, C), 1)
    tril = (ci <= ri).astype(BF16)
    tril2 = jnp.concatenate([tril, tril], axis=1)
    lane_kw = lax.broadcasted_iota(I32, (1, G_KW), 1)
    head_of_lane = lane_kw // G_DK
    row_kw = lax.broadcasted_iota(I32, (C, G_KW), 0)
    ur = lax.broadcasted_iota(I32, (2 * C, 4 * C), 0) - C
    uc = lax.broadcasted_iota(I32, (2 * C, 4 * C), 1)
    u_mat = ((ur >= 0) & ((uc >= C) | (ur <= uc))).astype(BF16)
    u_mat2 = jnp.concatenate([u_mat, u_mat], axis=0)

    def heads_on_rows(x):
        return jnp.concatenate([jnp.where(head_of_lane == h, x, 0.0) for h in range(G_HEADS)], axis=0)

    def chunk_group(g, S):
        ns = range(n_group)
        rs = [pl.multiple_of((g * n_group + n) * C, C) for n in ns]
        q = [q_ref[0, pl.ds(r, C), :].astype(F32) for r in rs]
        k = [k_ref[0, pl.ds(r, C), :].astype(F32) for r in rs]
        v = [v_ref[0, pl.ds(r, C), :] for r in rs]
        la = [la_ref[0, pl.ds(r, C), :] for r in rs]

        b = [jnp.dot(tril2, jnp.concatenate(_split_bf16(x), axis=0), preferred_element_type=F32)
             for x in la]
        xt = [jnp.concatenate([k[n], la[n]], axis=0).T for n in ns]
        xb = [jnp.dot(jnp.concatenate(_split_bf16(x), axis=1), u_mat2, preferred_element_type=F32)
              for x in xt]
        b_last = [x[:, LANES:] for x in xb]
        kd = [(xt[n] * jnp.exp(b_last[n] - xb[n][:, :LANES])).astype(BF16) for n in ns]
        zeros_v = jnp.zeros((C, G_VW), BF16)
        kv = [jnp.dot(kd[n], jnp.concatenate([v[n], zeros_v], axis=0), preferred_element_type=F32)
              for n in ns]
        kv_d = [jnp.concatenate([x[h * G_DK:(h + 1) * G_DK, h * G_DV:(h + 1) * G_DV] for h in range(G_HEADS)],
                                axis=0) for x in kv]

        states = [S]
        for n in ns:
            states.append(jnp.exp(b_last[n]) * states[n] + kv_d[n])

        r_inter = [jnp.dot(heads_on_rows(q[n] * jnp.exp(b[n])).astype(BF16), states[n].astype(BF16),
                           preferred_element_type=F32) for n in ns]
        o = [jnp.concatenate([x[h * C:(h + 1) * C] for h in range(G_HEADS)], axis=1) for x in r_inter]

        o_rows = [[] for _ in ns]
        for i in range(n_sub):
            r0, r1 = i * G_SUB, (i + 1) * G_SUB
            atts = []
            for n in ns:
                bs = b[n][r0 - 1:r0] if i > 0 else jnp.zeros((1, G_KW), F32)
                qe = q[n][r0:r1] * jnp.exp(b[n][r0:r1] - bs)
                ke = (k[n] * jnp.exp(jnp.where(row_kw < r1, bs - b[n], -jnp.inf))).astype(BF16)
                att = lax.dot_general(heads_on_rows(qe).astype(BF16), ke, (((1,), (1,)), ((), ())),
                                      preferred_element_type=F32)
                tt = lax.broadcasted_iota(I32, att.shape, 0) % G_SUB + r0
                ss = lax.broadcasted_iota(I32, att.shape, 1)
                atts.append(jnp.where(ss <= tt, att, 0.0).astype(BF16))
            for n in ns:
                ov = jnp.dot(atts[n], v[n], preferred_element_type=F32)
                o_rows[n].append(jnp.concatenate(
                    [ov[h * G_SUB:(h + 1) * G_SUB, h * G_DV:(h + 1) * G_DV] for h in range(G_HEADS)], axis=1))

        for n in ns:
            on = o[n] + jnp.concatenate(o_rows[n], axis=0)
            on = _head_rms(on, G_DV) * gain_ref[...] * gr_ref[0, pl.ds(rs[n], C), :].astype(F32)
            o_ref[0, pl.ds(rs[n], C), :] = on.astype(BF16)
        return states[-1]

    s_scr[...] = lax.fori_loop(0, n_chunks // n_group, chunk_group, s_scr[...])

    @pl.when(j == pl.num_programs(1) - 1)
    def _():
        sf_ref[0] = s_scr[...]


def _gla(gq, gk, gv, la, gr, gain, s0):
    B, T, _ = gq.shape
    C = CHUNK
    tb = min(ROW_TILE, T)
    assert T % tb == 0 and tb % C == 0 and C % G_SUB == 0 and 2 * C == LANES
    kw = pl.BlockSpec((1, tb, G_KW), lambda b, j: (b, j, 0))
    vw = pl.BlockSpec((1, tb, G_VW), lambda b, j: (b, j, 0))
    st = pl.BlockSpec((1, G_KW, G_DV), lambda b, j: (b, 0, 0))
    return pl.pallas_call(
        functools.partial(_gla_kernel, C=C, n_chunks=tb // C, n_group=8 if (tb // C) % 8 == 0 else 1),
        grid=(B, T // tb),
        in_specs=[kw, kw, vw, kw, vw, pl.BlockSpec(gain.shape, lambda b, j: (0, 0)), st],
        out_specs=[vw, st],
        out_shape=[jax.ShapeDtypeStruct((B, T, G_VW), BF16), jax.ShapeDtypeStruct((B, G_KW, G_DV), F32)],
        scratch_shapes=[pltpu.VMEM((G_KW, G_DV), F32)],
        compiler_params=_params(("arbitrary", "arbitrary")),
        name="gla",
    )(gq, gk, gv, la, gr, gain, s0)


def _memkv_kernel(mem_ref, gm_ref, w_ref, gk_ref, k_ref, v_ref):
    x = mem_ref[0]
    h = (x * lax.rsqrt(jnp.mean(x * x, -1, keepdims=True) + EPS) * gm_ref[...]).astype(BF16)
    k = jnp.dot(h, w_ref[:, :C_WIDTH], preferred_element_type=F32)
    k_ref[0] = _head_rms(k, C_HEAD_DIM) * gk_ref[...]
    v_ref[0] = jnp.dot(h, w_ref[:, C_WIDTH:], preferred_element_type=F32)


def _memkv(mem, gm, w_kv, gk):
    B, M, _ = mem.shape
    out = pl.BlockSpec((1, M, C_WIDTH), lambda b: (b, 0, 0))
    return pl.pallas_call(
        _memkv_kernel,
        grid=(B,),
        in_specs=[pl.BlockSpec((1, M, D_MODEL), lambda b: (b, 0, 0)),
                  pl.BlockSpec(gm.shape, lambda b: (0, 0)),
                  pl.BlockSpec(w_kv.shape, lambda b: (0, 0)),
                  pl.BlockSpec(gk.shape, lambda b: (0, 0))],
        out_specs=[out, out],
        out_shape=[jax.ShapeDtypeStruct((B, M, C_WIDTH), F32)] * 2,
        compiler_params=_params(("arbitrary",)),
        name="memkv",
    )(mem, gm, w_kv, gk)


def _merge_kernel(x_ref, ya_ref, yb_ref, cq_ref, gt_ref, mk_ref, mv_ref, wa_ref, wb_ref, wc_ref,
                  wo_ref, nf_ref, wr_ref, br_ref,
                  x1_ref, hp_ref, ri_ref, rw_ref, cnt_ref, run_scr, *, tok_base):
    first = jnp.logical_and(pl.program_id(0) == 0, pl.program_id(1) == 0)

    @pl.when(first)
    def _():
        run_scr[...] = jnp.zeros_like(run_scr)

    tm = x_ref.shape[1]
    n_part = 2 if tm % 256 == 0 else 1
    parts = [slice(n * (tm // n_part), (n + 1) * (tm // n_part)) for n in range(n_part)]
    heads = [slice(h * C_HEAD_DIM, (h + 1) * C_HEAD_DIM) for h in range(C_HEADS)]
    mk = [mk_ref[0, :, c].astype(BF16) for c in heads]
    mv = [mv_ref[0, :, c].astype(BF16) for c in heads]
    scores = [[lax.dot_general(cq_ref[0, r, c], k, (((1,), (1,)), ((), ())), preferred_element_type=F32)
               * (C_HEAD_DIM ** -0.5) for c, k in zip(heads, mk)] for r in parts]
    probs = [[jnp.exp(s - s.max(-1, keepdims=True)) for s in ss] for ss in scores]
    sums = [[p.sum(-1, keepdims=True) for p in ps] for ps in probs]
    yc_in = [jnp.concatenate([jnp.dot(p.astype(BF16), v, preferred_element_type=F32) / l
                              for p, l, v in zip(ps, ls, mv)], axis=-1).astype(BF16)
             for ps, ls in zip(probs, sums)]
    y_a = [jnp.dot(ya_ref[0, r, :], wa_ref[...], preferred_element_type=F32) for r in parts]
    y_b = [jnp.dot(yb_ref[0, r, :], wb_ref[...], preferred_element_type=F32) for r in parts]
    y_c = [jnp.dot(y, wc_ref[...], preferred_element_type=F32) for y in yc_in]
    merged = [(gt_ref[0, r, 0:D_MODEL].astype(F32) * a
               + gt_ref[0, r, D_MODEL:2 * D_MODEL].astype(F32) * b
               + gt_ref[0, r, 2 * D_MODEL:3 * D_MODEL].astype(F32) * c).astype(BF16)
              for r, a, b, c in zip(parts, y_a, y_b, y_c)]
    x1s = [x_ref[0, r, :] + jnp.dot(m, wo_ref[...], preferred_element_type=F32) for r, m in zip(parts, merged)]
    for r, v in zip(parts, x1s):
        x1_ref[0, r, :] = v
    h2s = [v * lax.rsqrt(jnp.mean(v * v, -1, keepdims=True) + EPS) * nf_ref[...] for v in x1s]

    logits = jnp.concatenate([jnp.dot(h.astype(BF16), wr_ref[...], preferred_element_type=F32) for h in h2s],
                             axis=0) + br_ref[...]
    h2 = jnp.concatenate(h2s, axis=0)
    lane = lax.broadcasted_iota(I32, (tm, LANES), 1)
    lane_f = lane.astype(F32)
    vals, sels, idxs = [], [], []
    l = logits
    for _ in range(TOP_K):
        m = l.max(-1, keepdims=True)
        idx = jnp.min(jnp.where(l == m, lane_f, float(LANES)), -1, keepdims=True)
        sel = lane_f == idx
        vals.append(m)
        idxs.append(idx)
        sels.append(sel)
        l = jnp.where(sel, -3e38, l)
    es = [jnp.exp(vk - vals[0]) for vk in vals]
    den = es[0] + es[1] + es[2] + es[3]
    cnt = jnp.zeros((tm, LANES), F32)
    for sel in sels:
        cnt = cnt + jnp.where(sel, 1.0, 0.0)
    tp = max(tm, LANES)
    cnt_p = cnt if tp == tm else jnp.concatenate([cnt, jnp.zeros((tp - tm, LANES), F32)], axis=0)
    ri = lax.broadcasted_iota(I32, (tp, tp), 0)
    ci = lax.broadcasted_iota(I32, (tp, tp), 1)
    before = jnp.dot(jnp.where(ci < ri, 1.0, 0.0).astype(BF16), cnt_p.astype(BF16),
                     preferred_element_type=F32)[0:tm] + run_scr[0:1, :]
    r_i = jnp.zeros((tm, LANES), I32)
    r_w = jnp.zeros((tm, LANES), F32)
    for kk in range(TOP_K):
        rank = jnp.sum(jnp.where(sels[kk], before, 0.0), -1, keepdims=True)
        r_i = jnp.where(lane == kk, idxs[kk].astype(I32), r_i)
        r_i = jnp.where(lane == TOP_K + kk, rank.astype(I32), r_i)
        r_w = jnp.where(lane == kk, es[kk] / den, r_w)
    ri_ref[0] = r_i
    rw_ref[0] = r_w
    run_scr[...] = run_scr[...] + jnp.sum(cnt, axis=0, keepdims=True)
    cnt_ref[...] = run_scr[...]

    tok = tok_base + (pl.program_id(0) * pl.num_programs(1) + pl.program_id(1)) * tm \
        + lax.broadcasted_iota(I32, (tm, LANES), 0)
    meta = jnp.where(lane == 0, tok, 0)
    for kk in range(TOP_K):
        meta = jnp.where(lane == 1 + kk, idxs[kk].astype(I32), meta)
    packed = _pack_rows(h2)
    zero = jnp.zeros((tm, LANES), U32)
    slabs = [packed[:, j * LANES:(j + 1) * LANES] for j in range(SLABS)]
    slabs += [lax.bitcast_convert_type(meta, U32)] + [zero] * (ROW_SLABS - SLABS - 1)
    hp_ref[0] = pltpu.einshape("jtl->tjl", jnp.stack(slabs, axis=0))


def _merge(x, ya, yb, cq, gt, mk, mv, wa, wb, wc, wo, nf, wr, br, tok_base):
    B, T, _ = x.shape
    tm = min(ROW_TILE, T)
    assert T % tm == 0

    def row(width):
        return pl.BlockSpec((1, tm, width), lambda b, j: (b, j, 0))

    def full(a):
        return pl.BlockSpec(a.shape, lambda b, j: (0,) * a.ndim)

    mem = pl.BlockSpec((1, N_MEM, C_WIDTH), lambda b, j: (b, 0, 0))
    return pl.pallas_call(
        functools.partial(_merge_kernel, tok_base=tok_base),
        grid=(B, T // tm),
        in_specs=[row(D_MODEL), row(A_WIDTH), row(G_VW), row(C_WIDTH), row(N_BRANCH * D_MODEL), mem, mem,
                  full(wa), full(wb), full(wc), full(wo), full(nf), full(wr), full(br)],
        out_specs=[row(D_MODEL), pl.BlockSpec((1, tm, ROW_SLABS, LANES), lambda b, j: (b, j, 0, 0)), row(LANES),
                   row(LANES), pl.BlockSpec((SUBLANES, LANES), lambda b, j: (0, 0))],
        out_shape=[jax.ShapeDtypeStruct((B, T, D_MODEL), F32), jax.ShapeDtypeStruct((B, T, ROW_SLABS, LANES), U32),
                   jax.ShapeDtypeStruct((B, T, LANES), I32), jax.ShapeDtypeStruct((B, T, LANES), F32),
                   jax.ShapeDtypeStruct((SUBLANES, LANES), F32)],
        scratch_shapes=[pltpu.VMEM((SUBLANES, LANES), F32)],
        compiler_params=_params(("arbitrary", "arbitrary")),
        name="merge",
    )(x, ya, yb, cq, gt, mk, mv, wa, wb, wc, wo, nf, wr, br)


def _dispatch_kernel(dest_ref, dest2_ref, pad_lo_ref, pad_n_ref, hp_ref, hp2_ref, xs_ref, filler_scr, sem, sem2,
                     psem, *, tm, tm2, n_pad, n_tokens):
    def scatter(dref, href, s, n):
        def issue(t, c):
            for kk in range(TOP_K):
                pltpu.make_async_copy(href.at[t], xs_ref.at[dref[0, 0, t * TOP_K + kk]], s).start(priority=kk % 2)
            return c
        lax.fori_loop(0, n, issue, 0, unroll=8)

    def drain(href, s, n):
        for _ in range(TOP_K):
            pltpu.make_async_copy(href, xs_ref.at[pl.ds(0, n)], s).wait()

    scatter(dest_ref, hp_ref, sem, tm)

    @pl.when(pl.program_id(0) == 0)
    def _():
        scatter(dest2_ref, hp2_ref, sem2, tm2)
        shape = (FFN_BLOCK, ROW_SLABS, LANES)
        sub = lax.broadcasted_iota(I32, shape, 1)
        lane = lax.broadcasted_iota(I32, shape, 2)
        filler = jnp.where((sub == INFO_SLAB) & (lane == 0), n_tokens,
                           jnp.where((sub == INFO_SLAB) & (lane <= TOP_K), N_EXPERTS, 0))
        filler_scr[...] = lax.bitcast_convert_type(filler, U32)

        log_blk = FFN_BLOCK.bit_length() - 1

        def span(e, wait):
            lo, n = pad_lo_ref[e], pad_n_ref[e]
            whole = n >> log_blk

            def go(cp):
                cp.wait() if wait else cp.start()

            def block(j, c):
                go(pltpu.make_async_copy(filler_scr, xs_ref.at[pl.ds(lo + j * FFN_BLOCK, FFN_BLOCK)], psem))
                return c

            lax.fori_loop(0, whole, block, 0)
            rem = n - (whole << log_blk)
            for b in reversed(range(log_blk)):
                off = lo + (whole << log_blk) + ((rem >> (b + 1)) << (b + 1))

                @pl.when((rem >> b) & 1 == 1)
                def _():
                    go(pltpu.make_async_copy(filler_scr.at[pl.ds(0, 1 << b)], xs_ref.at[pl.ds(off, 1 << b)], psem))

        def fill(e, c):
            span(e, False)
            return c

        def fill_wait(e, c):
            span(e, True)
            return c

        lax.fori_loop(0, n_pad, fill, 0)
        lax.fori_loop(0, n_pad, fill_wait, 0)
        drain(hp2_ref, sem2, tm2)

    drain(hp_ref, sem, tm)


def _dispatch(dest, hp, dest2, hp2, pad_lo, pad_n, rows):
    N, N2 = hp.shape[0], hp2.shape[0]
    tm = min(ROW_TILE, N)
    assert N % tm == 0
    n_steps = N // tm
    smem = functools.partial(pl.BlockSpec, memory_space=pltpu.SMEM)
    n_pad = pad_lo.shape[0]
    return pl.pallas_call(
        functools.partial(_dispatch_kernel, tm=tm, tm2=N2, n_pad=n_pad, n_tokens=N + N2),
        grid=(n_steps,),
        in_specs=[smem((1, 1, tm * TOP_K), lambda i: (i, 0, 0)),
                  smem((1, 1, N2 * TOP_K), lambda i: (0, 0, 0)),
                  smem((n_pad,), lambda i: (0,)), smem((n_pad,), lambda i: (0,)),
                  pl.BlockSpec((tm, ROW_SLABS, LANES), lambda i: (i, 0, 0)),
                  pl.BlockSpec((N2, ROW_SLABS, LANES), lambda i: (0, 0, 0))],
        out_specs=pl.BlockSpec(memory_space=pl.ANY),
        out_shape=jax.ShapeDtypeStruct((rows, ROW_SLABS, LANES), U32),
        scratch_shapes=[pltpu.VMEM((FFN_BLOCK, ROW_SLABS, LANES), U32), pltpu.SemaphoreType.DMA(()),
                        pltpu.SemaphoreType.DMA(()), pltpu.SemaphoreType.DMA(())],
        compiler_params=_params(("arbitrary",)),
        name="moe_dispatch",
    )(dest.reshape(n_steps, 1, tm * TOP_K), dest2.reshape(1, 1, N2 * TOP_K), pad_lo, pad_n, hp, hp2)


def _ffn_kernel(be_ref, nu_ref, x_ref, wgu_ref, bgu_ref, wd_ref, bd_ref, y4_ref,
                wgu_bf, wd_bf, ybuf, idv, ids_smem, sem_ids, sem_rows, *, n_blk, n_tokens):
    i = pl.program_id(0)
    n_used = nu_ref[0]
    n_assign = n_tokens * TOP_K
    prev = be_ref[jnp.maximum(i - 1, 0)]
    new_expert = jnp.logical_or(i == 0, be_ref[i] != prev)

    def rows_done(s):
        pltpu.make_async_copy(ybuf.at[s], y4_ref.at[pl.ds(0, FFN_BLOCK)], sem_rows.at[s]).wait()

    def ids_copy(s):
        return pltpu.make_async_copy(idv, ids_smem.at[s], sem_ids)

    def start_rows(s, lo, hi):
        for r in range(lo, hi):
            pltpu.make_async_copy(ybuf.at[s, r], y4_ref.at[ids_smem[s, 0, r]],
                                  sem_rows.at[s]).start(priority=r % 2)

    @pl.when(i == 0)
    def _():
        ybuf[1] = jnp.zeros(ybuf.shape[1:], U32)
        spare = pltpu.make_async_copy(ybuf.at[1], y4_ref.at[pl.ds(n_assign, FFN_BLOCK)], sem_rows.at[1])
        spare.start()
        spare.wait()

    @pl.when(new_expert)
    def _():
        wgu_bf[...] = wgu_ref[0].astype(BF16)
        wd_bf[...] = wd_ref[0].astype(BF16)

    def compute(s, flush):
        q = FFN_BLOCK // 4
        if flush:
            ids_copy(1 - s).wait()
            start_rows(1 - s, 0, q)
        xm = pltpu.einshape("tjl->jtl", x_ref[...])
        x_lo, x_hi = _unpack_rows(jnp.concatenate([xm[j] for j in range(SLABS)], axis=-1))
        meta = lax.bitcast_convert_type(xm[INFO_SLAB], I32)
        tok = meta[:, 0:1]
        choice = jnp.zeros_like(tok)
        for kk in range(1, TOP_K):
            choice = jnp.where(meta[:, 1 + kk:2 + kk] == be_ref[i], kk, choice)
        local = lax.broadcasted_iota(I32, tok.shape, 0)
        dest = jnp.where(tok >= n_tokens, n_assign + local, tok * TOP_K + choice)
        dest_t = jnp.broadcast_to(dest.astype(F32), (FFN_BLOCK, LANES)).T
        idv[...] = dest_t[0:SUBLANES].astype(I32)
        ids_copy(s).start()
        gu = (jnp.dot(x_lo.astype(BF16), wgu_bf[0:HALF, :], preferred_element_type=F32)
              + jnp.dot(x_hi.astype(BF16), wgu_bf[HALF:, :], preferred_element_type=F32)
              + bgu_ref[0])
        if flush:
            start_rows(1 - s, q, 2 * q)
        gate = jnp.minimum(gu[:, :D_EXPERT], SWIGLU_LIMIT)
        up = jnp.clip(gu[:, D_EXPERT:], -SWIGLU_LIMIT, SWIGLU_LIMIT)
        act = (up + 1.0) * (gate * _sigmoid(SWIGLU_ALPHA * gate))
        if flush:
            start_rows(1 - s, 2 * q, 3 * q)
        y = jnp.dot(act.astype(BF16), wd_bf[...], preferred_element_type=F32) + bd_ref[0]
        if flush:
            start_rows(1 - s, 3 * q, 4 * q)
        packed = _pack_rows(y)
        pl.when(i >= 2)(functools.partial(rows_done, s))
        _store_slabs(ybuf, (s,), packed)

    def flush_only(s, also):
        ids_copy(s).wait()
        start_rows(s, 0, FFN_BLOCK)
        pl.when(also)(functools.partial(rows_done, 1 - s))
        rows_done(s)

    real = i < n_used
    prev_real = jnp.logical_and(i >= 1, i - 1 < n_used)
    for s in range(2):
        mine = i % 2 == s
        pl.when(mine & real & prev_real)(functools.partial(compute, s, True))
        pl.when(mine & real & jnp.logical_not(prev_real))(functools.partial(compute, s, False))
        pl.when(mine & jnp.logical_not(real) & prev_real)(functools.partial(flush_only, 1 - s, i >= 2))
        pl.when(mine & real & (i == n_blk - 1))(functools.partial(flush_only, s, i >= 1))


def _ffn(blk_expert, n_used, xs, n_tokens, w_gu, b_gu, w_d, b_d):
    P = xs.shape[0]
    nblk = P // FFN_BLOCK
    grid_spec = pltpu.PrefetchScalarGridSpec(
        num_scalar_prefetch=2,
        grid=(nblk,),
        in_specs=[pl.BlockSpec((FFN_BLOCK, ROW_SLABS, LANES), lambda i, be, nu: (jnp.minimum(i, nu[0] - 1), 0, 0)),
                  pl.BlockSpec((1, D_MODEL, 2 * D_EXPERT), lambda i, be, nu: (be[i], 0, 0)),
                  pl.BlockSpec((1, 1, 2 * D_EXPERT), lambda i, be, nu: (be[i], 0, 0)),
                  pl.BlockSpec((1, D_EXPERT, D_MODEL), lambda i, be, nu: (be[i], 0, 0)),
                  pl.BlockSpec((1, 1, D_MODEL), lambda i, be, nu: (be[i], 0, 0))],
        out_specs=pl.BlockSpec(memory_space=pl.ANY),
        scratch_shapes=[pltpu.VMEM((D_MODEL, 2 * D_EXPERT), BF16), pltpu.VMEM((D_EXPERT, D_MODEL), BF16),
                        pltpu.VMEM((2, FFN_BLOCK, SLABS, LANES), U32), pltpu.VMEM((SUBLANES, FFN_BLOCK), I32),
                        pltpu.SMEM((2, SUBLANES, FFN_BLOCK), I32), pltpu.SemaphoreType.DMA(()),
                        pltpu.SemaphoreType.DMA((2,))],
    )
    return pl.pallas_call(
        functools.partial(_ffn_kernel, n_blk=nblk, n_tokens=n_tokens),
        grid_spec=grid_spec,
        out_shape=jax.ShapeDtypeStruct((n_tokens * TOP_K + FFN_BLOCK, SLABS, LANES), U32),
        compiler_params=_params(("arbitrary",)),
        name="moe_ffn",
    )(blk_expert, n_used, xs, w_gu, b_gu.reshape(N_EXPERTS, 1, -1), w_d, b_d.reshape(N_EXPERTS, 1, -1))


def _combine_kernel(x1_ref, rw_ref, y_ref, o_ref, *, tm):
    w = rw_ref[...]
    acc_lo = x1_ref[:, :HALF]
    acc_hi = x1_ref[:, HALF:]
    for kk in range(TOP_K):
        rows = y_ref[pl.ds(kk, tm, stride=TOP_K)]
        slab_major = pltpu.einshape("tjl->jtl", rows)
        lo, hi = _unpack_rows(jnp.concatenate([slab_major[j] for j in range(SLABS)], axis=-1))
        wk = w[:, kk:kk + 1]
        acc_lo = acc_lo + wk * lo
        acc_hi = acc_hi + wk * hi
    o_ref[:, :HALF] = acc_lo
    o_ref[:, HALF:] = acc_hi


def _combine(x1, rw, y4, tok_base):
    N = x1.shape[0]
    tm = min(256, N)
    assert N % tm == 0 and tok_base % tm == 0
    first = tok_base // tm
    return pl.pallas_call(
        functools.partial(_combine_kernel, tm=tm),
        grid=(N // tm,),
        in_specs=[pl.BlockSpec((tm, D_MODEL), lambda i: (i, 0)),
                  pl.BlockSpec((tm, LANES), lambda i: (i, 0)),
                  pl.BlockSpec((tm * TOP_K, SLABS, LANES), lambda i: (first + i, 0, 0))],
        out_specs=pl.BlockSpec((tm, D_MODEL), lambda i: (i, 0)),
        out_shape=jax.ShapeDtypeStruct((N, D_MODEL), F32),
        compiler_params=_params(("arbitrary",)),
        name="moe_combine",
    )(x1, rw, y4)


def _moe(groups, w_gu, b_gu, w_d, b_d):
    n_tokens = sum(g[0].shape[0] for g in groups)
    n_assign = n_tokens * TOP_K
    nblk = (n_assign + N_EXPERTS * (FFN_BLOCK - 1) + FFN_BLOCK - 1) // FFN_BLOCK
    counts = [g[4].astype(I32) for g in groups]
    total = sum(counts)
    padded = (total + FFN_BLOCK - 1) // FFN_BLOCK * FFN_BLOCK
    pend = jnp.cumsum(padded)
    pstart = pend - padded
    n_used = pend[-1:] // FFN_BLOCK
    blk = jnp.minimum(jnp.arange(nblk, dtype=I32), n_used[0] - 1) * FFN_BLOCK
    blk_expert = jnp.minimum(jnp.sum(pend[None, :] <= blk[:, None], axis=1), N_EXPERTS - 1).astype(I32)

    experts = jnp.arange(N_EXPERTS, dtype=I32)
    dests = []
    base = pstart
    for g, c in zip(groups, counts):
        idx, rank = g[2][:, :TOP_K], g[2][:, TOP_K:2 * TOP_K]
        dests.append(jnp.sum(jnp.where(idx[..., None] == experts, base, 0), axis=-1) + rank)
        base = base + c

    rows = nblk * FFN_BLOCK
    pad_lo = jnp.concatenate([pstart + total, pend[-1:]])
    pad_n = jnp.concatenate([padded - total, rows - pend[-1:]])
    (g_main, g_small), (d_main, d_small) = groups, dests
    xs = _dispatch(d_main, g_main[1], d_small, g_small[1], pad_lo, pad_n, rows)
    y4 = _ffn(blk_expert, n_used.astype(I32), xs, n_tokens, w_gu, b_gu, w_d, b_d)
    return [_combine(g_main[0], g_main[3], y4, 0), _combine(g_small[0], g_small[3], y4, g_main[0].shape[0])]


def _tile_lanes(g, reps):
    return jnp.tile(g.astype(F32), reps)[None, :]


def kernel(x_prompt, x_sample, mem_prompt, cache_attn_k, cache_attn_v, state_gla, cache_mem_k, cache_mem_v, norm_mix, w_in, a_q_norm, a_k_norm, rel_bias_table, w_a_o, w_gla_a_up, b_gla_a, gla_out_norm, w_b_o, c_q_norm, c_k_norm, norm_mem, w_mem_kv, w_c_o, b_gate, w_out, norm_ffn, w_router, b_router, w_gate_up, b_gate_up, w_down, b_down):
    depth = norm_mix.shape[0]
    assert depth == 1
    l = 0
    B, S, _ = x_prompt.shape
    Bs, Ts, _ = x_sample.shape
    keep = min(WINDOW, S)

    w = w_in[l]
    sizes = (A_WIDTH, A_WIDTH, A_WIDTH, G_KW, G_KW, G_VW, G_VW, G_RANK, C_WIDTH, N_BRANCH * D_MODEL)
    offs = [0]
    for s_ in sizes:
        offs.append(offs[-1] + s_)
    seg = [w[:, offs[i]:offs[i + 1]] for i in range(len(sizes))]
    w_r = jnp.concatenate(seg[0:7] + [seg[8], seg[9], seg[7], jnp.zeros((D_MODEL, LANES - G_RANK), F32)],
                          axis=1).astype(BF16)
    nm = norm_mix[l][None, :]
    aqn = _tile_lanes(a_q_norm[l], A_HEADS)
    akn = _tile_lanes(a_k_norm[l], A_HEADS)
    cqn = _tile_lanes(c_q_norm[l], C_HEADS)
    ckn = _tile_lanes(c_k_norm[l], C_HEADS)
    gon = _tile_lanes(gla_out_norm[l], G_HEADS)
    wup = jnp.concatenate([w_gla_a_up[l], jnp.zeros((LANES - G_RANK, G_KW), F32)], axis=0).astype(BF16)
    bla = b_gla_a[l][None, :]
    bg = b_gate[l][None, :]
    wa, wb, wc, wo = (t[l].astype(BF16) for t in (w_a_o, w_b_o, w_c_o, w_out))
    nf = norm_ffn[l][None, :]
    wr = jnp.concatenate([w_router[l], jnp.zeros((D_MODEL, LANES - N_EXPERTS), F32)], axis=1).astype(BF16)
    br = jnp.concatenate([b_router[l], jnp.full((LANES - N_EXPERTS,), NEG_INF, F32)])[None, :]
    table = rel_bias_table[l]

    mk, mv = _memkv(mem_prompt, norm_mem[l][None, :], w_mem_kv[l].astype(BF16), ckn)
    (aq, ak, av, gq, gk, gv, gr, la, cq, gt, ak_tail, av_tail) = _inproj(
        x_prompt, keep, nm, w_r, aqn, akn, cqn, wup, bla, bg)
    ya = _attn_prompt(aq, ak, av, table)
    yb, s_prompt = _gla(gq, gk, gv, la, gr, gon, jnp.zeros((B, G_KW, G_DV), F32))
    x1_p, hp_p, ri_p, rw_p, cnt_p = _merge(x_prompt, ya, yb, cq, gt, mk, mv, wa, wb, wc, wo, nf, wr, br, 0)

    (aq, ak, av, gq, gk, gv, gr, la, cq, gt, ak_new, av_new) = _inproj(
        x_sample.reshape(1, Bs * Ts, D_MODEL), Bs * Ts, nm, w_r, aqn, akn, cqn, wup, bla, bg)
    rs = lambda t: t.reshape(Bs, Ts, t.shape[-1])
    P = cache_attn_k.shape[2]
    ya = _attn_sample(rs(aq), rs(ak), rs(av), cache_attn_k[l].reshape(Bs, P, A_WIDTH),
                      cache_attn_v[l].reshape(Bs, P, A_WIDTH), table)
    t_pad = (Ts + CHUNK - 1) // CHUNK * CHUNK
    zp = lambda t: jnp.pad(rs(t), ((0, 0), (0, t_pad - Ts), (0, 0)))
    yb, s_sample = _gla(zp(gq), zp(gk), zp(gv), zp(la), zp(gr), gon, state_gla[l].reshape(Bs, G_KW, G_DV))
    yb = yb[:, :Ts]
    x1_s, hp_s, ri_s, rw_s, cnt_s = _merge(
        x_sample, ya, yb, rs(cq), rs(gt), cache_mem_k[l].reshape(Bs, N_MEM, C_WIDTH),
        cache_mem_v[l].reshape(Bs, N_MEM, C_WIDTH), wa, wb, wc, wo, nf, wr, br, B * S)

    flat = lambda t: t.reshape((-1,) + t.shape[2:])
    y_p, y_s = _moe(
        [(flat(x1_p), flat(hp_p), flat(ri_p), flat(rw_p), cnt_p[0, :N_EXPERTS]),
         (flat(x1_s), flat(hp_s), flat(ri_s), flat(rw_s), cnt_s[0, :N_EXPERTS])],
        w_gate_up[l], b_gate_up[l], w_down[l], b_down[l])

    return (y_p.reshape(B, S, D_MODEL), y_s.reshape(Bs, Ts, D_MODEL),
            ak_tail.reshape(1, B, keep, A_HEADS, A_HEAD_DIM), av_tail.reshape(1, B, keep, A_HEADS, A_HEAD_DIM),
            s_prompt.reshape(1, B, G_HEADS, G_DK, G_DV),
            mk.reshape(1, B, N_MEM, C_HEADS, C_HEAD_DIM), mv.reshape(1, B, N_MEM, C_HEADS, C_HEAD_DIM),
            ak_new.reshape(1, Bs, Ts, A_HEADS, A_HEAD_DIM), av_new.reshape(1, Bs, Ts, A_HEADS, A_HEAD_DIM),
            s_sample.reshape(1, Bs, G_HEADS, G_DK, G_DV))
```

```python
import functools

import jax
import jax.numpy as jnp
from jax import lax
from jax.experimental import pallas as pl
from jax.experimental.pallas import tpu as pltpu

F32 = jnp.float32
BF16 = jnp.bfloat16
U32 = jnp.uint32
I32 = jnp.int32

D_MODEL = 1024
CHUNK = 64
BAND_CHUNKS = 8
WINDOW = BAND_CHUNKS * CHUNK
N_MEM = 256
A_HEADS, A_HEAD_DIM = 8, 64
A_WIDTH = A_HEADS * A_HEAD_DIM
REL_MAX = 128
G_HEADS, G_DK, G_DV = 4, 64, 128
G_KW, G_VW = G_HEADS * G_DK, G_HEADS * G_DV
G_RANK = 16
G_TAU = 16.0
G_SUB = 16
C_HEADS, C_HEAD_DIM = 4, 128
C_WIDTH = C_HEADS * C_HEAD_DIM
N_BRANCH = 3
N_EXPERTS = 32
TOP_K = 4
D_EXPERT = 1024
SWIGLU_LIMIT = 7.0
SWIGLU_ALPHA = 1.702
EPS = 1e-6
NEG_INF = -1e30

LANES = 128
SUBLANES = 8
HALF = D_MODEL // 2
ROW_TILE = 512
ATTN_SUB = 128
FFN_BLOCK = 512
VMEM_LIMIT = 56 * 1024 * 1024

OFF_AQ, OFF_AK, OFF_AV = 0, 512, 1024
OFF_GQ, OFF_GK, OFF_GV, OFF_GR = 1536, 1792, 2048, 2560
OFF_CQ, OFF_GATE, OFF_LR = 3072, 3584, 6656
IN_COLS = OFF_LR + LANES


def _params(sem):
    return pltpu.CompilerParams(dimension_semantics=sem, vmem_limit_bytes=VMEM_LIMIT)


def _sigmoid(x):
    return 0.5 * jnp.tanh(0.5 * x) + 0.5


def _head_rms(y, head_dim):
    cols = []
    for p in range(y.shape[1] // LANES):
        blk = y[:, p * LANES:(p + 1) * LANES]
        sq = blk * blk
        if head_dim == LANES:
            sc = lax.rsqrt(jnp.sum(sq, -1, keepdims=True) * (1.0 / LANES) + EPS)
        else:
            lo = lax.broadcasted_iota(I32, blk.shape, 1) < head_dim
            s_lo = jnp.sum(jnp.where(lo, sq, 0.0), -1, keepdims=True)
            s_hi = jnp.sum(jnp.where(lo, 0.0, sq), -1, keepdims=True)
            sc = jnp.where(lo, lax.rsqrt(s_lo * (1.0 / head_dim) + EPS),
                           lax.rsqrt(s_hi * (1.0 / head_dim) + EPS))
        cols.append(blk * sc)
    return jnp.concatenate(cols, axis=-1)


def _split_bf16(x):
    hi = x.astype(BF16)
    lo = (x - hi.astype(F32)).astype(BF16)
    return hi, lo


def _pack_rows(x):
    lo = lax.bitcast_convert_type(x[:, :HALF].astype(BF16).astype(F32), U32)
    hi = lax.bitcast_convert_type(x[:, HALF:].astype(BF16).astype(F32), U32)
    return (lo >> 16) | (hi & jnp.uint32(0xFFFF0000))


def _unpack_rows(u):
    lo = lax.bitcast_convert_type(u << 16, F32)
    hi = lax.bitcast_convert_type(u & jnp.uint32(0xFFFF0000), F32)
    return lo, hi


SLABS = HALF // LANES
ROW_SLABS = SUBLANES
INFO_SLAB = SLABS


def _store_slabs(ref, lead, u):
    slab_major = jnp.stack([u[:, j * LANES:(j + 1) * LANES] for j in range(SLABS)], axis=0)
    ref[lead + (slice(None),) * 3] = pltpu.einshape("jtl->tjl", slab_major)


def _inproj_kernel(x_ref, nm_ref, w_ref, aqn_ref, akn_ref, cqn_ref, wup_ref, bla_ref, bg_ref,
                   aq_ref, ak_ref, av_ref, gq_ref, gk_ref, gv_ref, gr_ref, la_ref, cq_ref, gt_ref,
                   akt_ref, avt_ref, *, n_tiles, n_tail):
    j = pl.program_id(1)
    x = x_ref[0]
    h = (x * lax.rsqrt(jnp.mean(x * x, -1, keepdims=True) + EPS) * nm_ref[...]).astype(BF16)

    def seg(off, width):
        return jnp.dot(h, w_ref[:, off:off + width], preferred_element_type=F32)

    in_tail = j >= n_tiles - n_tail

    lr = seg(OFF_LR, LANES).astype(BF16)

    aq = _head_rms(seg(OFF_AQ, A_WIDTH), A_HEAD_DIM) * aqn_ref[...] * (A_HEAD_DIM ** -0.5)
    aq_ref[0] = aq.astype(BF16)

    ak = _head_rms(seg(OFF_AK, A_WIDTH), A_HEAD_DIM) * akn_ref[...]
    ak_ref[0] = ak.astype(BF16)

    @pl.when(in_tail)
    def _():
        akt_ref[0] = ak

    z = jnp.dot(lr, wup_ref[...], preferred_element_type=F32) + bla_ref[...]
    la_ref[0] = (jnp.minimum(z, 0.0) - jnp.log1p(jnp.exp(-jnp.abs(z)))) * (1.0 / G_TAU)

    gq_ref[0] = (seg(OFF_GQ, G_KW) * (G_DK ** -0.5)).astype(BF16)
    gk_ref[0] = seg(OFF_GK, G_KW).astype(BF16)
    gv_ref[0] = seg(OFF_GV, G_VW).astype(BF16)
    gr = seg(OFF_GR, G_VW)
    gr_ref[0] = (gr * _sigmoid(gr)).astype(BF16)

    cq = _head_rms(seg(OFF_CQ, C_WIDTH), C_HEAD_DIM) * cqn_ref[...]
    cq_ref[0] = cq.astype(BF16)

    gate_chunk = 512
    for c in range(N_BRANCH * D_MODEL // gate_chunk):
        lo = c * gate_chunk
        g = seg(OFF_GATE + lo, gate_chunk) + bg_ref[:, lo:lo + gate_chunk]
        gt_ref[0, :, lo:lo + gate_chunk] = _sigmoid(g).astype(BF16)

    av = seg(OFF_AV, A_WIDTH)
    av_ref[0] = av.astype(BF16)

    @pl.when(in_tail)
    def _():
        avt_ref[0] = av


def _inproj(x, keep, nm, w_r, aqn, akn, cqn, wup, bla, bg):
    G, R, _ = x.shape
    tm = min(ROW_TILE, R)
    n_tiles = R // tm
    n_tail = keep // tm
    assert R % tm == 0 and keep % tm == 0 and n_tail >= 1

    def row(width, dtype):
        return (jax.ShapeDtypeStruct((G, R, width), dtype),
                pl.BlockSpec((1, tm, width), lambda g, j: (g, j, 0)))

    def tail(width):
        return (jax.ShapeDtypeStruct((G, keep, width), F32),
                pl.BlockSpec((1, tm, width), lambda g, j: (g, jnp.maximum(j - (n_tiles - n_tail), 0), 0)))

    outs = [row(A_WIDTH, BF16), row(A_WIDTH, BF16), row(A_WIDTH, BF16), row(G_KW, BF16),
            row(G_KW, BF16), row(G_VW, BF16), row(G_VW, BF16), row(G_KW, F32), row(C_WIDTH, BF16),
            row(N_BRANCH * D_MODEL, BF16), tail(A_WIDTH), tail(A_WIDTH)]

    def full(a):
        return pl.BlockSpec(a.shape, lambda g, j: (0,) * a.ndim)

    return pl.pallas_call(
        functools.partial(_inproj_kernel, n_tiles=n_tiles, n_tail=n_tail),
        grid=(G, n_tiles),
        in_specs=[pl.BlockSpec((1, tm, D_MODEL), lambda g, j: (g, j, 0)), full(nm), full(w_r),
                  full(aqn), full(akn), full(cqn), full(wup), full(bla), full(bg)],
        out_specs=[o[1] for o in outs],
        out_shape=[o[0] for o in outs],
        compiler_params=_params(("arbitrary", "arbitrary")),
        name="inproj",
    )(x, nm, w_r, aqn, akn, cqn, wup, bla, bg)


def _attend(pairs, lo_mask):
    T = pairs[0][0].shape[0]
    scores = []
    for q, parts in pairs:
        zero = jnp.zeros_like(q)
        q2 = jnp.concatenate([jnp.where(lo_mask, q, zero), jnp.where(lo_mask, zero, q)], axis=0)
        ss = []
        for (k, _, bias2, valid) in parts:
            s = lax.dot_general(q2, k, (((1,), (1,)), ((), ())), preferred_element_type=F32) + bias2
            if valid is not None:
                s = jnp.where(valid, s, NEG_INF)
            ss.append(s)
        scores.append(ss)
    probs, sums = [], []
    for ss in scores:
        m = ss[0].max(-1, keepdims=True)
        for s in ss[1:]:
            m = jnp.maximum(m, s.max(-1, keepdims=True))
        ps = [jnp.exp(s - m) for s in ss]
        l = ps[0].sum(-1, keepdims=True)
        for p in ps[1:]:
            l = l + p.sum(-1, keepdims=True)
        probs.append([p.astype(BF16) for p in ps])
        sums.append(l)
    outs = []
    for (q, parts), ps, l in zip(pairs, probs, sums):
        o = jnp.dot(ps[0], parts[0][1], preferred_element_type=F32)
        for p, part in zip(ps[1:], parts[1:]):
            o = o + jnp.dot(p, part[1], preferred_element_type=F32)
        o = o / l
        outs.append(jnp.where(lo_mask, o[:T], o[T:]))
    return outs


def _attn_prompt_kernel(q_ref, kp_ref, kc_ref, vp_ref, vc_ref, bias_ref, o_ref, *, tb):
    j = pl.program_id(1)
    lo_mask = lax.broadcasted_iota(I32, (ATTN_SUB, LANES), 1) < A_HEAD_DIM

    def block(with_prev):
        for s in range(tb // ATTN_SUB):
            r0 = s * ATTN_SUB
            len_a = tb - r0
            len_b = r0 + ATTN_SUB
            pairs = []
            for p in range(A_WIDTH // LANES):
                c0 = p * LANES
                q = q_ref[0, r0:r0 + ATTN_SUB, c0:c0 + LANES]
                parts = [(kc_ref[0, 0:len_b, c0:c0 + LANES], vc_ref[0, 0:len_b, c0:c0 + LANES],
                          bias_ref[p, :, len_a:len_a + len_b], None)]
                if with_prev:
                    parts.insert(0, (kp_ref[0, r0:tb, c0:c0 + LANES], vp_ref[0, r0:tb, c0:c0 + LANES],
                                     bias_ref[p, :, 0:len_a], None))
                pairs.append((q, parts))
            outs = _attend(pairs[:2], lo_mask) + _attend(pairs[2:], lo_mask)
            for p, o in enumerate(outs):
                o_ref[0, r0:r0 + ATTN_SUB, p * LANES:(p + 1) * LANES] = o.astype(BF16)

    pl.when(j > 0)(functools.partial(block, True))
    pl.when(j == 0)(functools.partial(block, False))


def _rel_bias(table, n_q, n_k, offset):
    period = n_q + n_k - 1
    m = jnp.arange(period)
    u = table[:, jnp.clip(n_q - 1 + offset - m, -REL_MAX, REL_MAX) + REL_MAX].astype(F32)
    rows = jnp.tile(u, (1, n_q + 1))[:, :n_q * (period + 1)].reshape(-1, n_q, period + 1)[:, :, :n_k]
    return rows[:, ::-1, :]


def _band_bias(table):
    qc = jnp.arange(ATTN_SUB)[:, None] // CHUNK
    kc = jnp.arange(ATTN_SUB + WINDOW)[None, :] // CHUNK
    ok = (kc >= qc) & (kc <= qc + BAND_CHUNKS)
    return jnp.where(ok[None], _rel_bias(table, ATTN_SUB, ATTN_SUB + WINDOW, WINDOW), NEG_INF)


def _attn_prompt(aq, ak, av, table):
    B, S, _ = aq.shape
    tb = WINDOW
    assert S % tb == 0
    bias = _band_bias(table).reshape(A_WIDTH // LANES, 2 * ATTN_SUB, ATTN_SUB + WINDOW)
    cur = pl.BlockSpec((1, tb, A_WIDTH), lambda b, j: (b, j, 0))
    prev = pl.BlockSpec((1, tb, A_WIDTH), lambda b, j: (b, jnp.maximum(j - 1, 0), 0))
    return pl.pallas_call(
        functools.partial(_attn_prompt_kernel, tb=tb),
        grid=(B, S // tb),
        in_specs=[cur, prev, cur, prev, cur, pl.BlockSpec(bias.shape, lambda b, j: (0, 0, 0))],
        out_specs=cur,
        out_shape=jax.ShapeDtypeStruct((B, S, A_WIDTH), BF16),
        compiler_params=_params(("arbitrary", "arbitrary")),
        name="attn_prompt",
    )(aq, ak, ak, av, av, bias)


def _attn_sample_kernel(q_ref, k_ref, v_ref, bias_ref, o_ref):
    T = q_ref.shape[1]
    lo_mask = lax.broadcasted_iota(I32, (T, LANES), 1) < A_HEAD_DIM
    pairs = []
    for p in range(A_WIDTH // LANES):
        c0 = p * LANES
        parts = [(k_ref[0, :, c0:c0 + LANES], v_ref[0, :, c0:c0 + LANES], bias_ref[p], None)]
        pairs.append((q_ref[0, :, c0:c0 + LANES], parts))
    for p, o in enumerate(_attend(pairs, lo_mask)):
        o_ref[0, :, p * LANES:(p + 1) * LANES] = o.astype(BF16)


def _attn_sample(aq, ak, av, cache_k, cache_v, table):
    B, T, _ = aq.shape
    P = cache_k.shape[1]
    L = (P + T + LANES - 1) // LANES * LANES
    pad = jnp.zeros((B, L - P - T, A_WIDTH), BF16)
    kk = jnp.concatenate([cache_k.astype(BF16), ak, pad], axis=1)
    vv = jnp.concatenate([cache_v.astype(BF16), av, pad], axis=1)
    bias = jnp.where((jnp.arange(L) < P + T)[None, None, :], _rel_bias(table, T, L, P), NEG_INF)
    bias = bias.reshape(A_WIDTH // LANES, 2 * T, L)
    new = pl.BlockSpec((1, T, A_WIDTH), lambda b: (b, 0, 0))
    old = pl.BlockSpec((1, L, A_WIDTH), lambda b: (b, 0, 0))
    return pl.pallas_call(
        _attn_sample_kernel,
        grid=(B,),
        in_specs=[new, old, old, pl.BlockSpec(bias.shape, lambda b: (0, 0, 0))],
        out_specs=new,
        out_shape=jax.ShapeDtypeStruct((B, T, A_WIDTH), BF16),
        compiler_params=_params(("arbitrary",)),
        name="attn_sample",
    )(aq, kk, vv, bias)


def _gla_kernel(q_ref, k_ref, v_ref, la_ref, gr_ref, gain_ref, s0_ref, o_ref, sf_ref, s_scr, *, C, n_chunks,
                n_group):
    j = pl.program_id(1)

    @pl.when(j == 0)
    def _():
        s_scr[...] = s0_ref[0]

    n_sub = C // G_SUB
    ri = lax.broadcasted_iota(I32, (C, C), 0)
    ci = lax.broadcasted_iota(I32, (C, C), 1)
    tril = (ci <= ri).astype(BF16)
    tril2 = jnp.concatenate([tril, tril], axis=1)
    lane_kw = lax.broadcasted_iota(I32, (1, G_KW), 1)
    head_of_lane = lane_kw // G_DK
    row_kw = lax.broadcasted_iota(I32, (C, G_KW), 0)
    ur = lax.broadcasted_iota(I32, (2 * C, 4 * C), 0) - C
    uc = lax.broadcasted_iota(I32, (2 * C, 4 * C), 1)
    u_mat = ((ur >= 0) & ((uc >= C) | (ur <= uc))).astype(BF16)
    u_mat2 = jnp.concatenate([u_mat, u_mat], axis=0)

    def heads_on_rows(x):
        return jnp.concatenate([jnp.where(head_of_lane == h, x, 0.0) for h in range(G_HEADS)], axis=0)

    def chunk_group(g, S):
        ns = range(n_group)
        rs = [pl.multiple_of((g * n_group + n) * C, C) for n in ns]
        q = [q_ref[0, pl.ds(r, C), :].astype(F32) for r in rs]
        k = [k_ref[0, pl.ds(r, C), :].astype(F32) for r in rs]
        v = [v_ref[0, pl.ds(r, C), :] for r in rs]
        la = [la_ref[0, pl.ds(r, C), :] for r in rs]

        b = [jnp.dot(tril2, jnp.concatenate(_split_bf16(x), axis=0), preferred_element_type=F32)
             for x in la]
        xt = [jnp.concatenate([k[n], la[n]], axis=0).T for n in ns]
        xb = [jnp.dot(jnp.concatenate(_split_bf16(x), axis=1), u_mat2, preferred_element_type=F32)
              for x in xt]
        b_last = [x[:, LANES:] for x in xb]
        kd = [(xt[n] * jnp.exp(b_last[n] - xb[n][:, :LANES])).astype(BF16) for n in ns]
        zeros_v = jnp.zeros((C, G_VW), BF16)
        kv = [jnp.dot(kd[n], jnp.concatenate([v[n], zeros_v], axis=0), preferred_element_type=F32)
              for n in ns]
        kv_d = [jnp.concatenate([x[h * G_DK:(h + 1) * G_DK, h * G_DV:(h + 1) * G_DV] for h in range(G_HEADS)],
                                axis=0) for x in kv]

        states = [S]
        for n in ns:
            states.append(jnp.exp(b_last[n]) * states[n] + kv_d[n])

        r_inter = [jnp.dot(heads_on_rows(q[n] * jnp.exp(b[n])).astype(BF16), states[n].astype(BF16),
                           preferred_element_type=F32) for n in ns]
        o = [jnp.concatenate([x[h * C:(h + 1) * C] for h in range(G_HEADS)], axis=1) for x in r_inter]

        o_rows = [[] for _ in ns]
        for i in range(n_sub):
            r0, r1 = i * G_SUB, (i + 1) * G_SUB
            atts = []
            for n in ns:
                bs = b[n][r0 - 1:r0] if i > 0 else jnp.zeros((1, G_KW), F32)
                qe = q[n][r0:r1] * jnp.exp(b[n][r0:r1] - bs)
                ke = (k[n] * jnp.exp(jnp.where(row_kw < r1, bs - b[n], -jnp.inf))).astype(BF16)
                att = lax.dot_general(heads_on_rows(qe).astype(BF16), ke, (((1,), (1,)), ((), ())),
                                      preferred_element_type=F32)
                tt = lax.broadcasted_iota(I32, att.shape, 0) % G_SUB + r0
                ss = lax.broadcasted_iota(I32, att.shape, 1)
                atts.append(jnp.where(ss <= tt, att, 0.0).astype(BF16))
            for n in ns:
                ov = jnp.dot(atts[n], v[n], preferred_element_type=F32)
                o_rows[n].append(jnp.concatenate(
                    [ov[h * G_SUB:(h + 1) * G_SUB, h * G_DV:(h + 1) * G_DV] for h in range(G_HEADS)], axis=1))

        for n in ns:
            on = o[n] + jnp.concatenate(o_rows[n], axis=0)
            on = _head_rms(on, G_DV) * gain_ref[...] * gr_ref[0, pl.ds(rs[n], C), :].astype(F32)
            o_ref[0, pl.ds(rs[n], C), :] = on.astype(BF16)
        return states[-1]

    s_scr[...] = lax.fori_loop(0, n_chunks // n_group, chunk_group, s_scr[...])

    @pl.when(j == pl.num_programs(1) - 1)
    def _():
        sf_ref[0] = s_scr[...]


def _gla(gq, gk, gv, la, gr, gain, s0):
    B, T, _ = gq.shape
    C = CHUNK
    tb = min(ROW_TILE, T)
    assert T % tb == 0 and tb % C == 0 and C % G_SUB == 0 and 2 * C == LANES
    kw = pl.BlockSpec((1, tb, G_KW), lambda b, j: (b, j, 0))
    vw = pl.BlockSpec((1, tb, G_VW), lambda b, j: (b, j, 0))
    st = pl.BlockSpec((1, G_KW, G_DV), lambda b, j: (b, 0, 0))
    return pl.pallas_call(
        functools.partial(_gla_kernel, C=C, n_chunks=tb // C, n_group=8 if (tb // C) % 8 == 0 else 1),
        grid=(B, T // tb),
        in_specs=[kw, kw, vw, kw, vw, pl.BlockSpec(gain.shape, lambda b, j: (0, 0)), st],
        out_specs=[vw, st],
        out_shape=[jax.ShapeDtypeStruct((B, T, G_VW), BF16), jax.ShapeDtypeStruct((B, G_KW, G_DV), F32)],
        scratch_shapes=[pltpu.VMEM((G_KW, G_DV), F32)],
        compiler_params=_params(("arbitrary", "arbitrary")),
        name="gla",
    )(gq, gk, gv, la, gr, gain, s0)


def _memkv_kernel(mem_ref, gm_ref, w_ref, gk_ref, k_ref, v_ref):
    x = mem_ref[0]
    h = (x * lax.rsqrt(jnp.mean(x * x, -1, keepdims=True) + EPS) * gm_ref[...]).astype(BF16)
    k = jnp.dot(h, w_ref[:, :C_WIDTH], preferred_element_type=F32)
    k_ref[0] = _head_rms(k, C_HEAD_DIM) * gk_ref[...]
    v_ref[0] = jnp.dot(h, w_ref[:, C_WIDTH:], preferred_element_type=F32)


def _memkv(mem, gm, w_kv, gk):
    B, M, _ = mem.shape
    out = pl.BlockSpec((1, M, C_WIDTH), lambda b: (b, 0, 0))
    return pl.pallas_call(
        _memkv_kernel,
        grid=(B,),
        in_specs=[pl.BlockSpec((1, M, D_MODEL), lambda b: (b, 0, 0)),
                  pl.BlockSpec(gm.shape, lambda b: (0, 0)),
                  pl.BlockSpec(w_kv.shape, lambda b: (0, 0)),
                  pl.BlockSpec(gk.shape, lambda b: (0, 0))],
        out_specs=[out, out],
        out_shape=[jax.ShapeDtypeStruct((B, M, C_WIDTH), F32)] * 2,
        compiler_params=_params(("arbitrary",)),
        name="memkv",
    )(mem, gm, w_kv, gk)


def _merge_kernel(x_ref, ya_ref, yb_ref, cq_ref, gt_ref, mk_ref, mv_ref, wa_ref, wb_ref, wc_ref,
                  wo_ref, nf_ref, wr_ref, br_ref,
                  x1_ref, hp_ref, ri_ref, rw_ref, cnt_ref, run_scr, *, tok_base):
    first = jnp.logical_and(pl.program_id(0) == 0, pl.program_id(1) == 0)

    @pl.when(first)
    def _():
        run_scr[...] = jnp.zeros_like(run_scr)

    tm = x_ref.shape[1]
    n_part = 2 if tm % 256 == 0 else 1
    parts = [slice(n * (tm // n_part), (n + 1) * (tm // n_part)) for n in range(n_part)]
    heads = [slice(h * C_HEAD_DIM, (h + 1) * C_HEAD_DIM) for h in range(C_HEADS)]
    mk = [mk_ref[0, :, c].astype(BF16) for c in heads]
    mv = [mv_ref[0, :, c].astype(BF16) for c in heads]
    scores = [[lax.dot_general(cq_ref[0, r, c], k, (((1,), (1,)), ((), ())), preferred_element_type=F32)
               * (C_HEAD_DIM ** -0.5) for c, k in zip(heads, mk)] for r in parts]
    probs = [[jnp.exp(s - s.max(-1, keepdims=True)) for s in ss] for ss in scores]
    sums = [[p.sum(-1, keepdims=True) for p in ps] for ps in probs]
    yc_in = [jnp.concatenate([jnp.dot(p.astype(BF16), v, preferred_element_type=F32) / l
                              for p, l, v in zip(ps, ls, mv)], axis=-1).astype(BF16)
             for ps, ls in zip(probs, sums)]
    y_a = [jnp.dot(ya_ref[0, r, :], wa_ref[...], preferred_element_type=F32) for r in parts]
    y_b = [jnp.dot(yb_ref[0, r, :], wb_ref[...], preferred_element_type=F32) for r in parts]
    y_c = [jnp.dot(y, wc_ref[...], preferred_element_type=F32) for y in yc_in]
    merged = [(gt_ref[0, r, 0:D_MODEL].astype(F32) * a
               + gt_ref[0, r, D_MODEL:2 * D_MODEL].astype(F32) * b
               + gt_ref[0, r, 2 * D_MODEL:3 * D_MODEL].astype(F32) * c).astype(BF16)
              for r, a, b, c in zip(parts, y_a, y_b, y_c)]
    x1s = [x_ref[0, r, :] + jnp.dot(m, wo_ref[...], preferred_element_type=F32) for r, m in zip(parts, merged)]
    for r, v in zip(parts, x1s):
        x1_ref[0, r, :] = v
    h2s = [v * lax.rsqrt(jnp.mean(v * v, -1, keepdims=True) + EPS) * nf_ref[...] for v in x1s]

    logits = jnp.concatenate([jnp.dot(h.astype(BF16), wr_ref[...], preferred_element_type=F32) for h in h2s],
                             axis=0) + br_ref[...]
    h2 = jnp.concatenate(h2s, axis=0)
    lane = lax.broadcasted_iota(I32, (tm, LANES), 1)
    lane_f = lane.astype(F32)
    vals, sels, idxs = [], [], []
    l = logits
    for _ in range(TOP_K):
        m = l.max(-1, keepdims=True)
        idx = jnp.min(jnp.where(l == m, lane_f, float(LANES)), -1, keepdims=True)
        sel = lane_f == idx
        vals.append(m)
        idxs.append(idx)
        sels.append(sel)
        l = jnp.where(sel, -3e38, l)
    es = [jnp.exp(vk - vals[0]) for vk in vals]
    den = es[0] + es[1] + es[2] + es[3]
    cnt = jnp.zeros((tm, LANES), F32)
    for sel in sels:
        cnt = cnt + jnp.where(sel, 1.0, 0.0)
    tp = max(tm, LANES)
    cnt_p = cnt if tp == tm else jnp.concatenate([cnt, jnp.zeros((tp - tm, LANES), F32)], axis=0)
    ri = lax.broadcasted_iota(I32, (tp, tp), 0)
    ci = lax.broadcasted_iota(I32, (tp, tp), 1)
    before = jnp.dot(jnp.where(ci < ri, 1.0, 0.0).astype(BF16), cnt_p.astype(BF16),
                     preferred_element_type=F32)[0:tm] + run_scr[0:1, :]
    r_i = jnp.zeros((tm, LANES), I32)
    r_w = jnp.zeros((tm, LANES), F32)
    for kk in range(TOP_K):
        rank = jnp.sum(jnp.where(sels[kk], before, 0.0), -1, keepdims=True)
        r_i = jnp.where(lane == kk, idxs[kk].astype(I32), r_i)
        r_i = jnp.where(lane == TOP_K + kk, rank.astype(I32), r_i)
        r_w = jnp.where(lane == kk, es[kk] / den, r_w)
    ri_ref[0] = r_i
    rw_ref[0] = r_w
    run_scr[...] = run_scr[...] + jnp.sum(cnt, axis=0, keepdims=True)
    cnt_ref[...] = run_scr[...]

    tok = tok_base + (pl.program_id(0) * pl.num_programs(1) + pl.program_id(1)) * tm \
        + lax.broadcasted_iota(I32, (tm, LANES), 0)
    meta = jnp.where(lane == 0, tok, 0)
    for kk in range(TOP_K):
        meta = jnp.where(lane == 1 + kk, idxs[kk].astype(I32), meta)
    packed = _pack_rows(h2)
    zero = jnp.zeros((tm, LANES), U32)
    slabs = [packed[:, j * LANES:(j + 1) * LANES] for j in range(SLABS)]
    slabs += [lax.bitcast_convert_type(meta, U32)] + [zero] * (ROW_SLABS - SLABS - 1)
    hp_ref[0] = pltpu.einshape("jtl->tjl", jnp.stack(slabs, axis=0))


def _merge(x, ya, yb, cq, gt, mk, mv, wa, wb, wc, wo, nf, wr, br, tok_base):
    B, T, _ = x.shape
    tm = min(ROW_TILE, T)
    assert T % tm == 0

    def row(width):
        return pl.BlockSpec((1, tm, width), lambda b, j: (b, j, 0))

    def full(a):
        return pl.BlockSpec(a.shape, lambda b, j: (0,) * a.ndim)

    mem = pl.BlockSpec((1, N_MEM, C_WIDTH), lambda b, j: (b, 0, 0))
    return pl.pallas_call(
        functools.partial(_merge_kernel, tok_base=tok_base),
        grid=(B, T // tm),
        in_specs=[row(D_MODEL), row(A_WIDTH), row(G_VW), row(C_WIDTH), row(N_BRANCH * D_MODEL), mem, mem,
                  full(wa), full(wb), full(wc), full(wo), full(nf), full(wr), full(br)],
        out_specs=[row(D_MODEL), pl.BlockSpec((1, tm, ROW_SLABS, LANES), lambda b, j: (b, j, 0, 0)), row(LANES),
                   row(LANES), pl.BlockSpec((SUBLANES, LANES), lambda b, j: (0, 0))],
        out_shape=[jax.ShapeDtypeStruct((B, T, D_MODEL), F32), jax.ShapeDtypeStruct((B, T, ROW_SLABS, LANES), U32),
                   jax.ShapeDtypeStruct((B, T, LANES), I32), jax.ShapeDtypeStruct((B, T, LANES), F32),
                   jax.ShapeDtypeStruct((SUBLANES, LANES), F32)],
        scratch_shapes=[pltpu.VMEM((SUBLANES, LANES), F32)],
        compiler_params=_params(("arbitrary", "arbitrary")),
        name="merge",
    )(x, ya, yb, cq, gt, mk, mv, wa, wb, wc, wo, nf, wr, br)


def _dispatch_kernel(dest_ref, dest2_ref, pad_lo_ref, pad_n_ref, hp_ref, hp2_ref, xs_ref, filler_scr, sem, sem2,
                     psem, *, tm, tm2, n_pad, n_tokens):
    def scatter(dref, href, s, n):
        def issue(t, c):
            for kk in range(TOP_K):
                pltpu.make_async_copy(href.at[t], xs_ref.at[dref[0, 0, t * TOP_K + kk]], s).start(priority=kk % 2)
            return c
        lax.fori_loop(0, n, issue, 0, unroll=8)

    def drain(href, s, n):
        for _ in range(TOP_K):
            pltpu.make_async_copy(href, xs_ref.at[pl.ds(0, n)], s).wait()

    scatter(dest_ref, hp_ref, sem, tm)

    @pl.when(pl.program_id(0) == 0)
    def _():
        scatter(dest2_ref, hp2_ref, sem2, tm2)
        shape = (FFN_BLOCK, ROW_SLABS, LANES)
        sub = lax.broadcasted_iota(I32, shape, 1)
        lane = lax.broadcasted_iota(I32, shape, 2)
        filler = jnp.where((sub == INFO_SLAB) & (lane == 0), n_tokens,
                           jnp.where((sub == INFO_SLAB) & (lane <= TOP_K), N_EXPERTS, 0))
        filler_scr[...] = lax.bitcast_convert_type(filler, U32)

        log_blk = FFN_BLOCK.bit_length() - 1

        def span(e, wait):
            lo, n = pad_lo_ref[e], pad_n_ref[e]
            whole = n >> log_blk

            def go(cp):
                cp.wait() if wait else cp.start()

            def block(j, c):
                go(pltpu.make_async_copy(filler_scr, xs_ref.at[pl.ds(lo + j * FFN_BLOCK, FFN_BLOCK)], psem))
                return c

            lax.fori_loop(0, whole, block, 0)
            rem = n - (whole << log_blk)
            for b in reversed(range(log_blk)):
                off = lo + (whole << log_blk) + ((rem >> (b + 1)) << (b + 1))

                @pl.when((rem >> b) & 1 == 1)
                def _():
                    go(pltpu.make_async_copy(filler_scr.at[pl.ds(0, 1 << b)], xs_ref.at[pl.ds(off, 1 << b)], psem))

        def fill(e, c):
            span(e, False)
            return c

        def fill_wait(e, c):
            span(e, True)
            return c

        lax.fori_loop(0, n_pad, fill, 0)
        lax.fori_loop(0, n_pad, fill_wait, 0)
        drain(hp2_ref, sem2, tm2)

    drain(hp_ref, sem, tm)


def _dispatch(dest, hp, dest2, hp2, pad_lo, pad_n, rows):
    N, N2 = hp.shape[0], hp2.shape[0]
    tm = min(ROW_TILE, N)
    assert N % tm == 0
    n_steps = N // tm
    smem = functools.partial(pl.BlockSpec, memory_space=pltpu.SMEM)
    n_pad = pad_lo.shape[0]
    return pl.pallas_call(
        functools.partial(_dispatch_kernel, tm=tm, tm2=N2, n_pad=n_pad, n_tokens=N + N2),
        grid=(n_steps,),
        in_specs=[smem((1, 1, tm * TOP_K), lambda i: (i, 0, 0)),
                  smem((1, 1, N2 * TOP_K), lambda i: (0, 0, 0)),
                  smem((n_pad,), lambda i: (0,)), smem((n_pad,), lambda i: (0,)),
                  pl.BlockSpec((tm, ROW_SLABS, LANES), lambda i: (i, 0, 0)),
                  pl.BlockSpec((N2, ROW_SLABS, LANES), lambda i: (0, 0, 0))],
        out_specs=pl.BlockSpec(memory_space=pl.ANY),
        out_shape=jax.ShapeDtypeStruct((rows, ROW_SLABS, LANES), U32),
        scratch_shapes=[pltpu.VMEM((FFN_BLOCK, ROW_SLABS, LANES), U32), pltpu.SemaphoreType.DMA(()),
                        pltpu.SemaphoreType.DMA(()), pltpu.SemaphoreType.DMA(())],
        compiler_params=_params(("arbitrary",)),
        name="moe_dispatch",
    )(dest.reshape(n_steps, 1, tm * TOP_K), dest2.reshape(1, 1, N2 * TOP_K), pad_lo, pad_n, hp, hp2)


def _ffn_kernel(be_ref, nu_ref, x_ref, wgu_ref, bgu_ref, wd_ref, bd_ref, y4_ref,
                wgu_bf, wd_bf, ybuf, idv, ids_smem, sem_ids, sem_rows, *, n_blk, n_tokens):
    i = pl.program_id(0)
    n_used = nu_ref[0]
    n_assign = n_tokens * TOP_K
    prev = be_ref[jnp.maximum(i - 1, 0)]
    new_expert = jnp.logical_or(i == 0, be_ref[i] != prev)

    def rows_done(s):
        pltpu.make_async_copy(ybuf.at[s], y4_ref.at[pl.ds(0, FFN_BLOCK)], sem_rows.at[s]).wait()

    def ids_copy(s):
        return pltpu.make_async_copy(idv, ids_smem.at[s], sem_ids)

    def start_rows(s, lo, hi):
        for r in range(lo, hi):
            pltpu.make_async_copy(ybuf.at[s, r], y4_ref.at[ids_smem[s, 0, r]],
                                  sem_rows.at[s]).start(priority=r % 2)

    @pl.when(i == 0)
    def _():
        ybuf[1] = jnp.zeros(ybuf.shape[1:], U32)
        spare = pltpu.make_async_copy(ybuf.at[1], y4_ref.at[pl.ds(n_assign, FFN_BLOCK)], sem_rows.at[1])
        spare.start()
        spare.wait()

    @pl.when(new_expert)
    def _():
        wgu_bf[...] = wgu_ref[0].astype(BF16)
        wd_bf[...] = wd_ref[0].astype(BF16)

    def compute(s, flush):
        q = FFN_BLOCK // 4
        if flush:
            ids_copy(1 - s).wait()
            start_rows(1 - s, 0, q)
        xm = pltpu.einshape("tjl->jtl", x_ref[...])
        x_lo, x_hi = _unpack_rows(jnp.concatenate([xm[j] for j in range(SLABS)], axis=-1))
        meta = lax.bitcast_convert_type(xm[INFO_SLAB], I32)
        tok = meta[:, 0:1]
        choice = jnp.zeros_like(tok)
        for kk in range(1, TOP_K):
            choice = jnp.where(meta[:, 1 + kk:2 + kk] == be_ref[i], kk, choice)
        local = lax.broadcasted_iota(I32, tok.shape, 0)
        dest = jnp.where(tok >= n_tokens, n_assign + local, tok * TOP_K + choice)
        dest_t = jnp.broadcast_to(dest.astype(F32), (FFN_BLOCK, LANES)).T
        idv[...] = dest_t[0:SUBLANES].astype(I32)
        ids_copy(s).start()
        gu = (jnp.dot(x_lo.astype(BF16), wgu_bf[0:HALF, :], preferred_element_type=F32)
              + jnp.dot(x_hi.astype(BF16), wgu_bf[HALF:, :], preferred_element_type=F32)
              + bgu_ref[0])
        if flush:
            start_rows(1 - s, q, 2 * q)
        gate = jnp.minimum(gu[:, :D_EXPERT], SWIGLU_LIMIT)
        up = jnp.clip(gu[:, D_EXPERT:], -SWIGLU_LIMIT, SWIGLU_LIMIT)
        act = (up + 1.0) * (gate * _sigmoid(SWIGLU_ALPHA * gate))
        if flush:
            start_rows(1 - s, 2 * q, 3 * q)
        y = jnp.dot(act.astype(BF16), wd_bf[...], preferred_element_type=F32) + bd_ref[0]
        if flush:
            start_rows(1 - s, 3 * q, 4 * q)
        packed = _pack_rows(y)
        pl.when(i >= 2)(functools.partial(rows_done, s))
        _store_slabs(ybuf, (s,), packed)

    def flush_only(s, also):
        ids_copy(s).wait()
        start_rows(s, 0, FFN_BLOCK)
        pl.when(also)(functools.partial(rows_done, 1 - s))
        rows_done(s)

    real = i < n_used
    prev_real = jnp.logical_and(i >= 1, i - 1 < n_used)
    for s in range(2):
        mine = i % 2 == s
        pl.when(mine & real & prev_real)(functools.partial(compute, s, True))
        pl.when(mine & real & jnp.logical_not(prev_real))(functools.partial(compute, s, False))
        pl.when(mine & jnp.logical_not(real) & prev_real)(functools.partial(flush_only, 1 - s, i >= 2))
        pl.when(mine & real & (i == n_blk - 1))(functools.partial(flush_only, s, i >= 1))


def _ffn(blk_expert, n_used, xs, n_tokens, w_gu, b_gu, w_d, b_d):
    P = xs.shape[0]
    nblk = P // FFN_BLOCK
    grid_spec = pltpu.PrefetchScalarGridSpec(
        num_scalar_prefetch=2,
        grid=(nblk,),
        in_specs=[pl.BlockSpec((FFN_BLOCK, ROW_SLABS, LANES), lambda i, be, nu: (jnp.minimum(i, nu[0] - 1), 0, 0)),
                  pl.BlockSpec((1, D_MODEL, 2 * D_EXPERT), lambda i, be, nu: (be[i], 0, 0)),
                  pl.BlockSpec((1, 1, 2 * D_EXPERT), lambda i, be, nu: (be[i], 0, 0)),
                  pl.BlockSpec((1, D_EXPERT, D_MODEL), lambda i, be, nu: (be[i], 0, 0)),
                  pl.BlockSpec((1, 1, D_MODEL), lambda i, be, nu: (be[i], 0, 0))],
        out_specs=pl.BlockSpec(memory_space=pl.ANY),
        scratch_shapes=[pltpu.VMEM((D_MODEL, 2 * D_EXPERT), BF16), pltpu.VMEM((D_EXPERT, D_MODEL), BF16),
                        pltpu.VMEM((2, FFN_BLOCK, SLABS, LANES), U32), pltpu.VMEM((SUBLANES, FFN_BLOCK), I32),
                        pltpu.SMEM((2, SUBLANES, FFN_BLOCK), I32), pltpu.SemaphoreType.DMA(()),
                        pltpu.SemaphoreType.DMA((2,))],
    )
    return pl.pallas_call(
        functools.partial(_ffn_kernel, n_blk=nblk, n_tokens=n_tokens),
        grid_spec=grid_spec,
        out_shape=jax.ShapeDtypeStruct((n_tokens * TOP_K + FFN_BLOCK, SLABS, LANES), U32),
        compiler_params=_params(("arbitrary",)),
        name="moe_ffn",
    )(blk_expert, n_used, xs, w_gu, b_gu.reshape(N_EXPERTS, 1, -1), w_d, b_d.reshape(N_EXPERTS, 1, -1))


def _combine_kernel(x1_ref, rw_ref, y_ref, o_ref, *, tm):
    w = rw_ref[...]
    acc_lo = x1_ref[:, :HALF]
    acc_hi = x1_ref[:, HALF:]
    for kk in range(TOP_K):
        rows = y_ref[pl.ds(kk, tm, stride=TOP_K)]
        slab_major = pltpu.einshape("tjl->jtl", rows)
        lo, hi = _unpack_rows(jnp.concatenate([slab_major[j] for j in range(SLABS)], axis=-1))
        wk = w[:, kk:kk + 1]
        acc_lo = acc_lo + wk * lo
        acc_hi = acc_hi + wk * hi
    o_ref[:, :HALF] = acc_lo
    o_ref[:, HALF:] = acc_hi


def _combine(x1, rw, y4, tok_base):
    N = x1.shape[0]
    tm = min(ROW_TILE, N)
    assert N % tm == 0 and tok_base % tm == 0
    first = tok_base // tm
    return pl.pallas_call(
        functools.partial(_combine_kernel, tm=tm),
        grid=(N // tm,),
        in_specs=[pl.BlockSpec((tm, D_MODEL), lambda i: (i, 0)),
                  pl.BlockSpec((tm, LANES), lambda i: (i, 0)),
                  pl.BlockSpec((tm * TOP_K, SLABS, LANES), lambda i: (first + i, 0, 0))],
        out_specs=pl.BlockSpec((tm, D_MODEL), lambda i: (i, 0)),
        out_shape=jax.ShapeDtypeStruct((N, D_MODEL), F32),
        compiler_params=_params(("arbitrary",)),
        name="moe_combine",
    )(x1, rw, y4)


def _moe(groups, w_gu, b_gu, w_d, b_d):
    n_tokens = sum(g[0].shape[0] for g in groups)
    n_assign = n_tokens * TOP_K
    nblk = (n_assign + N_EXPERTS * (FFN_BLOCK - 1) + FFN_BLOCK - 1) // FFN_BLOCK
    counts = [g[4].astype(I32) for g in groups]
    total = sum(counts)
    padded = (total + FFN_BLOCK - 1) // FFN_BLOCK * FFN_BLOCK
    pend = jnp.cumsum(padded)
    pstart = pend - padded
    n_used = pend[-1:] // FFN_BLOCK
    blk = jnp.minimum(jnp.arange(nblk, dtype=I32), n_used[0] - 1) * FFN_BLOCK
    blk_expert = jnp.minimum(jnp.sum(pend[None, :] <= blk[:, None], axis=1), N_EXPERTS - 1).astype(I32)

    experts = jnp.arange(N_EXPERTS, dtype=I32)
    dests = []
    base = pstart
    for g, c in zip(groups, counts):
        idx, rank = g[2][:, :TOP_K], g[2][:, TOP_K:2 * TOP_K]
        dests.append(jnp.sum(jnp.where(idx[..., None] == experts, base, 0), axis=-1) + rank)
        base = base + c

    rows = nblk * FFN_BLOCK
    pad_lo = jnp.concatenate([pstart + total, pend[-1:]])
    pad_n = jnp.concatenate([padded - total, rows - pend[-1:]])
    (g_main, g_small), (d_main, d_small) = groups, dests
    xs = _dispatch(d_main, g_main[1], d_small, g_small[1], pad_lo, pad_n, rows)
    y4 = _ffn(blk_expert, n_used.astype(I32), xs, n_tokens, w_gu, b_gu, w_d, b_d)
    return [_combine(g_main[0], g_main[3], y4, 0), _combine(g_small[0], g_small[3], y4, g_main[0].shape[0])]


def _tile_lanes(g, reps):
    return jnp.tile(g.astype(F32), reps)[None, :]


def kernel(x_prompt, x_sample, mem_prompt, cache_attn_k, cache_attn_v, state_gla, cache_mem_k, cache_mem_v, norm_mix, w_in, a_q_norm, a_k_norm, rel_bias_table, w_a_o, w_gla_a_up, b_gla_a, gla_out_norm, w_b_o, c_q_norm, c_k_norm, norm_mem, w_mem_kv, w_c_o, b_gate, w_out, norm_ffn, w_router, b_router, w_gate_up, b_gate_up, w_down, b_down):
    depth = norm_mix.shape[0]
    assert depth == 1
    l = 0
    B, S, _ = x_prompt.shape
    Bs, Ts, _ = x_sample.shape
    keep = min(WINDOW, S)

    w = w_in[l]
    sizes = (A_WIDTH, A_WIDTH, A_WIDTH, G_KW, G_KW, G_VW, G_VW, G_RANK, C_WIDTH, N_BRANCH * D_MODEL)
    offs = [0]
    for s_ in sizes:
        offs.append(offs[-1] + s_)
    seg = [w[:, offs[i]:offs[i + 1]] for i in range(len(sizes))]
    w_r = jnp.concatenate(seg[0:7] + [seg[8], seg[9], seg[7], jnp.zeros((D_MODEL, LANES - G_RANK), F32)],
                          axis=1).astype(BF16)
    nm = norm_mix[l][None, :]
    aqn = _tile_lanes(a_q_norm[l], A_HEADS)
    akn = _tile_lanes(a_k_norm[l], A_HEADS)
    cqn = _tile_lanes(c_q_norm[l], C_HEADS)
    ckn = _tile_lanes(c_k_norm[l], C_HEADS)
    gon = _tile_lanes(gla_out_norm[l], G_HEADS)
    wup = jnp.concatenate([w_gla_a_up[l], jnp.zeros((LANES - G_RANK, G_KW), F32)], axis=0).astype(BF16)
    bla = b_gla_a[l][None, :]
    bg = b_gate[l][None, :]
    wa, wb, wc, wo = (t[l].astype(BF16) for t in (w_a_o, w_b_o, w_c_o, w_out))
    nf = norm_ffn[l][None, :]
    wr = jnp.concatenate([w_router[l], jnp.zeros((D_MODEL, LANES - N_EXPERTS), F32)], axis=1).astype(BF16)
    br = jnp.concatenate([b_router[l], jnp.full((LANES - N_EXPERTS,), NEG_INF, F32)])[None, :]
    table = rel_bias_table[l]

    mk, mv = _memkv(mem_prompt, norm_mem[l][None, :], w_mem_kv[l].astype(BF16), ckn)
    (aq, ak, av, gq, gk, gv, gr, la, cq, gt, ak_tail, av_tail) = _inproj(
        x_prompt, keep, nm, w_r, aqn, akn, cqn, wup, bla, bg)
    ya = _attn_prompt(aq, ak, av, table)
    yb, s_prompt = _gla(gq, gk, gv, la, gr, gon, jnp.zeros((B, G_KW, G_DV), F32))
    x1_p, hp_p, ri_p, rw_p, cnt_p = _merge(x_prompt, ya, yb, cq, gt, mk, mv, wa, wb, wc, wo, nf, wr, br, 0)

    (aq, ak, av, gq, gk, gv, gr, la, cq, gt, ak_new, av_new) = _inproj(
        x_sample.reshape(1, Bs * Ts, D_MODEL), Bs * Ts, nm, w_r, aqn, akn, cqn, wup, bla, bg)
    rs = lambda t: t.reshape(Bs, Ts, t.shape[-1])
    P = cache_attn_k.shape[2]
    ya = _attn_sample(rs(aq), rs(ak), rs(av), cache_attn_k[l].reshape(Bs, P, A_WIDTH),
                      cache_attn_v[l].reshape(Bs, P, A_WIDTH), table)
    t_pad = (Ts + CHUNK - 1) // CHUNK * CHUNK
    zp = lambda t: jnp.pad(rs(t), ((0, 0), (0, t_pad - Ts), (0, 0)))
    yb, s_sample = _gla(zp(gq), zp(gk), zp(gv), zp(la), zp(gr), gon, state_gla[l].reshape(Bs, G_KW, G_DV))
    yb = yb[:, :Ts]
    x1_s, hp_s, ri_s, rw_s, cnt_s = _merge(
        x_sample, ya, yb, rs(cq), rs(gt), cache_mem_k[l].reshape(Bs, N_MEM, C_WIDTH),
        cache_mem_v[l].reshape(Bs, N_MEM, C_WIDTH), wa, wb, wc, wo, nf, wr, br, B * S)

    flat = lambda t: t.reshape((-1,) + t.shape[2:])
    y_p, y_s = _moe(
        [(flat(x1_p), flat(hp_p), flat(ri_p), flat(rw_p), cnt_p[0, :N_EXPERTS]),
         (flat(x1_s), flat(hp_s), flat(ri_s), flat(rw_s), cnt_s[0, :N_EXPERTS])],
        w_gate_up[l], b_gate_up[l], w_down[l], b_down[l])

    return (y_p.reshape(B, S, D_MODEL), y_s.reshape(Bs, Ts, D_MODEL),
            ak_tail.reshape(1, B, keep, A_HEADS, A_HEAD_DIM), av_tail.reshape(1, B, keep, A_HEADS, A_HEAD_DIM),
            s_prompt.reshape(1, B, G_HEADS, G_DK, G_DV),
            mk.reshape(1, B, N_MEM, C_HEADS, C_HEAD_DIM), mv.reshape(1, B, N_MEM, C_HEADS, C_HEAD_DIM),
            ak_new.reshape(1, Bs, Ts, A_HEADS, A_HEAD_DIM), av_new.reshape(1, Bs, Ts, A_HEADS, A_HEAD_DIM),
            s_sample.reshape(1, Bs, G_HEADS, G_DK, G_DV))
```
